```python
import math
import jax
import jax.numpy as jnp
from jax import lax
import numpy as np

D_MODEL = 2048
BATCH = 32
SEQ = 256
DEPTH = 2
DEC_BATCH = 4
DEC_SEQ = 1024
PAST_LEN = 256

GRID_W = 64
NA_HEADS = 8
NA_HEAD_DIM = 128
NA_WIDTH = NA_HEADS * NA_HEAD_DIM
NA_WIN_ROWS = 8
NA_WIN_COLS = 16
NA_Q_BLOCK = 128
HY_WIDTH = D_MODEL // 4
HY_ORDER = 2
HY_BANDS = 16
HY_EMB = 1 + 2 * HY_BANDS
HY_FILTER_WIDTH = 64
GDN_HEADS = 4
GDN_DK = 128
GDN_DV = 128
GDN_WIDTH = GDN_HEADS * GDN_DV
GDN_CHUNK = 64
SHORT_CONV = 3
D_FF = 5632
N_BRANCH = 3
N_IN = 3 * NA_WIDTH + 3 * HY_WIDTH + 3 * GDN_HEADS * GDN_DK + GDN_WIDTH + 4 * GDN_HEADS + N_BRANCH * D_MODEL
NORM_EPS = 1e-6
NEG_INF = -1e30

kernel_name = 'hybrid_diffusion_na_hyena_gdn_step'


def _rmsnorm(x, g):
    xf = x.astype(jnp.float32)
    y = xf * lax.rsqrt(jnp.mean(xf * xf, axis=-1, keepdims=True) + NORM_EPS)
    return (y * g.astype(jnp.float32)).astype(x.dtype)


def _l2norm(x):
    return x * lax.rsqrt(jnp.sum(x * x, axis=-1, keepdims=True) + NORM_EPS)


def _modulate(x, g, shift, scale):
    return _rmsnorm(x, g) * (1 + scale[..., None, :]) + shift[..., None, :]


def _centred_conv(x, w, b=None):
    k = w.shape[0]
    pad = k // 2
    length = x.shape[1]
    xp = jnp.pad(x, ((0, 0), (pad, pad), (0, 0)))
    y = sum(xp[:, i:i + length] * w[i] for i in range(k))
    return y if b is None else y + b


def _context_attention(q, k, v):
    b, l, h, dh = q.shape
    nblk = l // NA_Q_BLOCK
    qb = (q * dh ** -0.5).reshape(b, nblk, NA_Q_BLOCK, h, dh).swapaxes(0, 1)

    def block(q_blk):
        s = jnp.einsum('bqhd,bkhd->bhqk', q_blk, k).astype(jnp.float32)
        p = jax.nn.softmax(s, axis=-1).astype(v.dtype)
        return jnp.einsum('bhqk,bkhd->bqhd', p, v)

    o = lax.map(block, qb)
    return o.swapaxes(0, 1).reshape(b, l, h * dh)


def _neighbourhood_attention(q, k, v, ck, cv, rpb):
    b, t, h, dh = q.shape
    rows = t // GRID_W
    wr = min(NA_WIN_ROWS, rows)
    qg = (q * dh ** -0.5).reshape(b, rows, GRID_W, h, dh)
    kg = k.reshape(b, rows, GRID_W, h, dh)
    vg = v.reshape(b, rows, GRID_W, h, dh)
    col = jnp.arange(GRID_W)
    col_start = jnp.clip(col - NA_WIN_COLS // 2, 0, GRID_W - NA_WIN_COLS)
    col_mask = (col[None, :] >= col_start[:, None]) & (col[None, :] < col_start[:, None] + NA_WIN_COLS)
    rel_col_idx = jnp.clip(col[None, :] - col[:, None] + NA_WIN_COLS - 1, 0, 2 * NA_WIN_COLS - 2)
    rpb_cols = rpb[:, :, rel_col_idx]
    n_win = wr * GRID_W

    def row_block(r):
        start = jnp.clip(r - wr // 2, 0, rows - wr)
        q_r = lax.dynamic_index_in_dim(qg, r, axis=1, keepdims=False)
        k_r = lax.dynamic_slice_in_dim(kg, start, wr, axis=1)
        v_r = lax.dynamic_slice_in_dim(vg, start, wr, axis=1)
        rel_row_idx = start + jnp.arange(wr) - r + NA_WIN_ROWS - 1
        bias = jnp.take(rpb_cols, rel_row_idx, axis=1).transpose(0, 2, 1, 3)
        s_win = jnp.einsum('bqhd,brkhd->bhqrk', q_r, k_r).astype(jnp.float32)
        s_win = jnp.where(col_mask[:, None, :], s_win + bias.astype(jnp.float32), NEG_INF)
        s_ctx = jnp.einsum('bqhd,blhd->bhql', q_r, ck).astype(jnp.float32)
        s = jnp.concatenate([s_win.reshape(b, h, GRID_W, n_win), s_ctx], axis=-1)
        p = jax.nn.softmax(s, axis=-1).astype(v.dtype)
        p_win = p[..., :n_win].reshape(b, h, GRID_W, wr, GRID_W)
        return (jnp.einsum('bhqrk,brkhd->bqhd', p_win, v_r)
                + jnp.einsum('bhql,blhd->bqhd', p[..., n_win:], cv))

    o = lax.map(row_block, jnp.arange(rows))
    return o.transpose(1, 0, 2, 3, 4).reshape(b, t, h * dh)


def _hyena_filters(length, w1, b1, freq, w2, b2, w3, decay):
    f32 = jnp.float32
    t_norm = jnp.linspace(0.0, 1.0, length, dtype=f32)
    t_idx = jnp.arange(length, dtype=f32)
    bands = jnp.linspace(1e-4, HY_BANDS - 1, HY_BANDS, dtype=f32)
    ang = (2.0 * math.pi / length) * t_idx[:, None] * bands[None, :]
    z = jnp.concatenate([t_norm[:, None], jnp.cos(ang), jnp.sin(ang)], axis=-1)
    freq = freq.astype(f32)
    hdn = jnp.sin(freq[0] * (z @ w1.astype(f32) + b1.astype(f32)))
    hdn = jnp.sin(freq[1] * (hdn @ w2.astype(f32) + b2.astype(f32)))
    filt = (hdn @ w3.astype(f32)).reshape(length, HY_ORDER, 2, HY_WIDTH)
    filt = filt * jnp.exp(-t_norm[:, None, None, None] * jnp.abs(decay.astype(f32)))
    fwd, bwd = filt[:, :, 0], filt[:, :, 1]
    return jnp.concatenate([fwd, jnp.zeros_like(fwd[:1]), bwd[:length - 1][::-1]], axis=0)


def _hyena(u, conv_w, conv_b, w1, b1, freq, w2, b2, w3, decay, skip):
    f32 = jnp.float32
    length = u.shape[1]
    n_fft = 2 * length
    uc = _centred_conv(u, conv_w, conv_b)
    v, x1, x2 = jnp.split(uc, 3, axis=-1)
    filt = _hyena_filters(length, w1, b1, freq, w2, b2, w3, decay)
    z = v.astype(f32)
    for o, gate in enumerate((x1, x2)):
        zf = jnp.fft.rfft(z, n=n_fft, axis=1)
        hf = jnp.fft.rfft(filt[:, o], n=n_fft, axis=0)
        conv = jnp.fft.irfft(zf * hf[None], n=n_fft, axis=1)[:, :length]
        z = gate.astype(f32) * (conv + z * skip[o].astype(f32))
    return z.astype(u.dtype)


def _gdn_chunk_scan(q, k, v, g, beta, s0):
    f32 = jnp.float32
    b, t, h, dk = q.shape
    dv = v.shape[-1]
    n = t // GDN_CHUNK

    def chunks(a):
        a = a.astype(f32).reshape((b, n, GDN_CHUNK, h) + a.shape[3:])
        return jnp.moveaxis(a, 3, 1)

    q = chunks(q) * dk ** -0.5
    k = chunks(k)
    v = chunks(v)
    g = chunks(g)
    beta = chunks(beta)
    gc = jnp.cumsum(g, axis=-1)
    idx = jnp.arange(GDN_CHUNK)
    causal = idx[:, None] >= idx[None, :]
    strict = idx[:, None] > idx[None, :]
    diff = gc[..., :, None] - gc[..., None, :]
    decay = jnp.where(causal, jnp.exp(jnp.where(causal, diff, 0.0)), 0.0)
    kb = k * beta[..., None]
    a_mat = (jnp.where(strict, jnp.einsum('bhnid,bhnjd->bhnij', kb, k) * decay, 0.0)
             + jnp.eye(GDN_CHUNK, dtype=f32))
    rhs = jnp.concatenate([v * beta[..., None], kb * jnp.exp(gc)[..., None]], axis=-1)
    sol = lax.linalg.triangular_solve(a_mat, rhs, left_side=True, lower=True, unit_diagonal=True)
    u, w = sol[..., :dv], sol[..., dv:]
    attn = jnp.where(causal, jnp.einsum('bhnid,bhnjd->bhnij', q, k) * decay, 0.0)
    q_dec = q * jnp.exp(gc)[..., None]
    k_dec = k * jnp.exp(gc[..., -1:] - gc)[..., None]
    g_tot = jnp.exp(gc[..., -1])
    xs = tuple(jnp.moveaxis(a, 2, 0) for a in (u, w, attn, q_dec, k_dec, g_tot))

    def step(s, inp):
        u_c, w_c, attn_c, qd_c, kd_c, gt_c = inp
        v_new = u_c - jnp.einsum('bhck,bhkv->bhcv', w_c, s)
        o_c = jnp.einsum('bhck,bhkv->bhcv', qd_c, s) + jnp.einsum('bhij,bhjv->bhiv', attn_c, v_new)
        s = s * gt_c[..., None, None] + jnp.einsum('bhck,bhcv->bhkv', kd_c, v_new)
        return s, o_c

    s_final, o = lax.scan(step, s0.astype(f32), xs)
    o = jnp.transpose(o, (1, 0, 3, 2, 4)).reshape(b, t, h, dv)
    return o, s_final


def _gated_deltanet(qkv, z, b_logit, a_logit, conv_w, a_log, dt_bias, norm_g, s_f0, s_b0):
    f32 = jnp.float32
    bsz, t, _ = qkv.shape
    qkv = jax.nn.silu(_centred_conv(qkv, conv_w))
    q, k, v = jnp.split(qkv, 3, axis=-1)
    q = _l2norm(q.reshape(bsz, t, GDN_HEADS, GDN_DK).astype(f32))
    k = _l2norm(k.reshape(bsz, t, GDN_HEADS, GDN_DK).astype(f32))
    v = v.reshape(bsz, t, GDN_HEADS, GDN_DV)
    beta = jax.nn.sigmoid(b_logit.astype(f32)).reshape(bsz, t, 2, GDN_HEADS)
    g = -jnp.exp(a_log.astype(f32)) * jax.nn.softplus(
        a_logit.astype(f32).reshape(bsz, t, 2, GDN_HEADS) + dt_bias.astype(f32))
    o_f, s_f = _gdn_chunk_scan(q, k, v, g[:, :, 0], beta[:, :, 0], s_f0)
    o_b, s_b = _gdn_chunk_scan(jnp.flip(q, 1), jnp.flip(k, 1), jnp.flip(v, 1),
                               jnp.flip(g[:, :, 1], 1), jnp.flip(beta[:, :, 1], 1), s_b0)
    o = o_f + jnp.flip(o_b, 1)
    o = _rmsnorm(o, norm_g) * jax.nn.silu(z.reshape(bsz, t, GDN_HEADS, GDN_DV).astype(f32))
    return o.reshape(bsz, t, GDN_WIDTH).astype(qkv.dtype), s_f, s_b


def _layer(x, cond, lp, ctx=None):
    b, l, _ = x.shape
    mod = jax.nn.silu(cond) @ lp['w_mod'] + lp['b_mod']
    sh1, sc1, gt1, sh2, sc2, gt2 = jnp.split(mod, 6, axis=-1)
    h = _modulate(x, lp['ln1_g'], sh1, sc1)
    proj = h @ lp['w_in']
    sizes = (NA_WIDTH, NA_WIDTH, NA_WIDTH, 3 * HY_WIDTH, 3 * GDN_HEADS * GDN_DK, GDN_WIDTH,
             2 * GDN_HEADS, 2 * GDN_HEADS, N_BRANCH * D_MODEL)
    qa, ka, va, hy_in, gdn_qkv, gdn_z, gdn_b, gdn_a, gate_logits = jnp.split(
        proj, np.cumsum(sizes)[:-1].tolist(), axis=-1)
    qa = qa.reshape(b, l, NA_HEADS, NA_HEAD_DIM)
    ka = ka.reshape(b, l, NA_HEADS, NA_HEAD_DIM)
    va = va.reshape(b, l, NA_HEADS, NA_HEAD_DIM)
    if ctx is None:
        y_na = _context_attention(qa, ka, va)
        s_f0 = jnp.zeros((b, GDN_HEADS, GDN_DK, GDN_DV), jnp.float32)
        s_b0 = s_f0
    else:
        ctx_k, ctx_v, s_f0, s_b0 = ctx
        y_na = _neighbourhood_attention(qa, ka, va, ctx_k, ctx_v, lp['na_rpb'])
    y_hy = _hyena(hy_in, lp['hy_conv_w'], lp['hy_conv_b'], lp['hy_w1'], lp['hy_b1'], lp['hy_freq'],
                  lp['hy_w2'], lp['hy_b2'], lp['hy_w3'], lp['hy_decay'], lp['hy_skip'])
    y_gdn, s_f, s_b = _gated_deltanet(gdn_qkv, gdn_z, gdn_b, gdn_a, lp['gdn_conv_w'], lp['gdn_a_log'],
                                      lp['gdn_dt_bias'], lp['gdn_norm_g'], s_f0, s_b0)
    gates = jax.nn.sigmoid((gate_logits + lp['b_gate']).astype(jnp.float32)).astype(x.dtype)
    g_na, g_hy, g_gdn = jnp.split(gates, 3, axis=-1)
    merged = g_na * (y_na @ lp['w_pa']) + g_hy * (y_hy @ lp['w_pb']) + g_gdn * (y_gdn @ lp['w_pc'])
    x = x + gt1[..., None, :] * (merged @ lp['w_out'])
    h = _modulate(x, lp['ln2_g'], sh2, sc2)
    up = _centred_conv(h @ lp['ffn_w_up'], lp['ffn_conv_w'], lp['ffn_conv_b'])
    ua, ub = jnp.split(up, 2, axis=-1)
    x = x + gt2[..., None, :] * ((jax.nn.silu(ua) * ub) @ lp['ffn_w_down'])
    return x, ka, va, s_f.astype(x.dtype), s_b.astype(x.dtype)


def setup_inputs(seed: int = 0) -> dict:
    key = jax.random.key(seed)
    keys = iter(jax.random.split(key, 48))
    f32 = jnp.float32
    d = D_MODEL

    def nrm(shape, scale):
        return jax.random.normal(next(keys), shape, f32) * scale

    def unif(shape, lo, hi):
        return jax.random.uniform(next(keys), shape, f32, lo, hi)

    dt = jnp.exp(unif((DEPTH, 2, GDN_HEADS), math.log(1e-3), math.log(1e-1)))
    return {
        'x_prompt': nrm((BATCH, SEQ, d), 1.0),
        'x_sample': nrm((DEC_BATCH, DEC_SEQ, d), 1.0),
        'cache_k': nrm((DEC_BATCH, DEPTH, PAST_LEN, NA_HEADS, NA_HEAD_DIM), 1.0),
        'cache_v': nrm((DEC_BATCH, DEPTH, PAST_LEN, NA_HEADS, NA_HEAD_DIM), 1.0),
        'state_fwd': nrm((DEC_BATCH, DEPTH, GDN_HEADS, GDN_DK, GDN_DV), 0.1),
        'state_bwd': nrm((DEC_BATCH, DEPTH, GDN_HEADS, GDN_DK, GDN_DV), 0.1),
        'c': nrm((DEC_BATCH, d), 1.0),
        'c_ctx': nrm((d,), 1.0),
        'ln1_g': 1.0 + nrm((DEPTH, d), 0.02),
        'ln2_g': 1.0 + nrm((DEPTH, d), 0.02),
        'w_mod': nrm((DEPTH, d, 6 * d), 0.5 * d ** -0.5),
        'b_mod': nrm((DEPTH, 6 * d), 0.01),
        'w_in': nrm((DEPTH, d, N_IN), d ** -0.5),
        'na_rpb': nrm((DEPTH, NA_HEADS, 2 * NA_WIN_ROWS - 1, 2 * NA_WIN_COLS - 1), 0.02),
        'hy_conv_w': nrm((DEPTH, SHORT_CONV, 3 * HY_WIDTH), 0.5),
        'hy_conv_b': nrm((DEPTH, 3 * HY_WIDTH), 0.01),
        'hy_w1': nrm((DEPTH, HY_EMB, HY_FILTER_WIDTH), HY_EMB ** -0.5),
        'hy_b1': nrm((DEPTH, HY_FILTER_WIDTH), 0.1),
        'hy_freq': 1.0 + nrm((DEPTH, 2, HY_FILTER_WIDTH), 0.02),
        'hy_w2': nrm((DEPTH, HY_FILTER_WIDTH, HY_FILTER_WIDTH), HY_FILTER_WIDTH ** -0.5),
        'hy_b2': nrm((DEPTH, HY_FILTER_WIDTH), 0.1),
        'hy_w3': nrm((DEPTH, HY_FILTER_WIDTH, HY_ORDER * 2 * HY_WIDTH), 0.1 * HY_FILTER_WIDTH ** -0.5),
        'hy_decay': unif((DEPTH, HY_ORDER, 2, HY_WIDTH), 3.07, 15.35),
        'hy_skip': nrm((DEPTH, HY_ORDER, HY_WIDTH), 0.5),
        'gdn_conv_w': nrm((DEPTH, SHORT_CONV, 3 * GDN_HEADS * GDN_DK), 0.5),
        'gdn_a_log': jnp.log(unif((DEPTH, 2, GDN_HEADS), 1.0, 16.0)),
        'gdn_dt_bias': dt + jnp.log(-jnp.expm1(-dt)),
        'gdn_norm_g': 1.0 + nrm((DEPTH, GDN_DV), 0.02),
        'w_pa': nrm((DEPTH, NA_WIDTH, d), NA_WIDTH ** -0.5),
        'w_pb': nrm((DEPTH, HY_WIDTH, d), HY_WIDTH ** -0.5),
        'w_pc': nrm((DEPTH, GDN_WIDTH, d), GDN_WIDTH ** -0.5),
        'b_gate': nrm((DEPTH, N_BRANCH * d), 0.01),
        'w_out': nrm((DEPTH, d, d), d ** -0.5),
        'ffn_w_up': nrm((DEPTH, d, 2 * D_FF), d ** -0.5),
        'ffn_conv_w': nrm((DEPTH, SHORT_CONV, 2 * D_FF), SHORT_CONV ** -0.5),
        'ffn_conv_b': nrm((DEPTH, 2 * D_FF), 0.01),
        'ffn_w_down': nrm((DEPTH, D_FF, d), D_FF ** -0.5),
        'final_g': 1.0 + nrm((d,), 0.02),
    }


def reference(x_prompt, x_sample, cache_k, cache_v, state_fwd, state_bwd, c, c_ctx,
              ln1_g, ln2_g, w_mod, b_mod, w_in, na_rpb, hy_conv_w, hy_conv_b, hy_w1, hy_b1, hy_freq,
              hy_w2, hy_b2, hy_w3, hy_decay, hy_skip, gdn_conv_w, gdn_a_log, gdn_dt_bias, gdn_norm_g,
              w_pa, w_pb, w_pc, b_gate, w_out, ffn_w_up, ffn_conv_w, ffn_conv_b, ffn_w_down, final_g):
    y_p = x_prompt
    y_s = x_sample
    ks, vs, sfs, sbs = [], [], [], []
    for i in range(DEPTH):
        lp = {
            'ln1_g': ln1_g[i], 'ln2_g': ln2_g[i], 'w_mod': w_mod[i], 'b_mod': b_mod[i], 'w_in': w_in[i],
            'na_rpb': na_rpb[i], 'hy_conv_w': hy_conv_w[i], 'hy_conv_b': hy_conv_b[i],
            'hy_w1': hy_w1[i], 'hy_b1': hy_b1[i], 'hy_freq': hy_freq[i], 'hy_w2': hy_w2[i],
            'hy_b2': hy_b2[i], 'hy_w3': hy_w3[i], 'hy_decay': hy_decay[i], 'hy_skip': hy_skip[i],
            'gdn_conv_w': gdn_conv_w[i], 'gdn_a_log': gdn_a_log[i], 'gdn_dt_bias': gdn_dt_bias[i],
            'gdn_norm_g': gdn_norm_g[i], 'w_pa': w_pa[i], 'w_pb': w_pb[i], 'w_pc': w_pc[i],
            'b_gate': b_gate[i], 'w_out': w_out[i], 'ffn_w_up': ffn_w_up[i], 'ffn_conv_w': ffn_conv_w[i],
            'ffn_conv_b': ffn_conv_b[i], 'ffn_w_down': ffn_w_down[i],
        }
        y_p, k_l, v_l, sf_l, sb_l = _layer(y_p, c_ctx, lp)
        ks.append(k_l)
        vs.append(v_l)
        sfs.append(sf_l)
        sbs.append(sb_l)
        y_s = _layer(y_s, c, lp, (cache_k[:, i], cache_v[:, i], state_fwd[:, i], state_bwd[:, i]))[0]
    y_prompt = _rmsnorm(y_p, final_g)
    y_sample = _rmsnorm(y_s, final_g)
    new_cache_k = jnp.stack(ks, axis=1)
    new_cache_v = jnp.stack(vs, axis=1)
    new_state_fwd = jnp.stack(sfs, axis=1)
    new_state_bwd = jnp.stack(sbs, axis=1)
    return (y_prompt, y_sample, new_cache_k, new_cache_v, new_state_fwd, new_state_bwd)
```

```python
import functools
import math

import jax
import jax.numpy as jnp
import numpy as np
from jax import lax
from jax.experimental import pallas as pl
from jax.experimental.pallas import tpu as pltpu

F32 = jnp.float32
BF16 = jnp.bfloat16

D = 2048
BATCH, SEQ = 32, 256
DEC_BATCH, DEC_SEQ = 4, 1024
DEPTH = 2
GRID_W = 64
NA_HEADS, NA_DH = 8, 128
NA_WIDTH = NA_HEADS * NA_DH
NA_WIN_ROWS, NA_WIN_COLS = 8, 16
HY_WIDTH = 512
HY_ORDER = 2
HY_BANDS = 16
GDN_HEADS, GDN_DK, GDN_DV = 4, 128, 128
GDN_WIDTH = GDN_HEADS * GDN_DV
GDN_CHUNK = 64
D_FF = 5632
N_BRANCH = 3
NORM_EPS = 1e-6
NEG_INF = -1e30

T_P = BATCH * SEQ
T_S = DEC_BATCH * DEC_SEQ
T = T_P + T_S
N_COND = 8

C_Q, C_K, C_V = 0, NA_WIDTH, 2 * NA_WIDTH
C_HY = 3 * NA_WIDTH
C_GDN = C_HY + 3 * HY_WIDTH
C_Z = C_GDN + 3 * GDN_HEADS * GDN_DK
C_BA = C_Z + GDN_WIDTH
C_GATE = C_BA + 4 * GDN_HEADS
N_IN = C_GATE + N_BRANCH * D
GATE_BLOCK = C_BA
GATE_SHIFT = C_GATE - C_BA

LANE = 128
VMEM_LIMIT = 56 * 1024 * 1024


def _cparams(sem):
    return pltpu.CompilerParams(dimension_semantics=sem, vmem_limit_bytes=VMEM_LIMIT)


def _cond_row(tile, rows_per_tile):
    first_latent = T_P // rows_per_tile
    per_seq = DEC_SEQ // rows_per_tile
    return jnp.where(tile < first_latent, 0, 1 + (tile - first_latent) // per_seq)


def _mod_kernel(c_ref, w_ref, b_ref, o_ref):
    c = c_ref[...]
    s = (c * jax.nn.sigmoid(c)).astype(BF16)
    o_ref[0] = jnp.dot(s, w_ref[0].astype(BF16), preferred_element_type=F32) + b_ref[0]


def _mod_table(cond, w_mod, b_mod):
    tn = 1024
    n = 6 * D
    return pl.pallas_call(
        _mod_kernel,
        out_shape=jax.ShapeDtypeStruct((DEPTH, N_COND, n), F32),
        grid=(DEPTH, n // tn),
        in_specs=[
            pl.BlockSpec((N_COND, D), lambda l, j: (0, 0)),
            pl.BlockSpec((1, D, tn), lambda l, j: (l, 0, j)),
            pl.BlockSpec((1, 1, tn), lambda l, j: (l, 0, j)),
        ],
        out_specs=pl.BlockSpec((1, N_COND, tn), lambda l, j: (l, 0, j)),
        compiler_params=_cparams(("arbitrary", "arbitrary")),
        name="mod_table",
    )(cond, w_mod, b_mod.reshape(DEPTH, 1, n))


def _norm_mod_kernel(x_ref, g_ref, sh_ref, sc_ref, o_ref):
    x = x_ref[...]
    y = x * lax.rsqrt(jnp.mean(x * x, axis=-1, keepdims=True) + NORM_EPS) * g_ref[0]
    o_ref[...] = (y * (1.0 + sc_ref[0]) + sh_ref[0]).astype(o_ref.dtype)


def _norm_mod(x, g, mod, layer, shift_chunk):
    tm = 256
    return pl.pallas_call(
        _norm_mod_kernel,
        out_shape=jax.ShapeDtypeStruct((T, D), BF16),
        grid=(T // tm,),
        in_specs=[
            pl.BlockSpec((tm, D), lambda i: (i, 0)),
            pl.BlockSpec((1, 1, D), lambda i: (layer, 0, 0)),
            pl.BlockSpec((1, 1, D), lambda i: (layer * N_COND + _cond_row(i, tm), 0, shift_chunk)),
            pl.BlockSpec((1, 1, D), lambda i: (layer * N_COND + _cond_row(i, tm), 0, shift_chunk + 1)),
        ],
        out_specs=pl.BlockSpec((tm, D), lambda i: (i, 0)),
        compiler_params=_cparams(("arbitrary",)),
        name="norm_mod",
    )(x, g.reshape(DEPTH, 1, D), mod, mod)


def _final_norm_kernel(x_ref, g_ref, o_ref):
    x = x_ref[...]
    o_ref[...] = x * lax.rsqrt(jnp.mean(x * x, axis=-1, keepdims=True) + NORM_EPS) * g_ref[...]


def _final_norm(x, g):
    tm = 256
    return pl.pallas_call(
        _final_norm_kernel,
        out_shape=jax.ShapeDtypeStruct((T, D), F32),
        grid=(T // tm,),
        in_specs=[pl.BlockSpec((tm, D), lambda i: (i, 0)), pl.BlockSpec((1, D), lambda i: (0, 0))],
        out_specs=pl.BlockSpec((tm, D), lambda i: (i, 0)),
        compiler_params=_cparams(("arbitrary",)),
        name="final_norm",
    )(x, g.reshape(1, D))


def _proj_kernel(h_ref, w_ref, o_ref, wb_ref):
    @pl.when(pl.program_id(1) == 0)
    def _():
        wb_ref[...] = w_ref[0].astype(BF16)

    o_ref[...] = jnp.dot(h_ref[...], wb_ref[...], preferred_element_type=F32)


def _in_proj(h, w_in, layer):
    tm, tn = 1024, 1024
    return pl.pallas_call(
        _proj_kernel,
        out_shape=jax.ShapeDtypeStruct((T, N_IN), F32),
        grid=(pl.cdiv(N_IN, tn), T // tm),
        in_specs=[
            pl.BlockSpec((tm, D), lambda j, i: (i, 0)),
            pl.BlockSpec((1, D, tn), lambda j, i: (layer, 0, j)),
        ],
        out_specs=pl.BlockSpec((tm, tn), lambda j, i: (i, j)),
        scratch_shapes=[pltpu.VMEM((D, tn), BF16)],
        compiler_params=_cparams(("arbitrary", "arbitrary")),
        name="in_proj",
    )(h, w_in)


def _mix_out_kernel(yna_ref, yhy_ref, ygdn_ref, gl_ref, bg_ref, wpa_ref, wpb_ref, wpc_ref, wout_ref,
                    x_ref, gt_ref, o_ref):
    width = gl_ref.shape[1]
    gl = pltpu.roll(gl_ref[...], width - GATE_SHIFT, 1)[:, :N_BRANCH * D] + bg_ref[0]
    gates = jax.nn.sigmoid(gl)
    merged = (gates[:, :D] * jnp.dot(yna_ref[...], wpa_ref[0], preferred_element_type=F32)
              + gates[:, D:2 * D] * jnp.dot(yhy_ref[...], wpb_ref[0], preferred_element_type=F32)
              + gates[:, 2 * D:] * jnp.dot(ygdn_ref[...], wpc_ref[0], preferred_element_type=F32))
    r = jnp.dot(merged.astype(BF16), wout_ref[0], preferred_element_type=F32)
    o_ref[...] = x_ref[...] + gt_ref[0] * r


def _mix_out(y_na, y_hy, y_gdn, proj, b_gate, w_pa, w_pb, w_pc, w_out, x, mod, layer):
    tm = 256
    resident = functools.partial(pl.BlockSpec, pipeline_mode=pl.Buffered(1))
    return pl.pallas_call(
        _mix_out_kernel,
        out_shape=jax.ShapeDtypeStruct((T, D), F32),
        grid=(T // tm,),
        in_specs=[
            pl.BlockSpec((tm, NA_WIDTH), lambda i: (i, 0)),
            pl.BlockSpec((tm, HY_WIDTH), lambda i: (i, 0)),
            pl.BlockSpec((tm, GDN_WIDTH), lambda i: (i, 0)),
            pl.BlockSpec((tm, GATE_BLOCK), lambda i: (i, 1)),
            pl.BlockSpec((1, 1, N_BRANCH * D), lambda i: (layer, 0, 0)),
            resident((1, NA_WIDTH, D), lambda i: (layer, 0, 0)),
            resident((1, HY_WIDTH, D), lambda i: (layer, 0, 0)),
            resident((1, GDN_WIDTH, D), lambda i: (layer, 0, 0)),
            resident((1, D, D), lambda i: (layer, 0, 0)),
            pl.BlockSpec((tm, D), lambda i: (i, 0)),
            pl.BlockSpec((1, 1, D), lambda i: (layer * N_COND + _cond_row(i, tm), 0, 2)),
        ],
        out_specs=pl.BlockSpec((tm, D), lambda i: (i, 0)),
        compiler_params=_cparams(("arbitrary",)),
        name="mix_out",
    )(y_na, y_hy, y_gdn, proj, b_gate.reshape(DEPTH, 1, N_BRANCH * D), w_pa, w_pb, w_pc, w_out, x, mod)


FFN_TM = 1024


def _ffn_up_kernel(h_ref, wa_ref, wb_ref, cwa_ref, cwb_ref, cba_ref, cbb_ref, o_ref, wab_ref, wbb_ref):
    m = pl.program_id(1)

    @pl.when(m == 0)
    def _():
        wab_ref[...] = wa_ref[0].astype(BF16)
        wbb_ref[...] = wb_ref[0].astype(BF16)

    seq_len = jnp.where(m < T_P // FFN_TM, SEQ, DEC_SEQ)
    pos = lax.broadcasted_iota(jnp.int32, (FFN_TM, 1), 0) & (seq_len - 1)
    has_prev = pos != 0
    has_next = pos != seq_len - 1
    h = h_ref[...]

    def conv(w_ref, cw_ref, cb_ref):
        up = jnp.dot(h, w_ref[...], preferred_element_type=F32)
        prev = jnp.where(has_prev, pltpu.roll(up, 1, 0), 0.0)
        nxt = jnp.where(has_next, pltpu.roll(up, FFN_TM - 1, 0), 0.0)
        cw = cw_ref[0]
        return prev * cw[0:1] + up * cw[1:2] + nxt * cw[2:3] + cb_ref[0]

    ua = conv(wab_ref, cwa_ref, cba_ref)
    ub = conv(wbb_ref, cwb_ref, cbb_ref)
    o_ref[...] = (ua * jax.nn.sigmoid(ua) * ub).astype(BF16)


def _ffn_up(h, w_up, conv_w, conv_b, layer):
    tn = 512
    nt = D_FF // tn
    conv_b = conv_b.reshape(DEPTH, 1, 2 * D_FF)
    return pl.pallas_call(
        _ffn_up_kernel,
        out_shape=jax.ShapeDtypeStruct((T, D_FF), BF16),
        grid=(nt, T // FFN_TM),
        in_specs=[
            pl.BlockSpec((FFN_TM, D), lambda j, i: (i, 0)),
            pl.BlockSpec((1, D, tn), lambda j, i: (layer, 0, j)),
            pl.BlockSpec((1, D, tn), lambda j, i: (layer, 0, nt + j)),
            pl.BlockSpec((1, 3, tn), lambda j, i: (layer, 0, j)),
            pl.BlockSpec((1, 3, tn), lambda j, i: (layer, 0, nt + j)),
            pl.BlockSpec((1, 1, tn), lambda j, i: (layer, 0, j)),
            pl.BlockSpec((1, 1, tn), lambda j, i: (layer, 0, nt + j)),
        ],
        out_specs=pl.BlockSpec((FFN_TM, tn), lambda j, i: (i, j)),
        scratch_shapes=[pltpu.VMEM((D, tn), BF16), pltpu.VMEM((D, tn), BF16)],
        compiler_params=_cparams(("arbitrary", "arbitrary")),
        name="ffn_up",
    )(h, w_up, w_up, conv_w, conv_w, conv_b, conv_b)


def _ffn_down_kernel(a_ref, w_ref, x_ref, gt_ref, o_ref):
    o_ref[...] = x_ref[...] + gt_ref[0] * jnp.dot(a_ref[...], w_ref[0], preferred_element_type=F32)


def _ffn_down(act, w_down, x, mod, layer):
    tm, tn = 256, 512
    return pl.pallas_call(
        _ffn_down_kernel,
        out_shape=jax.ShapeDtypeStruct((T, D), F32),
        grid=(D // tn, T // tm),
        in_specs=[
            pl.BlockSpec((tm, D_FF), lambda j, i: (i, 0)),
            pl.BlockSpec((1, D_FF, tn), lambda j, i: (layer, 0, j)),
            pl.BlockSpec((tm, tn), lambda j, i: (i, j)),
            pl.BlockSpec((1, 1, tn), lambda j, i: (layer * N_COND + _cond_row(i, tm), 0, 5 * (D // tn) + j)),
        ],
        out_specs=pl.BlockSpec((tm, tn), lambda j, i: (i, j)),
        compiler_params=_cparams(("arbitrary", "arbitrary")),
        name="ffn_down",
    )(act, w_down, x, mod)


def _centred_conv(x, w, b=None):
    k = w.shape[0]
    pad = k // 2
    length = x.shape[1]
    xp = jnp.pad(x, ((0, 0), (pad, pad), (0, 0)))
    y = sum(xp[:, i:i + length] * w[i] for i in range(k))
    return y if b is None else y + b


def _l2norm(x):
    return x * lax.rsqrt(jnp.sum(x * x, axis=-1, keepdims=True) + NORM_EPS)


def _context_attention(q, k, v):
    b, l, h, dh = q.shape
    s = jnp.einsum('bqhd,bkhd->bhqk', q * dh ** -0.5, k).astype(F32)
    p = jax.nn.softmax(s, axis=-1)
    return jnp.einsum('bhqk,bkhd->bqhd', p, v).reshape(b, l, h * dh)


def _neighbourhood_attention(q, k, v, ck, cv, rpb):
    b, t, h, dh = q.shape
    rows = t // GRID_W
    wr = min(NA_WIN_ROWS, rows)
    qg = (q * dh ** -0.5).reshape(b, rows, GRID_W, h, dh)
    kg = k.reshape(b, rows, GRID_W, h, dh)
    vg = v.reshape(b, rows, GRID_W, h, dh)
    col = jnp.arange(GRID_W)
    col_start = jnp.clip(col - NA_WIN_COLS // 2, 0, GRID_W - NA_WIN_COLS)
    col_mask = (col[None, :] >= col_start[:, None]) & (col[None, :] < col_start[:, None] + NA_WIN_COLS)
    rel_col_idx = jnp.clip(col[None, :] - col[:, None] + NA_WIN_COLS - 1, 0, 2 * NA_WIN_COLS - 2)
    rpb_cols = rpb[:, :, rel_col_idx]
    n_win = wr * GRID_W

    def row_block(r):
        start = jnp.clip(r - wr // 2, 0, rows - wr)
        q_r = lax.dynamic_index_in_dim(qg, r, axis=1, keepdims=False)
        k_r = lax.dynamic_slice_in_dim(kg, start, wr, axis=1)
        v_r = lax.dynamic_slice_in_dim(vg, start, wr, axis=1)
        rel_row_idx = start + jnp.arange(wr) - r + NA_WIN_ROWS - 1
        bias = jnp.take(rpb_cols, rel_row_idx, axis=1).transpose(0, 2, 1, 3)
        s_win = jnp.einsum('bqhd,brkhd->bhqrk', q_r, k_r).astype(F32)
        s_win = jnp.where(col_mask[:, None, :], s_win + bias.astype(F32), NEG_INF)
        s_ctx = jnp.einsum('bqhd,blhd->bhql', q_r, ck).astype(F32)
        s = jnp.concatenate([s_win.reshape(b, h, GRID_W, n_win), s_ctx], axis=-1)
        p = jax.nn.softmax(s, axis=-1).astype(v.dtype)
        p_win = p[..., :n_win].reshape(b, h, GRID_W, wr, GRID_W)
        return (jnp.einsum('bhqrk,brkhd->bqhd', p_win, v_r)
                + jnp.einsum('bhql,blhd->bqhd', p[..., n_win:], cv))

    o = lax.map(row_block, jnp.arange(rows))
    return o.transpose(1, 0, 2, 3, 4).reshape(b, t, h * dh)


def _hyena_filters(length, w1, b1, freq, w2, b2, w3, decay):
    t_norm = jnp.linspace(0.0, 1.0, length, dtype=F32)
    t_idx = jnp.arange(length, dtype=F32)
    bands = jnp.linspace(1e-4, HY_BANDS - 1, HY_BANDS, dtype=F32)
    ang = (2.0 * math.pi / length) * t_idx[:, None] * bands[None, :]
    z = jnp.concatenate([t_norm[:, None], jnp.cos(ang), jnp.sin(ang)], axis=-1)
    hdn = jnp.sin(freq[0] * (z @ w1 + b1))
    hdn = jnp.sin(freq[1] * (hdn @ w2 + b2))
    filt = (hdn @ w3).reshape(length, HY_ORDER, 2, HY_WIDTH)
    filt = filt * jnp.exp(-t_norm[:, None, None, None] * jnp.abs(decay))
    fwd, bwd = filt[:, :, 0], filt[:, :, 1]
    return jnp.concatenate([fwd, jnp.zeros_like(fwd[:1]), bwd[:length - 1][::-1]], axis=0)


def _hyena(u, conv_w, conv_b, w1, b1, freq, w2, b2, w3, decay, skip):
    length = u.shape[1]
    n_fft = 2 * length
    uc = _centred_conv(u, conv_w, conv_b)
    v, x1, x2 = jnp.split(uc, 3, axis=-1)
    filt = _hyena_filters(length, w1, b1, freq, w2, b2, w3, decay)
    z = v
    for o, gate in enumerate((x1, x2)):
        zf = jnp.fft.rfft(z, n=n_fft, axis=1)
        hf = jnp.fft.rfft(filt[:, o], n=n_fft, axis=0)
        conv = jnp.fft.irfft(zf * hf[None], n=n_fft, axis=1)[:, :length]
        z = gate * (conv + z * skip[o])
    return z


def _gdn_chunk_scan(q, k, v, g, beta, s0):
    b, t, h, dk = q.shape
    dv = v.shape[-1]
    n = t // GDN_CHUNK

    def chunks(a):
        a = a.reshape((b, n, GDN_CHUNK, h) + a.shape[3:])
        return jnp.moveaxis(a, 3, 1)

    q = chunks(q) * dk ** -0.5
    k = chunks(k)
    v = chunks(v)
    g = chunks(g)
    beta = chunks(beta)
    gc = jnp.cumsum(g, axis=-1)
    idx = jnp.arange(GDN_CHUNK)
    causal = idx[:, None] >= idx[None, :]
    strict = idx[:, None] > idx[None, :]
    diff = gc[..., :, None] - gc[..., None, :]
    decay = jnp.where(causal, jnp.exp(jnp.where(causal, diff, 0.0)), 0.0)
    kb = k * beta[..., None]
    a_mat = (jnp.where(strict, jnp.einsum('bhnid,bhnjd->bhnij', kb, k) * decay, 0.0)
             + jnp.eye(GDN_CHUNK, dtype=F32))
    rhs = jnp.concatenate([v * beta[..., None], kb * jnp.exp(gc)[..., None]], axis=-1)
    sol = lax.linalg.triangular_solve(a_mat, rhs, left_side=True, lower=True, unit_diagonal=True)
    u, w = sol[..., :dv], sol[..., dv:]
    attn = jnp.where(causal, jnp.einsum('bhnid,bhnjd->bhnij', q, k) * decay, 0.0)
    q_dec = q * jnp.exp(gc)[..., None]
    k_dec = k * jnp.exp(gc[..., -1:] - gc)[..., None]
    g_tot = jnp.exp(gc[..., -1])
    xs = tuple(jnp.moveaxis(a, 2, 0) for a in (u, w, attn, q_dec, k_dec, g_tot))

    def step(s, inp):
        u_c, w_c, attn_c, qd_c, kd_c, gt_c = inp
        v_new = u_c - jnp.einsum('bhck,bhkv->bhcv', w_c, s)
        o_c = jnp.einsum('bhck,bhkv->bhcv', qd_c, s) + jnp.einsum('bhij,bhjv->bhiv', attn_c, v_new)
        s = s * gt_c[..., None, None] + jnp.einsum('bhck,bhcv->bhkv', kd_c, v_new)
        return s, o_c

    s_final, o = lax.scan(step, s0, xs)
    o = jnp.transpose(o, (1, 0, 3, 2, 4)).reshape(b, t, h, dv)
    return o, s_final


def _gated_deltanet(qkv, z, b_logit, a_logit, conv_w, a_log, dt_bias, norm_g, s_f0, s_b0):
    bsz, t, _ = qkv.shape
    qkv = jax.nn.silu(_centred_conv(qkv, conv_w))
    q, k, v = jnp.split(qkv, 3, axis=-1)
    q = _l2norm(q.reshape(bsz, t, GDN_HEADS, GDN_DK))
    k = _l2norm(k.reshape(bsz, t, GDN_HEADS, GDN_DK))
    v = v.reshape(bsz, t, GDN_HEADS, GDN_DV)
    beta = jax.nn.sigmoid(b_logit).reshape(bsz, t, 2, GDN_HEADS)
    g = -jnp.exp(a_log) * jax.nn.softplus(a_logit.reshape(bsz, t, 2, GDN_HEADS) + dt_bias)
    o_f, s_f = _gdn_chunk_scan(q, k, v, g[:, :, 0], beta[:, :, 0], s_f0)
    o_b, s_b = _gdn_chunk_scan(jnp.flip(q, 1), jnp.flip(k, 1), jnp.flip(v, 1),
                               jnp.flip(g[:, :, 1], 1), jnp.flip(beta[:, :, 1], 1), s_b0)
    o = o_f + jnp.flip(o_b, 1)
    o = o * lax.rsqrt(jnp.mean(o * o, axis=-1, keepdims=True) + NORM_EPS) * norm_g
    o = o * jax.nn.silu(z.reshape(bsz, t, GDN_HEADS, GDN_DV))
    return o.reshape(bsz, t, GDN_WIDTH), s_f, s_b


def _mixers(proj, b, l, ctx, p):
    pr = proj.reshape(b, l, N_IN)
    qa = pr[..., C_Q:C_K].reshape(b, l, NA_HEADS, NA_DH)
    ka = pr[..., C_K:C_V].reshape(b, l, NA_HEADS, NA_DH)
    va = pr[..., C_V:C_HY].reshape(b, l, NA_HEADS, NA_DH)
    if ctx is None:
        y_na = _context_attention(qa, ka, va)
        s_f0 = jnp.zeros((b, GDN_HEADS, GDN_DK, GDN_DV), F32)
        s_b0 = s_f0
    else:
        ctx_k, ctx_v, s_f0, s_b0 = ctx
        y_na = _neighbourhood_attention(qa, ka, va, ctx_k, ctx_v, p['na_rpb'])
    y_hy = _hyena(pr[..., C_HY:C_GDN], p['hy_conv_w'], p['hy_conv_b'], p['hy_w1'], p['hy_b1'], p['hy_freq'],
                  p['hy_w2'], p['hy_b2'], p['hy_w3'], p['hy_decay'], p['hy_skip'])
    y_gdn, s_f, s_b = _gated_deltanet(pr[..., C_GDN:C_Z], pr[..., C_Z:C_BA], pr[..., C_BA:C_BA + 8],
                                      pr[..., C_BA + 8:C_GATE], p['gdn_conv_w'], p['gdn_a_log'],
                                      p['gdn_dt_bias'], p['gdn_norm_g'], s_f0, s_b0)
    return (y_na.reshape(b * l, -1), y_hy.reshape(b * l, -1), y_gdn.reshape(b * l, -1), ka, va, s_f, s_b)


def kernel(x_prompt, x_sample, cache_k, cache_v, state_fwd, state_bwd, c, c_ctx, ln1_g, ln2_g, w_mod, b_mod,
           w_in, na_rpb, hy_conv_w, hy_conv_b, hy_w1, hy_b1, hy_freq, hy_w2, hy_b2, hy_w3, hy_decay, hy_skip,
           gdn_conv_w, gdn_a_log, gdn_dt_bias, gdn_norm_g, w_pa, w_pb, w_pc, b_gate, w_out, ffn_w_up,
           ffn_conv_w, ffn_conv_b, ffn_w_down, final_g):
    x = jnp.concatenate([x_prompt.reshape(T_P, D), x_sample.reshape(T_S, D)], axis=0)
    cond = jnp.concatenate([c_ctx[None], c, jnp.zeros((N_COND - 1 - DEC_BATCH, D), F32)], axis=0)
    mod = _mod_table(cond, w_mod, b_mod).reshape(DEPTH * N_COND, 1, 6 * D)

    w_pa_b, w_pb_b, w_pc_b, w_out_b = (w.astype(BF16) for w in (w_pa, w_pb, w_pc, w_out))
    w_down_b = ffn_w_down.astype(BF16)

    ks, vs, sfs, sbs = [], [], [], []
    for layer in range(DEPTH):
        p = {'na_rpb': na_rpb[layer], 'hy_conv_w': hy_conv_w[layer], 'hy_conv_b': hy_conv_b[layer],
             'hy_w1': hy_w1[layer], 'hy_b1': hy_b1[layer], 'hy_freq': hy_freq[layer], 'hy_w2': hy_w2[layer],
             'hy_b2': hy_b2[layer], 'hy_w3': hy_w3[layer], 'hy_decay': hy_decay[layer],
             'hy_skip': hy_skip[layer], 'gdn_conv_w': gdn_conv_w[layer], 'gdn_a_log': gdn_a_log[layer],
             'gdn_dt_bias': gdn_dt_bias[layer], 'gdn_norm_g': gdn_norm_g[layer]}
        h = _norm_mod(x, ln1_g, mod, layer, 0)
        proj = _in_proj(h, w_in, layer)
        mp = _mixers(proj[:T_P], BATCH, SEQ, None, p)
        ms = _mixers(proj[T_P:], DEC_BATCH, DEC_SEQ,
                     (cache_k[:, layer], cache_v[:, layer], state_fwd[:, layer], state_bwd[:, layer]), p)
        y_na, y_hy, y_gdn = (jnp.concatenate([a, b_], axis=0).astype(BF16) for a, b_ in zip(mp[:3], ms[:3]))
        ks.append(mp[3])
        vs.append(mp[4])
        sfs.append(mp[5])
        sbs.append(mp[6])
        x = _mix_out(y_na, y_hy, y_gdn, proj, b_gate, w_pa_b, w_pb_b, w_pc_b, w_out_b, x, mod, layer)
        h = _norm_mod(x, ln2_g, mod, layer, 3)
        act = _ffn_up(h, ffn_w_up, ffn_conv_w, ffn_conv_b, layer)
        x = _ffn_down(act, w_down_b, x, mod, layer)

    y = _final_norm(x, final_g)
    return (y[:T_P].reshape(BATCH, SEQ, D), y[T_P:].reshape(DEC_BATCH, DEC_SEQ, D),
            jnp.stack(ks, axis=1), jnp.stack(vs, axis=1), jnp.stack(sfs, axis=1), jnp.stack(sbs, axis=1))
```

```python
import functools
import math

import jax
import jax.numpy as jnp
import numpy as np
from jax import lax
from jax.experimental import pallas as pl
from jax.experimental.pallas import tpu as pltpu

F32 = jnp.float32
BF16 = jnp.bfloat16

D = 2048
BATCH, SEQ = 32, 256
DEC_BATCH, DEC_SEQ = 4, 1024
DEPTH = 2
GRID_W = 64
NA_HEADS, NA_DH = 8, 128
NA_WIDTH = NA_HEADS * NA_DH
NA_WIN_ROWS, NA_WIN_COLS = 8, 16
HY_WIDTH = 512
HY_ORDER = 2
HY_BANDS = 16
GDN_HEADS, GDN_DK, GDN_DV = 4, 128, 128
GDN_WIDTH = GDN_HEADS * GDN_DV
D_FF = 5632
N_BRANCH = 3
NORM_EPS = 1e-6
NEG_INF = -1e30

T_P = BATCH * SEQ
T_S = DEC_BATCH * DEC_SEQ
T = T_P + T_S
N_COND = 8

C_Q, C_K, C_V = 0, NA_WIDTH, 2 * NA_WIDTH
C_HY = 3 * NA_WIDTH
C_GDN = C_HY + 3 * HY_WIDTH
C_Z = C_GDN + 3 * GDN_HEADS * GDN_DK
C_BA = C_Z + GDN_WIDTH
C_GATE = C_BA + 4 * GDN_HEADS
N_IN = C_GATE + N_BRANCH * D
GATE_BLOCK = C_BA
GATE_SHIFT = C_GATE - C_BA

LANE = 128
VMEM_LIMIT = 56 * 1024 * 1024
HIGHEST = lax.Precision.HIGHEST


def _cparams(sem):
    return pltpu.CompilerParams(dimension_semantics=sem, vmem_limit_bytes=VMEM_LIMIT)


def _cond_row(tile, rows_per_tile):
    first_latent = T_P // rows_per_tile
    per_seq = DEC_SEQ // rows_per_tile
    return jnp.where(tile < first_latent, 0, 1 + (tile - first_latent) // per_seq)


def _mod_kernel(c_ref, w_ref, b_ref, o_ref):
    c = c_ref[...]
    s = (c * jax.nn.sigmoid(c)).astype(BF16)
    o_ref[0] = jnp.dot(s, w_ref[0].astype(BF16), preferred_element_type=F32) + b_ref[0]


def _mod_table(cond, w_mod, b_mod):
    tn = 1024
    n = 6 * D
    return pl.pallas_call(
        _mod_kernel,
        out_shape=jax.ShapeDtypeStruct((DEPTH, N_COND, n), F32),
        grid=(DEPTH, n // tn),
        in_specs=[
            pl.BlockSpec((N_COND, D), lambda l, j: (0, 0)),
            pl.BlockSpec((1, D, tn), lambda l, j: (l, 0, j)),
            pl.BlockSpec((1, 1, tn), lambda l, j: (l, 0, j)),
        ],
        out_specs=pl.BlockSpec((1, N_COND, tn), lambda l, j: (l, 0, j)),
        compiler_params=_cparams(("arbitrary", "arbitrary")),
        name="mod_table",
    )(cond, w_mod, b_mod.reshape(DEPTH, 1, n))


def _norm_mod_kernel(x_ref, g_ref, sh_ref, sc_ref, o_ref):
    x = x_ref[...]
    y = x * lax.rsqrt(jnp.mean(x * x, axis=-1, keepdims=True) + NORM_EPS) * g_ref[0]
    o_ref[...] = (y * (1.0 + sc_ref[0]) + sh_ref[0]).astype(o_ref.dtype)


def _norm_mod(x, g, mod, layer, shift_chunk):
    tm = 256
    return pl.pallas_call(
        _norm_mod_kernel,
        out_shape=jax.ShapeDtypeStruct((T, D), BF16),
        grid=(T // tm,),
        in_specs=[
            pl.BlockSpec((tm, D), lambda i: (i, 0)),
            pl.BlockSpec((1, 1, D), lambda i: (layer, 0, 0)),
            pl.BlockSpec((1, 1, D), lambda i: (layer * N_COND + _cond_row(i, tm), 0, shift_chunk)),
            pl.BlockSpec((1, 1, D), lambda i: (layer * N_COND + _cond_row(i, tm), 0, shift_chunk + 1)),
        ],
        out_specs=pl.BlockSpec((tm, D), lambda i: (i, 0)),
        compiler_params=_cparams(("arbitrary",)),
        name="norm_mod",
    )(x, g.reshape(DEPTH, 1, D), mod, mod)


def _final_norm_kernel(x_ref, g_ref, o_ref):
    x = x_ref[...]
    o_ref[...] = x * lax.rsqrt(jnp.mean(x * x, axis=-1, keepdims=True) + NORM_EPS) * g_ref[...]


def _final_norm(x, g):
    tm = 256
    return pl.pallas_call(
        _final_norm_kernel,
        out_shape=jax.ShapeDtypeStruct((T, D), F32),
        grid=(T // tm,),
        in_specs=[pl.BlockSpec((tm, D), lambda i: (i, 0)), pl.BlockSpec((1, D), lambda i: (0, 0))],
        out_specs=pl.BlockSpec((tm, D), lambda i: (i, 0)),
        compiler_params=_cparams(("arbitrary",)),
        name="final_norm",
    )(x, g.reshape(1, D))


def _proj_kernel(h_ref, w_ref, o_ref, wb_ref):
    @pl.when(pl.program_id(1) == 0)
    def _():
        wb_ref[...] = w_ref[0].astype(BF16)

    o_ref[...] = jnp.dot(h_ref[...], wb_ref[...], preferred_element_type=F32)


def _in_proj(h, w_in, layer):
    tm, tn = 1024, 1024
    return pl.pallas_call(
        _proj_kernel,
        out_shape=jax.ShapeDtypeStruct((T, N_IN), F32),
        grid=(pl.cdiv(N_IN, tn), T // tm),
        in_specs=[
            pl.BlockSpec((tm, D), lambda j, i: (i, 0)),
            pl.BlockSpec((1, D, tn), lambda j, i: (layer, 0, j)),
        ],
        out_specs=pl.BlockSpec((tm, tn), lambda j, i: (i, j)),
        scratch_shapes=[pltpu.VMEM((D, tn), BF16)],
        compiler_params=_cparams(("arbitrary", "arbitrary")),
        name="in_proj",
    )(h, w_in)


def _mix_out_kernel(yna_ref, yhy_ref, ygdn_ref, gl_ref, bg_ref, wpa_ref, wpb_ref, wpc_ref, wout_ref,
                    x_ref, gt_ref, o_ref):
    width = gl_ref.shape[1]
    gl = pltpu.roll(gl_ref[...], width - GATE_SHIFT, 1)[:, :N_BRANCH * D] + bg_ref[0]
    gates = jax.nn.sigmoid(gl)
    merged = (gates[:, :D] * jnp.dot(yna_ref[...], wpa_ref[0], preferred_element_type=F32)
              + gates[:, D:2 * D] * jnp.dot(yhy_ref[...], wpb_ref[0], preferred_element_type=F32)
              + gates[:, 2 * D:] * jnp.dot(ygdn_ref[...], wpc_ref[0], preferred_element_type=F32))
    r = jnp.dot(merged.astype(BF16), wout_ref[0], preferred_element_type=F32)
    o_ref[...] = x_ref[...] + gt_ref[0] * r


def _mix_out(y_na, y_hy, y_gdn, proj, b_gate, w_pa, w_pb, w_pc, w_out, x, mod, layer):
    tm = 256
    resident = functools.partial(pl.BlockSpec, pipeline_mode=pl.Buffered(1))
    return pl.pallas_call(
        _mix_out_kernel,
        out_shape=jax.ShapeDtypeStruct((T, D), F32),
        grid=(T // tm,),
        in_specs=[
            pl.BlockSpec((tm, NA_WIDTH), lambda i: (i, 0)),
            pl.BlockSpec((tm, HY_WIDTH), lambda i: (i, 0)),
            pl.BlockSpec((tm, GDN_WIDTH), lambda i: (i, 0)),
            pl.BlockSpec((tm, GATE_BLOCK), lambda i: (i, 1)),
            pl.BlockSpec((1, 1, N_BRANCH * D), lambda i: (layer, 0, 0)),
            resident((1, NA_WIDTH, D), lambda i: (layer, 0, 0)),
            resident((1, HY_WIDTH, D), lambda i: (layer, 0, 0)),
            resident((1, GDN_WIDTH, D), lambda i: (layer, 0, 0)),
            resident((1, D, D), lambda i: (layer, 0, 0)),
            pl.BlockSpec((tm, D), lambda i: (i, 0)),
            pl.BlockSpec((1, 1, D), lambda i: (layer * N_COND + _cond_row(i, tm), 0, 2)),
        ],
        out_specs=pl.BlockSpec((tm, D), lambda i: (i, 0)),
        compiler_params=_cparams(("arbitrary",)),
        name="mix_out",
    )(y_na, y_hy, y_gdn, proj, b_gate.reshape(DEPTH, 1, N_BRANCH * D), w_pa, w_pb, w_pc, w_out, x, mod)


FFN_TM = 1024


def _ffn_up_kernel(h_ref, wa_ref, wb_ref, cwa_ref, cwb_ref, cba_ref, cbb_ref, o_ref, wab_ref, wbb_ref):
    m = pl.program_id(1)

    @pl.when(m == 0)
    def _():
        wab_ref[...] = wa_ref[0].astype(BF16)
        wbb_ref[...] = wb_ref[0].astype(BF16)

    seq_len = jnp.where(m < T_P // FFN_TM, SEQ, DEC_SEQ)
    pos = lax.broadcasted_iota(jnp.int32, (FFN_TM, 1), 0) & (seq_len - 1)
    has_prev = pos != 0
    has_next = pos != seq_len - 1
    h = h_ref[...]

    def conv(w_ref, cw_ref, cb_ref):
        up = jnp.dot(h, w_ref[...], preferred_element_type=F32)
        prev = jnp.where(has_prev, pltpu.roll(up, 1, 0), 0.0)
        nxt = jnp.where(has_next, pltpu.roll(up, FFN_TM - 1, 0), 0.0)
        cw = cw_ref[0]
        return prev * cw[0:1] + up * cw[1:2] + nxt * cw[2:3] + cb_ref[0]

    ua = conv(wab_ref, cwa_ref, cba_ref)
    ub = conv(wbb_ref, cwb_ref, cbb_ref)
    o_ref[...] = (ua * jax.nn.sigmoid(ua) * ub).astype(BF16)


def _ffn_up(h, w_up, conv_w, conv_b, layer):
    tn = 512
    nt = D_FF // tn
    conv_b = conv_b.reshape(DEPTH, 1, 2 * D_FF)
    return pl.pallas_call(
        _ffn_up_kernel,
        out_shape=jax.ShapeDtypeStruct((T, D_FF), BF16),
        grid=(nt, T // FFN_TM),
        in_specs=[
            pl.BlockSpec((FFN_TM, D), lambda j, i: (i, 0)),
            pl.BlockSpec((1, D, tn), lambda j, i: (layer, 0, j)),
            pl.BlockSpec((1, D, tn), lambda j, i: (layer, 0, nt + j)),
            pl.BlockSpec((1, 3, tn), lambda j, i: (layer, 0, j)),
            pl.BlockSpec((1, 3, tn), lambda j, i: (layer, 0, nt + j)),
            pl.BlockSpec((1, 1, tn), lambda j, i: (layer, 0, j)),
            pl.BlockSpec((1, 1, tn), lambda j, i: (layer, 0, nt + j)),
        ],
        out_specs=pl.BlockSpec((FFN_TM, tn), lambda j, i: (i, j)),
        scratch_shapes=[pltpu.VMEM((D, tn), BF16), pltpu.VMEM((D, tn), BF16)],
        compiler_params=_cparams(("arbitrary", "arbitrary")),
        name="ffn_up",
    )(h, w_up, w_up, conv_w, conv_w, conv_b, conv_b)


def _ffn_down_kernel(a_ref, w_ref, x_ref, gt_ref, o_ref):
    o_ref[...] = x_ref[...] + gt_ref[0] * jnp.dot(a_ref[...], w_ref[0], preferred_element_type=F32)


def _ffn_down(act, w_down, x, mod, layer):
    tm, tn = 256, 512
    return pl.pallas_call(
        _ffn_down_kernel,
        out_shape=jax.ShapeDtypeStruct((T, D), F32),
        grid=(D // tn, T // tm),
        in_specs=[
            pl.BlockSpec((tm, D_FF), lambda j, i: (i, 0)),
            pl.BlockSpec((1, D_FF, tn), lambda j, i: (layer, 0, j)),
            pl.BlockSpec((tm, tn), lambda j, i: (i, j)),
            pl.BlockSpec((1, 1, tn), lambda j, i: (layer * N_COND + _cond_row(i, tm), 0, 5 * (D // tn) + j)),
        ],
        out_specs=pl.BlockSpec((tm, tn), lambda j, i: (i, j)),
        compiler_params=_cparams(("arbitrary", "arbitrary")),
        name="ffn_down",
    )(act, w_down, x, mod)


ATT_SCALE = NA_DH ** -0.5


def _nt_dot(a, b):
    return lax.dot_general(a, b, (((1,), (1,)), ((), ())), preferred_element_type=F32)


def _ctx_attn_kernel(q_ref, k_ref, v_ref, o_ref, ko_ref, vo_ref):
    for h in range(NA_HEADS):
        sl = slice(h * NA_DH, (h + 1) * NA_DH)
        q = (q_ref[:, sl] * ATT_SCALE).astype(BF16)
        s = _nt_dot(q, k_ref[:, sl].astype(BF16))
        e = jnp.exp(s - jnp.max(s, axis=-1, keepdims=True))
        o = jnp.dot(e.astype(BF16), v_ref[:, sl].astype(BF16), preferred_element_type=F32)
        o_ref[:, sl] = (o / jnp.sum(e, axis=-1, keepdims=True)).astype(BF16)
    ko_ref[0, 0] = k_ref[...]
    vo_ref[0, 0] = v_ref[...]


def _ctx_attention(proj, layer, caches):
    cache_shape = jax.ShapeDtypeStruct((BATCH, DEPTH, SEQ, NA_WIDTH), F32)
    cache_spec = pl.BlockSpec((1, 1, SEQ, NA_WIDTH), lambda b: (b, layer, 0, 0))
    qkv_specs = [pl.BlockSpec((SEQ, NA_WIDTH), lambda b, j=j: (b, j)) for j in range(3)]
    out_shape = [jax.ShapeDtypeStruct((T, NA_WIDTH), BF16), cache_shape, cache_shape]
    out_specs = [pl.BlockSpec((SEQ, NA_WIDTH), lambda b: (b, 0)), cache_spec, cache_spec]
    if caches is None:
        return pl.pallas_call(
            _ctx_attn_kernel, out_shape=out_shape, grid=(BATCH,), in_specs=qkv_specs, out_specs=out_specs,
            compiler_params=_cparams(("arbitrary",)), name="ctx_attention",
        )(proj, proj, proj)

    def body(q_ref, k_ref, v_ref, kc_ref, vc_ref, o_ref, ko_ref, vo_ref):
        del kc_ref, vc_ref
        _ctx_attn_kernel(q_ref, k_ref, v_ref, o_ref, ko_ref, vo_ref)

    any_spec = pl.BlockSpec(memory_space=pl.ANY)
    return pl.pallas_call(
        body, out_shape=out_shape, grid=(BATCH,), in_specs=qkv_specs + [any_spec, any_spec],
        out_specs=out_specs, input_output_aliases={3: 1, 4: 2},
        compiler_params=_cparams(("arbitrary",)), name="ctx_attention",
    )(proj, proj, proj, *caches)


NA_ROWS = DEC_SEQ // GRID_W
NA_WIN_TOK = NA_WIN_ROWS * GRID_W


def _na_window_start(r):
    return jnp.clip(r - NA_WIN_ROWS // 2, 0, NA_ROWS - NA_WIN_ROWS)


def _na_bias_table(rpb):
    col = np.arange(GRID_W)
    col_start = np.clip(col - NA_WIN_COLS // 2, 0, GRID_W - NA_WIN_COLS)
    col_mask = (col[None, :] >= col_start[:, None]) & (col[None, :] < col_start[:, None] + NA_WIN_COLS)
    rel_col = np.clip(col[None, :] - col[:, None] + NA_WIN_COLS - 1, 0, 2 * NA_WIN_COLS - 2)
    banded = jnp.where(col_mask, rpb[:, :, rel_col], NEG_INF)
    per_offset = [banded[:, NA_WIN_ROWS - 1 - d:2 * NA_WIN_ROWS - 1 - d] for d in range(NA_WIN_ROWS)]
    tab = jnp.stack(per_offset, axis=0)
    return tab.transpose(0, 1, 3, 2, 4).reshape(NA_WIN_ROWS, NA_HEADS, GRID_W, NA_WIN_TOK)


def _na_attn_kernel(q_ref, k_ref, v_ref, ck_ref, cv_ref, bias_ref, y_in_ref, o_ref):
    del y_in_ref
    r = pl.program_id(1)
    start = pl.multiple_of(_na_window_start(r) * GRID_W, GRID_W)
    for h in range(NA_HEADS):
        sl = slice(h * NA_DH, (h + 1) * NA_DH)
        q = (q_ref[:, sl] * ATT_SCALE).astype(BF16)
        s_win = _nt_dot(q, k_ref[pl.ds(start, NA_WIN_TOK), sl].astype(BF16)) + bias_ref[0, h]
        s_ctx = _nt_dot(q, ck_ref[0, 0, :, sl].astype(BF16))
        m = jnp.maximum(jnp.max(s_win, axis=-1, keepdims=True), jnp.max(s_ctx, axis=-1, keepdims=True))
        e_win = jnp.exp(s_win - m)
        e_ctx = jnp.exp(s_ctx - m)
        o = (jnp.dot(e_win.astype(BF16), v_ref[pl.ds(start, NA_WIN_TOK), sl].astype(BF16),
                     preferred_element_type=F32)
             + jnp.dot(e_ctx.astype(BF16), cv_ref[0, 0, :, sl].astype(BF16), preferred_element_type=F32))
        denom = jnp.sum(e_win, axis=-1, keepdims=True) + jnp.sum(e_ctx, axis=-1, keepdims=True)
        o_ref[:, sl] = (o / denom).astype(BF16)


def _na_attention(proj, cache_k, cache_v, rpb, y_na, layer):
    q_row0 = T_P // GRID_W
    seq0 = T_P // DEC_SEQ
    kv_spec = lambda j: pl.BlockSpec((DEC_SEQ, NA_WIDTH), lambda b, r: (seq0 + b, j))
    ctx_spec = pl.BlockSpec((1, 1, SEQ, NA_WIDTH), lambda b, r: (b, layer, 0, 0))
    return pl.pallas_call(
        _na_attn_kernel,
        out_shape=jax.ShapeDtypeStruct((T, NA_WIDTH), BF16),
        grid=(DEC_BATCH, NA_ROWS),
        in_specs=[
            pl.BlockSpec((GRID_W, NA_WIDTH), lambda b, r: (q_row0 + b * NA_ROWS + r, 0)),
            kv_spec(1), kv_spec(2), ctx_spec, ctx_spec,
            pl.BlockSpec((1, NA_HEADS, GRID_W, NA_WIN_TOK), lambda b, r: (r - _na_window_start(r), 0, 0, 0)),
            pl.BlockSpec(memory_space=pl.ANY),
        ],
        out_specs=pl.BlockSpec((GRID_W, NA_WIDTH), lambda b, r: (q_row0 + b * NA_ROWS + r, 0)),
        input_output_aliases={6: 0},
        compiler_params=_cparams(("arbitrary", "arbitrary")),
        name="na_attention",
    )(proj, proj, proj, cache_k.reshape(DEC_BATCH, DEPTH, SEQ, NA_WIDTH),
      cache_v.reshape(DEC_BATCH, DEPTH, SEQ, NA_WIDTH), _na_bias_table(rpb), y_na)


HY_CB = 256


def _split_bf16(x):
    hi = x.astype(BF16)
    return hi, (x - hi.astype(F32)).astype(BF16)


def _dot3(a_hi, a_lo, b):
    b_hi, b_lo = _split_bf16(b)
    return (jnp.dot(a_hi, b_hi, preferred_element_type=F32) + jnp.dot(a_lo, b_hi, preferred_element_type=F32)
            + jnp.dot(a_hi, b_lo, preferred_element_type=F32))


def _trig_table_kernel(hi_ref, lo_ref, *, length, freq_on_rows, scale):
    rows, cols = hi_ref.shape
    r = lax.broadcasted_iota(jnp.int32, (rows, cols), 0) + pl.program_id(0) * rows
    c = lax.broadcasted_iota(jnp.int32, (rows, cols), 1)
    f, t = (r, c) if freq_on_rows else (c, r)
    k = f & (length - 1)
    quarter = jnp.where(f >= length, length, 0)
    phase = (2 * k + 1) * t + (-quarter if freq_on_rows else quarter)
    phase = phase & (4 * length - 1)
    phase = jnp.where(phase >= 2 * length, phase - 4 * length, phase)
    val = jnp.cos(phase.astype(F32) * (math.pi / (2 * length))) * scale
    hi, lo = _split_bf16(val)
    hi_ref[...] = hi
    lo_ref[...] = lo


def _trig_table(length, freq_on_rows):
    shape = (2 * length, length) if freq_on_rows else (length, 2 * length)
    tr = 256
    spec = pl.BlockSpec((tr, shape[1]), lambda i: (i, 0))
    return pl.pallas_call(
        functools.partial(_trig_table_kernel, length=length, freq_on_rows=freq_on_rows,
                          scale=1.0 if freq_on_rows else 1.0 / length),
        out_shape=[jax.ShapeDtypeStruct(shape, BF16)] * 2,
        grid=(shape[0] // tr,), out_specs=[spec, spec],
        compiler_params=_cparams(("arbitrary",)), name="dft_table",
    )()


def _hy_filter_kernel(tn_ref, band_ref, w1_ref, b1_ref, fr_ref, w2_ref, b2_ref, w3_ref, dec_ref,
                      fhi_ref, flo_ref, o_ref, *, length):
    t_norm = tn_ref[...]
    t_idx = lax.broadcasted_iota(jnp.int32, (length, LANE), 0).astype(F32)
    lane = lax.broadcasted_iota(jnp.int32, (length, LANE), 1)
    ang = (2.0 * math.pi / length) * t_idx * band_ref[...]
    z = jnp.where(lane == 0, t_norm,
                  jnp.where(lane <= HY_BANDS, jnp.cos(ang), jnp.where(lane <= 2 * HY_BANDS, jnp.sin(ang), 0.0)))
    hdn = jnp.sin(fr_ref[0:1] * (jnp.dot(z, w1_ref[...], precision=HIGHEST, preferred_element_type=F32)
                                 + b1_ref[...]))
    hdn = jnp.sin(fr_ref[1:2] * (jnp.dot(hdn, w2_ref[...], precision=HIGHEST, preferred_element_type=F32)
                                 + b2_ref[...]))
    filt = jnp.dot(hdn, w3_ref[...], precision=HIGHEST, preferred_element_type=F32)
    filt = filt * jnp.exp(-t_norm * jnp.abs(dec_ref[...]))
    first = lax.broadcasted_iota(jnp.int32, (length, HY_WIDTH), 0) == 0
    for o in range(HY_ORDER):
        fwd = filt[:, (2 * o) * HY_WIDTH:(2 * o + 1) * HY_WIDTH]
        bwd = filt[:, (2 * o + 1) * HY_WIDTH:(2 * o + 2) * HY_WIDTH]
        bwd = jnp.where(first, 0.0, pltpu.roll(bwd, 1, 0))
        o_ref[o, :length] = _dot3(fhi_ref[:length], flo_ref[:length], fwd + bwd)
        o_ref[o, length:] = _dot3(fhi_ref[length:], flo_ref[length:], bwd - fwd)


def _hy_filter_spectrum(length, f_hi, f_lo, w1, b1, freq, w2, b2, w3, decay):
    emb = 1 + 2 * HY_BANDS
    t_norm = jnp.linspace(0.0, 1.0, length, dtype=F32).reshape(length, 1)
    bands = np.zeros((1, LANE), np.float32)
    bands[0, 1:1 + HY_BANDS] = bands[0, 1 + HY_BANDS:emb] = np.linspace(1e-4, HY_BANDS - 1, HY_BANDS,
                                                                        dtype=np.float32)
    w1p = jnp.zeros((LANE, w1.shape[1]), F32).at[:emb].set(w1)
    return pl.pallas_call(
        functools.partial(_hy_filter_kernel, length=length),
        out_shape=jax.ShapeDtypeStruct((HY_ORDER, 2 * length, HY_WIDTH), F32),
        compiler_params=pltpu.CompilerParams(vmem_limit_bytes=VMEM_LIMIT),
        name="hyena_filter",
    )(t_norm, jnp.asarray(bands), w1p, b1.reshape(1, -1), freq, w2, b2.reshape(1, -1), w3,
      decay.reshape(1, -1), f_hi, f_lo)


def _hyena_kernel(v_ref, x1_ref, x2_ref, cwv_ref, cw1_ref, cw2_ref, cbv_ref, cb1_ref, cb2_ref, skip_ref,
                  hs_ref, fhi_ref, flo_ref, ghi_ref, glo_ref, *rest, length):
    o_ref = rest[-1]
    row = lax.broadcasted_iota(jnp.int32, (length, HY_CB), 0)
    first, last = row == 0, row == length - 1

    def short_conv(u_ref, cw_ref, cb_ref):
        u = u_ref[...]
        prev = jnp.where(first, 0.0, pltpu.roll(u, 1, 0))
        nxt = jnp.where(last, 0.0, pltpu.roll(u, length - 1, 0))
        return prev * cw_ref[0:1] + u * cw_ref[1:2] + nxt * cw_ref[2:3] + cb_ref[...]

    z = short_conv(v_ref, cwv_ref, cbv_ref)
    gates = (short_conv(x1_ref, cw1_ref, cb1_ref), short_conv(x2_ref, cw2_ref, cb2_ref))
    for o in range(HY_ORDER):
        zf = _dot3(fhi_ref[...], flo_ref[...], z)
        zc, zs = zf[:length], zf[length:]
        h_re, h_im = hs_ref[o, :length], hs_ref[o, length:]
        p = jnp.concatenate([zc * h_re + zs * h_im, zc * h_im - zs * h_re], axis=0)
        conv = _dot3(ghi_ref[...], glo_ref[...], p)
        z = gates[o] * (conv + z * skip_ref[o:o + 1])
    o_ref[...] = z.astype(BF16)


def _hyena(proj, y_hy, layer, length, tables, spectrum, conv_w, conv_b, skip):
    nseq = (T_P if length == SEQ else T_S) // length
    row0 = 0 if length == SEQ else T_P // length
    ncb = HY_WIDTH // HY_CB
    col0 = C_HY // HY_CB
    u_spec = lambda part: pl.BlockSpec((length, HY_CB), lambda s, j: (row0 + s, col0 + part * ncb + j))
    cw_spec = lambda part: pl.BlockSpec((1, 3, HY_CB), lambda s, j: (layer, 0, part * ncb + j))
    cb_spec = lambda part: pl.BlockSpec((1, 1, HY_CB), lambda s, j: (layer, 0, part * ncb + j))
    resident = functools.partial(pl.BlockSpec, pipeline_mode=pl.Buffered(1))
    tab_specs = [resident(t.shape, lambda s, j: (0, 0)) for t in tables]
    in_specs = ([u_spec(0), u_spec(1), u_spec(2), cw_spec(0), cw_spec(1), cw_spec(2),
                 cb_spec(0), cb_spec(1), cb_spec(2),
                 pl.BlockSpec((1, HY_ORDER, HY_CB), lambda s, j: (layer, 0, j)),
                 pl.BlockSpec((HY_ORDER, 2 * length, HY_CB), lambda s, j: (0, 0, j))] + tab_specs)
    args = [proj] * 3 + [conv_w] * 3 + [conv_b.reshape(DEPTH, 1, -1)] * 3 + [skip, spectrum] + list(tables)
    aliases = {}
    if y_hy is not None:
        in_specs.append(pl.BlockSpec(memory_space=pl.ANY))
        args.append(y_hy)
        aliases = {len(args) - 1: 0}

    def body(v_ref, x1_ref, x2_ref, cwv, cw1, cw2, cbv, cb1, cb2, skip_ref, hs_ref, *rest):
        _hyena_kernel(v_ref, x1_ref, x2_ref, cwv.at[0], cw1.at[0], cw2.at[0], cbv.at[0], cb1.at[0], cb2.at[0],
                      skip_ref.at[0], hs_ref, *rest, length=length)

    return pl.pallas_call(
        body,
        out_shape=jax.ShapeDtypeStruct((T, HY_WIDTH), BF16),
        grid=(nseq, ncb), in_specs=in_specs,
        out_specs=pl.BlockSpec((length, HY_CB), lambda s, j: (row0 + s, j)),
        input_output_aliases=aliases,
        compiler_params=_cparams(("arbitrary", "arbitrary")),
        name="hyena",
    )(*args)


GC = 128
GDN_SCALE = GDN_DK ** -0.5


def _unit_tri_inverse(x):
    eye = (lax.broadcasted_iota(jnp.int32, (GC, GC), 0) == lax.broadcasted_iota(jnp.int32, (GC, GC), 1))
    t = x + eye.astype(F32)
    p = x
    for _ in range(int(math.log2(GC)) - 1):
        p = jnp.dot(p, p, precision=HIGHEST, preferred_element_type=F32)
        t = t + jnp.dot(t, p, precision=HIGHEST, preferred_element_type=F32)
    return t


def _gdn_kernel(q_ref, k_ref, v_ref, z_ref, ba_ref, cwq_ref, cwk_ref, cwv_ref, alog_ref, dt_ref, ng_ref,
                sf0_ref, sb0_ref, *rest, length):
    y_ref, sf_ref, sb_ref, qn_ref, kn_ref, vn_ref, beta_ref, g_ref, of_ref, ob_ref, s_ref = rest[-11:]
    n_chunks = length // GC
    row = lax.broadcasted_iota(jnp.int32, (length, GDN_WIDTH), 0)
    first, last = row == 0, row == length - 1

    def conv_silu(u_ref, cw_ref):
        u = u_ref[...]
        prev = jnp.where(first, 0.0, pltpu.roll(u, 1, 0))
        nxt = jnp.where(last, 0.0, pltpu.roll(u, length - 1, 0))
        c = prev * cw_ref[0:1] + u * cw_ref[1:2] + nxt * cw_ref[2:3]
        return c * jax.nn.sigmoid(c)

    q = conv_silu(q_ref, cwq_ref)
    k = conv_silu(k_ref, cwk_ref)
    vn_ref[...] = conv_silu(v_ref, cwv_ref)
    for h in range(GDN_HEADS):
        sl = slice(h * GDN_DK, (h + 1) * GDN_DK)
        qh, kh = q[:, sl], k[:, sl]
        qn_ref[:, sl] = qh * lax.rsqrt(jnp.sum(qh * qh, axis=-1, keepdims=True) + NORM_EPS) * GDN_SCALE
        kn_ref[:, sl] = kh * lax.rsqrt(jnp.sum(kh * kh, axis=-1, keepdims=True) + NORM_EPS)
    ba = ba_ref[...]
    beta_ref[...] = jax.nn.sigmoid(ba)
    g_ref[...] = -jnp.exp(alog_ref[...]) * jax.nn.softplus(ba + dt_ref[...])

    ri = lax.broadcasted_iota(jnp.int32, (GC, GC), 0)
    ci = lax.broadcasted_iota(jnp.int32, (GC, GC), 1)
    lower = (ri >= ci).astype(F32)
    upper = (ri <= ci).astype(F32)

    def chunk_step(n, carry):
        for d in range(2):
            c0 = pl.multiple_of((n if d == 0 else n_chunks - 1 - n) * GC, GC)
            rows = pl.ds(c0, GC)
            incl = (ri >= ci) if d == 0 else (ri <= ci)
            strict = (ri > ci) if d == 0 else (ri < ci)
            gcs = jnp.dot(lower if d == 0 else upper, g_ref[rows, :], precision=HIGHEST,
                          preferred_element_type=F32)
            gcs_t = gcs.T
            beta_c = beta_ref[rows, :]
            edge = GC - 1 if d == 0 else 0
            for h in range(GDN_HEADS):
                sl = slice(h * GDN_DK, (h + 1) * GDN_DK)
                ib, ig = 4 * d + h, 8 + 4 * d + h
                s = s_ref[d * GDN_HEADS + h]
                qc, kc, vc = qn_ref[rows, sl], kn_ref[rows, sl], vn_ref[rows, sl]
                g_col = gcs[:, ig:ig + 1]
                g_row = gcs_t[ig:ig + 1, :]
                g_end = gcs_t[ig:ig + 1, edge:edge + 1]
                b_col = beta_c[:, ib:ib + 1]
                decay = jnp.where(incl, jnp.exp(jnp.where(incl, g_col - g_row, 0.0)), 0.0)
                kb = kc * b_col
                kcb = kc.astype(BF16)
                a = jnp.where(strict, _nt_dot(kb.astype(BF16), kcb) * decay, 0.0)
                t = _unit_tri_inverse(-a)
                e_col = jnp.exp(g_col)
                u = jnp.dot(t, vc * b_col, precision=HIGHEST, preferred_element_type=F32)
                w = jnp.dot(t, kb * e_col, precision=HIGHEST, preferred_element_type=F32)
                attn = jnp.where(incl, _nt_dot(qc.astype(BF16), kcb) * decay, 0.0)
                sb = s.astype(BF16)
                v_new = u - jnp.dot(w.astype(BF16), sb, preferred_element_type=F32)
                o = (jnp.dot((qc * e_col).astype(BF16), sb, preferred_element_type=F32)
                     + jnp.dot(attn.astype(BF16), v_new.astype(BF16), preferred_element_type=F32))
                k_dec = kc * jnp.exp(g_end - g_col)
                s = s * jnp.exp(g_end) + jnp.dot(k_dec.T.astype(BF16), v_new.astype(BF16),
                                                 preferred_element_type=F32)
                (of_ref if d == 0 else ob_ref)[rows, sl] = o
                s_ref[d * GDN_HEADS + h] = s
        return carry

    s_ref[:GDN_HEADS] = sf0_ref[0]
    s_ref[GDN_HEADS:] = sb0_ref[0]
    lax.fori_loop(0, n_chunks, chunk_step, 0)
    sf_ref[0, 0] = s_ref[:GDN_HEADS]
    sb_ref[0, 0] = s_ref[GDN_HEADS:]
    for h in range(GDN_HEADS):
        sl = slice(h * GDN_DV, (h + 1) * GDN_DV)
        o = of_ref[:, sl] + ob_ref[:, sl]
        o = o * lax.rsqrt(jnp.mean(o * o, axis=-1, keepdims=True) + NORM_EPS) * ng_ref[...]
        zh = z_ref[:, sl]
        y_ref[:, sl] = (o * (zh * jax.nn.sigmoid(zh))).astype(BF16)


def _gdn(proj, y_gdn, states_out, layer, length, s_f0, s_b0, conv_w, a_log, dt_bias, norm_g):
    nseq = (T_P if length == SEQ else T_S) // length
    row0 = 0 if length == SEQ else T_P // length
    col = lambda c: c // GDN_WIDTH
    blk = lambda c: pl.BlockSpec((length, GDN_WIDTH), lambda s: (row0 + s, col(c)))
    cw = lambda part: pl.BlockSpec((1, 3, GDN_WIDTH), lambda s: (layer, 0, part))
    vec = lambda: pl.BlockSpec((1, 1, LANE), lambda s: (layer, 0, 0))
    st_in = pl.BlockSpec((1, GDN_HEADS, GDN_DK, GDN_DV), lambda s: (s, 0, 0, 0))
    st_out = pl.BlockSpec((1, 1, GDN_HEADS, GDN_DK, GDN_DV), lambda s: (s, layer, 0, 0, 0))
    pad_lanes = lambda x, off: jnp.zeros((DEPTH, 1, LANE), F32).at[:, 0, off:off + 2 * GDN_HEADS].set(
        x.reshape(DEPTH, -1))
    in_specs = [blk(C_GDN), blk(C_GDN + GDN_WIDTH), blk(C_GDN + 2 * GDN_WIDTH), blk(C_Z),
                pl.BlockSpec((length, LANE), lambda s: (row0 + s, C_BA // LANE)),
                cw(0), cw(1), cw(2), vec(), vec(), vec(), st_in, st_in]
    args = [proj] * 5 + [conv_w] * 3 + [pad_lanes(a_log, 2 * GDN_HEADS), pad_lanes(dt_bias, 2 * GDN_HEADS),
                                        norm_g.reshape(DEPTH, 1, GDN_DV), s_f0, s_b0]
    st_shape = jax.ShapeDtypeStruct((nseq, DEPTH, GDN_HEADS, GDN_DK, GDN_DV), F32)
    aliases = {}
    for carried, out_idx in ((y_gdn, 0),) + (((states_out[0], 1), (states_out[1], 2)) if states_out else ()):
        if carried is not None:
            in_specs.append(pl.BlockSpec(memory_space=pl.ANY))
            args.append(carried)
            aliases[len(args) - 1] = out_idx

    def body(q_ref, k_ref, v_ref, z_ref, ba_ref, cwq, cwk, cwv, alog, dt, ng, sf0, sb0, *rest):
        _gdn_kernel(q_ref, k_ref, v_ref, z_ref, ba_ref, cwq.at[0], cwk.at[0], cwv.at[0], alog.at[0], dt.at[0],
                    ng.at[0], sf0, sb0, *rest, length=length)

    seq_buf = lambda w: pltpu.VMEM((length, w), F32)
    return pl.pallas_call(
        body,
        out_shape=[jax.ShapeDtypeStruct((T, GDN_WIDTH), BF16), st_shape, st_shape],
        grid=(nseq,), in_specs=in_specs,
        out_specs=[pl.BlockSpec((length, GDN_WIDTH), lambda s: (row0 + s, 0)), st_out, st_out],
        scratch_shapes=[seq_buf(GDN_WIDTH), seq_buf(GDN_WIDTH), seq_buf(GDN_WIDTH), seq_buf(LANE), seq_buf(LANE),
                        seq_buf(GDN_WIDTH), seq_buf(GDN_WIDTH),
                        pltpu.VMEM((2 * GDN_HEADS, GDN_DK, GDN_DV), F32)],
        input_output_aliases=aliases,
        compiler_params=_cparams(("arbitrary",)),
        name="gated_deltanet",
    )(*args)


def kernel(x_prompt, x_sample, cache_k, cache_v, state_fwd, state_bwd, c, c_ctx, ln1_g, ln2_g, w_mod, b_mod,
           w_in, na_rpb, hy_conv_w, hy_conv_b, hy_w1, hy_b1, hy_freq, hy_w2, hy_b2, hy_w3, hy_decay, hy_skip,
           gdn_conv_w, gdn_a_log, gdn_dt_bias, gdn_norm_g, w_pa, w_pb, w_pc, b_gate, w_out, ffn_w_up,
           ffn_conv_w, ffn_conv_b, ffn_w_down, final_g):
    x = jnp.concatenate([x_prompt.reshape(T_P, D), x_sample.reshape(T_S, D)], axis=0)
    cond = jnp.concatenate([c_ctx[None], c, jnp.zeros((N_COND - 1 - DEC_BATCH, D), F32)], axis=0)
    mod = _mod_table(cond, w_mod, b_mod).reshape(DEPTH * N_COND, 1, 6 * D)

    w_pa_b, w_pb_b, w_pc_b, w_out_b = (w.astype(BF16) for w in (w_pa, w_pb, w_pc, w_out))
    w_down_b = ffn_w_down.astype(BF16)
    tables = {n: tuple(_trig_table(n, True)) + tuple(_trig_table(n, False)) for n in (SEQ, DEC_SEQ)}
    zero_state = jnp.zeros((BATCH, GDN_HEADS, GDN_DK, GDN_DV), F32)

    caches = None
    states = None
    for layer in range(DEPTH):
        h = _norm_mod(x, ln1_g, mod, layer, 0)
        proj = _in_proj(h, w_in, layer)

        y_na, new_k, new_v = _ctx_attention(proj, layer, caches)
        caches = (new_k, new_v)
        y_na = _na_attention(proj, cache_k, cache_v, na_rpb[layer], y_na, layer)

        y_hy = None
        for n in (SEQ, DEC_SEQ):
            spectrum = _hy_filter_spectrum(n, tables[n][0], tables[n][1], hy_w1[layer], hy_b1[layer],
                                           hy_freq[layer], hy_w2[layer], hy_b2[layer], hy_w3[layer],
                                           hy_decay[layer])
            y_hy = _hyena(proj, y_hy, layer, n, tables[n], spectrum, hy_conv_w, hy_conv_b, hy_skip)

        y_gdn, s_f, s_b = _gdn(proj, None, states, layer, SEQ, zero_state, zero_state,
                               gdn_conv_w, gdn_a_log, gdn_dt_bias, gdn_norm_g)
        states = (s_f, s_b)
        y_gdn = _gdn(proj, y_gdn, None, layer, DEC_SEQ, state_fwd[:, layer], state_bwd[:, layer],
                     gdn_conv_w, gdn_a_log, gdn_dt_bias, gdn_norm_g)[0]

        x = _mix_out(y_na, y_hy, y_gdn, proj, b_gate, w_pa_b, w_pb_b, w_pc_b, w_out_b, x, mod, layer)
        h = _norm_mod(x, ln2_g, mod, layer, 3)
        act = _ffn_up(h, ffn_w_up, ffn_conv_w, ffn_conv_b, layer)
        x = _ffn_down(act, w_down_b, x, mod, layer)

    y = _final_norm(x, final_g)
    cache_shape = (BATCH, DEPTH, SEQ, NA_HEADS, NA_DH)
    return (y[:T_P].reshape(BATCH, SEQ, D), y[T_P:].reshape(DEC_BATCH, DEC_SEQ, D),
            caches[0].reshape(cache_shape), caches[1].reshape(cache_shape), states[0], states[1])
```

```python
import functools
import math

import jax
import jax.numpy as jnp
import numpy as np
from jax import lax
from jax.experimental import pallas as pl
from jax.experimental.pallas import tpu as pltpu

F32 = jnp.float32
BF16 = jnp.bfloat16

D = 2048
BATCH, SEQ = 32, 256
DEC_BATCH, DEC_SEQ = 4, 1024
DEPTH = 2
GRID_W = 64
NA_HEADS, NA_DH = 8, 128
NA_WIDTH = NA_HEADS * NA_DH
NA_WIN_ROWS, NA_WIN_COLS = 8, 16
HY_WIDTH = 512
HY_ORDER = 2
HY_BANDS = 16
GDN_HEADS, GDN_DK, GDN_DV = 4, 128, 128
GDN_WIDTH = GDN_HEADS * GDN_DV
D_FF = 5632
N_BRANCH = 3
NORM_EPS = 1e-6
NEG_INF = -1e30

T_P = BATCH * SEQ
T_S = DEC_BATCH * DEC_SEQ
T = T_P + T_S
N_COND = 8

C_Q, C_K, C_V = 0, NA_WIDTH, 2 * NA_WIDTH
C_HY = 3 * NA_WIDTH
C_GDN = C_HY + 3 * HY_WIDTH
C_Z = C_GDN + 3 * GDN_HEADS * GDN_DK
C_BA = C_Z + GDN_WIDTH
C_GATE = C_BA + 4 * GDN_HEADS
N_IN = C_GATE + N_BRANCH * D
GATE_BLOCK = C_BA
GATE_SHIFT = C_GATE - C_BA

LANE = 128
VMEM_LIMIT = 56 * 1024 * 1024
HIGHEST = lax.Precision.HIGHEST


def _cparams(sem):
    return pltpu.CompilerParams(dimension_semantics=sem, vmem_limit_bytes=VMEM_LIMIT)


def _cond_row(tile, rows_per_tile):
    first_latent = T_P // rows_per_tile
    per_seq = DEC_SEQ // rows_per_tile
    return jnp.where(tile < first_latent, 0, 1 + (tile - first_latent) // per_seq)


def _mod_kernel(c_ref, w_ref, b_ref, o_ref):
    c = c_ref[...]
    s = (c * jax.nn.sigmoid(c)).astype(BF16)
    o_ref[0] = jnp.dot(s, w_ref[0].astype(BF16), preferred_element_type=F32) + b_ref[0]


def _mod_table(cond, w_mod, b_mod):
    tn = 1024
    n = 6 * D
    return pl.pallas_call(
        _mod_kernel,
        out_shape=jax.ShapeDtypeStruct((DEPTH, N_COND, n), F32),
        grid=(DEPTH, n // tn),
        in_specs=[
            pl.BlockSpec((N_COND, D), lambda l, j: (0, 0)),
            pl.BlockSpec((1, D, tn), lambda l, j: (l, 0, j)),
            pl.BlockSpec((1, 1, tn), lambda l, j: (l, 0, j)),
        ],
        out_specs=pl.BlockSpec((1, N_COND, tn), lambda l, j: (l, 0, j)),
        compiler_params=_cparams(("arbitrary", "arbitrary")),
        name="mod_table",
    )(cond, w_mod, b_mod.reshape(DEPTH, 1, n))


def _norm_mod_kernel(x_ref, g_ref, sh_ref, sc_ref, o_ref):
    x = x_ref[...]
    y = x * lax.rsqrt(jnp.mean(x * x, axis=-1, keepdims=True) + NORM_EPS) * g_ref[0]
    o_ref[...] = (y * (1.0 + sc_ref[0]) + sh_ref[0]).astype(o_ref.dtype)


def _norm_mod(x, g, mod, layer, shift_chunk):
    tm = 256
    return pl.pallas_call(
        _norm_mod_kernel,
        out_shape=jax.ShapeDtypeStruct((T, D), BF16),
        grid=(T // tm,),
        in_specs=[
            pl.BlockSpec((tm, D), lambda i: (i, 0)),
            pl.BlockSpec((1, 1, D), lambda i: (layer, 0, 0)),
            pl.BlockSpec((1, 1, D), lambda i: (layer * N_COND + _cond_row(i, tm), 0, shift_chunk)),
            pl.BlockSpec((1, 1, D), lambda i: (layer * N_COND + _cond_row(i, tm), 0, shift_chunk + 1)),
        ],
        out_specs=pl.BlockSpec((tm, D), lambda i: (i, 0)),
        compiler_params=_cparams(("arbitrary",)),
        name="norm_mod",
    )(x, g.reshape(DEPTH, 1, D), mod, mod)


FINAL_TM = 256


def _final_norm_kernel(x_ref, g_ref, yp_ref, ys_ref):
    x = x_ref[...]
    y = x * lax.rsqrt(jnp.mean(x * x, axis=-1, keepdims=True) + NORM_EPS) * g_ref[...]
    is_context = pl.program_id(0) < T_P // FINAL_TM

    @pl.when(is_context)
    def _():
        yp_ref[...] = y

    @pl.when(jnp.logical_not(is_context))
    def _():
        ys_ref[...] = y


def _final_norm(x, g):
    tm = FINAL_TM
    n_ctx = T_P // tm
    return pl.pallas_call(
        _final_norm_kernel,
        out_shape=[jax.ShapeDtypeStruct((T_P, D), F32), jax.ShapeDtypeStruct((T_S, D), F32)],
        grid=(T // tm,),
        in_specs=[pl.BlockSpec((tm, D), lambda i: (i, 0)), pl.BlockSpec((1, D), lambda i: (0, 0))],
        out_specs=[pl.BlockSpec((tm, D), lambda i: (jnp.minimum(i, n_ctx - 1), 0)),
                   pl.BlockSpec((tm, D), lambda i: (jnp.maximum(i - n_ctx, 0), 0))],
        compiler_params=_cparams(("arbitrary",)),
        name="final_norm",
    )(x, g.reshape(1, D))


def _proj_kernel(h_ref, wt_ref, o_ref, wb_ref):
    @pl.when(pl.program_id(1) == 0)
    def _():
        wb_ref[...] = wt_ref[0].astype(BF16)

    o_ref[...] = lax.dot_general(h_ref[...], wb_ref[...], (((1,), (1,)), ((), ())), preferred_element_type=F32)


def _in_proj(h, w_in_t, layer):
    tm, tn = 1024, 1024
    return pl.pallas_call(
        _proj_kernel,
        out_shape=jax.ShapeDtypeStruct((T, N_IN), F32),
        grid=(pl.cdiv(N_IN, tn), T // tm),
        in_specs=[
            pl.BlockSpec((tm, D), lambda j, i: (i, 0)),
            pl.BlockSpec((1, tn, D), lambda j, i: (layer, j, 0)),
        ],
        out_specs=pl.BlockSpec((tm, tn), lambda j, i: (i, j)),
        scratch_shapes=[pltpu.VMEM((tn, D), BF16)],
        compiler_params=_cparams(("arbitrary", "arbitrary")),
        name="in_proj",
    )(h, w_in_t)


def _mix_out_kernel(yna_ref, yhy_ref, ygdn_ref, gl_ref, bg_ref, wpa_ref, wpb_ref, wpc_ref, wout_ref,
                    x_ref, gt_ref, o_ref):
    width = gl_ref.shape[1]
    gl = pltpu.roll(gl_ref[...], width - GATE_SHIFT, 1)[:, :N_BRANCH * D] + bg_ref[0]
    gates = jax.nn.sigmoid(gl)
    merged = (gates[:, :D] * jnp.dot(yna_ref[...], wpa_ref[0], preferred_element_type=F32)
              + gates[:, D:2 * D] * jnp.dot(yhy_ref[...], wpb_ref[0], preferred_element_type=F32)
              + gates[:, 2 * D:] * jnp.dot(ygdn_ref[...], wpc_ref[0], preferred_element_type=F32))
    r = jnp.dot(merged.astype(BF16), wout_ref[0], preferred_element_type=F32)
    o_ref[...] = x_ref[...] + gt_ref[0] * r


def _mix_out(y_na, y_hy, y_gdn, proj, b_gate, w_pa, w_pb, w_pc, w_out, x, mod, layer):
    tm = 256
    resident = functools.partial(pl.BlockSpec, pipeline_mode=pl.Buffered(1))
    return pl.pallas_call(
        _mix_out_kernel,
        out_shape=jax.ShapeDtypeStruct((T, D), F32),
        grid=(T // tm,),
        in_specs=[
            pl.BlockSpec((tm, NA_WIDTH), lambda i: (i, 0)),
            pl.BlockSpec((tm, HY_WIDTH), lambda i: (i, 0)),
            pl.BlockSpec((tm, GDN_WIDTH), lambda i: (i, 0)),
            pl.BlockSpec((tm, GATE_BLOCK), lambda i: (i, 1)),
            pl.BlockSpec((1, 1, N_BRANCH * D), lambda i: (layer, 0, 0)),
            resident((1, NA_WIDTH, D), lambda i: (layer, 0, 0)),
            resident((1, HY_WIDTH, D), lambda i: (layer, 0, 0)),
            resident((1, GDN_WIDTH, D), lambda i: (layer, 0, 0)),
            resident((1, D, D), lambda i: (layer, 0, 0)),
            pl.BlockSpec((tm, D), lambda i: (i, 0)),
            pl.BlockSpec((1, 1, D), lambda i: (layer * N_COND + _cond_row(i, tm), 0, 2)),
        ],
        out_specs=pl.BlockSpec((tm, D), lambda i: (i, 0)),
        compiler_params=_cparams(("arbitrary",)),
        name="mix_out",
    )(y_na, y_hy, y_gdn, proj, b_gate.reshape(DEPTH, 1, N_BRANCH * D), w_pa, w_pb, w_pc, w_out, x, mod)


FFN_TM = 1024


def _ffn_up_kernel(h_ref, wa_ref, wb_ref, cwa_ref, cwb_ref, cba_ref, cbb_ref, o_ref, wab_ref, wbb_ref):
    m = pl.program_id(1)

    @pl.when(m == 0)
    def _():
        wab_ref[...] = wa_ref[0].astype(BF16)
        wbb_ref[...] = wb_ref[0].astype(BF16)

    seq_len = jnp.where(m < T_P // FFN_TM, SEQ, DEC_SEQ)
    pos = lax.broadcasted_iota(jnp.int32, (FFN_TM, 1), 0) & (seq_len - 1)
    has_prev = pos != 0
    has_next = pos != seq_len - 1
    h = h_ref[...]

    def conv(w_ref, cw_ref, cb_ref):
        up = jnp.dot(h, w_ref[...], preferred_element_type=F32)
        prev = jnp.where(has_prev, pltpu.roll(up, 1, 0), 0.0)
        nxt = jnp.where(has_next, pltpu.roll(up, FFN_TM - 1, 0), 0.0)
        cw = cw_ref[0]
        return prev * cw[0:1] + up * cw[1:2] + nxt * cw[2:3] + cb_ref[0]

    ua = conv(wab_ref, cwa_ref, cba_ref)
    ub = conv(wbb_ref, cwb_ref, cbb_ref)
    o_ref[...] = (ua * jax.nn.sigmoid(ua) * ub).astype(BF16)


def _ffn_up(h, w_up, conv_w, conv_b, layer):
    tn = 512
    nt = D_FF // tn
    conv_b = conv_b.reshape(DEPTH, 1, 2 * D_FF)
    return pl.pallas_call(
        _ffn_up_kernel,
        out_shape=jax.ShapeDtypeStruct((T, D_FF), BF16),
        grid=(nt, T // FFN_TM),
        in_specs=[
            pl.BlockSpec((FFN_TM, D), lambda j, i: (i, 0)),
            pl.BlockSpec((1, D, tn), lambda j, i: (layer, 0, j)),
            pl.BlockSpec((1, D, tn), lambda j, i: (layer, 0, nt + j)),
            pl.BlockSpec((1, 3, tn), lambda j, i: (layer, 0, j)),
            pl.BlockSpec((1, 3, tn), lambda j, i: (layer, 0, nt + j)),
            pl.BlockSpec((1, 1, tn), lambda j, i: (layer, 0, j)),
            pl.BlockSpec((1, 1, tn), lambda j, i: (layer, 0, nt + j)),
        ],
        out_specs=pl.BlockSpec((FFN_TM, tn), lambda j, i: (i, j)),
        scratch_shapes=[pltpu.VMEM((D, tn), BF16), pltpu.VMEM((D, tn), BF16)],
        compiler_params=_cparams(("arbitrary", "arbitrary")),
        name="ffn_up",
    )(h, w_up, w_up, conv_w, conv_w, conv_b, conv_b)


def _ffn_down_kernel(a_ref, w_ref, x_ref, gt_ref, o_ref):
    o_ref[...] = x_ref[...] + gt_ref[0] * jnp.dot(a_ref[...], w_ref[0], preferred_element_type=F32)


def _ffn_down(act, w_down, x, mod, layer):
    tm, tn = 256, 512
    return pl.pallas_call(
        _ffn_down_kernel,
        out_shape=jax.ShapeDtypeStruct((T, D), F32),
        grid=(D // tn, T // tm),
        in_specs=[
            pl.BlockSpec((tm, D_FF), lambda j, i: (i, 0)),
            pl.BlockSpec((1, D_FF, tn), lambda j, i: (layer, 0, j)),
            pl.BlockSpec((tm, tn), lambda j, i: (i, j)),
            pl.BlockSpec((1, 1, tn), lambda j, i: (layer * N_COND + _cond_row(i, tm), 0, 5 * (D // tn) + j)),
        ],
        out_specs=pl.BlockSpec((tm, tn), lambda j, i: (i, j)),
        compiler_params=_cparams(("arbitrary", "arbitrary")),
        name="ffn_down",
    )(act, w_down, x, mod)


ATT_SCALE = NA_DH ** -0.5


def _nt_dot(a, b):
    return lax.dot_general(a, b, (((1,), (1,)), ((), ())), preferred_element_type=F32)


def _ctx_attn_kernel(q_ref, k_ref, v_ref, o_ref, ko_ref, vo_ref):
    for h in range(NA_HEADS):
        sl = slice(h * NA_DH, (h + 1) * NA_DH)
        q = (q_ref[:, sl] * ATT_SCALE).astype(BF16)
        s = _nt_dot(q, k_ref[:, sl].astype(BF16))
        e = jnp.exp(s - jnp.max(s, axis=-1, keepdims=True))
        o = jnp.dot(e.astype(BF16), v_ref[:, sl].astype(BF16), preferred_element_type=F32)
        o_ref[:, sl] = (o / jnp.sum(e, axis=-1, keepdims=True)).astype(BF16)
    ko_ref[0, 0] = k_ref[...]
    vo_ref[0, 0] = v_ref[...]


def _ctx_attention(proj, layer, caches):
    cache_shape = jax.ShapeDtypeStruct((BATCH, DEPTH, SEQ, NA_WIDTH), F32)
    cache_spec = pl.BlockSpec((1, 1, SEQ, NA_WIDTH), lambda b: (b, layer, 0, 0))
    qkv_specs = [pl.BlockSpec((SEQ, NA_WIDTH), lambda b, j=j: (b, j)) for j in range(3)]
    out_shape = [jax.ShapeDtypeStruct((T, NA_WIDTH), BF16), cache_shape, cache_shape]
    out_specs = [pl.BlockSpec((SEQ, NA_WIDTH), lambda b: (b, 0)), cache_spec, cache_spec]
    if caches is None:
        return pl.pallas_call(
            _ctx_attn_kernel, out_shape=out_shape, grid=(BATCH,), in_specs=qkv_specs, out_specs=out_specs,
            compiler_params=_cparams(("arbitrary",)), name="ctx_attention",
        )(proj, proj, proj)

    def body(q_ref, k_ref, v_ref, kc_ref, vc_ref, o_ref, ko_ref, vo_ref):
        del kc_ref, vc_ref
        _ctx_attn_kernel(q_ref, k_ref, v_ref, o_ref, ko_ref, vo_ref)

    any_spec = pl.BlockSpec(memory_space=pl.ANY)
    return pl.pallas_call(
        body, out_shape=out_shape, grid=(BATCH,), in_specs=qkv_specs + [any_spec, any_spec],
        out_specs=out_specs, input_output_aliases={3: 1, 4: 2},
        compiler_params=_cparams(("arbitrary",)), name="ctx_attention",
    )(proj, proj, proj, *caches)


NA_ROWS = DEC_SEQ // GRID_W
NA_WIN_TOK = NA_WIN_ROWS * GRID_W


def _na_window_start(r):
    return jnp.clip(r - NA_WIN_ROWS // 2, 0, NA_ROWS - NA_WIN_ROWS)


def _na_bias_table(rpb):
    col = np.arange(GRID_W)
    col_start = np.clip(col - NA_WIN_COLS // 2, 0, GRID_W - NA_WIN_COLS)
    col_mask = (col[None, :] >= col_start[:, None]) & (col[None, :] < col_start[:, None] + NA_WIN_COLS)
    rel_col = np.clip(col[None, :] - col[:, None] + NA_WIN_COLS - 1, 0, 2 * NA_WIN_COLS - 2)
    banded = jnp.where(col_mask, rpb[:, :, rel_col], NEG_INF)
    per_offset = [banded[:, NA_WIN_ROWS - 1 - d:2 * NA_WIN_ROWS - 1 - d] for d in range(NA_WIN_ROWS)]
    tab = jnp.stack(per_offset, axis=0)
    return tab.transpose(0, 1, 3, 2, 4).reshape(NA_WIN_ROWS, NA_HEADS, GRID_W, NA_WIN_TOK)


def _na_attn_kernel(q_ref, k_ref, v_ref, ck_ref, cv_ref, bias_ref, y_in_ref, o_ref):
    del y_in_ref
    r = pl.program_id(1)
    start = pl.multiple_of(_na_window_start(r) * GRID_W, GRID_W)
    for h in range(NA_HEADS):
        sl = slice(h * NA_DH, (h + 1) * NA_DH)
        q = (q_ref[:, sl] * ATT_SCALE).astype(BF16)
        s_win = _nt_dot(q, k_ref[pl.ds(start, NA_WIN_TOK), sl].astype(BF16)) + bias_ref[0, h]
        s_ctx = _nt_dot(q, ck_ref[0, 0, :, sl].astype(BF16))
        m = jnp.maximum(jnp.max(s_win, axis=-1, keepdims=True), jnp.max(s_ctx, axis=-1, keepdims=True))
        e_win = jnp.exp(s_win - m)
        e_ctx = jnp.exp(s_ctx - m)
        o = (jnp.dot(e_win.astype(BF16), v_ref[pl.ds(start, NA_WIN_TOK), sl].astype(BF16),
                     preferred_element_type=F32)
             + jnp.dot(e_ctx.astype(BF16), cv_ref[0, 0, :, sl].astype(BF16), preferred_element_type=F32))
        denom = jnp.sum(e_win, axis=-1, keepdims=True) + jnp.sum(e_ctx, axis=-1, keepdims=True)
        o_ref[:, sl] = (o / denom).astype(BF16)


def _na_attention(proj, cache_k, cache_v, rpb, y_na, layer):
    q_row0 = T_P // GRID_W
    seq0 = T_P // DEC_SEQ
    kv_spec = lambda j: pl.BlockSpec((DEC_SEQ, NA_WIDTH), lambda b, r: (seq0 + b, j))
    ctx_spec = pl.BlockSpec((1, 1, SEQ, NA_WIDTH), lambda b, r: (b, layer, 0, 0))
    return pl.pallas_call(
        _na_attn_kernel,
        out_shape=jax.ShapeDtypeStruct((T, NA_WIDTH), BF16),
        grid=(DEC_BATCH, NA_ROWS),
        in_specs=[
            pl.BlockSpec((GRID_W, NA_WIDTH), lambda b, r: (q_row0 + b * NA_ROWS + r, 0)),
            kv_spec(1), kv_spec(2), ctx_spec, ctx_spec,
            pl.BlockSpec((1, NA_HEADS, GRID_W, NA_WIN_TOK), lambda b, r: (r - _na_window_start(r), 0, 0, 0)),
            pl.BlockSpec(memory_space=pl.ANY),
        ],
        out_specs=pl.BlockSpec((GRID_W, NA_WIDTH), lambda b, r: (q_row0 + b * NA_ROWS + r, 0)),
        input_output_aliases={6: 0},
        compiler_params=_cparams(("arbitrary", "arbitrary")),
        name="na_attention",
    )(proj, proj, proj, cache_k.reshape(DEC_BATCH, DEPTH, SEQ, NA_WIDTH),
      cache_v.reshape(DEC_BATCH, DEPTH, SEQ, NA_WIDTH), _na_bias_table(rpb), y_na)


HY_CB = 256


def _split_bf16(x):
    hi = x.astype(BF16)
    return hi, (x - hi.astype(F32)).astype(BF16)


def _dot3(a_hi, a_lo, b):
    b_hi, b_lo = _split_bf16(b)
    return (jnp.dot(a_hi, b_hi, preferred_element_type=F32) + jnp.dot(a_lo, b_hi, preferred_element_type=F32)
            + jnp.dot(a_hi, b_lo, preferred_element_type=F32))


def _trig_table_kernel(hi_ref, lo_ref, *, length, freq_on_rows, scale):
    rows, cols = hi_ref.shape
    r = lax.broadcasted_iota(jnp.int32, (rows, cols), 0) + pl.program_id(0) * rows
    c = lax.broadcasted_iota(jnp.int32, (rows, cols), 1)
    f, t = (r, c) if freq_on_rows else (c, r)
    k = f & (length - 1)
    quarter = jnp.where(f >= length, length, 0)
    phase = (2 * k + 1) * t + (-quarter if freq_on_rows else quarter)
    phase = phase & (4 * length - 1)
    phase = jnp.where(phase >= 2 * length, phase - 4 * length, phase)
    val = jnp.cos(phase.astype(F32) * (math.pi / (2 * length))) * scale
    hi, lo = _split_bf16(val)
    hi_ref[...] = hi
    lo_ref[...] = lo


def _trig_table(length, freq_on_rows):
    shape = (2 * length, length) if freq_on_rows else (length, 2 * length)
    tr = 256
    spec = pl.BlockSpec((tr, shape[1]), lambda i: (i, 0))
    return pl.pallas_call(
        functools.partial(_trig_table_kernel, length=length, freq_on_rows=freq_on_rows,
                          scale=1.0 if freq_on_rows else 1.0 / length),
        out_shape=[jax.ShapeDtypeStruct(shape, BF16)] * 2,
        grid=(shape[0] // tr,), out_specs=[spec, spec],
        compiler_params=_cparams(("arbitrary",)), name="dft_table",
    )()


def _hy_filter_kernel(tn_ref, band_ref, w1_ref, b1_ref, fr_ref, w2_ref, b2_ref, w3_ref, dec_ref,
                      fhi_ref, flo_ref, o_ref, *, length):
    t_norm = tn_ref[...]
    t_idx = lax.broadcasted_iota(jnp.int32, (length, LANE), 0).astype(F32)
    lane = lax.broadcasted_iota(jnp.int32, (length, LANE), 1)
    ang = (2.0 * math.pi / length) * t_idx * band_ref[...]
    z = jnp.where(lane == 0, t_norm,
                  jnp.where(lane <= HY_BANDS, jnp.cos(ang), jnp.where(lane <= 2 * HY_BANDS, jnp.sin(ang), 0.0)))
    hdn = jnp.sin(fr_ref[0:1] * (jnp.dot(z, w1_ref[...], precision=HIGHEST, preferred_element_type=F32)
                                 + b1_ref[...]))
    hdn = jnp.sin(fr_ref[1:2] * (jnp.dot(hdn, w2_ref[...], precision=HIGHEST, preferred_element_type=F32)
                                 + b2_ref[...]))
    filt = jnp.dot(hdn, w3_ref[...], precision=HIGHEST, preferred_element_type=F32)
    filt = filt * jnp.exp(-t_norm * jnp.abs(dec_ref[...]))
    first = lax.broadcasted_iota(jnp.int32, (length, HY_WIDTH), 0) == 0
    for o in range(HY_ORDER):
        fwd = filt[:, (2 * o) * HY_WIDTH:(2 * o + 1) * HY_WIDTH]
        bwd = filt[:, (2 * o + 1) * HY_WIDTH:(2 * o + 2) * HY_WIDTH]
        bwd = jnp.where(first, 0.0, pltpu.roll(bwd, 1, 0))
        o_ref[o, :length] = _dot3(fhi_ref[:length], flo_ref[:length], fwd + bwd)
        o_ref[o, length:] = _dot3(fhi_ref[length:], flo_ref[length:], bwd - fwd)


def _hy_filter_spectrum(length, f_hi, f_lo, w1, b1, freq, w2, b2, w3, decay):
    emb = 1 + 2 * HY_BANDS
    t_norm = jnp.linspace(0.0, 1.0, length, dtype=F32).reshape(length, 1)
    bands = np.zeros((1, LANE), np.float32)
    bands[0, 1:1 + HY_BANDS] = bands[0, 1 + HY_BANDS:emb] = np.linspace(1e-4, HY_BANDS - 1, HY_BANDS,
                                                                        dtype=np.float32)
    w1p = jnp.zeros((LANE, w1.shape[1]), F32).at[:emb].set(w1)
    return pl.pallas_call(
        functools.partial(_hy_filter_kernel, length=length),
        out_shape=jax.ShapeDtypeStruct((HY_ORDER, 2 * length, HY_WIDTH), F32),
        compiler_params=pltpu.CompilerParams(vmem_limit_bytes=VMEM_LIMIT),
        name="hyena_filter",
    )(t_norm, jnp.asarray(bands), w1p, b1.reshape(1, -1), freq, w2, b2.reshape(1, -1), w3,
      decay.reshape(1, -1), f_hi, f_lo)


def _hyena_kernel(v_ref, x1_ref, x2_ref, cwv_ref, cw1_ref, cw2_ref, cbv_ref, cb1_ref, cb2_ref, skip_ref,
                  hs_ref, fhi_ref, flo_ref, ghi_ref, glo_ref, *rest, length):
    o_ref = rest[-1]
    row = lax.broadcasted_iota(jnp.int32, (length, HY_CB), 0)
    first, last = row == 0, row == length - 1

    def short_conv(u_ref, cw_ref, cb_ref):
        u = u_ref[...]
        prev = jnp.where(first, 0.0, pltpu.roll(u, 1, 0))
        nxt = jnp.where(last, 0.0, pltpu.roll(u, length - 1, 0))
        return prev * cw_ref[0:1] + u * cw_ref[1:2] + nxt * cw_ref[2:3] + cb_ref[...]

    z = short_conv(v_ref, cwv_ref, cbv_ref)
    gates = (short_conv(x1_ref, cw1_ref, cb1_ref), short_conv(x2_ref, cw2_ref, cb2_ref))
    for o in range(HY_ORDER):
        zf = _dot3(fhi_ref[...], flo_ref[...], z)
        zc, zs = zf[:length], zf[length:]
        h_re, h_im = hs_ref[o, :length], hs_ref[o, length:]
        p = jnp.concatenate([zc * h_re + zs * h_im, zc * h_im - zs * h_re], axis=0)
        conv = _dot3(ghi_ref[...], glo_ref[...], p)
        z = gates[o] * (conv + z * skip_ref[o:o + 1])
    o_ref[...] = z.astype(BF16)


def _hyena(proj, y_hy, layer, length, tables, spectrum, conv_w, conv_b, skip):
    nseq = (T_P if length == SEQ else T_S) // length
    row0 = 0 if length == SEQ else T_P // length
    ncb = HY_WIDTH // HY_CB
    col0 = C_HY // HY_CB
    u_spec = lambda part: pl.BlockSpec((length, HY_CB), lambda s, j: (row0 + s, col0 + part * ncb + j))
    cw_spec = lambda part: pl.BlockSpec((1, 3, HY_CB), lambda s, j: (layer, 0, part * ncb + j))
    cb_spec = lambda part: pl.BlockSpec((1, 1, HY_CB), lambda s, j: (layer, 0, part * ncb + j))
    resident = functools.partial(pl.BlockSpec, pipeline_mode=pl.Buffered(1))
    tab_specs = [resident(t.shape, lambda s, j: (0, 0)) for t in tables]
    in_specs = ([u_spec(0), u_spec(1), u_spec(2), cw_spec(0), cw_spec(1), cw_spec(2),
                 cb_spec(0), cb_spec(1), cb_spec(2),
                 pl.BlockSpec((1, HY_ORDER, HY_CB), lambda s, j: (layer, 0, j)),
                 pl.BlockSpec((HY_ORDER, 2 * length, HY_CB), lambda s, j: (0, 0, j))] + tab_specs)
    args = [proj] * 3 + [conv_w] * 3 + [conv_b.reshape(DEPTH, 1, -1)] * 3 + [skip, spectrum] + list(tables)
    aliases = {}
    if y_hy is not None:
        in_specs.append(pl.BlockSpec(memory_space=pl.ANY))
        args.append(y_hy)
        aliases = {len(args) - 1: 0}

    def body(v_ref, x1_ref, x2_ref, cwv, cw1, cw2, cbv, cb1, cb2, skip_ref, hs_ref, *rest):
        _hyena_kernel(v_ref, x1_ref, x2_ref, cwv.at[0], cw1.at[0], cw2.at[0], cbv.at[0], cb1.at[0], cb2.at[0],
                      skip_ref.at[0], hs_ref, *rest, length=length)

    return pl.pallas_call(
        body,
        out_shape=jax.ShapeDtypeStruct((T, HY_WIDTH), BF16),
        grid=(nseq, ncb), in_specs=in_specs,
        out_specs=pl.BlockSpec((length, HY_CB), lambda s, j: (row0 + s, j)),
        input_output_aliases=aliases,
        compiler_params=_cparams(("arbitrary", "arbitrary")),
        name="hyena",
    )(*args)


GC = 128
GDN_SCALE = GDN_DK ** -0.5


def _merge_masks(lower):
    ri = lax.broadcasted_iota(jnp.int32, (GC, GC), 0)
    ci = lax.broadcasted_iota(jnp.int32, (GC, GC), 1)
    hi, lo = (ri, ci) if lower else (ci, ri)
    return [((hi >> (b + 1)) == (lo >> (b + 1))) & ((hi >> b) > (lo >> b)) for b in range(int(math.log2(GC)))]


def _unit_tri_inverse(a, masks):
    ri = lax.broadcasted_iota(jnp.int32, (GC, GC), 0)
    ci = lax.broadcasted_iota(jnp.int32, (GC, GC), 1)
    t = (ri == ci).astype(F32) - jnp.where(masks[0], a, 0.0)
    for mask in masks[1:]:
        tb = t.astype(BF16)
        ta = jnp.dot(tb, jnp.where(mask, a, 0.0).astype(BF16), preferred_element_type=F32)
        t = t - jnp.dot(ta.astype(BF16), tb, preferred_element_type=F32)
    return t


def _gdn_kernel(q_ref, k_ref, v_ref, z_ref, ba_ref, cwq_ref, cwk_ref, cwv_ref, alog_ref, dt_ref, ng_ref,
                sf0_ref, sb0_ref, *rest, length):
    y_ref, sf_ref, sb_ref, qn_ref, kn_ref, vn_ref, beta_ref, g_ref, of_ref, ob_ref, s_ref = rest[-11:]
    n_chunks = length // GC
    row = lax.broadcasted_iota(jnp.int32, (length, GDN_WIDTH), 0)
    first, last = row == 0, row == length - 1

    def conv_silu(u_ref, cw_ref):
        u = u_ref[...]
        prev = jnp.where(first, 0.0, pltpu.roll(u, 1, 0))
        nxt = jnp.where(last, 0.0, pltpu.roll(u, length - 1, 0))
        c = prev * cw_ref[0:1] + u * cw_ref[1:2] + nxt * cw_ref[2:3]
        return c * jax.nn.sigmoid(c)

    q = conv_silu(q_ref, cwq_ref)
    k = conv_silu(k_ref, cwk_ref)
    vn_ref[...] = conv_silu(v_ref, cwv_ref)
    for h in range(GDN_HEADS):
        sl = slice(h * GDN_DK, (h + 1) * GDN_DK)
        qh, kh = q[:, sl], k[:, sl]
        qn_ref[:, sl] = qh * lax.rsqrt(jnp.sum(qh * qh, axis=-1, keepdims=True) + NORM_EPS) * GDN_SCALE
        kn_ref[:, sl] = kh * lax.rsqrt(jnp.sum(kh * kh, axis=-1, keepdims=True) + NORM_EPS)
    ba = ba_ref[...]
    beta_ref[...] = jax.nn.sigmoid(ba)
    g_ref[...] = -jnp.exp(alog_ref[...]) * jax.nn.softplus(ba + dt_ref[...])

    ri = lax.broadcasted_iota(jnp.int32, (GC, GC), 0)
    ci = lax.broadcasted_iota(jnp.int32, (GC, GC), 1)
    lower = (ri >= ci).astype(F32)
    upper = (ri <= ci).astype(F32)

    def chunk_step(n, carry):
        for d in range(2):
            c0 = pl.multiple_of((n if d == 0 else n_chunks - 1 - n) * GC, GC)
            rows = pl.ds(c0, GC)
            incl = (ri >= ci) if d == 0 else (ri <= ci)
            strict = (ri > ci) if d == 0 else (ri < ci)
            merge_masks = _merge_masks(d == 0)
            gcs = jnp.dot(lower if d == 0 else upper, g_ref[rows, :], precision=HIGHEST,
                          preferred_element_type=F32)
            gcs_t = gcs.T
            beta_c = beta_ref[rows, :]
            edge = GC - 1 if d == 0 else 0
            for h in range(GDN_HEADS):
                sl = slice(h * GDN_DK, (h + 1) * GDN_DK)
                ib, ig = 4 * d + h, 8 + 4 * d + h
                s = s_ref[d * GDN_HEADS + h]
                qc, kc, vc = qn_ref[rows, sl], kn_ref[rows, sl], vn_ref[rows, sl]
                g_col = gcs[:, ig:ig + 1]
                g_row = gcs_t[ig:ig + 1, :]
                g_end = gcs_t[ig:ig + 1, edge:edge + 1]
                b_col = beta_c[:, ib:ib + 1]
                decay = jnp.where(incl, jnp.exp(jnp.where(incl, g_col - g_row, 0.0)), 0.0)
                kb = kc * b_col
                kcb = kc.astype(BF16)
                a = jnp.where(strict, _nt_dot(kb.astype(BF16), kcb) * decay, 0.0)
                t = _unit_tri_inverse(a, merge_masks)
                e_col = jnp.exp(g_col)
                rhs = jnp.concatenate([vc * b_col, kb * e_col], axis=1).astype(BF16)
                sol = jnp.dot(t.astype(BF16), rhs, preferred_element_type=F32)
                u, w = sol[:, :GDN_DV], sol[:, GDN_DV:]
                attn = jnp.where(incl, _nt_dot(qc.astype(BF16), kcb) * decay, 0.0)
                sb = s.astype(BF16)
                v_new = u - jnp.dot(w.astype(BF16), sb, preferred_element_type=F32)
                o = (jnp.dot((qc * e_col).astype(BF16), sb, preferred_element_type=F32)
                     + jnp.dot(attn.astype(BF16), v_new.astype(BF16), preferred_element_type=F32))
                k_dec = kc * jnp.exp(g_end - g_col)
                s = s * jnp.exp(g_end) + jnp.dot(k_dec.T.astype(BF16), v_new.astype(BF16),
                                                 preferred_element_type=F32)
                (of_ref if d == 0 else ob_ref)[rows, sl] = o
                s_ref[d * GDN_HEADS + h] = s
        return carry

    s_ref[:GDN_HEADS] = sf0_ref[0]
    s_ref[GDN_HEADS:] = sb0_ref[0]
    lax.fori_loop(0, n_chunks, chunk_step, 0)
    sf_ref[0, 0] = s_ref[:GDN_HEADS]
    sb_ref[0, 0] = s_ref[GDN_HEADS:]
    for h in range(GDN_HEADS):
        sl = slice(h * GDN_DV, (h + 1) * GDN_DV)
        o = of_ref[:, sl] + ob_ref[:, sl]
        o = o * lax.rsqrt(jnp.mean(o * o, axis=-1, keepdims=True) + NORM_EPS) * ng_ref[...]
        zh = z_ref[:, sl]
        y_ref[:, sl] = (o * (zh * jax.nn.sigmoid(zh))).astype(BF16)


def _gdn(proj, y_gdn, states_out, layer, length, s_f0, s_b0, conv_w, a_log, dt_bias, norm_g):
    nseq = (T_P if length == SEQ else T_S) // length
    row0 = 0 if length == SEQ else T_P // length
    col = lambda c: c // GDN_WIDTH
    blk = lambda c: pl.BlockSpec((length, GDN_WIDTH), lambda s: (row0 + s, col(c)))
    cw = lambda part: pl.BlockSpec((1, 3, GDN_WIDTH), lambda s: (layer, 0, part))
    vec = lambda: pl.BlockSpec((1, 1, LANE), lambda s: (layer, 0, 0))
    st_in = pl.BlockSpec((1, GDN_HEADS, GDN_DK, GDN_DV), lambda s: (s, 0, 0, 0))
    st_out = pl.BlockSpec((1, 1, GDN_HEADS, GDN_DK, GDN_DV), lambda s: (s, layer, 0, 0, 0))
    pad_lanes = lambda x, off: jnp.zeros((DEPTH, 1, LANE), F32).at[:, 0, off:off + 2 * GDN_HEADS].set(
        x.reshape(DEPTH, -1))
    in_specs = [blk(C_GDN), blk(C_GDN + GDN_WIDTH), blk(C_GDN + 2 * GDN_WIDTH), blk(C_Z),
                pl.BlockSpec((length, LANE), lambda s: (row0 + s, C_BA // LANE)),
                cw(0), cw(1), cw(2), vec(), vec(), vec(), st_in, st_in]
    args = [proj] * 5 + [conv_w] * 3 + [pad_lanes(a_log, 2 * GDN_HEADS), pad_lanes(dt_bias, 2 * GDN_HEADS),
                                        norm_g.reshape(DEPTH, 1, GDN_DV), s_f0, s_b0]
    st_shape = jax.ShapeDtypeStruct((nseq, DEPTH, GDN_HEADS, GDN_DK, GDN_DV), F32)
    aliases = {}
    for carried, out_idx in ((y_gdn, 0),) + (((states_out[0], 1), (states_out[1], 2)) if states_out else ()):
        if carried is not None:
            in_specs.append(pl.BlockSpec(memory_space=pl.ANY))
            args.append(carried)
            aliases[len(args) - 1] = out_idx

    def body(q_ref, k_ref, v_ref, z_ref, ba_ref, cwq, cwk, cwv, alog, dt, ng, sf0, sb0, *rest):
        _gdn_kernel(q_ref, k_ref, v_ref, z_ref, ba_ref, cwq.at[0], cwk.at[0], cwv.at[0], alog.at[0], dt.at[0],
                    ng.at[0], sf0, sb0, *rest, length=length)

    seq_buf = lambda w: pltpu.VMEM((length, w), F32)
    return pl.pallas_call(
        body,
        out_shape=[jax.ShapeDtypeStruct((T, GDN_WIDTH), BF16), st_shape, st_shape],
        grid=(nseq,), in_specs=in_specs,
        out_specs=[pl.BlockSpec((length, GDN_WIDTH), lambda s: (row0 + s, 0)), st_out, st_out],
        scratch_shapes=[seq_buf(GDN_WIDTH), seq_buf(GDN_WIDTH), seq_buf(GDN_WIDTH), seq_buf(LANE), seq_buf(LANE),
                        seq_buf(GDN_WIDTH), seq_buf(GDN_WIDTH),
                        pltpu.VMEM((2 * GDN_HEADS, GDN_DK, GDN_DV), F32)],
        input_output_aliases=aliases,
        compiler_params=_cparams(("arbitrary",)),
        name="gated_deltanet",
    )(*args)


def kernel(x_prompt, x_sample, cache_k, cache_v, state_fwd, state_bwd, c, c_ctx, ln1_g, ln2_g, w_mod, b_mod,
           w_in, na_rpb, hy_conv_w, hy_conv_b, hy_w1, hy_b1, hy_freq, hy_w2, hy_b2, hy_w3, hy_decay, hy_skip,
           gdn_conv_w, gdn_a_log, gdn_dt_bias, gdn_norm_g, w_pa, w_pb, w_pc, b_gate, w_out, ffn_w_up,
           ffn_conv_w, ffn_conv_b, ffn_w_down, final_g):
    x = jnp.concatenate([x_prompt.reshape(T_P, D), x_sample.reshape(T_S, D)], axis=0)
    cond = jnp.concatenate([c_ctx[None], c, jnp.zeros((N_COND - 1 - DEC_BATCH, D), F32)], axis=0)
    mod = _mod_table(cond, w_mod, b_mod).reshape(DEPTH * N_COND, 1, 6 * D)

    w_pa_b, w_pb_b, w_pc_b, w_out_b = (w.astype(BF16) for w in (w_pa, w_pb, w_pc, w_out))
    w_down_b = ffn_w_down.astype(BF16)
    w_in_t = jnp.swapaxes(w_in, 1, 2)
    tables = {n: tuple(_trig_table(n, True)) + tuple(_trig_table(n, False)) for n in (SEQ, DEC_SEQ)}
    zero_state = jnp.zeros((BATCH, GDN_HEADS, GDN_DK, GDN_DV), F32)

    caches = None
    states = None
    for layer in range(DEPTH):
        h = _norm_mod(x, ln1_g, mod, layer, 0)
        proj = _in_proj(h, w_in_t, layer)

        y_na, new_k, new_v = _ctx_attention(proj, layer, caches)
        caches = (new_k, new_v)
        y_na = _na_attention(proj, cache_k, cache_v, na_rpb[layer], y_na, layer)

        y_hy = None
        for n in (SEQ, DEC_SEQ):
            spectrum = _hy_filter_spectrum(n, tables[n][0], tables[n][1], hy_w1[layer], hy_b1[layer],
                                           hy_freq[layer], hy_w2[layer], hy_b2[layer], hy_w3[layer],
                                           hy_decay[layer])
            y_hy = _hyena(proj, y_hy, layer, n, tables[n], spectrum, hy_conv_w, hy_conv_b, hy_skip)

        y_gdn, s_f, s_b = _gdn(proj, None, states, layer, SEQ, zero_state, zero_state,
                               gdn_conv_w, gdn_a_log, gdn_dt_bias, gdn_norm_g)
        states = (s_f, s_b)
        y_gdn = _gdn(proj, y_gdn, None, layer, DEC_SEQ, state_fwd[:, layer], state_bwd[:, layer],
                     gdn_conv_w, gdn_a_log, gdn_dt_bias, gdn_norm_g)[0]

        x = _mix_out(y_na, y_hy, y_gdn, proj, b_gate, w_pa_b, w_pb_b, w_pc_b, w_out_b, x, mod, layer)
        h = _norm_mod(x, ln2_g, mod, layer, 3)
        act = _ffn_up(h, ffn_w_up, ffn_conv_w, ffn_conv_b, layer)
        x = _ffn_down(act, w_down_b, x, mod, layer)

    y_p, y_s = _final_norm(x, final_g)
    cache_shape = (BATCH, DEPTH, SEQ, NA_HEADS, NA_DH)
    return (y_p.reshape(BATCH, SEQ, D), y_s.reshape(DEC_BATCH, DEC_SEQ, D),
            caches[0].reshape(cache_shape), caches[1].reshape(cache_shape), states[0], states[1])
```

```python
import functools
import math

import jax
import jax.numpy as jnp
import numpy as np
from jax import lax
from jax.experimental import pallas as pl
from jax.experimental.pallas import tpu as pltpu

F32 = jnp.float32
BF16 = jnp.bfloat16

D = 2048
BATCH, SEQ = 32, 256
DEC_BATCH, DEC_SEQ = 4, 1024
DEPTH = 2
GRID_W = 64
NA_HEADS, NA_DH = 8, 128
NA_WIDTH = NA_HEADS * NA_DH
NA_WIN_ROWS, NA_WIN_COLS = 8, 16
HY_WIDTH = 512
HY_ORDER = 2
HY_BANDS = 16
GDN_HEADS, GDN_DK, GDN_DV = 4, 128, 128
GDN_WIDTH = GDN_HEADS * GDN_DV
D_FF = 5632
N_BRANCH = 3
NORM_EPS = 1e-6
NEG_INF = -1e30

T_P = BATCH * SEQ
T_S = DEC_BATCH * DEC_SEQ
T = T_P + T_S
N_COND = 8

C_Q, C_K, C_V = 0, NA_WIDTH, 2 * NA_WIDTH
C_HY = 3 * NA_WIDTH
C_GDN = C_HY + 3 * HY_WIDTH
C_Z = C_GDN + 3 * GDN_HEADS * GDN_DK
C_BA = C_Z + GDN_WIDTH
C_GATE = C_BA + 4 * GDN_HEADS
N_IN = C_GATE + N_BRANCH * D
GATE_BLOCK = C_BA
GATE_SHIFT = C_GATE - C_BA

LANE = 128
VMEM_LIMIT = 56 * 1024 * 1024
HIGHEST = lax.Precision.HIGHEST


def _cparams(sem):
    return pltpu.CompilerParams(dimension_semantics=sem, vmem_limit_bytes=VMEM_LIMIT)


def _cond_row(tile, rows_per_tile):
    first_latent = T_P // rows_per_tile
    per_seq = DEC_SEQ // rows_per_tile
    return jnp.where(tile < first_latent, 0, 1 + (tile - first_latent) // per_seq)


def _mod_kernel(c_ref, w_ref, b_ref, o_ref):
    c = c_ref[...]
    s = (c * jax.nn.sigmoid(c)).astype(BF16)
    o_ref[0] = jnp.dot(s, w_ref[0].astype(BF16), preferred_element_type=F32) + b_ref[0]


def _mod_table(cond, w_mod, b_mod):
    tn = 1024
    n = 6 * D
    return pl.pallas_call(
        _mod_kernel,
        out_shape=jax.ShapeDtypeStruct((DEPTH, N_COND, n), F32),
        grid=(DEPTH, n // tn),
        in_specs=[
            pl.BlockSpec((N_COND, D), lambda l, j: (0, 0)),
            pl.BlockSpec((1, D, tn), lambda l, j: (l, 0, j)),
            pl.BlockSpec((1, 1, tn), lambda l, j: (l, 0, j)),
        ],
        out_specs=pl.BlockSpec((1, N_COND, tn), lambda l, j: (l, 0, j)),
        compiler_params=_cparams(("arbitrary", "arbitrary")),
        name="mod_table",
    )(cond, w_mod, b_mod.reshape(DEPTH, 1, n))


def _norm_mod_kernel(x_ref, g_ref, sh_ref, sc_ref, o_ref):
    x = x_ref[...]
    y = x * lax.rsqrt(jnp.mean(x * x, axis=-1, keepdims=True) + NORM_EPS) * g_ref[0]
    o_ref[...] = (y * (1.0 + sc_ref[0]) + sh_ref[0]).astype(o_ref.dtype)


def _norm_mod(x, g, mod, layer, shift_chunk):
    tm = 256
    return pl.pallas_call(
        _norm_mod_kernel,
        out_shape=jax.ShapeDtypeStruct((T, D), BF16),
        grid=(T // tm,),
        in_specs=[
            pl.BlockSpec((tm, D), lambda i: (i, 0)),
            pl.BlockSpec((1, 1, D), lambda i: (layer, 0, 0)),
            pl.BlockSpec((1, 1, D), lambda i: (layer * N_COND + _cond_row(i, tm), 0, shift_chunk)),
            pl.BlockSpec((1, 1, D), lambda i: (layer * N_COND + _cond_row(i, tm), 0, shift_chunk + 1)),
        ],
        out_specs=pl.BlockSpec((tm, D), lambda i: (i, 0)),
        compiler_params=_cparams(("arbitrary",)),
        name="norm_mod",
    )(x, g.reshape(DEPTH, 1, D), mod, mod)


FINAL_TM = 256


def _final_norm_kernel(x_ref, g_ref, yp_ref, ys_ref):
    x = x_ref[...]
    y = x * lax.rsqrt(jnp.mean(x * x, axis=-1, keepdims=True) + NORM_EPS) * g_ref[...]
    is_context = pl.program_id(0) < T_P // FINAL_TM

    @pl.when(is_context)
    def _():
        yp_ref[...] = y

    @pl.when(jnp.logical_not(is_context))
    def _():
        ys_ref[...] = y


def _final_norm(x, g):
    tm = FINAL_TM
    n_ctx = T_P // tm
    return pl.pallas_call(
        _final_norm_kernel,
        out_shape=[jax.ShapeDtypeStruct((T_P, D), F32), jax.ShapeDtypeStruct((T_S, D), F32)],
        grid=(T // tm,),
        in_specs=[pl.BlockSpec((tm, D), lambda i: (i, 0)), pl.BlockSpec((1, D), lambda i: (0, 0))],
        out_specs=[pl.BlockSpec((tm, D), lambda i: (jnp.minimum(i, n_ctx - 1), 0)),
                   pl.BlockSpec((tm, D), lambda i: (jnp.maximum(i - n_ctx, 0), 0))],
        compiler_params=_cparams(("arbitrary",)),
        name="final_norm",
    )(x, g.reshape(1, D))


def _proj_kernel(h_ref, wt_ref, o_ref, wb_ref):
    @pl.when(pl.program_id(1) == 0)
    def _():
        wb_ref[...] = wt_ref[0].astype(BF16)

    o_ref[...] = lax.dot_general(h_ref[...], wb_ref[...], (((1,), (1,)), ((), ())), preferred_element_type=F32)


def _in_proj(h, w_in_t, layer):
    tm, tn = 1024, 1024
    return pl.pallas_call(
        _proj_kernel,
        out_shape=jax.ShapeDtypeStruct((T, N_IN), F32),
        grid=(pl.cdiv(N_IN, tn), T // tm),
        in_specs=[
            pl.BlockSpec((tm, D), lambda j, i: (i, 0)),
            pl.BlockSpec((1, tn, D), lambda j, i: (layer, j, 0)),
        ],
        out_specs=pl.BlockSpec((tm, tn), lambda j, i: (i, j)),
        scratch_shapes=[pltpu.VMEM((tn, D), BF16)],
        compiler_params=_cparams(("arbitrary", "arbitrary")),
        name="in_proj",
    )(h, w_in_t)


def _mix_out_kernel(yna_ref, yhy_ref, ygdn_ref, gl_ref, bg_ref, wpa_ref, wpb_ref, wpc_ref, wout_ref,
                    x_ref, gt_ref, o_ref):
    width = gl_ref.shape[1]
    gl = pltpu.roll(gl_ref[...], width - GATE_SHIFT, 1)[:, :N_BRANCH * D] + bg_ref[0]
    gates = jax.nn.sigmoid(gl)
    merged = (gates[:, :D] * jnp.dot(yna_ref[...], wpa_ref[0], preferred_element_type=F32)
              + gates[:, D:2 * D] * jnp.dot(yhy_ref[...], wpb_ref[0], preferred_element_type=F32)
              + gates[:, 2 * D:] * jnp.dot(ygdn_ref[...], wpc_ref[0], preferred_element_type=F32))
    r = jnp.dot(merged.astype(BF16), wout_ref[0], preferred_element_type=F32)
    o_ref[...] = x_ref[...] + gt_ref[0] * r


def _mix_out(y_na, y_hy, y_gdn, proj, b_gate, w_pa, w_pb, w_pc, w_out, x, mod, layer):
    tm = 256
    resident = functools.partial(pl.BlockSpec, pipeline_mode=pl.Buffered(1))
    return pl.pallas_call(
        _mix_out_kernel,
        out_shape=jax.ShapeDtypeStruct((T, D), F32),
        grid=(T // tm,),
        in_specs=[
            pl.BlockSpec((tm, NA_WIDTH), lambda i: (i, 0)),
            pl.BlockSpec((tm, HY_WIDTH), lambda i: (i, 0)),
            pl.BlockSpec((tm, GDN_WIDTH), lambda i: (i, 0)),
            pl.BlockSpec((tm, GATE_BLOCK), lambda i: (i, 1)),
            pl.BlockSpec((1, 1, N_BRANCH * D), lambda i: (layer, 0, 0)),
            resident((1, NA_WIDTH, D), lambda i: (layer, 0, 0)),
            resident((1, HY_WIDTH, D), lambda i: (layer, 0, 0)),
            resident((1, GDN_WIDTH, D), lambda i: (layer, 0, 0)),
            resident((1, D, D), lambda i: (layer, 0, 0)),
            pl.BlockSpec((tm, D), lambda i: (i, 0)),
            pl.BlockSpec((1, 1, D), lambda i: (layer * N_COND + _cond_row(i, tm), 0, 2)),
        ],
        out_specs=pl.BlockSpec((tm, D), lambda i: (i, 0)),
        compiler_params=_cparams(("arbitrary",)),
        name="mix_out",
    )(y_na, y_hy, y_gdn, proj, b_gate.reshape(DEPTH, 1, N_BRANCH * D), w_pa, w_pb, w_pc, w_out, x, mod)


FFN_TM = 1024


def _ffn_up_kernel(h_ref, wa_ref, wb_ref, cwa_ref, cwb_ref, cba_ref, cbb_ref, o_ref, wab_ref, wbb_ref):
    m = pl.program_id(1)

    @pl.when(m == 0)
    def _():
        wab_ref[...] = wa_ref[0].astype(BF16)
        wbb_ref[...] = wb_ref[0].astype(BF16)

    seq_len = jnp.where(m < T_P // FFN_TM, SEQ, DEC_SEQ)
    pos = lax.broadcasted_iota(jnp.int32, (FFN_TM, 1), 0) & (seq_len - 1)
    has_prev = pos != 0
    has_next = pos != seq_len - 1
    h = h_ref[...]

    def conv(w_ref, cw_ref, cb_ref):
        up = jnp.dot(h, w_ref[...], preferred_element_type=F32)
        prev = jnp.where(has_prev, pltpu.roll(up, 1, 0), 0.0)
        nxt = jnp.where(has_next, pltpu.roll(up, FFN_TM - 1, 0), 0.0)
        cw = cw_ref[0]
        return prev * cw[0:1] + up * cw[1:2] + nxt * cw[2:3] + cb_ref[0]

    ua = conv(wab_ref, cwa_ref, cba_ref)
    ub = conv(wbb_ref, cwb_ref, cbb_ref)
    o_ref[...] = (ua * jax.nn.sigmoid(ua) * ub).astype(BF16)


def _ffn_up(h, w_up, conv_w, conv_b, layer):
    tn = 512
    nt = D_FF // tn
    conv_b = conv_b.reshape(DEPTH, 1, 2 * D_FF)
    return pl.pallas_call(
        _ffn_up_kernel,
        out_shape=jax.ShapeDtypeStruct((T, D_FF), BF16),
        grid=(nt, T // FFN_TM),
        in_specs=[
            pl.BlockSpec((FFN_TM, D), lambda j, i: (i, 0)),
            pl.BlockSpec((1, D, tn), lambda j, i: (layer, 0, j)),
            pl.BlockSpec((1, D, tn), lambda j, i: (layer, 0, nt + j)),
            pl.BlockSpec((1, 3, tn), lambda j, i: (layer, 0, j)),
            pl.BlockSpec((1, 3, tn), lambda j, i: (layer, 0, nt + j)),
            pl.BlockSpec((1, 1, tn), lambda j, i: (layer, 0, j)),
            pl.BlockSpec((1, 1, tn), lambda j, i: (layer, 0, nt + j)),
        ],
        out_specs=pl.BlockSpec((FFN_TM, tn), lambda j, i: (i, j)),
        scratch_shapes=[pltpu.VMEM((D, tn), BF16), pltpu.VMEM((D, tn), BF16)],
        compiler_params=_cparams(("arbitrary", "arbitrary")),
        name="ffn_up",
    )(h, w_up, w_up, conv_w, conv_w, conv_b, conv_b)


def _ffn_down_kernel(a_ref, w_ref, x_ref, gt_ref, o_ref):
    o_ref[...] = x_ref[...] + gt_ref[0] * jnp.dot(a_ref[...], w_ref[0], preferred_element_type=F32)


def _ffn_down(act, w_down, x, mod, layer):
    tm, tn = 256, 512
    return pl.pallas_call(
        _ffn_down_kernel,
        out_shape=jax.ShapeDtypeStruct((T, D), F32),
        grid=(D // tn, T // tm),
        in_specs=[
            pl.BlockSpec((tm, D_FF), lambda j, i: (i, 0)),
            pl.BlockSpec((1, D_FF, tn), lambda j, i: (layer, 0, j)),
            pl.BlockSpec((tm, tn), lambda j, i: (i, j)),
            pl.BlockSpec((1, 1, tn), lambda j, i: (layer * N_COND + _cond_row(i, tm), 0, 5 * (D // tn) + j)),
        ],
        out_specs=pl.BlockSpec((tm, tn), lambda j, i: (i, j)),
        compiler_params=_cparams(("arbitrary", "arbitrary")),
        name="ffn_down",
    )(act, w_down, x, mod)


ATT_SCALE = NA_DH ** -0.5


def _nt_dot(a, b):
    return lax.dot_general(a, b, (((1,), (1,)), ((), ())), preferred_element_type=F32)


def _ctx_attn_kernel(q_ref, k_ref, v_ref, o_ref, ko_ref, vo_ref):
    for h in range(NA_HEADS):
        sl = slice(h * NA_DH, (h + 1) * NA_DH)
        q = (q_ref[:, sl] * ATT_SCALE).astype(BF16)
        s = _nt_dot(q, k_ref[:, sl].astype(BF16))
        e = jnp.exp(s - jnp.max(s, axis=-1, keepdims=True))
        o = jnp.dot(e.astype(BF16), v_ref[:, sl].astype(BF16), preferred_element_type=F32)
        o_ref[:, sl] = (o / jnp.sum(e, axis=-1, keepdims=True)).astype(BF16)
    ko_ref[0, 0] = k_ref[...]
    vo_ref[0, 0] = v_ref[...]


def _ctx_attention(proj, layer, caches):
    cache_shape = jax.ShapeDtypeStruct((BATCH, DEPTH, SEQ, NA_WIDTH), F32)
    cache_spec = pl.BlockSpec((1, 1, SEQ, NA_WIDTH), lambda b: (b, layer, 0, 0))
    qkv_specs = [pl.BlockSpec((SEQ, NA_WIDTH), lambda b, j=j: (b, j)) for j in range(3)]
    out_shape = [jax.ShapeDtypeStruct((T, NA_WIDTH), BF16), cache_shape, cache_shape]
    out_specs = [pl.BlockSpec((SEQ, NA_WIDTH), lambda b: (b, 0)), cache_spec, cache_spec]
    if caches is None:
        return pl.pallas_call(
            _ctx_attn_kernel, out_shape=out_shape, grid=(BATCH,), in_specs=qkv_specs, out_specs=out_specs,
            compiler_params=_cparams(("arbitrary",)), name="ctx_attention",
        )(proj, proj, proj)

    def body(q_ref, k_ref, v_ref, kc_ref, vc_ref, o_ref, ko_ref, vo_ref):
        del kc_ref, vc_ref
        _ctx_attn_kernel(q_ref, k_ref, v_ref, o_ref, ko_ref, vo_ref)

    any_spec = pl.BlockSpec(memory_space=pl.ANY)
    return pl.pallas_call(
        body, out_shape=out_shape, grid=(BATCH,), in_specs=qkv_specs + [any_spec, any_spec],
        out_specs=out_specs, input_output_aliases={3: 1, 4: 2},
        compiler_params=_cparams(("arbitrary",)), name="ctx_attention",
    )(proj, proj, proj, *caches)


NA_ROWS = DEC_SEQ // GRID_W
NA_WIN_TOK = NA_WIN_ROWS * GRID_W


def _na_window_start(r):
    return jnp.clip(r - NA_WIN_ROWS // 2, 0, NA_ROWS - NA_WIN_ROWS)


def _na_bias_table(rpb):
    col = np.arange(GRID_W)
    col_start = np.clip(col - NA_WIN_COLS // 2, 0, GRID_W - NA_WIN_COLS)
    col_mask = (col[None, :] >= col_start[:, None]) & (col[None, :] < col_start[:, None] + NA_WIN_COLS)
    rel_col = np.clip(col[None, :] - col[:, None] + NA_WIN_COLS - 1, 0, 2 * NA_WIN_COLS - 2)
    banded = jnp.where(col_mask, rpb[:, :, rel_col], NEG_INF)
    per_offset = [banded[:, NA_WIN_ROWS - 1 - d:2 * NA_WIN_ROWS - 1 - d] for d in range(NA_WIN_ROWS)]
    tab = jnp.stack(per_offset, axis=0)
    return tab.transpose(0, 1, 3, 2, 4).reshape(NA_WIN_ROWS, NA_HEADS, GRID_W, NA_WIN_TOK)


def _na_attn_kernel(q_ref, k_ref, v_ref, ck_ref, cv_ref, bias_ref, y_in_ref, o_ref):
    del y_in_ref
    r = pl.program_id(1)
    start = pl.multiple_of(_na_window_start(r) * GRID_W, GRID_W)
    for h in range(NA_HEADS):
        sl = slice(h * NA_DH, (h + 1) * NA_DH)
        q = (q_ref[:, sl] * ATT_SCALE).astype(BF16)
        s_win = _nt_dot(q, k_ref[pl.ds(start, NA_WIN_TOK), sl].astype(BF16)) + bias_ref[0, h]
        s_ctx = _nt_dot(q, ck_ref[0, 0, :, sl].astype(BF16))
        m = jnp.maximum(jnp.max(s_win, axis=-1, keepdims=True), jnp.max(s_ctx, axis=-1, keepdims=True))
        e_win = jnp.exp(s_win - m)
        e_ctx = jnp.exp(s_ctx - m)
        o = (jnp.dot(e_win.astype(BF16), v_ref[pl.ds(start, NA_WIN_TOK), sl].astype(BF16),
                     preferred_element_type=F32)
             + jnp.dot(e_ctx.astype(BF16), cv_ref[0, 0, :, sl].astype(BF16), preferred_element_type=F32))
        denom = jnp.sum(e_win, axis=-1, keepdims=True) + jnp.sum(e_ctx, axis=-1, keepdims=True)
        o_ref[:, sl] = (o / denom).astype(BF16)


def _na_attention(proj, cache_k, cache_v, rpb, y_na, layer):
    q_row0 = T_P // GRID_W
    seq0 = T_P // DEC_SEQ
    kv_spec = lambda j: pl.BlockSpec((DEC_SEQ, NA_WIDTH), lambda b, r: (seq0 + b, j))
    ctx_spec = pl.BlockSpec((1, 1, SEQ, NA_WIDTH), lambda b, r: (b, layer, 0, 0))
    return pl.pallas_call(
        _na_attn_kernel,
        out_shape=jax.ShapeDtypeStruct((T, NA_WIDTH), BF16),
        grid=(DEC_BATCH, NA_ROWS),
        in_specs=[
            pl.BlockSpec((GRID_W, NA_WIDTH), lambda b, r: (q_row0 + b * NA_ROWS + r, 0)),
            kv_spec(1), kv_spec(2), ctx_spec, ctx_spec,
            pl.BlockSpec((1, NA_HEADS, GRID_W, NA_WIN_TOK), lambda b, r: (r - _na_window_start(r), 0, 0, 0)),
            pl.BlockSpec(memory_space=pl.ANY),
        ],
        out_specs=pl.BlockSpec((GRID_W, NA_WIDTH), lambda b, r: (q_row0 + b * NA_ROWS + r, 0)),
        input_output_aliases={6: 0},
        compiler_params=_cparams(("arbitrary", "arbitrary")),
        name="na_attention",
    )(proj, proj, proj, cache_k.reshape(DEC_BATCH, DEPTH, SEQ, NA_WIDTH),
      cache_v.reshape(DEC_BATCH, DEPTH, SEQ, NA_WIDTH), _na_bias_table(rpb), y_na)


HY_CB = 256


def _split_bf16(x):
    hi = x.astype(BF16)
    return hi, (x - hi.astype(F32)).astype(BF16)


def _dot3(a_hi, a_lo, b):
    b_hi, b_lo = _split_bf16(b)
    return (jnp.dot(a_hi, b_hi, preferred_element_type=F32) + jnp.dot(a_lo, b_hi, preferred_element_type=F32)
            + jnp.dot(a_hi, b_lo, preferred_element_type=F32))


def _trig_table_kernel(hi_ref, lo_ref, *, length, freq_on_rows, scale):
    rows, cols = hi_ref.shape
    r = lax.broadcasted_iota(jnp.int32, (rows, cols), 0) + pl.program_id(0) * rows
    c = lax.broadcasted_iota(jnp.int32, (rows, cols), 1)
    f, t = (r, c) if freq_on_rows else (c, r)
    k = f & (length - 1)
    quarter = jnp.where(f >= length, length, 0)
    phase = (2 * k + 1) * t + (-quarter if freq_on_rows else quarter)
    phase = phase & (4 * length - 1)
    phase = jnp.where(phase >= 2 * length, phase - 4 * length, phase)
    val = jnp.cos(phase.astype(F32) * (math.pi / (2 * length))) * scale
    hi, lo = _split_bf16(val)
    hi_ref[...] = hi
    lo_ref[...] = lo


def _trig_table(length, freq_on_rows):
    shape = (2 * length, length) if freq_on_rows else (length, 2 * length)
    tr = 256
    spec = pl.BlockSpec((tr, shape[1]), lambda i: (i, 0))
    return pl.pallas_call(
        functools.partial(_trig_table_kernel, length=length, freq_on_rows=freq_on_rows,
                          scale=1.0 if freq_on_rows else 1.0 / length),
        out_shape=[jax.ShapeDtypeStruct(shape, BF16)] * 2,
        grid=(shape[0] // tr,), out_specs=[spec, spec],
        compiler_params=_cparams(("arbitrary",)), name="dft_table",
    )()


def _hy_filter_kernel(tn_ref, band_ref, w1_ref, b1_ref, fr_ref, w2_ref, b2_ref, w3_ref, dec_ref,
                      fhi_ref, flo_ref, o_ref, *, length):
    t_norm = tn_ref[...]
    t_idx = lax.broadcasted_iota(jnp.int32, (length, LANE), 0).astype(F32)
    lane = lax.broadcasted_iota(jnp.int32, (length, LANE), 1)
    ang = (2.0 * math.pi / length) * t_idx * band_ref[...]
    z = jnp.where(lane == 0, t_norm,
                  jnp.where(lane <= HY_BANDS, jnp.cos(ang), jnp.where(lane <= 2 * HY_BANDS, jnp.sin(ang), 0.0)))
    hdn = jnp.sin(fr_ref[0:1] * (jnp.dot(z, w1_ref[...], precision=HIGHEST, preferred_element_type=F32)
                                 + b1_ref[...]))
    hdn = jnp.sin(fr_ref[1:2] * (jnp.dot(hdn, w2_ref[...], precision=HIGHEST, preferred_element_type=F32)
                                 + b2_ref[...]))
    filt = jnp.dot(hdn, w3_ref[...], precision=HIGHEST, preferred_element_type=F32)
    filt = filt * jnp.exp(-t_norm * jnp.abs(dec_ref[...]))
    first = lax.broadcasted_iota(jnp.int32, (length, HY_WIDTH), 0) == 0
    for o in range(HY_ORDER):
        fwd = filt[:, (2 * o) * HY_WIDTH:(2 * o + 1) * HY_WIDTH]
        bwd = filt[:, (2 * o + 1) * HY_WIDTH:(2 * o + 2) * HY_WIDTH]
        bwd = jnp.where(first, 0.0, pltpu.roll(bwd, 1, 0))
        o_ref[o, :length] = _dot3(fhi_ref[:length], flo_ref[:length], fwd + bwd)
        o_ref[o, length:] = _dot3(fhi_ref[length:], flo_ref[length:], bwd - fwd)


def _hy_filter_spectrum(length, f_hi, f_lo, w1, b1, freq, w2, b2, w3, decay):
    emb = 1 + 2 * HY_BANDS
    t_norm = jnp.linspace(0.0, 1.0, length, dtype=F32).reshape(length, 1)
    bands = np.zeros((1, LANE), np.float32)
    bands[0, 1:1 + HY_BANDS] = bands[0, 1 + HY_BANDS:emb] = np.linspace(1e-4, HY_BANDS - 1, HY_BANDS,
                                                                        dtype=np.float32)
    w1p = jnp.zeros((LANE, w1.shape[1]), F32).at[:emb].set(w1)
    return pl.pallas_call(
        functools.partial(_hy_filter_kernel, length=length),
        out_shape=jax.ShapeDtypeStruct((HY_ORDER, 2 * length, HY_WIDTH), F32),
        compiler_params=pltpu.CompilerParams(vmem_limit_bytes=VMEM_LIMIT),
        name="hyena_filter",
    )(t_norm, jnp.asarray(bands), w1p, b1.reshape(1, -1), freq, w2, b2.reshape(1, -1), w3,
      decay.reshape(1, -1), f_hi, f_lo)


def _hyena_kernel(v_ref, x1_ref, x2_ref, cwv_ref, cw1_ref, cw2_ref, cbv_ref, cb1_ref, cb2_ref, skip_ref,
                  hs_ref, fhi_ref, flo_ref, ghi_ref, glo_ref, *rest, length):
    o_ref = rest[-1]
    row = lax.broadcasted_iota(jnp.int32, (length, HY_CB), 0)
    first, last = row == 0, row == length - 1

    def short_conv(u_ref, cw_ref, cb_ref):
        u = u_ref[...]
        prev = jnp.where(first, 0.0, pltpu.roll(u, 1, 0))
        nxt = jnp.where(last, 0.0, pltpu.roll(u, length - 1, 0))
        return prev * cw_ref[0:1] + u * cw_ref[1:2] + nxt * cw_ref[2:3] + cb_ref[...]

    z = short_conv(v_ref, cwv_ref, cbv_ref)
    gates = (short_conv(x1_ref, cw1_ref, cb1_ref), short_conv(x2_ref, cw2_ref, cb2_ref))
    for o in range(HY_ORDER):
        zf = _dot3(fhi_ref[...], flo_ref[...], z)
        zc, zs = zf[:length], zf[length:]
        h_re, h_im = hs_ref[o, :length], hs_ref[o, length:]
        p = jnp.concatenate([zc * h_re + zs * h_im, zc * h_im - zs * h_re], axis=0)
        conv = _dot3(ghi_ref[...], glo_ref[...], p)
        z = gates[o] * (conv + z * skip_ref[o:o + 1])
    o_ref[...] = z.astype(BF16)


def _hyena(proj, y_hy, layer, length, tables, spectrum, conv_w, conv_b, skip):
    nseq = (T_P if length == SEQ else T_S) // length
    row0 = 0 if length == SEQ else T_P // length
    ncb = HY_WIDTH // HY_CB
    col0 = C_HY // HY_CB
    u_spec = lambda part: pl.BlockSpec((length, HY_CB), lambda s, j: (row0 + s, col0 + part * ncb + j))
    cw_spec = lambda part: pl.BlockSpec((1, 3, HY_CB), lambda s, j: (layer, 0, part * ncb + j))
    cb_spec = lambda part: pl.BlockSpec((1, 1, HY_CB), lambda s, j: (layer, 0, part * ncb + j))
    resident = functools.partial(pl.BlockSpec, pipeline_mode=pl.Buffered(1))
    tab_specs = [resident(t.shape, lambda s, j: (0, 0)) for t in tables]
    in_specs = ([u_spec(0), u_spec(1), u_spec(2), cw_spec(0), cw_spec(1), cw_spec(2),
                 cb_spec(0), cb_spec(1), cb_spec(2),
                 pl.BlockSpec((1, HY_ORDER, HY_CB), lambda s, j: (layer, 0, j)),
                 pl.BlockSpec((HY_ORDER, 2 * length, HY_CB), lambda s, j: (0, 0, j))] + tab_specs)
    args = [proj] * 3 + [conv_w] * 3 + [conv_b.reshape(DEPTH, 1, -1)] * 3 + [skip, spectrum] + list(tables)
    aliases = {}
    if y_hy is not None:
        in_specs.append(pl.BlockSpec(memory_space=pl.ANY))
        args.append(y_hy)
        aliases = {len(args) - 1: 0}

    def body(v_ref, x1_ref, x2_ref, cwv, cw1, cw2, cbv, cb1, cb2, skip_ref, hs_ref, *rest):
        _hyena_kernel(v_ref, x1_ref, x2_ref, cwv.at[0], cw1.at[0], cw2.at[0], cbv.at[0], cb1.at[0], cb2.at[0],
                      skip_ref.at[0], hs_ref, *rest, length=length)

    return pl.pallas_call(
        body,
        out_shape=jax.ShapeDtypeStruct((T, HY_WIDTH), BF16),
        grid=(nseq, ncb), in_specs=in_specs,
        out_specs=pl.BlockSpec((length, HY_CB), lambda s, j: (row0 + s, j)),
        input_output_aliases=aliases,
        compiler_params=_cparams(("arbitrary", "arbitrary")),
        name="hyena",
    )(*args)


GC = 128
GDN_SCALE = GDN_DK ** -0.5


def _merge_masks(lower):
    ri = lax.broadcasted_iota(jnp.int32, (GC, GC), 0)
    ci = lax.broadcasted_iota(jnp.int32, (GC, GC), 1)
    hi, lo = (ri, ci) if lower else (ci, ri)
    return [((hi >> (b + 1)) == (lo >> (b + 1))) & ((hi >> b) > (lo >> b)) for b in range(int(math.log2(GC)))]


def _unit_tri_inverse(a, masks):
    ri = lax.broadcasted_iota(jnp.int32, (GC, GC), 0)
    ci = lax.broadcasted_iota(jnp.int32, (GC, GC), 1)
    eye = (ri == ci).astype(F32)
    n = range(len(a))
    t = [eye - jnp.where(masks[i][0], a[i], 0.0) for i in n]
    for level in range(1, len(masks[0])):
        tb = [t[i].astype(BF16) for i in n]
        ta = [jnp.dot(tb[i], jnp.where(masks[i][level], a[i], 0.0).astype(BF16), preferred_element_type=F32)
              for i in n]
        tat = [jnp.dot(ta[i].astype(BF16), tb[i], preferred_element_type=F32) for i in n]
        t = [t[i] - tat[i] for i in n]
    return t


def _gdn_kernel(q_ref, k_ref, v_ref, z_ref, ba_ref, cwq_ref, cwk_ref, cwv_ref, alog_ref, dt_ref, ng_ref,
                sf0_ref, sb0_ref, *rest, length):
    y_ref, sf_ref, sb_ref, qn_ref, kn_ref, vn_ref, beta_ref, g_ref, of_ref, ob_ref, s_ref = rest[-11:]
    n_chunks = length // GC
    row = lax.broadcasted_iota(jnp.int32, (length, GDN_WIDTH), 0)
    first, last = row == 0, row == length - 1

    def conv_silu(u_ref, cw_ref):
        u = u_ref[...]
        prev = jnp.where(first, 0.0, pltpu.roll(u, 1, 0))
        nxt = jnp.where(last, 0.0, pltpu.roll(u, length - 1, 0))
        c = prev * cw_ref[0:1] + u * cw_ref[1:2] + nxt * cw_ref[2:3]
        return c * jax.nn.sigmoid(c)

    q = conv_silu(q_ref, cwq_ref)
    k = conv_silu(k_ref, cwk_ref)
    vn_ref[...] = conv_silu(v_ref, cwv_ref)
    for h in range(GDN_HEADS):
        sl = slice(h * GDN_DK, (h + 1) * GDN_DK)
        qh, kh = q[:, sl], k[:, sl]
        qn_ref[:, sl] = qh * lax.rsqrt(jnp.sum(qh * qh, axis=-1, keepdims=True) + NORM_EPS) * GDN_SCALE
        kn_ref[:, sl] = kh * lax.rsqrt(jnp.sum(kh * kh, axis=-1, keepdims=True) + NORM_EPS)
    ba = ba_ref[...]
    beta_ref[...] = jax.nn.sigmoid(ba)
    g_ref[...] = -jnp.exp(alog_ref[...]) * jax.nn.softplus(ba + dt_ref[...])

    ri = lax.broadcasted_iota(jnp.int32, (GC, GC), 0)
    ci = lax.broadcasted_iota(jnp.int32, (GC, GC), 1)
    lower = (ri >= ci).astype(F32)
    upper = (ri <= ci).astype(F32)

    def chunk_step(n, carry):
        rows = [pl.ds(pl.multiple_of((n if d == 0 else n_chunks - 1 - n) * GC, GC), GC) for d in range(2)]
        loaded = {}
        for d in range(2):
            loaded[d] = (g_ref[rows[d], :], beta_ref[rows[d], :])
            for h in range(GDN_HEADS):
                sl = slice(h * GDN_DK, (h + 1) * GDN_DK)
                loaded[d, h] = (qn_ref[rows[d], sl], kn_ref[rows[d], sl], vn_ref[rows[d], sl],
                                s_ref[d * GDN_HEADS + h])
        units = [(d, h) for d in range(2) for h in range(GDN_HEADS)]
        incl = [(ri >= ci), (ri <= ci)]
        strict = [(ri > ci), (ri < ci)]
        masks = [_merge_masks(True), _merge_masks(False)]
        gcs = [jnp.dot(tri, loaded[d][0], precision=HIGHEST, preferred_element_type=F32)
               for d, tri in enumerate((lower, upper))]
        gcs_t = [g.T for g in gcs]
        edge = [GC - 1, 0]
        g_col = [gcs[d][:, 8 + 4 * d + h:9 + 4 * d + h] for d, h in units]
        g_row = [gcs_t[d][8 + 4 * d + h:9 + 4 * d + h, :] for d, h in units]
        g_end = [gcs_t[d][8 + 4 * d + h:9 + 4 * d + h, edge[d]:edge[d] + 1] for d, h in units]
        b_col = [loaded[d][1][:, 4 * d + h:4 * d + h + 1] for d, h in units]
        qc, kc, vc, st = (list(x) for x in zip(*(loaded[u] for u in units)))
        nu = range(len(units))
        decay = [jnp.where(incl[d], jnp.exp(jnp.where(incl[d], g_col[i] - g_row[i], 0.0)), 0.0)
                 for i, (d, h) in enumerate(units)]
        kb = [kc[i] * b_col[i] for i in nu]
        kcb = [kc[i].astype(BF16) for i in nu]
        kk = [_nt_dot(kb[i].astype(BF16), kcb[i]) for i in nu]
        qk = [_nt_dot(qc[i].astype(BF16), kcb[i]) for i in nu]
        a = [jnp.where(strict[d], kk[i] * decay[i], 0.0) for i, (d, h) in enumerate(units)]
        t = _unit_tri_inverse(a, [masks[d] for d, h in units])
        e_col = [jnp.exp(g_col[i]) for i in nu]
        rhs = [jnp.concatenate([vc[i] * b_col[i], kb[i] * e_col[i]], axis=1).astype(BF16) for i in nu]
        sol = [jnp.dot(t[i].astype(BF16), rhs[i], preferred_element_type=F32) for i in nu]
        attn = [jnp.where(incl[d], qk[i] * decay[i], 0.0).astype(BF16) for i, (d, h) in enumerate(units)]
        sb = [st[i].astype(BF16) for i in nu]
        ws = [jnp.dot(sol[i][:, GDN_DV:].astype(BF16), sb[i], preferred_element_type=F32) for i in nu]
        qs = [jnp.dot((qc[i] * e_col[i]).astype(BF16), sb[i], preferred_element_type=F32) for i in nu]
        v_new = [(sol[i][:, :GDN_DV] - ws[i]).astype(BF16) for i in nu]
        av = [jnp.dot(attn[i], v_new[i], preferred_element_type=F32) for i in nu]
        k_dec_t = [(kc[i] * jnp.exp(g_end[i] - g_col[i])).T.astype(BF16) for i in nu]
        kv = [jnp.dot(k_dec_t[i], v_new[i], preferred_element_type=F32) for i in nu]
        results = [(qs[i] + av[i], st[i] * jnp.exp(g_end[i]) + kv[i]) for i in nu]
        for d in range(2):
            for h in range(GDN_HEADS):
                o, s = results[d * GDN_HEADS + h]
                (of_ref if d == 0 else ob_ref)[rows[d], h * GDN_DV:(h + 1) * GDN_DV] = o
                s_ref[d * GDN_HEADS + h] = s
        return carry

    s_ref[:GDN_HEADS] = sf0_ref[0]
    s_ref[GDN_HEADS:] = sb0_ref[0]
    lax.fori_loop(0, n_chunks, chunk_step, 0)
    sf_ref[0, 0] = s_ref[:GDN_HEADS]
    sb_ref[0, 0] = s_ref[GDN_HEADS:]
    for h in range(GDN_HEADS):
        sl = slice(h * GDN_DV, (h + 1) * GDN_DV)
        o = of_ref[:, sl] + ob_ref[:, sl]
        o = o * lax.rsqrt(jnp.mean(o * o, axis=-1, keepdims=True) + NORM_EPS) * ng_ref[...]
        zh = z_ref[:, sl]
        y_ref[:, sl] = (o * (zh * jax.nn.sigmoid(zh))).astype(BF16)


def _gdn(proj, y_gdn, states_out, layer, length, s_f0, s_b0, conv_w, a_log, dt_bias, norm_g):
    nseq = (T_P if length == SEQ else T_S) // length
    row0 = 0 if length == SEQ else T_P // length
    col = lambda c: c // GDN_WIDTH
    blk = lambda c: pl.BlockSpec((length, GDN_WIDTH), lambda s: (row0 + s, col(c)))
    cw = lambda part: pl.BlockSpec((1, 3, GDN_WIDTH), lambda s: (layer, 0, part))
    vec = lambda: pl.BlockSpec((1, 1, LANE), lambda s: (layer, 0, 0))
    st_in = pl.BlockSpec((1, GDN_HEADS, GDN_DK, GDN_DV), lambda s: (s, 0, 0, 0))
    st_out = pl.BlockSpec((1, 1, GDN_HEADS, GDN_DK, GDN_DV), lambda s: (s, layer, 0, 0, 0))
    pad_lanes = lambda x, off: jnp.zeros((DEPTH, 1, LANE), F32).at[:, 0, off:off + 2 * GDN_HEADS].set(
        x.reshape(DEPTH, -1))
    in_specs = [blk(C_GDN), blk(C_GDN + GDN_WIDTH), blk(C_GDN + 2 * GDN_WIDTH), blk(C_Z),
                pl.BlockSpec((length, LANE), lambda s: (row0 + s, C_BA // LANE)),
                cw(0), cw(1), cw(2), vec(), vec(), vec(), st_in, st_in]
    args = [proj] * 5 + [conv_w] * 3 + [pad_lanes(a_log, 2 * GDN_HEADS), pad_lanes(dt_bias, 2 * GDN_HEADS),
                                        norm_g.reshape(DEPTH, 1, GDN_DV), s_f0, s_b0]
    st_shape = jax.ShapeDtypeStruct((nseq, DEPTH, GDN_HEADS, GDN_DK, GDN_DV), F32)
    aliases = {}
    for carried, out_idx in ((y_gdn, 0),) + (((states_out[0], 1), (states_out[1], 2)) if states_out else ()):
        if carried is not None:
            in_specs.append(pl.BlockSpec(memory_space=pl.ANY))
            args.append(carried)
            aliases[len(args) - 1] = out_idx

    def body(q_ref, k_ref, v_ref, z_ref, ba_ref, cwq, cwk, cwv, alog, dt, ng, sf0, sb0, *rest):
        _gdn_kernel(q_ref, k_ref, v_ref, z_ref, ba_ref, cwq.at[0], cwk.at[0], cwv.at[0], alog.at[0], dt.at[0],
                    ng.at[0], sf0, sb0, *rest, length=length)

    seq_buf = lambda w: pltpu.VMEM((length, w), F32)
    return pl.pallas_call(
        body,
        out_shape=[jax.ShapeDtypeStruct((T, GDN_WIDTH), BF16), st_shape, st_shape],
        grid=(nseq,), in_specs=in_specs,
        out_specs=[pl.BlockSpec((length, GDN_WIDTH), lambda s: (row0 + s, 0)), st_out, st_out],
        scratch_shapes=[seq_buf(GDN_WIDTH), seq_buf(GDN_WIDTH), seq_buf(GDN_WIDTH), seq_buf(LANE), seq_buf(LANE),
                        seq_buf(GDN_WIDTH), seq_buf(GDN_WIDTH),
                        pltpu.VMEM((2 * GDN_HEADS, GDN_DK, GDN_DV), F32)],
        input_output_aliases=aliases,
        compiler_params=_cparams(("arbitrary",)),
        name="gated_deltanet",
    )(*args)


def kernel(x_prompt, x_sample, cache_k, cache_v, state_fwd, state_bwd, c, c_ctx, ln1_g, ln2_g, w_mod, b_mod,
           w_in, na_rpb, hy_conv_w, hy_conv_b, hy_w1, hy_b1, hy_freq, hy_w2, hy_b2, hy_w3, hy_decay, hy_skip,
           gdn_conv_w, gdn_a_log, gdn_dt_bias, gdn_norm_g, w_pa, w_pb, w_pc, b_gate, w_out, ffn_w_up,
           ffn_conv_w, ffn_conv_b, ffn_w_down, final_g):
    x = jnp.concatenate([x_prompt.reshape(T_P, D), x_sample.reshape(T_S, D)], axis=0)
    cond = jnp.concatenate([c_ctx[None], c, jnp.zeros((N_COND - 1 - DEC_BATCH, D), F32)], axis=0)
    mod = _mod_table(cond, w_mod, b_mod).reshape(DEPTH * N_COND, 1, 6 * D)

    w_pa_b, w_pb_b, w_pc_b, w_out_b = (w.astype(BF16) for w in (w_pa, w_pb, w_pc, w_out))
    w_down_b = ffn_w_down.astype(BF16)
    w_in_t = jnp.swapaxes(w_in, 1, 2)
    tables = {n: tuple(_trig_table(n, True)) + tuple(_trig_table(n, False)) for n in (SEQ, DEC_SEQ)}
    zero_state = jnp.zeros((BATCH, GDN_HEADS, GDN_DK, GDN_DV), F32)

    caches = None
    states = None
    for layer in range(DEPTH):
        h = _norm_mod(x, ln1_g, mod, layer, 0)
        proj = _in_proj(h, w_in_t, layer)

        y_na, new_k, new_v = _ctx_attention(proj, layer, caches)
        caches = (new_k, new_v)
        y_na = _na_attention(proj, cache_k, cache_v, na_rpb[layer], y_na, layer)

        y_hy = None
        for n in (SEQ, DEC_SEQ):
            spectrum = _hy_filter_spectrum(n, tables[n][0], tables[n][1], hy_w1[layer], hy_b1[layer],
                                           hy_freq[layer], hy_w2[layer], hy_b2[layer], hy_w3[layer],
                                           hy_decay[layer])
            y_hy = _hyena(proj, y_hy, layer, n, tables[n], spectrum, hy_conv_w, hy_conv_b, hy_skip)

        y_gdn, s_f, s_b = _gdn(proj, None, states, layer, SEQ, zero_state, zero_state,
                               gdn_conv_w, gdn_a_log, gdn_dt_bias, gdn_norm_g)
        states = (s_f, s_b)
        y_gdn = _gdn(proj, y_gdn, None, layer, DEC_SEQ, state_fwd[:, layer], state_bwd[:, layer],
                     gdn_conv_w, gdn_a_log, gdn_dt_bias, gdn_norm_g)[0]

        x = _mix_out(y_na, y_hy, y_gdn, proj, b_gate, w_pa_b, w_pb_b, w_pc_b, w_out_b, x, mod, layer)
        h = _norm_mod(x, ln2_g, mod, layer, 3)
        act = _ffn_up(h, ffn_w_up, ffn_conv_w, ffn_conv_b, layer)
        x = _ffn_down(act, w_down_b, x, mod, layer)

    y_p, y_s = _final_norm(x, final_g)
    cache_shape = (BATCH, DEPTH, SEQ, NA_HEADS, NA_DH)
    return (y_p.reshape(BATCH, SEQ, D), y_s.reshape(DEC_BATCH, DEC_SEQ, D),
            caches[0].reshape(cache_shape), caches[1].reshape(cache_shape), states[0], states[1])
```

```python
import functools
import math

import jax
import jax.numpy as jnp
import numpy as np
from jax import lax
from jax.experimental import pallas as pl
from jax.experimental.pallas import tpu as pltpu

F32 = jnp.float32
BF16 = jnp.bfloat16

D = 2048
BATCH, SEQ = 32, 256
DEC_BATCH, DEC_SEQ = 4, 1024
DEPTH = 2
GRID_W = 64
NA_HEADS, NA_DH = 8, 128
NA_WIDTH = NA_HEADS * NA_DH
NA_WIN_ROWS, NA_WIN_COLS = 8, 16
HY_WIDTH = 512
HY_ORDER = 2
HY_BANDS = 16
GDN_HEADS, GDN_DK, GDN_DV = 4, 128, 128
GDN_WIDTH = GDN_HEADS * GDN_DV
D_FF = 5632
N_BRANCH = 3
NORM_EPS = 1e-6
NEG_INF = -1e30

T_P = BATCH * SEQ
T_S = DEC_BATCH * DEC_SEQ
T = T_P + T_S
N_COND = 8

C_Q, C_K, C_V = 0, NA_WIDTH, 2 * NA_WIDTH
C_HY = 3 * NA_WIDTH
C_GDN = C_HY + 3 * HY_WIDTH
C_Z = C_GDN + 3 * GDN_HEADS * GDN_DK
C_BA = C_Z + GDN_WIDTH
C_GATE = C_BA + 4 * GDN_HEADS
N_IN = C_GATE + N_BRANCH * D
GATE_BLOCK = C_BA
GATE_SHIFT = C_GATE - C_BA

LANE = 128
VMEM_LIMIT = 56 * 1024 * 1024
HIGHEST = lax.Precision.HIGHEST


def _cparams(sem):
    return pltpu.CompilerParams(dimension_semantics=sem, vmem_limit_bytes=VMEM_LIMIT)


def _cond_row(tile, rows_per_tile):
    first_latent = T_P // rows_per_tile
    per_seq = DEC_SEQ // rows_per_tile
    return jnp.where(tile < first_latent, 0, 1 + (tile - first_latent) // per_seq)


def _mod_kernel(c_ref, w_ref, b_ref, o_ref):
    c = c_ref[...]
    s = (c * jax.nn.sigmoid(c)).astype(BF16)
    o_ref[0] = jnp.dot(s, w_ref[0].astype(BF16), preferred_element_type=F32) + b_ref[0]


def _mod_table(cond, w_mod, b_mod):
    tn = 1024
    n = 6 * D
    return pl.pallas_call(
        _mod_kernel,
        out_shape=jax.ShapeDtypeStruct((DEPTH, N_COND, n), F32),
        grid=(DEPTH, n // tn),
        in_specs=[
            pl.BlockSpec((N_COND, D), lambda l, j: (0, 0)),
            pl.BlockSpec((1, D, tn), lambda l, j: (l, 0, j)),
            pl.BlockSpec((1, 1, tn), lambda l, j: (l, 0, j)),
        ],
        out_specs=pl.BlockSpec((1, N_COND, tn), lambda l, j: (l, 0, j)),
        compiler_params=_cparams(("arbitrary", "arbitrary")),
        name="mod_table",
    )(cond, w_mod, b_mod.reshape(DEPTH, 1, n))


def _norm_mod_kernel(x_ref, g_ref, sh_ref, sc_ref, o_ref):
    x = x_ref[...]
    y = x * lax.rsqrt(jnp.mean(x * x, axis=-1, keepdims=True) + NORM_EPS) * g_ref[0]
    o_ref[...] = (y * (1.0 + sc_ref[0]) + sh_ref[0]).astype(o_ref.dtype)


def _norm_mod(x, g, mod, layer, shift_chunk):
    tm = 256
    return pl.pallas_call(
        _norm_mod_kernel,
        out_shape=jax.ShapeDtypeStruct((T, D), BF16),
        grid=(T // tm,),
        in_specs=[
            pl.BlockSpec((tm, D), lambda i: (i, 0)),
            pl.BlockSpec((1, 1, D), lambda i: (layer, 0, 0)),
            pl.BlockSpec((1, 1, D), lambda i: (layer * N_COND + _cond_row(i, tm), 0, shift_chunk)),
            pl.BlockSpec((1, 1, D), lambda i: (layer * N_COND + _cond_row(i, tm), 0, shift_chunk + 1)),
        ],
        out_specs=pl.BlockSpec((tm, D), lambda i: (i, 0)),
        compiler_params=_cparams(("arbitrary",)),
        name="norm_mod",
    )(x, g.reshape(DEPTH, 1, D), mod, mod)


FINAL_TM = 256


def _final_norm_kernel(x_ref, g_ref, yp_ref, ys_ref):
    x = x_ref[...]
    y = x * lax.rsqrt(jnp.mean(x * x, axis=-1, keepdims=True) + NORM_EPS) * g_ref[...]
    is_context = pl.program_id(0) < T_P // FINAL_TM

    @pl.when(is_context)
    def _():
        yp_ref[...] = y

    @pl.when(jnp.logical_not(is_context))
    def _():
        ys_ref[...] = y


def _final_norm(x, g):
    tm = FINAL_TM
    n_ctx = T_P // tm
    return pl.pallas_call(
        _final_norm_kernel,
        out_shape=[jax.ShapeDtypeStruct((T_P, D), F32), jax.ShapeDtypeStruct((T_S, D), F32)],
        grid=(T // tm,),
        in_specs=[pl.BlockSpec((tm, D), lambda i: (i, 0)), pl.BlockSpec((1, D), lambda i: (0, 0))],
        out_specs=[pl.BlockSpec((tm, D), lambda i: (jnp.minimum(i, n_ctx - 1), 0)),
                   pl.BlockSpec((tm, D), lambda i: (jnp.maximum(i - n_ctx, 0), 0))],
        compiler_params=_cparams(("arbitrary",)),
        name="final_norm",
    )(x, g.reshape(1, D))


def _proj_kernel(h_ref, wt_ref, o_ref, wb_ref):
    @pl.when(pl.program_id(1) == 0)
    def _():
        wb_ref[...] = wt_ref[0].astype(BF16)

    o_ref[...] = lax.dot_general(h_ref[...], wb_ref[...], (((1,), (1,)), ((), ())), preferred_element_type=F32)


def _in_proj(h, w_in_t, layer):
    tm, tn = 1024, 1024
    return pl.pallas_call(
        _proj_kernel,
        out_shape=jax.ShapeDtypeStruct((T, N_IN), F32),
        grid=(pl.cdiv(N_IN, tn), T // tm),
        in_specs=[
            pl.BlockSpec((tm, D), lambda j, i: (i, 0)),
            pl.BlockSpec((1, tn, D), lambda j, i: (layer, j, 0)),
        ],
        out_specs=pl.BlockSpec((tm, tn), lambda j, i: (i, j)),
        scratch_shapes=[pltpu.VMEM((tn, D), BF16)],
        compiler_params=_cparams(("arbitrary", "arbitrary")),
        name="in_proj",
    )(h, w_in_t)


def _mix_out_kernel(yna_ref, yhy_ref, ygdn_ref, gl_ref, bg_ref, wpa_ref, wpb_ref, wpc_ref, wout_ref,
                    x_ref, gt_ref, o_ref):
    width = gl_ref.shape[1]
    gl = pltpu.roll(gl_ref[...], width - GATE_SHIFT, 1)[:, :N_BRANCH * D] + bg_ref[0]
    gates = jax.nn.sigmoid(gl)
    merged = (gates[:, :D] * jnp.dot(yna_ref[...], wpa_ref[0], preferred_element_type=F32)
              + gates[:, D:2 * D] * jnp.dot(yhy_ref[...], wpb_ref[0], preferred_element_type=F32)
              + gates[:, 2 * D:] * jnp.dot(ygdn_ref[...], wpc_ref[0], preferred_element_type=F32))
    r = jnp.dot(merged.astype(BF16), wout_ref[0], preferred_element_type=F32)
    o_ref[...] = x_ref[...] + gt_ref[0] * r


def _mix_out(y_na, y_hy, y_gdn, proj, b_gate, w_pa, w_pb, w_pc, w_out, x, mod, layer):
    tm = 256
    resident = functools.partial(pl.BlockSpec, pipeline_mode=pl.Buffered(1))
    return pl.pallas_call(
        _mix_out_kernel,
        out_shape=jax.ShapeDtypeStruct((T, D), F32),
        grid=(T // tm,),
        in_specs=[
            pl.BlockSpec((tm, NA_WIDTH), lambda i: (i, 0)),
            pl.BlockSpec((tm, HY_WIDTH), lambda i: (i, 0)),
            pl.BlockSpec((tm, GDN_WIDTH), lambda i: (i, 0)),
            pl.BlockSpec((tm, GATE_BLOCK), lambda i: (i, 1)),
            pl.BlockSpec((1, 1, N_BRANCH * D), lambda i: (layer, 0, 0)),
            resident((1, NA_WIDTH, D), lambda i: (layer, 0, 0)),
            resident((1, HY_WIDTH, D), lambda i: (layer, 0, 0)),
            resident((1, GDN_WIDTH, D), lambda i: (layer, 0, 0)),
            resident((1, D, D), lambda i: (layer, 0, 0)),
            pl.BlockSpec((tm, D), lambda i: (i, 0)),
            pl.BlockSpec((1, 1, D), lambda i: (layer * N_COND + _cond_row(i, tm), 0, 2)),
        ],
        out_specs=pl.BlockSpec((tm, D), lambda i: (i, 0)),
        compiler_params=_cparams(("arbitrary",)),
        name="mix_out",
    )(y_na, y_hy, y_gdn, proj, b_gate.reshape(DEPTH, 1, N_BRANCH * D), w_pa, w_pb, w_pc, w_out, x, mod)


FFN_TM = 1024


def _ffn_up_kernel(h_ref, wa_ref, wb_ref, cwa_ref, cwb_ref, cba_ref, cbb_ref, o_ref, wab_ref, wbb_ref):
    m = pl.program_id(1)

    @pl.when(m == 0)
    def _():
        wab_ref[...] = wa_ref[0].astype(BF16)
        wbb_ref[...] = wb_ref[0].astype(BF16)

    seq_len = jnp.where(m < T_P // FFN_TM, SEQ, DEC_SEQ)
    pos = lax.broadcasted_iota(jnp.int32, (FFN_TM, 1), 0) & (seq_len - 1)
    has_prev = pos != 0
    has_next = pos != seq_len - 1
    h = h_ref[...]

    def conv(w_ref, cw_ref, cb_ref):
        up = jnp.dot(h, w_ref[...], preferred_element_type=F32)
        prev = jnp.where(has_prev, pltpu.roll(up, 1, 0), 0.0)
        nxt = jnp.where(has_next, pltpu.roll(up, FFN_TM - 1, 0), 0.0)
        cw = cw_ref[0]
        return prev * cw[0:1] + up * cw[1:2] + nxt * cw[2:3] + cb_ref[0]

    ua = conv(wab_ref, cwa_ref, cba_ref)
    ub = conv(wbb_ref, cwb_ref, cbb_ref)
    o_ref[...] = (ua * jax.nn.sigmoid(ua) * ub).astype(BF16)


def _ffn_up(h, w_up, conv_w, conv_b, layer):
    tn = 512
    nt = D_FF // tn
    conv_b = conv_b.reshape(DEPTH, 1, 2 * D_FF)
    return pl.pallas_call(
        _ffn_up_kernel,
        out_shape=jax.ShapeDtypeStruct((T, D_FF), BF16),
        grid=(nt, T // FFN_TM),
        in_specs=[
            pl.BlockSpec((FFN_TM, D), lambda j, i: (i, 0)),
            pl.BlockSpec((1, D, tn), lambda j, i: (layer, 0, j)),
            pl.BlockSpec((1, D, tn), lambda j, i: (layer, 0, nt + j)),
            pl.BlockSpec((1, 3, tn), lambda j, i: (layer, 0, j)),
            pl.BlockSpec((1, 3, tn), lambda j, i: (layer, 0, nt + j)),
            pl.BlockSpec((1, 1, tn), lambda j, i: (layer, 0, j)),
            pl.BlockSpec((1, 1, tn), lambda j, i: (layer, 0, nt + j)),
        ],
        out_specs=pl.BlockSpec((FFN_TM, tn), lambda j, i: (i, j)),
        scratch_shapes=[pltpu.VMEM((D, tn), BF16), pltpu.VMEM((D, tn), BF16)],
        compiler_params=_cparams(("arbitrary", "arbitrary")),
        name="ffn_up",
    )(h, w_up, w_up, conv_w, conv_w, conv_b, conv_b)


def _ffn_down_kernel(a_ref, w_ref, x_ref, gt_ref, o_ref):
    o_ref[...] = x_ref[...] + gt_ref[0] * jnp.dot(a_ref[...], w_ref[0], preferred_element_type=F32)


def _ffn_down(act, w_down, x, mod, layer):
    tm, tn = 256, 512
    return pl.pallas_call(
        _ffn_down_kernel,
        out_shape=jax.ShapeDtypeStruct((T, D), F32),
        grid=(D // tn, T // tm),
        in_specs=[
            pl.BlockSpec((tm, D_FF), lambda j, i: (i, 0)),
            pl.BlockSpec((1, D_FF, tn), lambda j, i: (layer, 0, j)),
            pl.BlockSpec((tm, tn), lambda j, i: (i, j)),
            pl.BlockSpec((1, 1, tn), lambda j, i: (layer * N_COND + _cond_row(i, tm), 0, 5 * (D // tn) + j)),
        ],
        out_specs=pl.BlockSpec((tm, tn), lambda j, i: (i, j)),
        compiler_params=_cparams(("arbitrary", "arbitrary")),
        name="ffn_down",
    )(act, w_down, x, mod)


ATT_SCALE = NA_DH ** -0.5


def _nt_dot(a, b):
    return lax.dot_general(a, b, (((1,), (1,)), ((), ())), preferred_element_type=F32)


def _head_slices():
    return [slice(h * NA_DH, (h + 1) * NA_DH) for h in range(NA_HEADS)]


def _ctx_attn_kernel(q_ref, k_ref, v_ref, o_ref, ko_ref, vo_ref):
    heads = _head_slices()
    q = [(q_ref[:, sl] * ATT_SCALE).astype(BF16) for sl in heads]
    s = [_nt_dot(q[h], k_ref[:, sl].astype(BF16)) for h, sl in enumerate(heads)]
    e = [x - jnp.max(x, axis=-1, keepdims=True) for x in s]
    e = [jnp.exp(x) for x in e]
    o = [jnp.dot(e[h].astype(BF16), v_ref[:, sl].astype(BF16), preferred_element_type=F32)
         for h, sl in enumerate(heads)]
    for h, sl in enumerate(heads):
        o_ref[:, sl] = (o[h] / jnp.sum(e[h], axis=-1, keepdims=True)).astype(BF16)
    ko_ref[0, 0] = k_ref[...]
    vo_ref[0, 0] = v_ref[...]


def _ctx_attention(proj, layer, caches):
    cache_shape = jax.ShapeDtypeStruct((BATCH, DEPTH, SEQ, NA_WIDTH), F32)
    cache_spec = pl.BlockSpec((1, 1, SEQ, NA_WIDTH), lambda b: (b, layer, 0, 0))
    qkv_specs = [pl.BlockSpec((SEQ, NA_WIDTH), lambda b, j=j: (b, j)) for j in range(3)]
    out_shape = [jax.ShapeDtypeStruct((T, NA_WIDTH), BF16), cache_shape, cache_shape]
    out_specs = [pl.BlockSpec((SEQ, NA_WIDTH), lambda b: (b, 0)), cache_spec, cache_spec]
    if caches is None:
        return pl.pallas_call(
            _ctx_attn_kernel, out_shape=out_shape, grid=(BATCH,), in_specs=qkv_specs, out_specs=out_specs,
            compiler_params=_cparams(("arbitrary",)), name="ctx_attention",
        )(proj, proj, proj)

    def body(q_ref, k_ref, v_ref, kc_ref, vc_ref, o_ref, ko_ref, vo_ref):
        del kc_ref, vc_ref
        _ctx_attn_kernel(q_ref, k_ref, v_ref, o_ref, ko_ref, vo_ref)

    any_spec = pl.BlockSpec(memory_space=pl.ANY)
    return pl.pallas_call(
        body, out_shape=out_shape, grid=(BATCH,), in_specs=qkv_specs + [any_spec, any_spec],
        out_specs=out_specs, input_output_aliases={3: 1, 4: 2},
        compiler_params=_cparams(("arbitrary",)), name="ctx_attention",
    )(proj, proj, proj, *caches)


NA_ROWS = DEC_SEQ // GRID_W
NA_WIN_TOK = NA_WIN_ROWS * GRID_W


def _na_window_start(r):
    return jnp.clip(r - NA_WIN_ROWS // 2, 0, NA_ROWS - NA_WIN_ROWS)


def _na_bias_table(rpb):
    col = np.arange(GRID_W)
    col_start = np.clip(col - NA_WIN_COLS // 2, 0, GRID_W - NA_WIN_COLS)
    col_mask = (col[None, :] >= col_start[:, None]) & (col[None, :] < col_start[:, None] + NA_WIN_COLS)
    rel_col = np.clip(col[None, :] - col[:, None] + NA_WIN_COLS - 1, 0, 2 * NA_WIN_COLS - 2)
    banded = jnp.where(col_mask, rpb[:, :, rel_col], NEG_INF)
    per_offset = [banded[:, NA_WIN_ROWS - 1 - d:2 * NA_WIN_ROWS - 1 - d] for d in range(NA_WIN_ROWS)]
    tab = jnp.stack(per_offset, axis=0)
    return tab.transpose(0, 1, 3, 2, 4).reshape(NA_WIN_ROWS, NA_HEADS, GRID_W, NA_WIN_TOK)


def _na_attn_kernel(q_ref, k_ref, v_ref, ck_ref, cv_ref, bias_ref, y_in_ref, o_ref):
    del y_in_ref
    r = pl.program_id(1)
    start = pl.multiple_of(_na_window_start(r) * GRID_W, GRID_W)
    win = pl.ds(start, NA_WIN_TOK)
    heads = _head_slices()
    nh = range(NA_HEADS)
    q = [(q_ref[:, sl] * ATT_SCALE).astype(BF16) for sl in heads]
    s_win = [_nt_dot(q[h], k_ref[win, sl].astype(BF16)) + bias_ref[0, h] for h, sl in enumerate(heads)]
    s_ctx = [_nt_dot(q[h], ck_ref[0, 0, :, sl].astype(BF16)) for h, sl in enumerate(heads)]
    m = [jnp.maximum(jnp.max(s_win[h], axis=-1, keepdims=True), jnp.max(s_ctx[h], axis=-1, keepdims=True))
         for h in nh]
    e_win = [jnp.exp(s_win[h] - m[h]) for h in nh]
    e_ctx = [jnp.exp(s_ctx[h] - m[h]) for h in nh]
    o_win = [jnp.dot(e_win[h].astype(BF16), v_ref[win, sl].astype(BF16), preferred_element_type=F32)
             for h, sl in enumerate(heads)]
    o_ctx = [jnp.dot(e_ctx[h].astype(BF16), cv_ref[0, 0, :, sl].astype(BF16), preferred_element_type=F32)
             for h, sl in enumerate(heads)]
    for h, sl in enumerate(heads):
        denom = jnp.sum(e_win[h], axis=-1, keepdims=True) + jnp.sum(e_ctx[h], axis=-1, keepdims=True)
        o_ref[:, sl] = ((o_win[h] + o_ctx[h]) / denom).astype(BF16)


def _na_attention(proj, cache_k, cache_v, rpb, y_na, layer):
    q_row0 = T_P // GRID_W
    seq0 = T_P // DEC_SEQ
    kv_spec = lambda j: pl.BlockSpec((DEC_SEQ, NA_WIDTH), lambda b, r: (seq0 + b, j))
    ctx_spec = pl.BlockSpec((1, 1, SEQ, NA_WIDTH), lambda b, r: (b, layer, 0, 0))
    return pl.pallas_call(
        _na_attn_kernel,
        out_shape=jax.ShapeDtypeStruct((T, NA_WIDTH), BF16),
        grid=(DEC_BATCH, NA_ROWS),
        in_specs=[
            pl.BlockSpec((GRID_W, NA_WIDTH), lambda b, r: (q_row0 + b * NA_ROWS + r, 0)),
            kv_spec(1), kv_spec(2), ctx_spec, ctx_spec,
            pl.BlockSpec((1, NA_HEADS, GRID_W, NA_WIN_TOK), lambda b, r: (r - _na_window_start(r), 0, 0, 0)),
            pl.BlockSpec(memory_space=pl.ANY),
        ],
        out_specs=pl.BlockSpec((GRID_W, NA_WIDTH), lambda b, r: (q_row0 + b * NA_ROWS + r, 0)),
        input_output_aliases={6: 0},
        compiler_params=_cparams(("arbitrary", "arbitrary")),
        name="na_attention",
    )(proj, proj, proj, cache_k.reshape(DEC_BATCH, DEPTH, SEQ, NA_WIDTH),
      cache_v.reshape(DEC_BATCH, DEPTH, SEQ, NA_WIDTH), _na_bias_table(rpb), y_na)


HY_CB = 256


def _split_bf16(x):
    hi = x.astype(BF16)
    return hi, (x - hi.astype(F32)).astype(BF16)


def _dot3(a_hi, a_lo, b):
    b_hi, b_lo = _split_bf16(b)
    return (jnp.dot(a_hi, b_hi, preferred_element_type=F32) + jnp.dot(a_lo, b_hi, preferred_element_type=F32)
            + jnp.dot(a_hi, b_lo, preferred_element_type=F32))


def _trig_table_kernel(hi_ref, lo_ref, *, length, freq_on_rows, scale):
    rows, cols = hi_ref.shape
    r = lax.broadcasted_iota(jnp.int32, (rows, cols), 0) + pl.program_id(0) * rows
    c = lax.broadcasted_iota(jnp.int32, (rows, cols), 1)
    f, t = (r, c) if freq_on_rows else (c, r)
    k = f & (length - 1)
    quarter = jnp.where(f >= length, length, 0)
    phase = (2 * k + 1) * t + (-quarter if freq_on_rows else quarter)
    phase = phase & (4 * length - 1)
    phase = jnp.where(phase >= 2 * length, phase - 4 * length, phase)
    val = jnp.cos(phase.astype(F32) * (math.pi / (2 * length))) * scale
    hi, lo = _split_bf16(val)
    hi_ref[...] = hi
    lo_ref[...] = lo


def _trig_table(length, freq_on_rows):
    shape = (2 * length, length) if freq_on_rows else (length, 2 * length)
    tr = 256
    spec = pl.BlockSpec((tr, shape[1]), lambda i: (i, 0))
    return pl.pallas_call(
        functools.partial(_trig_table_kernel, length=length, freq_on_rows=freq_on_rows,
                          scale=1.0 if freq_on_rows else 1.0 / length),
        out_shape=[jax.ShapeDtypeStruct(shape, BF16)] * 2,
        grid=(shape[0] // tr,), out_specs=[spec, spec],
        compiler_params=_cparams(("arbitrary",)), name="dft_table",
    )()


def _hy_filter_kernel(tn_ref, band_ref, w1_ref, b1_ref, fr_ref, w2_ref, b2_ref, w3_ref, dec_ref,
                      fhi_ref, flo_ref, o_ref, *, length):
    t_norm = tn_ref[...]
    t_idx = lax.broadcasted_iota(jnp.int32, (length, LANE), 0).astype(F32)
    lane = lax.broadcasted_iota(jnp.int32, (length, LANE), 1)
    ang = (2.0 * math.pi / length) * t_idx * band_ref[...]
    z = jnp.where(lane == 0, t_norm,
                  jnp.where(lane <= HY_BANDS, jnp.cos(ang), jnp.where(lane <= 2 * HY_BANDS, jnp.sin(ang), 0.0)))
    hdn = jnp.sin(fr_ref[0:1] * (jnp.dot(z, w1_ref[...], precision=HIGHEST, preferred_element_type=F32)
                                 + b1_ref[...]))
    hdn = jnp.sin(fr_ref[1:2] * (jnp.dot(hdn, w2_ref[...], precision=HIGHEST, preferred_element_type=F32)
                                 + b2_ref[...]))
    filt = jnp.dot(hdn, w3_ref[...], precision=HIGHEST, preferred_element_type=F32)
    filt = filt * jnp.exp(-t_norm * jnp.abs(dec_ref[...]))
    first = lax.broadcasted_iota(jnp.int32, (length, HY_WIDTH), 0) == 0
    for o in range(HY_ORDER):
        fwd = filt[:, (2 * o) * HY_WIDTH:(2 * o + 1) * HY_WIDTH]
        bwd = filt[:, (2 * o + 1) * HY_WIDTH:(2 * o + 2) * HY_WIDTH]
        bwd = jnp.where(first, 0.0, pltpu.roll(bwd, 1, 0))
        o_ref[o, :length] = _dot3(fhi_ref[:length], flo_ref[:length], fwd + bwd)
        o_ref[o, length:] = _dot3(fhi_ref[length:], flo_ref[length:], bwd - fwd)


def _hy_filter_spectrum(length, f_hi, f_lo, w1, b1, freq, w2, b2, w3, decay):
    emb = 1 + 2 * HY_BANDS
    t_norm = jnp.linspace(0.0, 1.0, length, dtype=F32).reshape(length, 1)
    bands = np.zeros((1, LANE), np.float32)
    bands[0, 1:1 + HY_BANDS] = bands[0, 1 + HY_BANDS:emb] = np.linspace(1e-4, HY_BANDS - 1, HY_BANDS,
                                                                        dtype=np.float32)
    w1p = jnp.zeros((LANE, w1.shape[1]), F32).at[:emb].set(w1)
    return pl.pallas_call(
        functools.partial(_hy_filter_kernel, length=length),
        out_shape=jax.ShapeDtypeStruct((HY_ORDER, 2 * length, HY_WIDTH), F32),
        compiler_params=pltpu.CompilerParams(vmem_limit_bytes=VMEM_LIMIT),
        name="hyena_filter",
    )(t_norm, jnp.asarray(bands), w1p, b1.reshape(1, -1), freq, w2, b2.reshape(1, -1), w3,
      decay.reshape(1, -1), f_hi, f_lo)


def _hyena_kernel(v_ref, x1_ref, x2_ref, cwv_ref, cw1_ref, cw2_ref, cbv_ref, cb1_ref, cb2_ref, skip_ref,
                  hs_ref, f_ref, g_ref, *rest, length):
    o_ref = rest[-1]
    row = lax.broadcasted_iota(jnp.int32, v_ref.shape, 0)
    first, last = row == 0, row == length - 1

    def short_conv(u_ref, cw_ref, cb_ref):
        u = u_ref[...]
        prev = jnp.where(first, 0.0, pltpu.roll(u, 1, 0))
        nxt = jnp.where(last, 0.0, pltpu.roll(u, length - 1, 0))
        return prev * cw_ref[0:1] + u * cw_ref[1:2] + nxt * cw_ref[2:3] + cb_ref[...]

    z = short_conv(v_ref, cwv_ref, cbv_ref)
    gates = (short_conv(x1_ref, cw1_ref, cb1_ref), short_conv(x2_ref, cw2_ref, cb2_ref))
    for o in range(HY_ORDER):
        zf = jnp.dot(f_ref[...], z.astype(BF16), preferred_element_type=F32)
        zc, zs = zf[:length], zf[length:]
        h_re, h_im = hs_ref[o, :length], hs_ref[o, length:]
        p = jnp.concatenate([zc * h_re + zs * h_im, zc * h_im - zs * h_re], axis=0)
        conv = jnp.dot(g_ref[...], p.astype(BF16), preferred_element_type=F32)
        z = gates[o] * (conv + z * skip_ref[o:o + 1])
    o_ref[...] = z.astype(BF16)


def _hyena(proj, y_hy, layer, length, tables, spectrum, conv_w, conv_b, skip):
    nseq = (T_P if length == SEQ else T_S) // length
    row0 = 0 if length == SEQ else T_P // length
    cb = HY_WIDTH if length == SEQ else HY_CB
    ncb = HY_WIDTH // cb
    col0 = C_HY // cb
    u_spec = lambda part: pl.BlockSpec((length, cb), lambda s, j: (row0 + s, col0 + part * ncb + j))
    cw_spec = lambda part: pl.BlockSpec((1, 3, cb), lambda s, j: (layer, 0, part * ncb + j))
    cb_spec = lambda part: pl.BlockSpec((1, 1, cb), lambda s, j: (layer, 0, part * ncb + j))
    resident = functools.partial(pl.BlockSpec, pipeline_mode=pl.Buffered(1))
    tab_specs = [resident(t.shape, lambda s, j: (0, 0)) for t in tables]
    in_specs = ([u_spec(0), u_spec(1), u_spec(2), cw_spec(0), cw_spec(1), cw_spec(2),
                 cb_spec(0), cb_spec(1), cb_spec(2),
                 pl.BlockSpec((1, HY_ORDER, cb), lambda s, j: (layer, 0, j)),
                 pl.BlockSpec((HY_ORDER, 2 * length, cb), lambda s, j: (0, 0, j))] + tab_specs)
    args = [proj] * 3 + [conv_w] * 3 + [conv_b.reshape(DEPTH, 1, -1)] * 3 + [skip, spectrum] + list(tables)
    aliases = {}
    if y_hy is not None:
        in_specs.append(pl.BlockSpec(memory_space=pl.ANY))
        args.append(y_hy)
        aliases = {len(args) - 1: 0}

    def body(v_ref, x1_ref, x2_ref, cwv, cw1, cw2, cbv, cb1, cb2, skip_ref, hs_ref, *rest):
        _hyena_kernel(v_ref, x1_ref, x2_ref, cwv.at[0], cw1.at[0], cw2.at[0], cbv.at[0], cb1.at[0], cb2.at[0],
                      skip_ref.at[0], hs_ref, *rest, length=length)

    return pl.pallas_call(
        body,
        out_shape=jax.ShapeDtypeStruct((T, HY_WIDTH), BF16),
        grid=(nseq, ncb), in_specs=in_specs,
        out_specs=pl.BlockSpec((length, cb), lambda s, j: (row0 + s, j)),
        input_output_aliases=aliases,
        compiler_params=_cparams(("arbitrary", "arbitrary")),
        name="hyena",
    )(*args)


GC = 128
GDN_SCALE = GDN_DK ** -0.5


def _merge_masks(lower):
    ri = lax.broadcasted_iota(jnp.int32, (GC, GC), 0)
    ci = lax.broadcasted_iota(jnp.int32, (GC, GC), 1)
    hi, lo = (ri, ci) if lower else (ci, ri)
    return [((hi >> (b + 1)) == (lo >> (b + 1))) & ((hi >> b) > (lo >> b)) for b in range(int(math.log2(GC)))]


def _unit_tri_inverse(a, masks):
    ri = lax.broadcasted_iota(jnp.int32, (GC, GC), 0)
    ci = lax.broadcasted_iota(jnp.int32, (GC, GC), 1)
    eye = (ri == ci).astype(F32)
    n = range(len(a))
    t = [eye - jnp.where(masks[i][0], a[i], 0.0) for i in n]
    for level in range(1, len(masks[0])):
        tb = [t[i].astype(BF16) for i in n]
        ta = [jnp.dot(tb[i], jnp.where(masks[i][level], a[i], 0.0).astype(BF16), preferred_element_type=F32)
              for i in n]
        tat = [jnp.dot(ta[i].astype(BF16), tb[i], preferred_element_type=F32) for i in n]
        t = [t[i] - tat[i] for i in n]
    return t


def _gdn_kernel(q_ref, k_ref, v_ref, z_ref, ba_ref, cwq_ref, cwk_ref, cwv_ref, alog_ref, dt_ref, ng_ref,
                sf0_ref, sb0_ref, *rest, length):
    y_ref, sf_ref, sb_ref, qn_ref, kn_ref, vn_ref, beta_ref, g_ref, of_ref, ob_ref, s_ref = rest[-11:]
    n_chunks = length // GC
    row = lax.broadcasted_iota(jnp.int32, (length, GDN_WIDTH), 0)
    first, last = row == 0, row == length - 1

    def conv_silu(u_ref, cw_ref):
        u = u_ref[...]
        prev = jnp.where(first, 0.0, pltpu.roll(u, 1, 0))
        nxt = jnp.where(last, 0.0, pltpu.roll(u, length - 1, 0))
        c = prev * cw_ref[0:1] + u * cw_ref[1:2] + nxt * cw_ref[2:3]
        return c * jax.nn.sigmoid(c)

    q = conv_silu(q_ref, cwq_ref)
    k = conv_silu(k_ref, cwk_ref)
    vn_ref[...] = conv_silu(v_ref, cwv_ref)
    for h in range(GDN_HEADS):
        sl = slice(h * GDN_DK, (h + 1) * GDN_DK)
        qh, kh = q[:, sl], k[:, sl]
        qn_ref[:, sl] = qh * lax.rsqrt(jnp.sum(qh * qh, axis=-1, keepdims=True) + NORM_EPS) * GDN_SCALE
        kn_ref[:, sl] = kh * lax.rsqrt(jnp.sum(kh * kh, axis=-1, keepdims=True) + NORM_EPS)
    ba = ba_ref[...]
    beta_ref[...] = jax.nn.sigmoid(ba)
    g_ref[...] = -jnp.exp(alog_ref[...]) * jax.nn.softplus(ba + dt_ref[...])

    ri = lax.broadcasted_iota(jnp.int32, (GC, GC), 0)
    ci = lax.broadcasted_iota(jnp.int32, (GC, GC), 1)
    lower = (ri >= ci).astype(F32)
    upper = (ri <= ci).astype(F32)

    def chunk_step(n, carry):
        rows = [pl.ds(pl.multiple_of((n if d == 0 else n_chunks - 1 - n) * GC, GC), GC) for d in range(2)]
        loaded = {}
        for d in range(2):
            loaded[d] = (g_ref[rows[d], :], beta_ref[rows[d], :])
            for h in range(GDN_HEADS):
                sl = slice(h * GDN_DK, (h + 1) * GDN_DK)
                loaded[d, h] = (qn_ref[rows[d], sl], kn_ref[rows[d], sl], vn_ref[rows[d], sl],
                                s_ref[d * GDN_HEADS + h])
        units = [(d, h) for d in range(2) for h in range(GDN_HEADS)]
        incl = [(ri >= ci), (ri <= ci)]
        strict = [(ri > ci), (ri < ci)]
        masks = [_merge_masks(True), _merge_masks(False)]
        gcs = [jnp.dot(tri, loaded[d][0], precision=HIGHEST, preferred_element_type=F32)
               for d, tri in enumerate((lower, upper))]
        gcs_t = [g.T for g in gcs]
        edge = [GC - 1, 0]
        g_col = [gcs[d][:, 8 + 4 * d + h:9 + 4 * d + h] for d, h in units]
        g_row = [gcs_t[d][8 + 4 * d + h:9 + 4 * d + h, :] for d, h in units]
        g_end = [gcs_t[d][8 + 4 * d + h:9 + 4 * d + h, edge[d]:edge[d] + 1] for d, h in units]
        b_col = [loaded[d][1][:, 4 * d + h:4 * d + h + 1] for d, h in units]
        qc, kc, vc, st = (list(x) for x in zip(*(loaded[u] for u in units)))
        nu = range(len(units))
        decay = [jnp.where(incl[d], jnp.exp(jnp.where(incl[d], g_col[i] - g_row[i], 0.0)), 0.0)
                 for i, (d, h) in enumerate(units)]
        kb = [kc[i] * b_col[i] for i in nu]
        kcb = [kc[i].astype(BF16) for i in nu]
        kk = [_nt_dot(kb[i].astype(BF16), kcb[i]) for i in nu]
        qk = [_nt_dot(qc[i].astype(BF16), kcb[i]) for i in nu]
        a = [jnp.where(strict[d], kk[i] * decay[i], 0.0) for i, (d, h) in enumerate(units)]
        t = _unit_tri_inverse(a, [masks[d] for d, h in units])
        e_col = [jnp.exp(g_col[i]) for i in nu]
        rhs = [jnp.concatenate([vc[i] * b_col[i], kb[i] * e_col[i]], axis=1).astype(BF16) for i in nu]
        sol = [jnp.dot(t[i].astype(BF16), rhs[i], preferred_element_type=F32) for i in nu]
        attn = [jnp.where(incl[d], qk[i] * decay[i], 0.0).astype(BF16) for i, (d, h) in enumerate(units)]
        sb = [st[i].astype(BF16) for i in nu]
        ws = [jnp.dot(sol[i][:, GDN_DV:].astype(BF16), sb[i], preferred_element_type=F32) for i in nu]
        qs = [jnp.dot((qc[i] * e_col[i]).astype(BF16), sb[i], preferred_element_type=F32) for i in nu]
        v_new = [(sol[i][:, :GDN_DV] - ws[i]).astype(BF16) for i in nu]
        av = [jnp.dot(attn[i], v_new[i], preferred_element_type=F32) for i in nu]
        k_dec_t = [(kc[i] * jnp.exp(g_end[i] - g_col[i])).T.astype(BF16) for i in nu]
        kv = [jnp.dot(k_dec_t[i], v_new[i], preferred_element_type=F32) for i in nu]
        results = [(qs[i] + av[i], st[i] * jnp.exp(g_end[i]) + kv[i]) for i in nu]
        for d in range(2):
            for h in range(GDN_HEADS):
                o, s = results[d * GDN_HEADS + h]
                (of_ref if d == 0 else ob_ref)[rows[d], h * GDN_DV:(h + 1) * GDN_DV] = o
                s_ref[d * GDN_HEADS + h] = s
        return carry

    s_ref[:GDN_HEADS] = sf0_ref[0]
    s_ref[GDN_HEADS:] = sb0_ref[0]
    lax.fori_loop(0, n_chunks, chunk_step, 0)
    sf_ref[0, 0] = s_ref[:GDN_HEADS]
    sb_ref[0, 0] = s_ref[GDN_HEADS:]
    for h in range(GDN_HEADS):
        sl = slice(h * GDN_DV, (h + 1) * GDN_DV)
        o = of_ref[:, sl] + ob_ref[:, sl]
        o = o * lax.rsqrt(jnp.mean(o * o, axis=-1, keepdims=True) + NORM_EPS) * ng_ref[...]
        zh = z_ref[:, sl]
        y_ref[:, sl] = (o * (zh * jax.nn.sigmoid(zh))).astype(BF16)


def _gdn(proj, y_gdn, states_out, layer, length, s_f0, s_b0, conv_w, a_log, dt_bias, norm_g):
    nseq = (T_P if length == SEQ else T_S) // length
    row0 = 0 if length == SEQ else T_P // length
    col = lambda c: c // GDN_WIDTH
    blk = lambda c: pl.BlockSpec((length, GDN_WIDTH), lambda s: (row0 + s, col(c)))
    cw = lambda part: pl.BlockSpec((1, 3, GDN_WIDTH), lambda s: (layer, 0, part))
    vec = lambda: pl.BlockSpec((1, 1, LANE), lambda s: (layer, 0, 0))
    st_in = pl.BlockSpec((1, GDN_HEADS, GDN_DK, GDN_DV), lambda s: (s, 0, 0, 0))
    st_out = pl.BlockSpec((1, 1, GDN_HEADS, GDN_DK, GDN_DV), lambda s: (s, layer, 0, 0, 0))
    pad_lanes = lambda x, off: jnp.zeros((DEPTH, 1, LANE), F32).at[:, 0, off:off + 2 * GDN_HEADS].set(
        x.reshape(DEPTH, -1))
    in_specs = [blk(C_GDN), blk(C_GDN + GDN_WIDTH), blk(C_GDN + 2 * GDN_WIDTH), blk(C_Z),
                pl.BlockSpec((length, LANE), lambda s: (row0 + s, C_BA // LANE)),
                cw(0), cw(1), cw(2), vec(), vec(), vec(), st_in, st_in]
    args = [proj] * 5 + [conv_w] * 3 + [pad_lanes(a_log, 2 * GDN_HEADS), pad_lanes(dt_bias, 2 * GDN_HEADS),
                                        norm_g.reshape(DEPTH, 1, GDN_DV), s_f0, s_b0]
    st_shape = jax.ShapeDtypeStruct((nseq, DEPTH, GDN_HEADS, GDN_DK, GDN_DV), F32)
    aliases = {}
    for carried, out_idx in ((y_gdn, 0),) + (((states_out[0], 1), (states_out[1], 2)) if states_out else ()):
        if carried is not None:
            in_specs.append(pl.BlockSpec(memory_space=pl.ANY))
            args.append(carried)
            aliases[len(args) - 1] = out_idx

    def body(q_ref, k_ref, v_ref, z_ref, ba_ref, cwq, cwk, cwv, alog, dt, ng, sf0, sb0, *rest):
        _gdn_kernel(q_ref, k_ref, v_ref, z_ref, ba_ref, cwq.at[0], cwk.at[0], cwv.at[0], alog.at[0], dt.at[0],
                    ng.at[0], sf0, sb0, *rest, length=length)

    seq_buf = lambda w: pltpu.VMEM((length, w), F32)
    return pl.pallas_call(
        body,
        out_shape=[jax.ShapeDtypeStruct((T, GDN_WIDTH), BF16), st_shape, st_shape],
        grid=(nseq,), in_specs=in_specs,
        out_specs=[pl.BlockSpec((length, GDN_WIDTH), lambda s: (row0 + s, 0)), st_out, st_out],
        scratch_shapes=[seq_buf(GDN_WIDTH), seq_buf(GDN_WIDTH), seq_buf(GDN_WIDTH), seq_buf(LANE), seq_buf(LANE),
                        seq_buf(GDN_WIDTH), seq_buf(GDN_WIDTH),
                        pltpu.VMEM((2 * GDN_HEADS, GDN_DK, GDN_DV), F32)],
        input_output_aliases=aliases,
        compiler_params=_cparams(("arbitrary",)),
        name="gated_deltanet",
    )(*args)


def kernel(x_prompt, x_sample, cache_k, cache_v, state_fwd, state_bwd, c, c_ctx, ln1_g, ln2_g, w_mod, b_mod,
           w_in, na_rpb, hy_conv_w, hy_conv_b, hy_w1, hy_b1, hy_freq, hy_w2, hy_b2, hy_w3, hy_decay, hy_skip,
           gdn_conv_w, gdn_a_log, gdn_dt_bias, gdn_norm_g, w_pa, w_pb, w_pc, b_gate, w_out, ffn_w_up,
           ffn_conv_w, ffn_conv_b, ffn_w_down, final_g):
    x = jnp.concatenate([x_prompt.reshape(T_P, D), x_sample.reshape(T_S, D)], axis=0)
    cond = jnp.concatenate([c_ctx[None], c, jnp.zeros((N_COND - 1 - DEC_BATCH, D), F32)], axis=0)
    mod = _mod_table(cond, w_mod, b_mod).reshape(DEPTH * N_COND, 1, 6 * D)

    w_pa_b, w_pb_b, w_pc_b, w_out_b = (w.astype(BF16) for w in (w_pa, w_pb, w_pc, w_out))
    w_down_b = ffn_w_down.astype(BF16)
    w_in_t = jnp.swapaxes(w_in, 1, 2)
    tables = {n: tuple(_trig_table(n, True)) + tuple(_trig_table(n, False)) for n in (SEQ, DEC_SEQ)}
    zero_state = jnp.zeros((BATCH, GDN_HEADS, GDN_DK, GDN_DV), F32)

    caches = None
    states = None
    for layer in range(DEPTH):
        h = _norm_mod(x, ln1_g, mod, layer, 0)
        proj = _in_proj(h, w_in_t, layer)

        y_na, new_k, new_v = _ctx_attention(proj, layer, caches)
        caches = (new_k, new_v)
        y_na = _na_attention(proj, cache_k, cache_v, na_rpb[layer], y_na, layer)

        y_hy = None
        for n in (SEQ, DEC_SEQ):
            spectrum = _hy_filter_spectrum(n, tables[n][0], tables[n][1], hy_w1[layer], hy_b1[layer],
                                           hy_freq[layer], hy_w2[layer], hy_b2[layer], hy_w3[layer],
                                           hy_decay[layer])
            y_hy = _hyena(proj, y_hy, layer, n, (tables[n][0], tables[n][2]), spectrum,
                          hy_conv_w, hy_conv_b, hy_skip)

        y_gdn, s_f, s_b = _gdn(proj, None, states, layer, SEQ, zero_state, zero_state,
                               gdn_conv_w, gdn_a_log, gdn_dt_bias, gdn_norm_g)
        states = (s_f, s_b)
        y_gdn = _gdn(proj, y_gdn, None, layer, DEC_SEQ, state_fwd[:, layer], state_bwd[:, layer],
                     gdn_conv_w, gdn_a_log, gdn_dt_bias, gdn_norm_g)[0]

        x = _mix_out(y_na, y_hy, y_gdn, proj, b_gate, w_pa_b, w_pb_b, w_pc_b, w_out_b, x, mod, layer)
        h = _norm_mod(x, ln2_g, mod, layer, 3)
        act = _ffn_up(h, ffn_w_up, ffn_conv_w, ffn_conv_b, layer)
        x = _ffn_down(act, w_down_b, x, mod, layer)

    y_p, y_s = _final_norm(x, final_g)
    cache_shape = (BATCH, DEPTH, SEQ, NA_HEADS, NA_DH)
    return (y_p.reshape(BATCH, SEQ, D), y_s.reshape(DEC_BATCH, DEC_SEQ, D),
            caches[0].reshape(cache_shape), caches[1].reshape(cache_shape), states[0], states[1])
```

```python
import functools
import math

import jax
import jax.numpy as jnp
import numpy as np
from jax import lax
from jax.experimental import pallas as pl
from jax.experimental.pallas import tpu as pltpu

F32 = jnp.float32
BF16 = jnp.bfloat16

D = 2048
BATCH, SEQ = 32, 256
DEC_BATCH, DEC_SEQ = 4, 1024
DEPTH = 2
GRID_W = 64
NA_HEADS, NA_DH = 8, 128
NA_WIDTH = NA_HEADS * NA_DH
NA_WIN_ROWS, NA_WIN_COLS = 8, 16
HY_WIDTH = 512
HY_ORDER = 2
HY_BANDS = 16
GDN_HEADS, GDN_DK, GDN_DV = 4, 128, 128
GDN_WIDTH = GDN_HEADS * GDN_DV
D_FF = 5632
N_BRANCH = 3
NORM_EPS = 1e-6
NEG_INF = -1e30

T_P = BATCH * SEQ
T_S = DEC_BATCH * DEC_SEQ
T = T_P + T_S
N_COND = 8

C_Q, C_K, C_V = 0, NA_WIDTH, 2 * NA_WIDTH
C_HY = 3 * NA_WIDTH
C_GDN = C_HY + 3 * HY_WIDTH
C_Z = C_GDN + 3 * GDN_HEADS * GDN_DK
C_BA = C_Z + GDN_WIDTH
C_GATE = C_BA + 4 * GDN_HEADS
N_IN = C_GATE + N_BRANCH * D
GATE_BLOCK = C_BA
GATE_SHIFT = C_GATE - C_BA

LANE = 128
VMEM_LIMIT = 56 * 1024 * 1024
HIGHEST = lax.Precision.HIGHEST


def _cparams(sem):
    return pltpu.CompilerParams(dimension_semantics=sem, vmem_limit_bytes=VMEM_LIMIT)


def _cond_row(tile, rows_per_tile):
    first_latent = T_P // rows_per_tile
    per_seq = DEC_SEQ // rows_per_tile
    return jnp.where(tile < first_latent, 0, 1 + (tile - first_latent) // per_seq)


def _mod_kernel(c_ref, w_ref, b_ref, o_ref):
    c = c_ref[...]
    s = (c * jax.nn.sigmoid(c)).astype(BF16)
    o_ref[0] = jnp.dot(s, w_ref[0].astype(BF16), preferred_element_type=F32) + b_ref[0]


def _mod_table(cond, w_mod, b_mod):
    tn = 1024
    n = 6 * D
    return pl.pallas_call(
        _mod_kernel,
        out_shape=jax.ShapeDtypeStruct((DEPTH, N_COND, n), F32),
        grid=(DEPTH, n // tn),
        in_specs=[
            pl.BlockSpec((N_COND, D), lambda l, j: (0, 0)),
            pl.BlockSpec((1, D, tn), lambda l, j: (l, 0, j)),
            pl.BlockSpec((1, 1, tn), lambda l, j: (l, 0, j)),
        ],
        out_specs=pl.BlockSpec((1, N_COND, tn), lambda l, j: (l, 0, j)),
        compiler_params=_cparams(("arbitrary", "arbitrary")),
        name="mod_table",
    )(cond, w_mod, b_mod.reshape(DEPTH, 1, n))


def _norm_mod_kernel(x_ref, g_ref, sh_ref, sc_ref, o_ref):
    x = x_ref[...]
    y = x * lax.rsqrt(jnp.mean(x * x, axis=-1, keepdims=True) + NORM_EPS) * g_ref[0]
    o_ref[...] = (y * (1.0 + sc_ref[0]) + sh_ref[0]).astype(o_ref.dtype)


def _norm_mod(x, g, mod, layer, shift_chunk):
    tm = 1024
    return pl.pallas_call(
        _norm_mod_kernel,
        out_shape=jax.ShapeDtypeStruct((T, D), BF16),
        grid=(T // tm,),
        in_specs=[
            pl.BlockSpec((tm, D), lambda i: (i, 0)),
            pl.BlockSpec((1, 1, D), lambda i: (layer, 0, 0)),
            pl.BlockSpec((1, 1, D), lambda i: (layer * N_COND + _cond_row(i, tm), 0, shift_chunk)),
            pl.BlockSpec((1, 1, D), lambda i: (layer * N_COND + _cond_row(i, tm), 0, shift_chunk + 1)),
        ],
        out_specs=pl.BlockSpec((tm, D), lambda i: (i, 0)),
        compiler_params=_cparams(("arbitrary",)),
        name="norm_mod",
    )(x, g.reshape(DEPTH, 1, D), mod, mod)


FINAL_TM = 1024


def _final_norm_kernel(x_ref, g_ref, yp_ref, ys_ref):
    x = x_ref[...]
    y = x * lax.rsqrt(jnp.mean(x * x, axis=-1, keepdims=True) + NORM_EPS) * g_ref[...]
    is_context = pl.program_id(0) < T_P // FINAL_TM

    @pl.when(is_context)
    def _():
        yp_ref[...] = y

    @pl.when(jnp.logical_not(is_context))
    def _():
        ys_ref[...] = y


def _final_norm(x, g):
    tm = FINAL_TM
    n_ctx = T_P // tm
    return pl.pallas_call(
        _final_norm_kernel,
        out_shape=[jax.ShapeDtypeStruct((T_P, D), F32), jax.ShapeDtypeStruct((T_S, D), F32)],
        grid=(T // tm,),
        in_specs=[pl.BlockSpec((tm, D), lambda i: (i, 0)), pl.BlockSpec((1, D), lambda i: (0, 0))],
        out_specs=[pl.BlockSpec((tm, D), lambda i: (jnp.minimum(i, n_ctx - 1), 0)),
                   pl.BlockSpec((tm, D), lambda i: (jnp.maximum(i - n_ctx, 0), 0))],
        compiler_params=_cparams(("arbitrary",)),
        name="final_norm",
    )(x, g.reshape(1, D))


def _proj_kernel(h_ref, wt_ref, o_ref, wb_ref):
    @pl.when(pl.program_id(1) == 0)
    def _():
        wb_ref[...] = wt_ref[0].astype(BF16)

    o_ref[...] = lax.dot_general(h_ref[...], wb_ref[...], (((1,), (1,)), ((), ())), preferred_element_type=F32)


def _in_proj(h, w_in_t, layer):
    tm, tn = 1024, 1024
    return pl.pallas_call(
        _proj_kernel,
        out_shape=jax.ShapeDtypeStruct((T, N_IN), F32),
        grid=(pl.cdiv(N_IN, tn), T // tm),
        in_specs=[
            pl.BlockSpec((tm, D), lambda j, i: (i, 0)),
            pl.BlockSpec((1, tn, D), lambda j, i: (layer, j, 0)),
        ],
        out_specs=pl.BlockSpec((tm, tn), lambda j, i: (i, j)),
        scratch_shapes=[pltpu.VMEM((tn, D), BF16)],
        compiler_params=_cparams(("arbitrary", "arbitrary")),
        name="in_proj",
    )(h, w_in_t)


def _mix_out_kernel(yna_ref, yhy_ref, ygdn_ref, gl_ref, bg_ref, wpa_ref, wpb_ref, wpc_ref, wout_ref,
                    x_ref, gt_ref, o_ref):
    width = gl_ref.shape[1]
    gl = pltpu.roll(gl_ref[...], width - GATE_SHIFT, 1)[:, :N_BRANCH * D] + bg_ref[0]
    gates = jax.nn.sigmoid(gl)
    merged = (gates[:, :D] * jnp.dot(yna_ref[...], wpa_ref[0], preferred_element_type=F32)
              + gates[:, D:2 * D] * jnp.dot(yhy_ref[...], wpb_ref[0], preferred_element_type=F32)
              + gates[:, 2 * D:] * jnp.dot(ygdn_ref[...], wpc_ref[0], preferred_element_type=F32))
    r = jnp.dot(merged.astype(BF16), wout_ref[0], preferred_element_type=F32)
    o_ref[...] = x_ref[...] + gt_ref[0] * r


def _mix_out(y_na, y_hy, y_gdn, proj, b_gate, w_pa, w_pb, w_pc, w_out, x, mod, layer):
    tm = 256
    resident = functools.partial(pl.BlockSpec, pipeline_mode=pl.Buffered(1))
    return pl.pallas_call(
        _mix_out_kernel,
        out_shape=jax.ShapeDtypeStruct((T, D), F32),
        grid=(T // tm,),
        in_specs=[
            pl.BlockSpec((tm, NA_WIDTH), lambda i: (i, 0)),
            pl.BlockSpec((tm, HY_WIDTH), lambda i: (i, 0)),
            pl.BlockSpec((tm, GDN_WIDTH), lambda i: (i, 0)),
            pl.BlockSpec((tm, GATE_BLOCK), lambda i: (i, 1)),
            pl.BlockSpec((1, 1, N_BRANCH * D), lambda i: (layer, 0, 0)),
            resident((1, NA_WIDTH, D), lambda i: (layer, 0, 0)),
            resident((1, HY_WIDTH, D), lambda i: (layer, 0, 0)),
            resident((1, GDN_WIDTH, D), lambda i: (layer, 0, 0)),
            resident((1, D, D), lambda i: (layer, 0, 0)),
            pl.BlockSpec((tm, D), lambda i: (i, 0)),
            pl.BlockSpec((1, 1, D), lambda i: (layer * N_COND + _cond_row(i, tm), 0, 2)),
        ],
        out_specs=pl.BlockSpec((tm, D), lambda i: (i, 0)),
        compiler_params=_cparams(("arbitrary",)),
        name="mix_out",
    )(y_na, y_hy, y_gdn, proj, b_gate.reshape(DEPTH, 1, N_BRANCH * D), w_pa, w_pb, w_pc, w_out, x, mod)


FFN_TM = 1024


def _ffn_up_kernel(h_ref, wa_ref, wb_ref, cwa_ref, cwb_ref, cba_ref, cbb_ref, o_ref, wab_ref, wbb_ref):
    m = pl.program_id(1)

    @pl.when(m == 0)
    def _():
        wab_ref[...] = wa_ref[0].astype(BF16)
        wbb_ref[...] = wb_ref[0].astype(BF16)

    seq_len = jnp.where(m < T_P // FFN_TM, SEQ, DEC_SEQ)
    pos = lax.broadcasted_iota(jnp.int32, (FFN_TM, 1), 0) & (seq_len - 1)
    has_prev = pos != 0
    has_next = pos != seq_len - 1
    h = h_ref[...]

    def conv(w_ref, cw_ref, cb_ref):
        up = jnp.dot(h, w_ref[...], preferred_element_type=F32)
        prev = jnp.where(has_prev, pltpu.roll(up, 1, 0), 0.0)
        nxt = jnp.where(has_next, pltpu.roll(up, FFN_TM - 1, 0), 0.0)
        cw = cw_ref[0]
        return prev * cw[0:1] + up * cw[1:2] + nxt * cw[2:3] + cb_ref[0]

    ua = conv(wab_ref, cwa_ref, cba_ref)
    ub = conv(wbb_ref, cwb_ref, cbb_ref)
    o_ref[...] = (ua * jax.nn.sigmoid(ua) * ub).astype(BF16)


def _ffn_up(h, w_up, conv_w, conv_b, layer):
    tn = 512
    nt = D_FF // tn
    conv_b = conv_b.reshape(DEPTH, 1, 2 * D_FF)
    return pl.pallas_call(
        _ffn_up_kernel,
        out_shape=jax.ShapeDtypeStruct((T, D_FF), BF16),
        grid=(nt, T // FFN_TM),
        in_specs=[
            pl.BlockSpec((FFN_TM, D), lambda j, i: (i, 0)),
            pl.BlockSpec((1, D, tn), lambda j, i: (layer, 0, j)),
            pl.BlockSpec((1, D, tn), lambda j, i: (layer, 0, nt + j)),
            pl.BlockSpec((1, 3, tn), lambda j, i: (layer, 0, j)),
            pl.BlockSpec((1, 3, tn), lambda j, i: (layer, 0, nt + j)),
            pl.BlockSpec((1, 1, tn), lambda j, i: (layer, 0, j)),
            pl.BlockSpec((1, 1, tn), lambda j, i: (layer, 0, nt + j)),
        ],
        out_specs=pl.BlockSpec((FFN_TM, tn), lambda j, i: (i, j)),
        scratch_shapes=[pltpu.VMEM((D, tn), BF16), pltpu.VMEM((D, tn), BF16)],
        compiler_params=_cparams(("arbitrary", "arbitrary")),
        name="ffn_up",
    )(h, w_up, w_up, conv_w, conv_w, conv_b, conv_b)


def _ffn_down_kernel(a_ref, w_ref, x_ref, gt_ref, o_ref):
    o_ref[...] = x_ref[...] + gt_ref[0] * jnp.dot(a_ref[...], w_ref[0], preferred_element_type=F32)


def _ffn_down(act, w_down, x, mod, layer):
    tm, tn = 1024, 512
    return pl.pallas_call(
        _ffn_down_kernel,
        out_shape=jax.ShapeDtypeStruct((T, D), F32),
        grid=(D // tn, T // tm),
        in_specs=[
            pl.BlockSpec((tm, D_FF), lambda j, i: (i, 0)),
            pl.BlockSpec((1, D_FF, tn), lambda j, i: (layer, 0, j)),
            pl.BlockSpec((tm, tn), lambda j, i: (i, j)),
            pl.BlockSpec((1, 1, tn), lambda j, i: (layer * N_COND + _cond_row(i, tm), 0, 5 * (D // tn) + j)),
        ],
        out_specs=pl.BlockSpec((tm, tn), lambda j, i: (i, j)),
        compiler_params=_cparams(("arbitrary", "arbitrary")),
        name="ffn_down",
    )(act, w_down, x, mod)


ATT_SCALE = NA_DH ** -0.5


def _nt_dot(a, b):
    return lax.dot_general(a, b, (((1,), (1,)), ((), ())), preferred_element_type=F32)


def _head_slices():
    return [slice(h * NA_DH, (h + 1) * NA_DH) for h in range(NA_HEADS)]


def _ctx_attn_kernel(q_ref, k_ref, v_ref, o_ref, ko_ref, vo_ref):
    heads = _head_slices()
    q = [(q_ref[:, sl] * ATT_SCALE).astype(BF16) for sl in heads]
    s = [_nt_dot(q[h], k_ref[:, sl].astype(BF16)) for h, sl in enumerate(heads)]
    e = [x - jnp.max(x, axis=-1, keepdims=True) for x in s]
    e = [jnp.exp(x) for x in e]
    o = [jnp.dot(e[h].astype(BF16), v_ref[:, sl].astype(BF16), preferred_element_type=F32)
         for h, sl in enumerate(heads)]
    for h, sl in enumerate(heads):
        o_ref[:, sl] = (o[h] / jnp.sum(e[h], axis=-1, keepdims=True)).astype(BF16)
    ko_ref[0, 0] = k_ref[...]
    vo_ref[0, 0] = v_ref[...]


def _ctx_attention(proj, layer, caches):
    cache_shape = jax.ShapeDtypeStruct((BATCH, DEPTH, SEQ, NA_WIDTH), F32)
    cache_spec = pl.BlockSpec((1, 1, SEQ, NA_WIDTH), lambda b: (b, layer, 0, 0))
    qkv_specs = [pl.BlockSpec((SEQ, NA_WIDTH), lambda b, j=j: (b, j)) for j in range(3)]
    out_shape = [jax.ShapeDtypeStruct((T, NA_WIDTH), BF16), cache_shape, cache_shape]
    out_specs = [pl.BlockSpec((SEQ, NA_WIDTH), lambda b: (b, 0)), cache_spec, cache_spec]
    if caches is None:
        return pl.pallas_call(
            _ctx_attn_kernel, out_shape=out_shape, grid=(BATCH,), in_specs=qkv_specs, out_specs=out_specs,
            compiler_params=_cparams(("arbitrary",)), name="ctx_attention",
        )(proj, proj, proj)

    def body(q_ref, k_ref, v_ref, kc_ref, vc_ref, o_ref, ko_ref, vo_ref):
        del kc_ref, vc_ref
        _ctx_attn_kernel(q_ref, k_ref, v_ref, o_ref, ko_ref, vo_ref)

    any_spec = pl.BlockSpec(memory_space=pl.ANY)
    return pl.pallas_call(
        body, out_shape=out_shape, grid=(BATCH,), in_specs=qkv_specs + [any_spec, any_spec],
        out_specs=out_specs, input_output_aliases={3: 1, 4: 2},
        compiler_params=_cparams(("arbitrary",)), name="ctx_attention",
    )(proj, proj, proj, *caches)


NA_ROWS = DEC_SEQ // GRID_W
NA_WIN_TOK = NA_WIN_ROWS * GRID_W


def _na_window_start(r):
    return jnp.clip(r - NA_WIN_ROWS // 2, 0, NA_ROWS - NA_WIN_ROWS)


def _na_bias_table(rpb):
    col = np.arange(GRID_W)
    col_start = np.clip(col - NA_WIN_COLS // 2, 0, GRID_W - NA_WIN_COLS)
    col_mask = (col[None, :] >= col_start[:, None]) & (col[None, :] < col_start[:, None] + NA_WIN_COLS)
    rel_col = np.clip(col[None, :] - col[:, None] + NA_WIN_COLS - 1, 0, 2 * NA_WIN_COLS - 2)
    banded = jnp.where(col_mask, rpb[:, :, rel_col], NEG_INF)
    per_offset = [banded[:, NA_WIN_ROWS - 1 - d:2 * NA_WIN_ROWS - 1 - d] for d in range(NA_WIN_ROWS)]
    tab = jnp.stack(per_offset, axis=0)
    return tab.transpose(0, 1, 3, 2, 4).reshape(NA_WIN_ROWS, NA_HEADS, GRID_W, NA_WIN_TOK)


def _na_attn_kernel(q_ref, k_ref, v_ref, ck_ref, cv_ref, bias_ref, y_in_ref, o_ref):
    del y_in_ref
    r = pl.program_id(1)
    start = pl.multiple_of(_na_window_start(r) * GRID_W, GRID_W)
    win = pl.ds(start, NA_WIN_TOK)
    heads = _head_slices()
    nh = range(NA_HEADS)
    q = [(q_ref[:, sl] * ATT_SCALE).astype(BF16) for sl in heads]
    s_win = [_nt_dot(q[h], k_ref[win, sl].astype(BF16)) + bias_ref[0, h] for h, sl in enumerate(heads)]
    s_ctx = [_nt_dot(q[h], ck_ref[0, 0, :, sl].astype(BF16)) for h, sl in enumerate(heads)]
    m = [jnp.maximum(jnp.max(s_win[h], axis=-1, keepdims=True), jnp.max(s_ctx[h], axis=-1, keepdims=True))
         for h in nh]
    e_win = [jnp.exp(s_win[h] - m[h]) for h in nh]
    e_ctx = [jnp.exp(s_ctx[h] - m[h]) for h in nh]
    o_win = [jnp.dot(e_win[h].astype(BF16), v_ref[win, sl].astype(BF16), preferred_element_type=F32)
             for h, sl in enumerate(heads)]
    o_ctx = [jnp.dot(e_ctx[h].astype(BF16), cv_ref[0, 0, :, sl].astype(BF16), preferred_element_type=F32)
             for h, sl in enumerate(heads)]
    for h, sl in enumerate(heads):
        denom = jnp.sum(e_win[h], axis=-1, keepdims=True) + jnp.sum(e_ctx[h], axis=-1, keepdims=True)
        o_ref[:, sl] = ((o_win[h] + o_ctx[h]) / denom).astype(BF16)


def _na_attention(proj, cache_k, cache_v, rpb, y_na, layer):
    q_row0 = T_P // GRID_W
    seq0 = T_P // DEC_SEQ
    kv_spec = lambda j: pl.BlockSpec((DEC_SEQ, NA_WIDTH), lambda b, r: (seq0 + b, j))
    ctx_spec = pl.BlockSpec((1, 1, SEQ, NA_WIDTH), lambda b, r: (b, layer, 0, 0))
    return pl.pallas_call(
        _na_attn_kernel,
        out_shape=jax.ShapeDtypeStruct((T, NA_WIDTH), BF16),
        grid=(DEC_BATCH, NA_ROWS),
        in_specs=[
            pl.BlockSpec((GRID_W, NA_WIDTH), lambda b, r: (q_row0 + b * NA_ROWS + r, 0)),
            kv_spec(1), kv_spec(2), ctx_spec, ctx_spec,
            pl.BlockSpec((1, NA_HEADS, GRID_W, NA_WIN_TOK), lambda b, r: (r - _na_window_start(r), 0, 0, 0)),
            pl.BlockSpec(memory_space=pl.ANY),
        ],
        out_specs=pl.BlockSpec((GRID_W, NA_WIDTH), lambda b, r: (q_row0 + b * NA_ROWS + r, 0)),
        input_output_aliases={6: 0},
        compiler_params=_cparams(("arbitrary", "arbitrary")),
        name="na_attention",
    )(proj, proj, proj, cache_k.reshape(DEC_BATCH, DEPTH, SEQ, NA_WIDTH),
      cache_v.reshape(DEC_BATCH, DEPTH, SEQ, NA_WIDTH), _na_bias_table(rpb), y_na)


HY_CB = 256


def _split_bf16(x):
    hi = x.astype(BF16)
    return hi, (x - hi.astype(F32)).astype(BF16)


def _dot3(a_hi, a_lo, b):
    b_hi, b_lo = _split_bf16(b)
    return (jnp.dot(a_hi, b_hi, preferred_element_type=F32) + jnp.dot(a_lo, b_hi, preferred_element_type=F32)
            + jnp.dot(a_hi, b_lo, preferred_element_type=F32))


def _trig_table_kernel(hi_ref, lo_ref, *, length, freq_on_rows, scale):
    rows, cols = hi_ref.shape
    r = lax.broadcasted_iota(jnp.int32, (rows, cols), 0) + pl.program_id(0) * rows
    c = lax.broadcasted_iota(jnp.int32, (rows, cols), 1)
    f, t = (r, c) if freq_on_rows else (c, r)
    k = f & (length - 1)
    quarter = jnp.where(f >= length, length, 0)
    phase = (2 * k + 1) * t + (-quarter if freq_on_rows else quarter)
    phase = phase & (4 * length - 1)
    phase = jnp.where(phase >= 2 * length, phase - 4 * length, phase)
    val = jnp.cos(phase.astype(F32) * (math.pi / (2 * length))) * scale
    hi, lo = _split_bf16(val)
    hi_ref[...] = hi
    lo_ref[...] = lo


def _trig_table(length, freq_on_rows):
    shape = (2 * length, length) if freq_on_rows else (length, 2 * length)
    tr = 256
    spec = pl.BlockSpec((tr, shape[1]), lambda i: (i, 0))
    return pl.pallas_call(
        functools.partial(_trig_table_kernel, length=length, freq_on_rows=freq_on_rows,
                          scale=1.0 if freq_on_rows else 1.0 / length),
        out_shape=[jax.ShapeDtypeStruct(shape, BF16)] * 2,
        grid=(shape[0] // tr,), out_specs=[spec, spec],
        compiler_params=_cparams(("arbitrary",)), name="dft_table",
    )()


def _hy_filter_kernel(tn_ref, band_ref, w1_ref, b1_ref, fr_ref, w2_ref, b2_ref, w3_ref, dec_ref,
                      fhi_ref, flo_ref, o_ref, *, length):
    t_norm = tn_ref[...]
    t_idx = lax.broadcasted_iota(jnp.int32, (length, LANE), 0).astype(F32)
    lane = lax.broadcasted_iota(jnp.int32, (length, LANE), 1)
    ang = (2.0 * math.pi / length) * t_idx * band_ref[...]
    z = jnp.where(lane == 0, t_norm,
                  jnp.where(lane <= HY_BANDS, jnp.cos(ang), jnp.where(lane <= 2 * HY_BANDS, jnp.sin(ang), 0.0)))
    hdn = jnp.sin(fr_ref[0:1] * (jnp.dot(z, w1_ref[...], precision=HIGHEST, preferred_element_type=F32)
                                 + b1_ref[...]))
    hdn = jnp.sin(fr_ref[1:2] * (jnp.dot(hdn, w2_ref[...], precision=HIGHEST, preferred_element_type=F32)
                                 + b2_ref[...]))
    filt = jnp.dot(hdn, w3_ref[...], precision=HIGHEST, preferred_element_type=F32)
    filt = filt * jnp.exp(-t_norm * jnp.abs(dec_ref[...]))
    first = lax.broadcasted_iota(jnp.int32, (length, HY_WIDTH), 0) == 0
    for o in range(HY_ORDER):
        fwd = filt[:, (2 * o) * HY_WIDTH:(2 * o + 1) * HY_WIDTH]
        bwd = filt[:, (2 * o + 1) * HY_WIDTH:(2 * o + 2) * HY_WIDTH]
        bwd = jnp.where(first, 0.0, pltpu.roll(bwd, 1, 0))
        o_ref[o, :length] = _dot3(fhi_ref[:length], flo_ref[:length], fwd + bwd)
        o_ref[o, length:] = _dot3(fhi_ref[length:], flo_ref[length:], bwd - fwd)


def _hy_filter_spectrum(length, f_hi, f_lo, w1, b1, freq, w2, b2, w3, decay):
    emb = 1 + 2 * HY_BANDS
    t_norm = jnp.linspace(0.0, 1.0, length, dtype=F32).reshape(length, 1)
    bands = np.zeros((1, LANE), np.float32)
    bands[0, 1:1 + HY_BANDS] = bands[0, 1 + HY_BANDS:emb] = np.linspace(1e-4, HY_BANDS - 1, HY_BANDS,
                                                                        dtype=np.float32)
    w1p = jnp.zeros((LANE, w1.shape[1]), F32).at[:emb].set(w1)
    return pl.pallas_call(
        functools.partial(_hy_filter_kernel, length=length),
        out_shape=jax.ShapeDtypeStruct((HY_ORDER, 2 * length, HY_WIDTH), F32),
        compiler_params=pltpu.CompilerParams(vmem_limit_bytes=VMEM_LIMIT),
        name="hyena_filter",
    )(t_norm, jnp.asarray(bands), w1p, b1.reshape(1, -1), freq, w2, b2.reshape(1, -1), w3,
      decay.reshape(1, -1), f_hi, f_lo)


def _hyena_kernel(v_ref, x1_ref, x2_ref, cwv_ref, cw1_ref, cw2_ref, cbv_ref, cb1_ref, cb2_ref, skip_ref,
                  hs_ref, f_ref, g_ref, *rest, length):
    o_ref = rest[-1]
    row = lax.broadcasted_iota(jnp.int32, v_ref.shape, 0)
    first, last = row == 0, row == length - 1

    def short_conv(u_ref, cw_ref, cb_ref):
        u = u_ref[...]
        prev = jnp.where(first, 0.0, pltpu.roll(u, 1, 0))
        nxt = jnp.where(last, 0.0, pltpu.roll(u, length - 1, 0))
        return prev * cw_ref[0:1] + u * cw_ref[1:2] + nxt * cw_ref[2:3] + cb_ref[...]

    z = short_conv(v_ref, cwv_ref, cbv_ref)
    gates = (short_conv(x1_ref, cw1_ref, cb1_ref), short_conv(x2_ref, cw2_ref, cb2_ref))
    for o in range(HY_ORDER):
        zf = jnp.dot(f_ref[...], z.astype(BF16), preferred_element_type=F32)
        zc, zs = zf[:length], zf[length:]
        h_re, h_im = hs_ref[o, :length], hs_ref[o, length:]
        p = jnp.concatenate([zc * h_re + zs * h_im, zc * h_im - zs * h_re], axis=0)
        conv = jnp.dot(g_ref[...], p.astype(BF16), preferred_element_type=F32)
        z = gates[o] * (conv + z * skip_ref[o:o + 1])
    o_ref[...] = z.astype(BF16)


def _hyena(proj, y_hy, layer, length, tables, spectrum, conv_w, conv_b, skip):
    nseq = (T_P if length == SEQ else T_S) // length
    row0 = 0 if length == SEQ else T_P // length
    cb = HY_WIDTH if length == SEQ else HY_CB
    ncb = HY_WIDTH // cb
    col0 = C_HY // cb
    u_spec = lambda part: pl.BlockSpec((length, cb), lambda s, j: (row0 + s, col0 + part * ncb + j))
    cw_spec = lambda part: pl.BlockSpec((1, 3, cb), lambda s, j: (layer, 0, part * ncb + j))
    cb_spec = lambda part: pl.BlockSpec((1, 1, cb), lambda s, j: (layer, 0, part * ncb + j))
    resident = functools.partial(pl.BlockSpec, pipeline_mode=pl.Buffered(1))
    tab_specs = [resident(t.shape, lambda s, j: (0, 0)) for t in tables]
    in_specs = ([u_spec(0), u_spec(1), u_spec(2), cw_spec(0), cw_spec(1), cw_spec(2),
                 cb_spec(0), cb_spec(1), cb_spec(2),
                 pl.BlockSpec((1, HY_ORDER, cb), lambda s, j: (layer, 0, j)),
                 pl.BlockSpec((HY_ORDER, 2 * length, cb), lambda s, j: (0, 0, j))] + tab_specs)
    args = [proj] * 3 + [conv_w] * 3 + [conv_b.reshape(DEPTH, 1, -1)] * 3 + [skip, spectrum] + list(tables)
    aliases = {}
    if y_hy is not None:
        in_specs.append(pl.BlockSpec(memory_space=pl.ANY))
        args.append(y_hy)
        aliases = {len(args) - 1: 0}

    def body(v_ref, x1_ref, x2_ref, cwv, cw1, cw2, cbv, cb1, cb2, skip_ref, hs_ref, *rest):
        _hyena_kernel(v_ref, x1_ref, x2_ref, cwv.at[0], cw1.at[0], cw2.at[0], cbv.at[0], cb1.at[0], cb2.at[0],
                      skip_ref.at[0], hs_ref, *rest, length=length)

    return pl.pallas_call(
        body,
        out_shape=jax.ShapeDtypeStruct((T, HY_WIDTH), BF16),
        grid=(nseq, ncb), in_specs=in_specs,
        out_specs=pl.BlockSpec((length, cb), lambda s, j: (row0 + s, j)),
        input_output_aliases=aliases,
        compiler_params=_cparams(("arbitrary", "arbitrary")),
        name="hyena",
    )(*args)


GC = 128
GDN_SCALE = GDN_DK ** -0.5


def _merge_masks(lower):
    ri = lax.broadcasted_iota(jnp.int32, (GC, GC), 0)
    ci = lax.broadcasted_iota(jnp.int32, (GC, GC), 1)
    hi, lo = (ri, ci) if lower else (ci, ri)
    return [((hi >> (b + 1)) == (lo >> (b + 1))) & ((hi >> b) > (lo >> b)) for b in range(int(math.log2(GC)))]


def _unit_tri_inverse(a, masks):
    ri = lax.broadcasted_iota(jnp.int32, (GC, GC), 0)
    ci = lax.broadcasted_iota(jnp.int32, (GC, GC), 1)
    eye = (ri == ci).astype(F32)
    n = range(len(a))
    t = [eye - jnp.where(masks[i][0], a[i], 0.0) for i in n]
    for level in range(1, len(masks[0])):
        tb = [t[i].astype(BF16) for i in n]
        ta = [jnp.dot(tb[i], jnp.where(masks[i][level], a[i], 0.0).astype(BF16), preferred_element_type=F32)
              for i in n]
        tat = [jnp.dot(ta[i].astype(BF16), tb[i], preferred_element_type=F32) for i in n]
        t = [t[i] - tat[i] for i in n]
    return t


def _gdn_kernel(q_ref, k_ref, v_ref, z_ref, ba_ref, cwq_ref, cwk_ref, cwv_ref, alog_ref, dt_ref, ng_ref,
                sf0_ref, sb0_ref, *rest, length):
    y_ref, sf_ref, sb_ref, qn_ref, kn_ref, vn_ref, beta_ref, g_ref, of_ref, ob_ref, s_ref = rest[-11:]
    n_chunks = length // GC
    row = lax.broadcasted_iota(jnp.int32, (length, GDN_WIDTH), 0)
    first, last = row == 0, row == length - 1

    def conv_silu(u_ref, cw_ref):
        u = u_ref[...]
        prev = jnp.where(first, 0.0, pltpu.roll(u, 1, 0))
        nxt = jnp.where(last, 0.0, pltpu.roll(u, length - 1, 0))
        c = prev * cw_ref[0:1] + u * cw_ref[1:2] + nxt * cw_ref[2:3]
        return c * jax.nn.sigmoid(c)

    q = conv_silu(q_ref, cwq_ref)
    k = conv_silu(k_ref, cwk_ref)
    vn_ref[...] = conv_silu(v_ref, cwv_ref)
    for h in range(GDN_HEADS):
        sl = slice(h * GDN_DK, (h + 1) * GDN_DK)
        qh, kh = q[:, sl], k[:, sl]
        qn_ref[:, sl] = qh * lax.rsqrt(jnp.sum(qh * qh, axis=-1, keepdims=True) + NORM_EPS) * GDN_SCALE
        kn_ref[:, sl] = kh * lax.rsqrt(jnp.sum(kh * kh, axis=-1, keepdims=True) + NORM_EPS)
    ba = ba_ref[...]
    beta_ref[...] = jax.nn.sigmoid(ba)
    g_ref[...] = -jnp.exp(alog_ref[...]) * jax.nn.softplus(ba + dt_ref[...])

    ri = lax.broadcasted_iota(jnp.int32, (GC, GC), 0)
    ci = lax.broadcasted_iota(jnp.int32, (GC, GC), 1)
    lower = (ri >= ci).astype(F32)
    upper = (ri <= ci).astype(F32)

    def chunk_step(n, carry):
        rows = [pl.ds(pl.multiple_of((n if d == 0 else n_chunks - 1 - n) * GC, GC), GC) for d in range(2)]
        loaded = {}
        for d in range(2):
            loaded[d] = (g_ref[rows[d], :], beta_ref[rows[d], :])
            for h in range(GDN_HEADS):
                sl = slice(h * GDN_DK, (h + 1) * GDN_DK)
                loaded[d, h] = (qn_ref[rows[d], sl], kn_ref[rows[d], sl], vn_ref[rows[d], sl],
                                s_ref[d * GDN_HEADS + h])
        units = [(d, h) for d in range(2) for h in range(GDN_HEADS)]
        incl = [(ri >= ci), (ri <= ci)]
        strict = [(ri > ci), (ri < ci)]
        masks = [_merge_masks(True), _merge_masks(False)]
        gcs = [jnp.dot(tri, loaded[d][0], precision=HIGHEST, preferred_element_type=F32)
               for d, tri in enumerate((lower, upper))]
        gcs_t = [g.T for g in gcs]
        edge = [GC - 1, 0]
        g_col = [gcs[d][:, 8 + 4 * d + h:9 + 4 * d + h] for d, h in units]
        g_row = [gcs_t[d][8 + 4 * d + h:9 + 4 * d + h, :] for d, h in units]
        g_end = [gcs_t[d][8 + 4 * d + h:9 + 4 * d + h, edge[d]:edge[d] + 1] for d, h in units]
        b_col = [loaded[d][1][:, 4 * d + h:4 * d + h + 1] for d, h in units]
        qc, kc, vc, st = (list(x) for x in zip(*(loaded[u] for u in units)))
        nu = range(len(units))
        decay = [jnp.where(incl[d], jnp.exp(jnp.where(incl[d], g_col[i] - g_row[i], 0.0)), 0.0)
                 for i, (d, h) in enumerate(units)]
        kb = [kc[i] * b_col[i] for i in nu]
        kcb = [kc[i].astype(BF16) for i in nu]
        kk = [_nt_dot(kb[i].astype(BF16), kcb[i]) for i in nu]
        qk = [_nt_dot(qc[i].astype(BF16), kcb[i]) for i in nu]
        a = [jnp.where(strict[d], kk[i] * decay[i], 0.0) for i, (d, h) in enumerate(units)]
        t = _unit_tri_inverse(a, [masks[d] for d, h in units])
        e_col = [jnp.exp(g_col[i]) for i in nu]
        rhs = [jnp.concatenate([vc[i] * b_col[i], kb[i] * e_col[i]], axis=1).astype(BF16) for i in nu]
        sol = [jnp.dot(t[i].astype(BF16), rhs[i], preferred_element_type=F32) for i in nu]
        attn = [jnp.where(incl[d], qk[i] * decay[i], 0.0).astype(BF16) for i, (d, h) in enumerate(units)]
        sb = [st[i].astype(BF16) for i in nu]
        ws = [jnp.dot(sol[i][:, GDN_DV:].astype(BF16), sb[i], preferred_element_type=F32) for i in nu]
        qs = [jnp.dot((qc[i] * e_col[i]).astype(BF16), sb[i], preferred_element_type=F32) for i in nu]
        v_new = [(sol[i][:, :GDN_DV] - ws[i]).astype(BF16) for i in nu]
        av = [jnp.dot(attn[i], v_new[i], preferred_element_type=F32) for i in nu]
        k_dec_t = [(kc[i] * jnp.exp(g_end[i] - g_col[i])).T.astype(BF16) for i in nu]
        kv = [jnp.dot(k_dec_t[i], v_new[i], preferred_element_type=F32) for i in nu]
        results = [(qs[i] + av[i], st[i] * jnp.exp(g_end[i]) + kv[i]) for i in nu]
        for d in range(2):
            for h in range(GDN_HEADS):
                o, s = results[d * GDN_HEADS + h]
                (of_ref if d == 0 else ob_ref)[rows[d], h * GDN_DV:(h + 1) * GDN_DV] = o
                s_ref[d * GDN_HEADS + h] = s
        return carry

    s_ref[:GDN_HEADS] = sf0_ref[0]
    s_ref[GDN_HEADS:] = sb0_ref[0]
    lax.fori_loop(0, n_chunks, chunk_step, 0)
    sf_ref[0, 0] = s_ref[:GDN_HEADS]
    sb_ref[0, 0] = s_ref[GDN_HEADS:]
    for h in range(GDN_HEADS):
        sl = slice(h * GDN_DV, (h + 1) * GDN_DV)
        o = of_ref[:, sl] + ob_ref[:, sl]
        o = o * lax.rsqrt(jnp.mean(o * o, axis=-1, keepdims=True) + NORM_EPS) * ng_ref[...]
        zh = z_ref[:, sl]
        y_ref[:, sl] = (o * (zh * jax.nn.sigmoid(zh))).astype(BF16)


def _gdn(proj, y_gdn, states_out, layer, length, s_f0, s_b0, conv_w, a_log, dt_bias, norm_g):
    nseq = (T_P if length == SEQ else T_S) // length
    row0 = 0 if length == SEQ else T_P // length
    col = lambda c: c // GDN_WIDTH
    blk = lambda c: pl.BlockSpec((length, GDN_WIDTH), lambda s: (row0 + s, col(c)))
    cw = lambda part: pl.BlockSpec((1, 3, GDN_WIDTH), lambda s: (layer, 0, part))
    vec = lambda: pl.BlockSpec((1, 1, LANE), lambda s: (layer, 0, 0))
    st_in = pl.BlockSpec((1, GDN_HEADS, GDN_DK, GDN_DV), lambda s: (s, 0, 0, 0))
    st_out = pl.BlockSpec((1, 1, GDN_HEADS, GDN_DK, GDN_DV), lambda s: (s, layer, 0, 0, 0))
    pad_lanes = lambda x, off: jnp.zeros((DEPTH, 1, LANE), F32).at[:, 0, off:off + 2 * GDN_HEADS].set(
        x.reshape(DEPTH, -1))
    in_specs = [blk(C_GDN), blk(C_GDN + GDN_WIDTH), blk(C_GDN + 2 * GDN_WIDTH), blk(C_Z),
                pl.BlockSpec((length, LANE), lambda s: (row0 + s, C_BA // LANE)),
                cw(0), cw(1), cw(2), vec(), vec(), vec(), st_in, st_in]
    args = [proj] * 5 + [conv_w] * 3 + [pad_lanes(a_log, 2 * GDN_HEADS), pad_lanes(dt_bias, 2 * GDN_HEADS),
                                        norm_g.reshape(DEPTH, 1, GDN_DV), s_f0, s_b0]
    st_shape = jax.ShapeDtypeStruct((nseq, DEPTH, GDN_HEADS, GDN_DK, GDN_DV), F32)
    aliases = {}
    for carried, out_idx in ((y_gdn, 0),) + (((states_out[0], 1), (states_out[1], 2)) if states_out else ()):
        if carried is not None:
            in_specs.append(pl.BlockSpec(memory_space=pl.ANY))
            args.append(carried)
            aliases[len(args) - 1] = out_idx

    def body(q_ref, k_ref, v_ref, z_ref, ba_ref, cwq, cwk, cwv, alog, dt, ng, sf0, sb0, *rest):
        _gdn_kernel(q_ref, k_ref, v_ref, z_ref, ba_ref, cwq.at[0], cwk.at[0], cwv.at[0], alog.at[0], dt.at[0],
                    ng.at[0], sf0, sb0, *rest, length=length)

    seq_buf = lambda w: pltpu.VMEM((length, w), F32)
    return pl.pallas_call(
        body,
        out_shape=[jax.ShapeDtypeStruct((T, GDN_WIDTH), BF16), st_shape, st_shape],
        grid=(nseq,), in_specs=in_specs,
        out_specs=[pl.BlockSpec((length, GDN_WIDTH), lambda s: (row0 + s, 0)), st_out, st_out],
        scratch_shapes=[seq_buf(GDN_WIDTH), seq_buf(GDN_WIDTH), seq_buf(GDN_WIDTH), seq_buf(LANE), seq_buf(LANE),
                        seq_buf(GDN_WIDTH), seq_buf(GDN_WIDTH),
                        pltpu.VMEM((2 * GDN_HEADS, GDN_DK, GDN_DV), F32)],
        input_output_aliases=aliases,
        compiler_params=_cparams(("arbitrary",)),
        name="gated_deltanet",
    )(*args)


def kernel(x_prompt, x_sample, cache_k, cache_v, state_fwd, state_bwd, c, c_ctx, ln1_g, ln2_g, w_mod, b_mod,
           w_in, na_rpb, hy_conv_w, hy_conv_b, hy_w1, hy_b1, hy_freq, hy_w2, hy_b2, hy_w3, hy_decay, hy_skip,
           gdn_conv_w, gdn_a_log, gdn_dt_bias, gdn_norm_g, w_pa, w_pb, w_pc, b_gate, w_out, ffn_w_up,
           ffn_conv_w, ffn_conv_b, ffn_w_down, final_g):
    x = jnp.concatenate([x_prompt.reshape(T_P, D), x_sample.reshape(T_S, D)], axis=0)
    cond = jnp.concatenate([c_ctx[None], c, jnp.zeros((N_COND - 1 - DEC_BATCH, D), F32)], axis=0)
    mod = _mod_table(cond, w_mod, b_mod).reshape(DEPTH * N_COND, 1, 6 * D)

    w_pa_b, w_pb_b, w_pc_b, w_out_b = (w.astype(BF16) for w in (w_pa, w_pb, w_pc, w_out))
    w_down_b = ffn_w_down.astype(BF16)
    w_in_t = jnp.swapaxes(w_in, 1, 2)
    tables = {n: tuple(_trig_table(n, True)) + tuple(_trig_table(n, False)) for n in (SEQ, DEC_SEQ)}
    zero_state = jnp.zeros((BATCH, GDN_HEADS, GDN_DK, GDN_DV), F32)

    caches = None
    states = None
    for layer in range(DEPTH):
        h = _norm_mod(x, ln1_g, mod, layer, 0)
        proj = _in_proj(h, w_in_t, layer)

        y_na, new_k, new_v = _ctx_attention(proj, layer, caches)
        caches = (new_k, new_v)
        y_na = _na_attention(proj, cache_k, cache_v, na_rpb[layer], y_na, layer)

        y_hy = None
        for n in (SEQ, DEC_SEQ):
            spectrum = _hy_filter_spectrum(n, tables[n][0], tables[n][1], hy_w1[layer], hy_b1[layer],
                                           hy_freq[layer], hy_w2[layer], hy_b2[layer], hy_w3[layer],
                                           hy_decay[layer])
            y_hy = _hyena(proj, y_hy, layer, n, (tables[n][0], tables[n][2]), spectrum,
                          hy_conv_w, hy_conv_b, hy_skip)

        y_gdn, s_f, s_b = _gdn(proj, None, states, layer, SEQ, zero_state, zero_state,
                               gdn_conv_w, gdn_a_log, gdn_dt_bias, gdn_norm_g)
        states = (s_f, s_b)
        y_gdn = _gdn(proj, y_gdn, None, layer, DEC_SEQ, state_fwd[:, layer], state_bwd[:, layer],
                     gdn_conv_w, gdn_a_log, gdn_dt_bias, gdn_norm_g)[0]

        x = _mix_out(y_na, y_hy, y_gdn, proj, b_gate, w_pa_b, w_pb_b, w_pc_b, w_out_b, x, mod, layer)
        h = _norm_mod(x, ln2_g, mod, layer, 3)
        act = _ffn_up(h, ffn_w_up, ffn_conv_w, ffn_conv_b, layer)
        x = _ffn_down(act, w_down_b, x, mod, layer)

    y_p, y_s = _final_norm(x, final_g)
    cache_shape = (BATCH, DEPTH, SEQ, NA_HEADS, NA_DH)
    return (y_p.reshape(BATCH, SEQ, D), y_s.reshape(DEC_BATCH, DEC_SEQ, D),
            caches[0].reshape(cache_shape), caches[1].reshape(cache_shape), states[0], states[1])
```

```python
import functools
import math

import jax
import jax.numpy as jnp
import numpy as np
from jax import lax
from jax.experimental import pallas as pl
from jax.experimental.pallas import tpu as pltpu

F32 = jnp.float32
BF16 = jnp.bfloat16

D = 2048
BATCH, SEQ = 32, 256
DEC_BATCH, DEC_SEQ = 4, 1024
DEPTH = 2
GRID_W = 64
NA_HEADS, NA_DH = 8, 128
NA_WIDTH = NA_HEADS * NA_DH
NA_WIN_ROWS, NA_WIN_COLS = 8, 16
HY_WIDTH = 512
HY_ORDER = 2
HY_BANDS = 16
GDN_HEADS, GDN_DK, GDN_DV = 4, 128, 128
GDN_WIDTH = GDN_HEADS * GDN_DV
D_FF = 5632
N_BRANCH = 3
NORM_EPS = 1e-6
NEG_INF = -1e30

T_P = BATCH * SEQ
T_S = DEC_BATCH * DEC_SEQ
T = T_P + T_S
N_COND = 8

C_Q, C_K, C_V = 0, NA_WIDTH, 2 * NA_WIDTH
C_HY = 3 * NA_WIDTH
C_GDN = C_HY + 3 * HY_WIDTH
C_Z = C_GDN + 3 * GDN_HEADS * GDN_DK
C_BA = C_Z + GDN_WIDTH
C_GATE = C_BA + 4 * GDN_HEADS
N_IN = C_GATE + N_BRANCH * D
GATE_BLOCK = C_BA
GATE_SHIFT = C_GATE - C_BA

LANE = 128
VMEM_LIMIT = 56 * 1024 * 1024
HIGHEST = lax.Precision.HIGHEST


def _cparams(sem):
    return pltpu.CompilerParams(dimension_semantics=sem, vmem_limit_bytes=VMEM_LIMIT)


def _cond_row(tile, rows_per_tile):
    first_latent = T_P // rows_per_tile
    per_seq = DEC_SEQ // rows_per_tile
    return jnp.where(tile < first_latent, 0, 1 + (tile - first_latent) // per_seq)


def _mod_kernel(c_ref, w_ref, b_ref, o_ref):
    c = c_ref[...]
    s = (c * jax.nn.sigmoid(c)).astype(BF16)
    o_ref[0] = jnp.dot(s, w_ref[0].astype(BF16), preferred_element_type=F32) + b_ref[0]


def _mod_table(cond, w_mod, b_mod):
    tn = 1024
    n = 6 * D
    return pl.pallas_call(
        _mod_kernel,
        out_shape=jax.ShapeDtypeStruct((DEPTH, N_COND, n), F32),
        grid=(DEPTH, n // tn),
        in_specs=[
            pl.BlockSpec((N_COND, D), lambda l, j: (0, 0)),
            pl.BlockSpec((1, D, tn), lambda l, j: (l, 0, j)),
            pl.BlockSpec((1, 1, tn), lambda l, j: (l, 0, j)),
        ],
        out_specs=pl.BlockSpec((1, N_COND, tn), lambda l, j: (l, 0, j)),
        compiler_params=_cparams(("arbitrary", "arbitrary")),
        name="mod_table",
    )(cond, w_mod, b_mod.reshape(DEPTH, 1, n))


def _token_specs(x, tm):
    if not isinstance(x, tuple):
        return [pl.BlockSpec((tm, D), lambda i: (i, 0))], [x]
    n_ctx = T_P // tm
    return ([pl.BlockSpec((tm, D), lambda i: (jnp.minimum(i, n_ctx - 1), 0)),
             pl.BlockSpec((tm, D), lambda i: (jnp.maximum(i - n_ctx, 0), 0))], list(x))


def _token_tile(x_refs, tm):
    if len(x_refs) == 1:
        return x_refs[0][...]
    return jnp.where(pl.program_id(0) < T_P // tm, x_refs[0][...], x_refs[1][...])


def _modulated_norm(x, g, shift, scale):
    y = x * lax.rsqrt(jnp.mean(x * x, axis=-1, keepdims=True) + NORM_EPS) * g
    return (y * (1.0 + scale) + shift).astype(BF16)


NORM_TM = 1024


def _norm_mod_kernel(*refs):
    g_ref, sh_ref, sc_ref, o_ref = refs[-4:]
    o_ref[...] = _modulated_norm(_token_tile(refs[:-4], NORM_TM), g_ref[0], sh_ref[0], sc_ref[0])


def _norm_mod(x, g, mod, layer, shift_chunk):
    tm = NORM_TM
    x_specs, x_args = _token_specs(x, tm)
    return pl.pallas_call(
        _norm_mod_kernel,
        out_shape=jax.ShapeDtypeStruct((T, D), BF16),
        grid=(T // tm,),
        in_specs=x_specs + [
            pl.BlockSpec((1, 1, D), lambda i: (layer, 0, 0)),
            pl.BlockSpec((1, 1, D), lambda i: (layer * N_COND + _cond_row(i, tm), 0, shift_chunk)),
            pl.BlockSpec((1, 1, D), lambda i: (layer * N_COND + _cond_row(i, tm), 0, shift_chunk + 1)),
        ],
        out_specs=pl.BlockSpec((tm, D), lambda i: (i, 0)),
        compiler_params=_cparams(("arbitrary",)),
        name="norm_mod",
    )(*x_args, g.reshape(DEPTH, 1, D), mod, mod)


FINAL_TM = 1024


def _final_norm_kernel(x_ref, g_ref, yp_ref, ys_ref):
    x = x_ref[...]
    y = x * lax.rsqrt(jnp.mean(x * x, axis=-1, keepdims=True) + NORM_EPS) * g_ref[...]
    is_context = pl.program_id(0) < T_P // FINAL_TM

    @pl.when(is_context)
    def _():
        yp_ref[...] = y

    @pl.when(jnp.logical_not(is_context))
    def _():
        ys_ref[...] = y


def _final_norm(x, g):
    tm = FINAL_TM
    n_ctx = T_P // tm
    return pl.pallas_call(
        _final_norm_kernel,
        out_shape=[jax.ShapeDtypeStruct((T_P, D), F32), jax.ShapeDtypeStruct((T_S, D), F32)],
        grid=(T // tm,),
        in_specs=[pl.BlockSpec((tm, D), lambda i: (i, 0)), pl.BlockSpec((1, D), lambda i: (0, 0))],
        out_specs=[pl.BlockSpec((tm, D), lambda i: (jnp.minimum(i, n_ctx - 1), 0)),
                   pl.BlockSpec((tm, D), lambda i: (jnp.maximum(i - n_ctx, 0), 0))],
        compiler_params=_cparams(("arbitrary",)),
        name="final_norm",
    )(x, g.reshape(1, D))


def _proj_kernel(h_ref, wt_ref, o_ref, wb_ref):
    @pl.when(pl.program_id(1) == 0)
    def _():
        wb_ref[...] = wt_ref[0].astype(BF16)

    o_ref[...] = lax.dot_general(h_ref[...], wb_ref[...], (((1,), (1,)), ((), ())), preferred_element_type=F32)


def _in_proj(h, w_in_t, layer):
    tm, tn = 1024, 1024
    return pl.pallas_call(
        _proj_kernel,
        out_shape=jax.ShapeDtypeStruct((T, N_IN), F32),
        grid=(pl.cdiv(N_IN, tn), T // tm),
        in_specs=[
            pl.BlockSpec((tm, D), lambda j, i: (i, 0)),
            pl.BlockSpec((1, tn, D), lambda j, i: (layer, j, 0)),
        ],
        out_specs=pl.BlockSpec((tm, tn), lambda j, i: (i, j)),
        scratch_shapes=[pltpu.VMEM((tn, D), BF16)],
        compiler_params=_cparams(("arbitrary", "arbitrary")),
        name="in_proj",
    )(h, w_in_t)


MIX_TM = 256


def _mix_out_kernel(yna_ref, yhy_ref, ygdn_ref, gl_ref, bg_ref, wpa_ref, wpb_ref, wpc_ref, wout_ref, *refs):
    gt_ref, g2_ref, sh2_ref, sc2_ref, o_ref, h2_ref = refs[-6:]
    width = gl_ref.shape[1]
    gl = pltpu.roll(gl_ref[...], width - GATE_SHIFT, 1)[:, :N_BRANCH * D] + bg_ref[0]
    gates = jax.nn.sigmoid(gl)
    merged = (gates[:, :D] * jnp.dot(yna_ref[...], wpa_ref[0], preferred_element_type=F32)
              + gates[:, D:2 * D] * jnp.dot(yhy_ref[...], wpb_ref[0], preferred_element_type=F32)
              + gates[:, 2 * D:] * jnp.dot(ygdn_ref[...], wpc_ref[0], preferred_element_type=F32))
    r = jnp.dot(merged.astype(BF16), wout_ref[0], preferred_element_type=F32)
    x = _token_tile(refs[:-6], MIX_TM) + gt_ref[0] * r
    o_ref[...] = x
    h2_ref[...] = _modulated_norm(x, g2_ref[0], sh2_ref[0], sc2_ref[0])


def _mix_out(y_na, y_hy, y_gdn, proj, b_gate, w_pa, w_pb, w_pc, w_out, x, mod, ln2_g, layer):
    tm = MIX_TM
    resident = functools.partial(pl.BlockSpec, pipeline_mode=pl.Buffered(1))
    x_specs, x_args = _token_specs(x, tm)
    mod_spec = lambda chunk: pl.BlockSpec((1, 1, D), lambda i: (layer * N_COND + _cond_row(i, tm), 0, chunk))
    row_spec = pl.BlockSpec((tm, D), lambda i: (i, 0))
    return pl.pallas_call(
        _mix_out_kernel,
        out_shape=[jax.ShapeDtypeStruct((T, D), F32), jax.ShapeDtypeStruct((T, D), BF16)],
        grid=(T // tm,),
        in_specs=[
            pl.BlockSpec((tm, NA_WIDTH), lambda i: (i, 0)),
            pl.BlockSpec((tm, HY_WIDTH), lambda i: (i, 0)),
            pl.BlockSpec((tm, GDN_WIDTH), lambda i: (i, 0)),
            pl.BlockSpec((tm, GATE_BLOCK), lambda i: (i, 1)),
            pl.BlockSpec((1, 1, N_BRANCH * D), lambda i: (layer, 0, 0)),
            resident((1, NA_WIDTH, D), lambda i: (layer, 0, 0)),
            resident((1, HY_WIDTH, D), lambda i: (layer, 0, 0)),
            resident((1, GDN_WIDTH, D), lambda i: (layer, 0, 0)),
            resident((1, D, D), lambda i: (layer, 0, 0)),
        ] + x_specs + [mod_spec(2), pl.BlockSpec((1, 1, D), lambda i: (layer, 0, 0)), mod_spec(3), mod_spec(4)],
        out_specs=[row_spec, row_spec],
        compiler_params=_cparams(("arbitrary",)),
        name="mix_out",
    )(y_na, y_hy, y_gdn, proj, b_gate.reshape(DEPTH, 1, N_BRANCH * D), w_pa, w_pb, w_pc, w_out, *x_args,
      mod, ln2_g.reshape(DEPTH, 1, D), mod, mod)


FFN_TM = 1024


def _ffn_up_kernel(h_ref, wa_ref, wb_ref, cwa_ref, cwb_ref, cba_ref, cbb_ref, o_ref, wab_ref, wbb_ref):
    m = pl.program_id(1)

    @pl.when(m == 0)
    def _():
        wab_ref[...] = wa_ref[0].astype(BF16)
        wbb_ref[...] = wb_ref[0].astype(BF16)

    seq_len = jnp.where(m < T_P // FFN_TM, SEQ, DEC_SEQ)
    pos = lax.broadcasted_iota(jnp.int32, (FFN_TM, 1), 0) & (seq_len - 1)
    has_prev = pos != 0
    has_next = pos != seq_len - 1
    h = h_ref[...]

    def conv(w_ref, cw_ref, cb_ref):
        up = jnp.dot(h, w_ref[...], preferred_element_type=F32)
        prev = jnp.where(has_prev, pltpu.roll(up, 1, 0), 0.0)
        nxt = jnp.where(has_next, pltpu.roll(up, FFN_TM - 1, 0), 0.0)
        cw = cw_ref[0]
        return prev * cw[0:1] + up * cw[1:2] + nxt * cw[2:3] + cb_ref[0]

    ua = conv(wab_ref, cwa_ref, cba_ref)
    ub = conv(wbb_ref, cwb_ref, cbb_ref)
    o_ref[...] = (ua * jax.nn.sigmoid(ua) * ub).astype(BF16)


def _ffn_up(h, w_up, conv_w, conv_b, layer):
    tn = 512
    nt = D_FF // tn
    conv_b = conv_b.reshape(DEPTH, 1, 2 * D_FF)
    return pl.pallas_call(
        _ffn_up_kernel,
        out_shape=jax.ShapeDtypeStruct((T, D_FF), BF16),
        grid=(nt, T // FFN_TM),
        in_specs=[
            pl.BlockSpec((FFN_TM, D), lambda j, i: (i, 0)),
            pl.BlockSpec((1, D, tn), lambda j, i: (layer, 0, j)),
            pl.BlockSpec((1, D, tn), lambda j, i: (layer, 0, nt + j)),
            pl.BlockSpec((1, 3, tn), lambda j, i: (layer, 0, j)),
            pl.BlockSpec((1, 3, tn), lambda j, i: (layer, 0, nt + j)),
            pl.BlockSpec((1, 1, tn), lambda j, i: (layer, 0, j)),
            pl.BlockSpec((1, 1, tn), lambda j, i: (layer, 0, nt + j)),
        ],
        out_specs=pl.BlockSpec((FFN_TM, tn), lambda j, i: (i, j)),
        scratch_shapes=[pltpu.VMEM((D, tn), BF16), pltpu.VMEM((D, tn), BF16)],
        compiler_params=_cparams(("arbitrary", "arbitrary")),
        name="ffn_up",
    )(h, w_up, w_up, conv_w, conv_w, conv_b, conv_b)


def _ffn_down_kernel(a_ref, w_ref, x_ref, gt_ref, o_ref):
    o_ref[...] = x_ref[...] + gt_ref[0] * jnp.dot(a_ref[...], w_ref[0], preferred_element_type=F32)


def _ffn_down(act, w_down, x, mod, layer):
    tm, tn = 1024, 512
    return pl.pallas_call(
        _ffn_down_kernel,
        out_shape=jax.ShapeDtypeStruct((T, D), F32),
        grid=(D // tn, T // tm),
        in_specs=[
            pl.BlockSpec((tm, D_FF), lambda j, i: (i, 0)),
            pl.BlockSpec((1, D_FF, tn), lambda j, i: (layer, 0, j)),
            pl.BlockSpec((tm, tn), lambda j, i: (i, j)),
            pl.BlockSpec((1, 1, tn), lambda j, i: (layer * N_COND + _cond_row(i, tm), 0, 5 * (D // tn) + j)),
        ],
        out_specs=pl.BlockSpec((tm, tn), lambda j, i: (i, j)),
        compiler_params=_cparams(("arbitrary", "arbitrary")),
        name="ffn_down",
    )(act, w_down, x, mod)


ATT_SCALE = NA_DH ** -0.5


def _nt_dot(a, b):
    return lax.dot_general(a, b, (((1,), (1,)), ((), ())), preferred_element_type=F32)


def _head_slices():
    return [slice(h * NA_DH, (h + 1) * NA_DH) for h in range(NA_HEADS)]


def _ctx_attn_kernel(q_ref, k_ref, v_ref, o_ref, ko_ref, vo_ref):
    heads = _head_slices()
    q = [(q_ref[:, sl] * ATT_SCALE).astype(BF16) for sl in heads]
    s = [_nt_dot(q[h], k_ref[:, sl].astype(BF16)) for h, sl in enumerate(heads)]
    e = [x - jnp.max(x, axis=-1, keepdims=True) for x in s]
    e = [jnp.exp(x) for x in e]
    o = [jnp.dot(e[h].astype(BF16), v_ref[:, sl].astype(BF16), preferred_element_type=F32)
         for h, sl in enumerate(heads)]
    for h, sl in enumerate(heads):
        o_ref[:, sl] = (o[h] / jnp.sum(e[h], axis=-1, keepdims=True)).astype(BF16)
    ko_ref[0, 0] = k_ref[...]
    vo_ref[0, 0] = v_ref[...]


def _ctx_attention(proj, layer, caches):
    cache_shape = jax.ShapeDtypeStruct((BATCH, DEPTH, SEQ, NA_WIDTH), F32)
    cache_spec = pl.BlockSpec((1, 1, SEQ, NA_WIDTH), lambda b: (b, layer, 0, 0))
    qkv_specs = [pl.BlockSpec((SEQ, NA_WIDTH), lambda b, j=j: (b, j)) for j in range(3)]
    out_shape = [jax.ShapeDtypeStruct((T, NA_WIDTH), BF16), cache_shape, cache_shape]
    out_specs = [pl.BlockSpec((SEQ, NA_WIDTH), lambda b: (b, 0)), cache_spec, cache_spec]
    if caches is None:
        return pl.pallas_call(
            _ctx_attn_kernel, out_shape=out_shape, grid=(BATCH,), in_specs=qkv_specs, out_specs=out_specs,
            compiler_params=_cparams(("arbitrary",)), name="ctx_attention",
        )(proj, proj, proj)

    def body(q_ref, k_ref, v_ref, kc_ref, vc_ref, o_ref, ko_ref, vo_ref):
        del kc_ref, vc_ref
        _ctx_attn_kernel(q_ref, k_ref, v_ref, o_ref, ko_ref, vo_ref)

    any_spec = pl.BlockSpec(memory_space=pl.ANY)
    return pl.pallas_call(
        body, out_shape=out_shape, grid=(BATCH,), in_specs=qkv_specs + [any_spec, any_spec],
        out_specs=out_specs, input_output_aliases={3: 1, 4: 2},
        compiler_params=_cparams(("arbitrary",)), name="ctx_attention",
    )(proj, proj, proj, *caches)


NA_ROWS = DEC_SEQ // GRID_W
NA_WIN_TOK = NA_WIN_ROWS * GRID_W


def _na_window_start(r):
    return jnp.clip(r - NA_WIN_ROWS // 2, 0, NA_ROWS - NA_WIN_ROWS)


def _na_bias_table(rpb):
    col = np.arange(GRID_W)
    col_start = np.clip(col - NA_WIN_COLS // 2, 0, GRID_W - NA_WIN_COLS)
    col_mask = (col[None, :] >= col_start[:, None]) & (col[None, :] < col_start[:, None] + NA_WIN_COLS)
    rel_col = np.clip(col[None, :] - col[:, None] + NA_WIN_COLS - 1, 0, 2 * NA_WIN_COLS - 2)
    banded = jnp.where(col_mask, rpb[:, :, rel_col], NEG_INF)
    per_offset = [banded[:, NA_WIN_ROWS - 1 - d:2 * NA_WIN_ROWS - 1 - d] for d in range(NA_WIN_ROWS)]
    tab = jnp.stack(per_offset, axis=0)
    return tab.transpose(0, 1, 3, 2, 4).reshape(NA_WIN_ROWS, NA_HEADS, GRID_W, NA_WIN_TOK)


def _na_attn_kernel(q_ref, k_ref, v_ref, ck_ref, cv_ref, bias_ref, y_in_ref, o_ref):
    del y_in_ref
    r = pl.program_id(1)
    start = pl.multiple_of(_na_window_start(r) * GRID_W, GRID_W)
    win = pl.ds(start, NA_WIN_TOK)
    heads = _head_slices()
    nh = range(NA_HEADS)
    q = [(q_ref[:, sl] * ATT_SCALE).astype(BF16) for sl in heads]
    s_win = [_nt_dot(q[h], k_ref[win, sl].astype(BF16)) + bias_ref[0, h] for h, sl in enumerate(heads)]
    s_ctx = [_nt_dot(q[h], ck_ref[0, 0, :, sl].astype(BF16)) for h, sl in enumerate(heads)]
    m = [jnp.maximum(jnp.max(s_win[h], axis=-1, keepdims=True), jnp.max(s_ctx[h], axis=-1, keepdims=True))
         for h in nh]
    e_win = [jnp.exp(s_win[h] - m[h]) for h in nh]
    e_ctx = [jnp.exp(s_ctx[h] - m[h]) for h in nh]
    o_win = [jnp.dot(e_win[h].astype(BF16), v_ref[win, sl].astype(BF16), preferred_element_type=F32)
             for h, sl in enumerate(heads)]
    o_ctx = [jnp.dot(e_ctx[h].astype(BF16), cv_ref[0, 0, :, sl].astype(BF16), preferred_element_type=F32)
             for h, sl in enumerate(heads)]
    for h, sl in enumerate(heads):
        denom = jnp.sum(e_win[h], axis=-1, keepdims=True) + jnp.sum(e_ctx[h], axis=-1, keepdims=True)
        o_ref[:, sl] = ((o_win[h] + o_ctx[h]) / denom).astype(BF16)


def _na_attention(proj, cache_k, cache_v, rpb, y_na, layer):
    q_row0 = T_P // GRID_W
    seq0 = T_P // DEC_SEQ
    kv_spec = lambda j: pl.BlockSpec((DEC_SEQ, NA_WIDTH), lambda b, r: (seq0 + b, j))
    ctx_spec = pl.BlockSpec((1, 1, SEQ, NA_WIDTH), lambda b, r: (b, layer, 0, 0))
    return pl.pallas_call(
        _na_attn_kernel,
        out_shape=jax.ShapeDtypeStruct((T, NA_WIDTH), BF16),
        grid=(DEC_BATCH, NA_ROWS),
        in_specs=[
            pl.BlockSpec((GRID_W, NA_WIDTH), lambda b, r: (q_row0 + b * NA_ROWS + r, 0)),
            kv_spec(1), kv_spec(2), ctx_spec, ctx_spec,
            pl.BlockSpec((1, NA_HEADS, GRID_W, NA_WIN_TOK), lambda b, r: (r - _na_window_start(r), 0, 0, 0)),
            pl.BlockSpec(memory_space=pl.ANY),
        ],
        out_specs=pl.BlockSpec((GRID_W, NA_WIDTH), lambda b, r: (q_row0 + b * NA_ROWS + r, 0)),
        input_output_aliases={6: 0},
        compiler_params=_cparams(("arbitrary", "arbitrary")),
        name="na_attention",
    )(proj, proj, proj, cache_k.reshape(DEC_BATCH, DEPTH, SEQ, NA_WIDTH),
      cache_v.reshape(DEC_BATCH, DEPTH, SEQ, NA_WIDTH), _na_bias_table(rpb), y_na)


HY_CB = 256


def _split_bf16(x):
    hi = x.astype(BF16)
    return hi, (x - hi.astype(F32)).astype(BF16)


def _dot3(a_hi, a_lo, b):
    b_hi, b_lo = _split_bf16(b)
    return (jnp.dot(a_hi, b_hi, preferred_element_type=F32) + jnp.dot(a_lo, b_hi, preferred_element_type=F32)
            + jnp.dot(a_hi, b_lo, preferred_element_type=F32))


def _trig_table_kernel(*out_refs, length, freq_on_rows, scale):
    rows, cols = out_refs[0].shape
    r = lax.broadcasted_iota(jnp.int32, (rows, cols), 0) + pl.program_id(0) * rows
    c = lax.broadcasted_iota(jnp.int32, (rows, cols), 1)
    f, t = (r, c) if freq_on_rows else (c, r)
    k = f & (length - 1)
    quarter = jnp.where(f >= length, length, 0)
    phase = (2 * k + 1) * t + (-quarter if freq_on_rows else quarter)
    phase = phase & (4 * length - 1)
    phase = jnp.where(phase >= 2 * length, phase - 4 * length, phase)
    val = jnp.cos(phase.astype(F32) * (math.pi / (2 * length))) * scale
    for ref, part in zip(out_refs, _split_bf16(val)):
        ref[...] = part


def _trig_table(length, freq_on_rows):
    shape = (2 * length, length) if freq_on_rows else (length, 2 * length)
    tr = 256
    n_parts = 2 if freq_on_rows else 1
    spec = pl.BlockSpec((tr, shape[1]), lambda i: (i, 0))
    return pl.pallas_call(
        functools.partial(_trig_table_kernel, length=length, freq_on_rows=freq_on_rows,
                          scale=1.0 if freq_on_rows else 1.0 / length),
        out_shape=[jax.ShapeDtypeStruct(shape, BF16)] * n_parts,
        grid=(shape[0] // tr,), out_specs=[spec] * n_parts,
        compiler_params=_cparams(("arbitrary",)), name="dft_table",
    )()


def _hy_filter_kernel(tn_ref, band_ref, w1_ref, b1_ref, fr_ref, w2_ref, b2_ref, w3_ref, dec_ref,
                      fhi_ref, flo_ref, o_ref, *, length):
    t_norm = tn_ref[...]
    t_idx = lax.broadcasted_iota(jnp.int32, (length, LANE), 0).astype(F32)
    lane = lax.broadcasted_iota(jnp.int32, (length, LANE), 1)
    ang = (2.0 * math.pi / length) * t_idx * band_ref[...]
    z = jnp.where(lane == 0, t_norm,
                  jnp.where(lane <= HY_BANDS, jnp.cos(ang), jnp.where(lane <= 2 * HY_BANDS, jnp.sin(ang), 0.0)))
    hdn = jnp.sin(fr_ref[0:1] * (jnp.dot(z, w1_ref[...], precision=HIGHEST, preferred_element_type=F32)
                                 + b1_ref[...]))
    hdn = jnp.sin(fr_ref[1:2] * (jnp.dot(hdn, w2_ref[...], precision=HIGHEST, preferred_element_type=F32)
                                 + b2_ref[...]))
    filt = jnp.dot(hdn, w3_ref[...], precision=HIGHEST, preferred_element_type=F32)
    filt = filt * jnp.exp(-t_norm * jnp.abs(dec_ref[...]))
    first = lax.broadcasted_iota(jnp.int32, (length, HY_WIDTH), 0) == 0
    for o in range(HY_ORDER):
        fwd = filt[:, (2 * o) * HY_WIDTH:(2 * o + 1) * HY_WIDTH]
        bwd = filt[:, (2 * o + 1) * HY_WIDTH:(2 * o + 2) * HY_WIDTH]
        bwd = jnp.where(first, 0.0, pltpu.roll(bwd, 1, 0))
        o_ref[o, :length] = _dot3(fhi_ref[:length], flo_ref[:length], fwd + bwd)
        o_ref[o, length:] = _dot3(fhi_ref[length:], flo_ref[length:], bwd - fwd)


def _hy_filter_spectrum(length, f_hi, f_lo, w1, b1, freq, w2, b2, w3, decay):
    emb = 1 + 2 * HY_BANDS
    t_norm = jnp.linspace(0.0, 1.0, length, dtype=F32).reshape(length, 1)
    bands = np.zeros((1, LANE), np.float32)
    bands[0, 1:1 + HY_BANDS] = bands[0, 1 + HY_BANDS:emb] = np.linspace(1e-4, HY_BANDS - 1, HY_BANDS,
                                                                        dtype=np.float32)
    w1p = jnp.zeros((LANE, w1.shape[1]), F32).at[:emb].set(w1)
    return pl.pallas_call(
        functools.partial(_hy_filter_kernel, length=length),
        out_shape=jax.ShapeDtypeStruct((HY_ORDER, 2 * length, HY_WIDTH), F32),
        compiler_params=pltpu.CompilerParams(vmem_limit_bytes=VMEM_LIMIT),
        name="hyena_filter",
    )(t_norm, jnp.asarray(bands), w1p, b1.reshape(1, -1), freq, w2, b2.reshape(1, -1), w3,
      decay.reshape(1, -1), f_hi, f_lo)


def _hyena_kernel(v_ref, x1_ref, x2_ref, cwv_ref, cw1_ref, cw2_ref, cbv_ref, cb1_ref, cb2_ref, skip_ref,
                  hs_ref, f_ref, g_ref, *rest, length):
    o_ref = rest[-1]
    row = lax.broadcasted_iota(jnp.int32, v_ref.shape, 0)
    first, last = row == 0, row == length - 1

    def short_conv(u_ref, cw_ref, cb_ref):
        u = u_ref[...]
        prev = jnp.where(first, 0.0, pltpu.roll(u, 1, 0))
        nxt = jnp.where(last, 0.0, pltpu.roll(u, length - 1, 0))
        return prev * cw_ref[0:1] + u * cw_ref[1:2] + nxt * cw_ref[2:3] + cb_ref[...]

    z = short_conv(v_ref, cwv_ref, cbv_ref)
    gates = (short_conv(x1_ref, cw1_ref, cb1_ref), short_conv(x2_ref, cw2_ref, cb2_ref))
    for o in range(HY_ORDER):
        zf = jnp.dot(f_ref[...], z.astype(BF16), preferred_element_type=F32)
        zc, zs = zf[:length], zf[length:]
        h_re, h_im = hs_ref[o, :length], hs_ref[o, length:]
        p = jnp.concatenate([zc * h_re + zs * h_im, zc * h_im - zs * h_re], axis=0)
        conv = jnp.dot(g_ref[...], p.astype(BF16), preferred_element_type=F32)
        z = gates[o] * (conv + z * skip_ref[o:o + 1])
    o_ref[...] = z.astype(BF16)


def _hyena(proj, y_hy, layer, length, tables, spectrum, conv_w, conv_b, skip):
    nseq = (T_P if length == SEQ else T_S) // length
    row0 = 0 if length == SEQ else T_P // length
    cb = HY_WIDTH if length == SEQ else HY_CB
    ncb = HY_WIDTH // cb
    col0 = C_HY // cb
    u_spec = lambda part: pl.BlockSpec((length, cb), lambda s, j: (row0 + s, col0 + part * ncb + j))
    cw_spec = lambda part: pl.BlockSpec((1, 3, cb), lambda s, j: (layer, 0, part * ncb + j))
    cb_spec = lambda part: pl.BlockSpec((1, 1, cb), lambda s, j: (layer, 0, part * ncb + j))
    resident = functools.partial(pl.BlockSpec, pipeline_mode=pl.Buffered(1))
    tab_specs = [resident(t.shape, lambda s, j: (0, 0)) for t in tables]
    in_specs = ([u_spec(0), u_spec(1), u_spec(2), cw_spec(0), cw_spec(1), cw_spec(2),
                 cb_spec(0), cb_spec(1), cb_spec(2),
                 pl.BlockSpec((1, HY_ORDER, cb), lambda s, j: (layer, 0, j)),
                 pl.BlockSpec((HY_ORDER, 2 * length, cb), lambda s, j: (0, 0, j))] + tab_specs)
    args = [proj] * 3 + [conv_w] * 3 + [conv_b.reshape(DEPTH, 1, -1)] * 3 + [skip, spectrum] + list(tables)
    aliases = {}
    if y_hy is not None:
        in_specs.append(pl.BlockSpec(memory_space=pl.ANY))
        args.append(y_hy)
        aliases = {len(args) - 1: 0}

    def body(v_ref, x1_ref, x2_ref, cwv, cw1, cw2, cbv, cb1, cb2, skip_ref, hs_ref, *rest):
        _hyena_kernel(v_ref, x1_ref, x2_ref, cwv.at[0], cw1.at[0], cw2.at[0], cbv.at[0], cb1.at[0], cb2.at[0],
                      skip_ref.at[0], hs_ref, *rest, length=length)

    return pl.pallas_call(
        body,
        out_shape=jax.ShapeDtypeStruct((T, HY_WIDTH), BF16),
        grid=(nseq, ncb), in_specs=in_specs,
        out_specs=pl.BlockSpec((length, cb), lambda s, j: (row0 + s, j)),
        input_output_aliases=aliases,
        compiler_params=_cparams(("arbitrary", "arbitrary")),
        name="hyena",
    )(*args)


GC = 128
GDN_SCALE = GDN_DK ** -0.5


def _merge_masks(lower):
    ri = lax.broadcasted_iota(jnp.int32, (GC, GC), 0)
    ci = lax.broadcasted_iota(jnp.int32, (GC, GC), 1)
    hi, lo = (ri, ci) if lower else (ci, ri)
    return [((hi >> (b + 1)) == (lo >> (b + 1))) & ((hi >> b) > (lo >> b)) for b in range(int(math.log2(GC)))]


def _unit_tri_inverse(a, masks):
    ri = lax.broadcasted_iota(jnp.int32, (GC, GC), 0)
    ci = lax.broadcasted_iota(jnp.int32, (GC, GC), 1)
    eye = (ri == ci).astype(F32)
    n = range(len(a))
    t = [eye - jnp.where(masks[i][0], a[i], 0.0) for i in n]
    for level in range(1, len(masks[0])):
        tb = [t[i].astype(BF16) for i in n]
        ta = [jnp.dot(tb[i], jnp.where(masks[i][level], a[i], 0.0).astype(BF16), preferred_element_type=F32)
              for i in n]
        tat = [jnp.dot(ta[i].astype(BF16), tb[i], preferred_element_type=F32) for i in n]
        t = [t[i] - tat[i] for i in n]
    return t


def _gdn_kernel(q_ref, k_ref, v_ref, z_ref, ba_ref, cwq_ref, cwk_ref, cwv_ref, alog_ref, dt_ref, ng_ref,
                sf0_ref, sb0_ref, *rest, length):
    y_ref, sf_ref, sb_ref, qn_ref, kn_ref, vn_ref, beta_ref, g_ref, of_ref, ob_ref, s_ref = rest[-11:]
    n_chunks = length // GC
    row = lax.broadcasted_iota(jnp.int32, (length, GDN_WIDTH), 0)
    first, last = row == 0, row == length - 1

    def conv_silu(u_ref, cw_ref):
        u = u_ref[...]
        prev = jnp.where(first, 0.0, pltpu.roll(u, 1, 0))
        nxt = jnp.where(last, 0.0, pltpu.roll(u, length - 1, 0))
        c = prev * cw_ref[0:1] + u * cw_ref[1:2] + nxt * cw_ref[2:3]
        return c * jax.nn.sigmoid(c)

    q = conv_silu(q_ref, cwq_ref)
    k = conv_silu(k_ref, cwk_ref)
    vn_ref[...] = conv_silu(v_ref, cwv_ref)
    for h in range(GDN_HEADS):
        sl = slice(h * GDN_DK, (h + 1) * GDN_DK)
        qh, kh = q[:, sl], k[:, sl]
        qn_ref[:, sl] = qh * lax.rsqrt(jnp.sum(qh * qh, axis=-1, keepdims=True) + NORM_EPS) * GDN_SCALE
        kn_ref[:, sl] = kh * lax.rsqrt(jnp.sum(kh * kh, axis=-1, keepdims=True) + NORM_EPS)
    ba = ba_ref[...]
    beta_ref[...] = jax.nn.sigmoid(ba)
    g_ref[...] = -jnp.exp(alog_ref[...]) * jax.nn.softplus(ba + dt_ref[...])

    ri = lax.broadcasted_iota(jnp.int32, (GC, GC), 0)
    ci = lax.broadcasted_iota(jnp.int32, (GC, GC), 1)
    lower = (ri >= ci).astype(F32)
    upper = (ri <= ci).astype(F32)

    def chunk_step(n, carry):
        rows = [pl.ds(pl.multiple_of((n if d == 0 else n_chunks - 1 - n) * GC, GC), GC) for d in range(2)]
        loaded = {}
        for d in range(2):
            loaded[d] = (g_ref[rows[d], :], beta_ref[rows[d], :])
            for h in range(GDN_HEADS):
                sl = slice(h * GDN_DK, (h + 1) * GDN_DK)
                loaded[d, h] = (qn_ref[rows[d], sl], kn_ref[rows[d], sl], vn_ref[rows[d], sl],
                                s_ref[d * GDN_HEADS + h])
        units = [(d, h) for d in range(2) for h in range(GDN_HEADS)]
        incl = [(ri >= ci), (ri <= ci)]
        strict = [(ri > ci), (ri < ci)]
        masks = [_merge_masks(True), _merge_masks(False)]
        gcs = [jnp.dot(tri, loaded[d][0], precision=HIGHEST, preferred_element_type=F32)
               for d, tri in enumerate((lower, upper))]
        gcs_t = [g.T for g in gcs]
        edge = [GC - 1, 0]
        g_col = [gcs[d][:, 8 + 4 * d + h:9 + 4 * d + h] for d, h in units]
        g_row = [gcs_t[d][8 + 4 * d + h:9 + 4 * d + h, :] for d, h in units]
        g_end = [gcs_t[d][8 + 4 * d + h:9 + 4 * d + h, edge[d]:edge[d] + 1] for d, h in units]
        b_col = [loaded[d][1][:, 4 * d + h:4 * d + h + 1] for d, h in units]
        qc, kc, vc, st = (list(x) for x in zip(*(loaded[u] for u in units)))
        nu = range(len(units))
        decay = [jnp.where(incl[d], jnp.exp(jnp.where(incl[d], g_col[i] - g_row[i], 0.0)), 0.0)
                 for i, (d, h) in enumerate(units)]
        kb = [kc[i] * b_col[i] for i in nu]
        kcb = [kc[i].astype(BF16) for i in nu]
        kk = [_nt_dot(kb[i].astype(BF16), kcb[i]) for i in nu]
        qk = [_nt_dot(qc[i].astype(BF16), kcb[i]) for i in nu]
        a = [jnp.where(strict[d], kk[i] * decay[i], 0.0) for i, (d, h) in enumerate(units)]
        t = _unit_tri_inverse(a, [masks[d] for d, h in units])
        e_col = [jnp.exp(g_col[i]) for i in nu]
        rhs = [jnp.concatenate([vc[i] * b_col[i], kb[i] * e_col[i]], axis=1).astype(BF16) for i in nu]
        sol = [jnp.dot(t[i].astype(BF16), rhs[i], preferred_element_type=F32) for i in nu]
        attn = [jnp.where(incl[d], qk[i] * decay[i], 0.0).astype(BF16) for i, (d, h) in enumerate(units)]
        sb = [st[i].astype(BF16) for i in nu]
        ws = [jnp.dot(sol[i][:, GDN_DV:].astype(BF16), sb[i], preferred_element_type=F32) for i in nu]
        qs = [jnp.dot((qc[i] * e_col[i]).astype(BF16), sb[i], preferred_element_type=F32) for i in nu]
        v_new = [(sol[i][:, :GDN_DV] - ws[i]).astype(BF16) for i in nu]
        av = [jnp.dot(attn[i], v_new[i], preferred_element_type=F32) for i in nu]
        k_dec_t = [(kc[i] * jnp.exp(g_end[i] - g_col[i])).T.astype(BF16) for i in nu]
        kv = [jnp.dot(k_dec_t[i], v_new[i], preferred_element_type=F32) for i in nu]
        results = [(qs[i] + av[i], st[i] * jnp.exp(g_end[i]) + kv[i]) for i in nu]
        for d in range(2):
            for h in range(GDN_HEADS):
                o, s = results[d * GDN_HEADS + h]
                (of_ref if d == 0 else ob_ref)[rows[d], h * GDN_DV:(h + 1) * GDN_DV] = o
                s_ref[d * GDN_HEADS + h] = s
        return carry

    s_ref[:GDN_HEADS] = sf0_ref[0]
    s_ref[GDN_HEADS:] = sb0_ref[0]
    lax.fori_loop(0, n_chunks, chunk_step, 0)
    sf_ref[0, 0] = s_ref[:GDN_HEADS]
    sb_ref[0, 0] = s_ref[GDN_HEADS:]
    for h in range(GDN_HEADS):
        sl = slice(h * GDN_DV, (h + 1) * GDN_DV)
        o = of_ref[:, sl] + ob_ref[:, sl]
        o = o * lax.rsqrt(jnp.mean(o * o, axis=-1, keepdims=True) + NORM_EPS) * ng_ref[...]
        zh = z_ref[:, sl]
        y_ref[:, sl] = (o * (zh * jax.nn.sigmoid(zh))).astype(BF16)


def _gdn(proj, y_gdn, states_out, layer, length, s_f0, s_b0, conv_w, a_log, dt_bias, norm_g):
    nseq = (T_P if length == SEQ else T_S) // length
    row0 = 0 if length == SEQ else T_P // length
    col = lambda c: c // GDN_WIDTH
    blk = lambda c: pl.BlockSpec((length, GDN_WIDTH), lambda s: (row0 + s, col(c)))
    cw = lambda part: pl.BlockSpec((1, 3, GDN_WIDTH), lambda s: (layer, 0, part))
    vec = lambda: pl.BlockSpec((1, 1, LANE), lambda s: (layer, 0, 0))
    st_in = pl.BlockSpec((1, GDN_HEADS, GDN_DK, GDN_DV), lambda s: (s, 0, 0, 0))
    st_out = pl.BlockSpec((1, 1, GDN_HEADS, GDN_DK, GDN_DV), lambda s: (s, layer, 0, 0, 0))
    pad_lanes = lambda x, off: jnp.zeros((DEPTH, 1, LANE), F32).at[:, 0, off:off + 2 * GDN_HEADS].set(
        x.reshape(DEPTH, -1))
    in_specs = [blk(C_GDN), blk(C_GDN + GDN_WIDTH), blk(C_GDN + 2 * GDN_WIDTH), blk(C_Z),
                pl.BlockSpec((length, LANE), lambda s: (row0 + s, C_BA // LANE)),
                cw(0), cw(1), cw(2), vec(), vec(), vec(), st_in, st_in]
    args = [proj] * 5 + [conv_w] * 3 + [pad_lanes(a_log, 2 * GDN_HEADS), pad_lanes(dt_bias, 2 * GDN_HEADS),
                                        norm_g.reshape(DEPTH, 1, GDN_DV), s_f0, s_b0]
    st_shape = jax.ShapeDtypeStruct((nseq, DEPTH, GDN_HEADS, GDN_DK, GDN_DV), F32)
    aliases = {}
    for carried, out_idx in ((y_gdn, 0),) + (((states_out[0], 1), (states_out[1], 2)) if states_out else ()):
        if carried is not None:
            in_specs.append(pl.BlockSpec(memory_space=pl.ANY))
            args.append(carried)
            aliases[len(args) - 1] = out_idx

    def body(q_ref, k_ref, v_ref, z_ref, ba_ref, cwq, cwk, cwv, alog, dt, ng, sf0, sb0, *rest):
        _gdn_kernel(q_ref, k_ref, v_ref, z_ref, ba_ref, cwq.at[0], cwk.at[0], cwv.at[0], alog.at[0], dt.at[0],
                    ng.at[0], sf0, sb0, *rest, length=length)

    seq_buf = lambda w: pltpu.VMEM((length, w), F32)
    return pl.pallas_call(
        body,
        out_shape=[jax.ShapeDtypeStruct((T, GDN_WIDTH), BF16), st_shape, st_shape],
        grid=(nseq,), in_specs=in_specs,
        out_specs=[pl.BlockSpec((length, GDN_WIDTH), lambda s: (row0 + s, 0)), st_out, st_out],
        scratch_shapes=[seq_buf(GDN_WIDTH), seq_buf(GDN_WIDTH), seq_buf(GDN_WIDTH), seq_buf(LANE), seq_buf(LANE),
                        seq_buf(GDN_WIDTH), seq_buf(GDN_WIDTH),
                        pltpu.VMEM((2 * GDN_HEADS, GDN_DK, GDN_DV), F32)],
        input_output_aliases=aliases,
        compiler_params=_cparams(("arbitrary",)),
        name="gated_deltanet",
    )(*args)


def kernel(x_prompt, x_sample, cache_k, cache_v, state_fwd, state_bwd, c, c_ctx, ln1_g, ln2_g, w_mod, b_mod,
           w_in, na_rpb, hy_conv_w, hy_conv_b, hy_w1, hy_b1, hy_freq, hy_w2, hy_b2, hy_w3, hy_decay, hy_skip,
           gdn_conv_w, gdn_a_log, gdn_dt_bias, gdn_norm_g, w_pa, w_pb, w_pc, b_gate, w_out, ffn_w_up,
           ffn_conv_w, ffn_conv_b, ffn_w_down, final_g):
    x = (x_prompt.reshape(T_P, D), x_sample.reshape(T_S, D))
    cond = jnp.concatenate([c_ctx[None], c, jnp.zeros((N_COND - 1 - DEC_BATCH, D), F32)], axis=0)
    mod = _mod_table(cond, w_mod, b_mod).reshape(DEPTH * N_COND, 1, 6 * D)

    w_pa_b, w_pb_b, w_pc_b, w_out_b = (w.astype(BF16) for w in (w_pa, w_pb, w_pc, w_out))
    w_down_b = ffn_w_down.astype(BF16)
    w_in_t = jnp.swapaxes(w_in, 1, 2)
    tables = {n: tuple(_trig_table(n, True)) + tuple(_trig_table(n, False)) for n in (SEQ, DEC_SEQ)}
    zero_state = jnp.zeros((BATCH, GDN_HEADS, GDN_DK, GDN_DV), F32)

    caches = None
    states = None
    for layer in range(DEPTH):
        h = _norm_mod(x, ln1_g, mod, layer, 0)
        proj = _in_proj(h, w_in_t, layer)

        y_na, new_k, new_v = _ctx_attention(proj, layer, caches)
        caches = (new_k, new_v)
        y_na = _na_attention(proj, cache_k, cache_v, na_rpb[layer], y_na, layer)

        y_hy = None
        for n in (SEQ, DEC_SEQ):
            spectrum = _hy_filter_spectrum(n, tables[n][0], tables[n][1], hy_w1[layer], hy_b1[layer],
                                           hy_freq[layer], hy_w2[layer], hy_b2[layer], hy_w3[layer],
                                           hy_decay[layer])
            y_hy = _hyena(proj, y_hy, layer, n, (tables[n][0], tables[n][2]), spectrum,
                          hy_conv_w, hy_conv_b, hy_skip)

        y_gdn, s_f, s_b = _gdn(proj, None, states, layer, SEQ, zero_state, zero_state,
                               gdn_conv_w, gdn_a_log, gdn_dt_bias, gdn_norm_g)
        states = (s_f, s_b)
        y_gdn = _gdn(proj, y_gdn, None, layer, DEC_SEQ, state_fwd[:, layer], state_bwd[:, layer],
                     gdn_conv_w, gdn_a_log, gdn_dt_bias, gdn_norm_g)[0]

        x, h = _mix_out(y_na, y_hy, y_gdn, proj, b_gate, w_pa_b, w_pb_b, w_pc_b, w_out_b, x, mod, ln2_g, layer)
        act = _ffn_up(h, ffn_w_up, ffn_conv_w, ffn_conv_b, layer)
        x = _ffn_down(act, w_down_b, x, mod, layer)

    y_p, y_s = _final_norm(x, final_g)
    cache_shape = (BATCH, DEPTH, SEQ, NA_HEADS, NA_DH)
    return (y_p.reshape(BATCH, SEQ, D), y_s.reshape(DEC_BATCH, DEC_SEQ, D),
            caches[0].reshape(cache_shape), caches[1].reshape(cache_shape), states[0], states[1])
```

```python
import functools
import math

import jax
import jax.numpy as jnp
import numpy as np
from jax import lax
from jax.experimental import pallas as pl
from jax.experimental.pallas import tpu as pltpu

F32 = jnp.float32
BF16 = jnp.bfloat16

D = 2048
BATCH, SEQ = 32, 256
DEC_BATCH, DEC_SEQ = 4, 1024
DEPTH = 2
GRID_W = 64
NA_HEADS, NA_DH = 8, 128
NA_WIDTH = NA_HEADS * NA_DH
NA_WIN_ROWS, NA_WIN_COLS = 8, 16
HY_WIDTH = 512
HY_ORDER = 2
HY_BANDS = 16
GDN_HEADS, GDN_DK, GDN_DV = 4, 128, 128
GDN_WIDTH = GDN_HEADS * GDN_DV
D_FF = 5632
N_BRANCH = 3
NORM_EPS = 1e-6
NEG_INF = -1e30

T_P = BATCH * SEQ
T_S = DEC_BATCH * DEC_SEQ
T = T_P + T_S
N_COND = 8

C_Q, C_K, C_V = 0, NA_WIDTH, 2 * NA_WIDTH
C_HY = 3 * NA_WIDTH
C_GDN = C_HY + 3 * HY_WIDTH
C_Z = C_GDN + 3 * GDN_HEADS * GDN_DK
C_BA = C_Z + GDN_WIDTH
C_GATE = C_BA + 4 * GDN_HEADS
N_IN = C_GATE + N_BRANCH * D
GATE_BLOCK = C_BA
GATE_SHIFT = C_GATE - C_BA

LANE = 128
VMEM_LIMIT = 56 * 1024 * 1024
HIGHEST = lax.Precision.HIGHEST


def _cparams(sem):
    return pltpu.CompilerParams(dimension_semantics=sem, vmem_limit_bytes=VMEM_LIMIT)


def _cond_row(tile, rows_per_tile):
    first_latent = T_P // rows_per_tile
    per_seq = DEC_SEQ // rows_per_tile
    return jnp.where(tile < first_latent, 0, 1 + (tile - first_latent) // per_seq)


def _mod_kernel(c_ref, w_ref, b_ref, o_ref):
    c = c_ref[...]
    s = (c * jax.nn.sigmoid(c)).astype(BF16)
    o_ref[0] = jnp.dot(s, w_ref[0].astype(BF16), preferred_element_type=F32) + b_ref[0]


def _mod_table(cond, w_mod, b_mod):
    tn = 1024
    n = 6 * D
    return pl.pallas_call(
        _mod_kernel,
        out_shape=jax.ShapeDtypeStruct((DEPTH, N_COND, n), F32),
        grid=(DEPTH, n // tn),
        in_specs=[
            pl.BlockSpec((N_COND, D), lambda l, j: (0, 0)),
            pl.BlockSpec((1, D, tn), lambda l, j: (l, 0, j)),
            pl.BlockSpec((1, 1, tn), lambda l, j: (l, 0, j)),
        ],
        out_specs=pl.BlockSpec((1, N_COND, tn), lambda l, j: (l, 0, j)),
        compiler_params=_cparams(("arbitrary", "arbitrary")),
        name="mod_table",
    )(cond, w_mod, b_mod.reshape(DEPTH, 1, n))


def _token_specs(x, tm):
    if not isinstance(x, tuple):
        return [pl.BlockSpec((tm, D), lambda i: (i, 0))], [x]
    n_ctx = T_P // tm
    return ([pl.BlockSpec((tm, D), lambda i: (jnp.minimum(i, n_ctx - 1), 0)),
             pl.BlockSpec((tm, D), lambda i: (jnp.maximum(i - n_ctx, 0), 0))], list(x))


def _token_tile(x_refs, tm):
    if len(x_refs) == 1:
        return x_refs[0][...]
    return jnp.where(pl.program_id(0) < T_P // tm, x_refs[0][...], x_refs[1][...])


def _modulated_norm(x, g, shift, scale):
    y = x * lax.rsqrt(jnp.mean(x * x, axis=-1, keepdims=True) + NORM_EPS) * g
    return (y * (1.0 + scale) + shift).astype(BF16)


NORM_TM = 1024


def _norm_mod_kernel(*refs):
    g_ref, sh_ref, sc_ref, o_ref = refs[-4:]
    o_ref[...] = _modulated_norm(_token_tile(refs[:-4], NORM_TM), g_ref[0], sh_ref[0], sc_ref[0])


def _norm_mod(x, g, mod, layer, shift_chunk):
    tm = NORM_TM
    x_specs, x_args = _token_specs(x, tm)
    return pl.pallas_call(
        _norm_mod_kernel,
        out_shape=jax.ShapeDtypeStruct((T, D), BF16),
        grid=(T // tm,),
        in_specs=x_specs + [
            pl.BlockSpec((1, 1, D), lambda i: (layer, 0, 0)),
            pl.BlockSpec((1, 1, D), lambda i: (layer * N_COND + _cond_row(i, tm), 0, shift_chunk)),
            pl.BlockSpec((1, 1, D), lambda i: (layer * N_COND + _cond_row(i, tm), 0, shift_chunk + 1)),
        ],
        out_specs=pl.BlockSpec((tm, D), lambda i: (i, 0)),
        compiler_params=_cparams(("arbitrary",)),
        name="norm_mod",
    )(*x_args, g.reshape(DEPTH, 1, D), mod, mod)


def _proj_kernel(h_ref, wt_ref, o_ref, wb_ref):
    @pl.when(pl.program_id(1) == 0)
    def _():
        wb_ref[...] = wt_ref[0].astype(BF16)

    o_ref[...] = lax.dot_general(h_ref[...], wb_ref[...], (((1,), (1,)), ((), ())), preferred_element_type=F32)


def _in_proj(h, w_in_t, layer):
    tm, tn = 1024, 1024
    return pl.pallas_call(
        _proj_kernel,
        out_shape=jax.ShapeDtypeStruct((T, N_IN), F32),
        grid=(pl.cdiv(N_IN, tn), T // tm),
        in_specs=[
            pl.BlockSpec((tm, D), lambda j, i: (i, 0)),
            pl.BlockSpec((1, tn, D), lambda j, i: (layer, j, 0)),
        ],
        out_specs=pl.BlockSpec((tm, tn), lambda j, i: (i, j)),
        scratch_shapes=[pltpu.VMEM((tn, D), BF16)],
        compiler_params=_cparams(("arbitrary", "arbitrary")),
        name="in_proj",
    )(h, w_in_t)


MIX_TM = 256


def _mix_out_kernel(yna_ref, yhy_ref, ygdn_ref, gl_ref, bg_ref, wpa_ref, wpb_ref, wpc_ref, wout_ref, *refs):
    gt_ref, g2_ref, sh2_ref, sc2_ref, o_ref, h2_ref = refs[-6:]
    width = gl_ref.shape[1]
    gl = pltpu.roll(gl_ref[...], width - GATE_SHIFT, 1)[:, :N_BRANCH * D] + bg_ref[0]
    gates = jax.nn.sigmoid(gl)
    merged = (gates[:, :D] * jnp.dot(yna_ref[...], wpa_ref[0], preferred_element_type=F32)
              + gates[:, D:2 * D] * jnp.dot(yhy_ref[...], wpb_ref[0], preferred_element_type=F32)
              + gates[:, 2 * D:] * jnp.dot(ygdn_ref[...], wpc_ref[0], preferred_element_type=F32))
    r = jnp.dot(merged.astype(BF16), wout_ref[0], preferred_element_type=F32)
    x = _token_tile(refs[:-6], MIX_TM) + gt_ref[0] * r
    o_ref[...] = x
    h2_ref[...] = _modulated_norm(x, g2_ref[0], sh2_ref[0], sc2_ref[0])


def _mix_out(y_na, y_hy, y_gdn, proj, b_gate, w_pa, w_pb, w_pc, w_out, x, mod, ln2_g, layer):
    tm = MIX_TM
    resident = functools.partial(pl.BlockSpec, pipeline_mode=pl.Buffered(1))
    x_specs, x_args = _token_specs(x, tm)
    mod_spec = lambda chunk: pl.BlockSpec((1, 1, D), lambda i: (layer * N_COND + _cond_row(i, tm), 0, chunk))
    row_spec = pl.BlockSpec((tm, D), lambda i: (i, 0))
    return pl.pallas_call(
        _mix_out_kernel,
        out_shape=[jax.ShapeDtypeStruct((T, D), F32), jax.ShapeDtypeStruct((T, D), BF16)],
        grid=(T // tm,),
        in_specs=[
            pl.BlockSpec((tm, NA_WIDTH), lambda i: (i, 0)),
            pl.BlockSpec((tm, HY_WIDTH), lambda i: (i, 0)),
            pl.BlockSpec((tm, GDN_WIDTH), lambda i: (i, 0)),
            pl.BlockSpec((tm, GATE_BLOCK), lambda i: (i, 1)),
            pl.BlockSpec((1, 1, N_BRANCH * D), lambda i: (layer, 0, 0)),
            resident((1, NA_WIDTH, D), lambda i: (layer, 0, 0)),
            resident((1, HY_WIDTH, D), lambda i: (layer, 0, 0)),
            resident((1, GDN_WIDTH, D), lambda i: (layer, 0, 0)),
            resident((1, D, D), lambda i: (layer, 0, 0)),
        ] + x_specs + [mod_spec(2), pl.BlockSpec((1, 1, D), lambda i: (layer, 0, 0)), mod_spec(3), mod_spec(4)],
        out_specs=[row_spec, row_spec],
        compiler_params=_cparams(("arbitrary",)),
        name="mix_out",
    )(y_na, y_hy, y_gdn, proj, b_gate.reshape(DEPTH, 1, N_BRANCH * D), w_pa, w_pb, w_pc, w_out, *x_args,
      mod, ln2_g.reshape(DEPTH, 1, D), mod, mod)


FFN_TM = 1024


def _ffn_up_kernel(h_ref, wa_ref, wb_ref, cwa_ref, cwb_ref, cba_ref, cbb_ref, o_ref, wab_ref, wbb_ref):
    m = pl.program_id(1)

    @pl.when(m == 0)
    def _():
        wab_ref[...] = wa_ref[0].astype(BF16)
        wbb_ref[...] = wb_ref[0].astype(BF16)

    seq_len = jnp.where(m < T_P // FFN_TM, SEQ, DEC_SEQ)
    pos = lax.broadcasted_iota(jnp.int32, (FFN_TM, 1), 0) & (seq_len - 1)
    has_prev = pos != 0
    has_next = pos != seq_len - 1
    h = h_ref[...]

    def conv(w_ref, cw_ref, cb_ref):
        up = jnp.dot(h, w_ref[...], preferred_element_type=F32)
        prev = jnp.where(has_prev, pltpu.roll(up, 1, 0), 0.0)
        nxt = jnp.where(has_next, pltpu.roll(up, FFN_TM - 1, 0), 0.0)
        cw = cw_ref[0]
        return prev * cw[0:1] + up * cw[1:2] + nxt * cw[2:3] + cb_ref[0]

    ua = conv(wab_ref, cwa_ref, cba_ref)
    ub = conv(wbb_ref, cwb_ref, cbb_ref)
    o_ref[...] = (ua * jax.nn.sigmoid(ua) * ub).astype(BF16)


def _ffn_up(h, w_up, conv_w, conv_b, layer):
    tn = 512
    nt = D_FF // tn
    conv_b = conv_b.reshape(DEPTH, 1, 2 * D_FF)
    return pl.pallas_call(
        _ffn_up_kernel,
        out_shape=jax.ShapeDtypeStruct((T, D_FF), BF16),
        grid=(nt, T // FFN_TM),
        in_specs=[
            pl.BlockSpec((FFN_TM, D), lambda j, i: (i, 0)),
            pl.BlockSpec((1, D, tn), lambda j, i: (layer, 0, j)),
            pl.BlockSpec((1, D, tn), lambda j, i: (layer, 0, nt + j)),
            pl.BlockSpec((1, 3, tn), lambda j, i: (layer, 0, j)),
            pl.BlockSpec((1, 3, tn), lambda j, i: (layer, 0, nt + j)),
            pl.BlockSpec((1, 1, tn), lambda j, i: (layer, 0, j)),
            pl.BlockSpec((1, 1, tn), lambda j, i: (layer, 0, nt + j)),
        ],
        out_specs=pl.BlockSpec((FFN_TM, tn), lambda j, i: (i, j)),
        scratch_shapes=[pltpu.VMEM((D, tn), BF16), pltpu.VMEM((D, tn), BF16)],
        compiler_params=_cparams(("arbitrary", "arbitrary")),
        name="ffn_up",
    )(h, w_up, w_up, conv_w, conv_w, conv_b, conv_b)


DOWN_TM = 256


def _ffn_down_kernel(a_ref, w_ref, x_ref, gt_ref, g_ref, sh_ref, sc_ref, o_ref, h_ref):
    x = x_ref[...] + gt_ref[0] * jnp.dot(a_ref[...], w_ref[0], preferred_element_type=F32)
    o_ref[...] = x
    h_ref[...] = _modulated_norm(x, g_ref[0], sh_ref[0], sc_ref[0])


def _ffn_down_final_kernel(a_ref, w_ref, x_ref, gt_ref, g_ref, yp_ref, ys_ref):
    x = x_ref[...] + gt_ref[0] * jnp.dot(a_ref[...], w_ref[0], preferred_element_type=F32)
    y = x * lax.rsqrt(jnp.mean(x * x, axis=-1, keepdims=True) + NORM_EPS) * g_ref[...]
    is_context = pl.program_id(0) < T_P // DOWN_TM

    @pl.when(is_context)
    def _():
        yp_ref[...] = y

    @pl.when(jnp.logical_not(is_context))
    def _():
        ys_ref[...] = y


def _ffn_down(act, w_down, x, mod, layer, ln1_g, final_g):
    tm = DOWN_TM
    last = layer == DEPTH - 1
    mod_spec = lambda lyr, chunk: pl.BlockSpec((1, 1, D), lambda i: (lyr * N_COND + _cond_row(i, tm), 0, chunk))
    row_spec = pl.BlockSpec((tm, D), lambda i: (i, 0))
    in_specs = [
        pl.BlockSpec((tm, D_FF), lambda i: (i, 0)),
        pl.BlockSpec((1, D_FF, D), lambda i: (layer, 0, 0), pipeline_mode=pl.Buffered(1)),
        row_spec,
        mod_spec(layer, 5),
    ]
    if last:
        n_ctx = T_P // tm
        return pl.pallas_call(
            _ffn_down_final_kernel,
            out_shape=[jax.ShapeDtypeStruct((T_P, D), F32), jax.ShapeDtypeStruct((T_S, D), F32)],
            grid=(T // tm,),
            in_specs=in_specs + [pl.BlockSpec((1, D), lambda i: (0, 0))],
            out_specs=[pl.BlockSpec((tm, D), lambda i: (jnp.minimum(i, n_ctx - 1), 0)),
                       pl.BlockSpec((tm, D), lambda i: (jnp.maximum(i - n_ctx, 0), 0))],
            compiler_params=_cparams(("arbitrary",)),
            name="ffn_down_final",
        )(act, w_down, x, mod, final_g.reshape(1, D))
    return pl.pallas_call(
        _ffn_down_kernel,
        out_shape=[jax.ShapeDtypeStruct((T, D), F32), jax.ShapeDtypeStruct((T, D), BF16)],
        grid=(T // tm,),
        in_specs=in_specs + [pl.BlockSpec((1, 1, D), lambda i: (layer + 1, 0, 0)),
                             mod_spec(layer + 1, 0), mod_spec(layer + 1, 1)],
        out_specs=[row_spec, row_spec],
        compiler_params=_cparams(("arbitrary",)),
        name="ffn_down",
    )(act, w_down, x, mod, ln1_g.reshape(DEPTH, 1, D), mod, mod)


ATT_SCALE = NA_DH ** -0.5


def _nt_dot(a, b):
    return lax.dot_general(a, b, (((1,), (1,)), ((), ())), preferred_element_type=F32)


def _head_slices():
    return [slice(h * NA_DH, (h + 1) * NA_DH) for h in range(NA_HEADS)]


def _ctx_attn_kernel(q_ref, k_ref, v_ref, o_ref, ko_ref, vo_ref):
    heads = _head_slices()
    q = [(q_ref[:, sl] * ATT_SCALE).astype(BF16) for sl in heads]
    s = [_nt_dot(q[h], k_ref[:, sl].astype(BF16)) for h, sl in enumerate(heads)]
    e = [x - jnp.max(x, axis=-1, keepdims=True) for x in s]
    e = [jnp.exp(x) for x in e]
    o = [jnp.dot(e[h].astype(BF16), v_ref[:, sl].astype(BF16), preferred_element_type=F32)
         for h, sl in enumerate(heads)]
    for h, sl in enumerate(heads):
        o_ref[:, sl] = (o[h] / jnp.sum(e[h], axis=-1, keepdims=True)).astype(BF16)
    ko_ref[0, 0] = k_ref[...]
    vo_ref[0, 0] = v_ref[...]


def _ctx_attention(proj, layer, caches):
    cache_shape = jax.ShapeDtypeStruct((BATCH, DEPTH, SEQ, NA_WIDTH), F32)
    cache_spec = pl.BlockSpec((1, 1, SEQ, NA_WIDTH), lambda b: (b, layer, 0, 0))
    qkv_specs = [pl.BlockSpec((SEQ, NA_WIDTH), lambda b, j=j: (b, j)) for j in range(3)]
    out_shape = [jax.ShapeDtypeStruct((T, NA_WIDTH), BF16), cache_shape, cache_shape]
    out_specs = [pl.BlockSpec((SEQ, NA_WIDTH), lambda b: (b, 0)), cache_spec, cache_spec]
    if caches is None:
        return pl.pallas_call(
            _ctx_attn_kernel, out_shape=out_shape, grid=(BATCH,), in_specs=qkv_specs, out_specs=out_specs,
            compiler_params=_cparams(("arbitrary",)), name="ctx_attention",
        )(proj, proj, proj)

    def body(q_ref, k_ref, v_ref, kc_ref, vc_ref, o_ref, ko_ref, vo_ref):
        del kc_ref, vc_ref
        _ctx_attn_kernel(q_ref, k_ref, v_ref, o_ref, ko_ref, vo_ref)

    any_spec = pl.BlockSpec(memory_space=pl.ANY)
    return pl.pallas_call(
        body, out_shape=out_shape, grid=(BATCH,), in_specs=qkv_specs + [any_spec, any_spec],
        out_specs=out_specs, input_output_aliases={3: 1, 4: 2},
        compiler_params=_cparams(("arbitrary",)), name="ctx_attention",
    )(proj, proj, proj, *caches)


NA_ROWS = DEC_SEQ // GRID_W
NA_WIN_TOK = NA_WIN_ROWS * GRID_W


def _na_window_start(r):
    return jnp.clip(r - NA_WIN_ROWS // 2, 0, NA_ROWS - NA_WIN_ROWS)


def _na_bias_table(rpb):
    col = np.arange(GRID_W)
    col_start = np.clip(col - NA_WIN_COLS // 2, 0, GRID_W - NA_WIN_COLS)
    col_mask = (col[None, :] >= col_start[:, None]) & (col[None, :] < col_start[:, None] + NA_WIN_COLS)
    rel_col = np.clip(col[None, :] - col[:, None] + NA_WIN_COLS - 1, 0, 2 * NA_WIN_COLS - 2)
    banded = jnp.where(col_mask, rpb[:, :, rel_col], NEG_INF)
    per_offset = [banded[:, NA_WIN_ROWS - 1 - d:2 * NA_WIN_ROWS - 1 - d] for d in range(NA_WIN_ROWS)]
    tab = jnp.stack(per_offset, axis=0)
    return tab.transpose(0, 1, 3, 2, 4).reshape(NA_WIN_ROWS, NA_HEADS, GRID_W, NA_WIN_TOK)


def _na_attn_kernel(q_ref, k_ref, v_ref, ck_ref, cv_ref, bias_ref, y_in_ref, o_ref):
    del y_in_ref
    r = pl.program_id(1)
    start = pl.multiple_of(_na_window_start(r) * GRID_W, GRID_W)
    win = pl.ds(start, NA_WIN_TOK)
    heads = _head_slices()
    nh = range(NA_HEADS)
    q = [(q_ref[:, sl] * ATT_SCALE).astype(BF16) for sl in heads]
    s_win = [_nt_dot(q[h], k_ref[win, sl].astype(BF16)) + bias_ref[0, h] for h, sl in enumerate(heads)]
    s_ctx = [_nt_dot(q[h], ck_ref[0, 0, :, sl].astype(BF16)) for h, sl in enumerate(heads)]
    m = [jnp.maximum(jnp.max(s_win[h], axis=-1, keepdims=True), jnp.max(s_ctx[h], axis=-1, keepdims=True))
         for h in nh]
    e_win = [jnp.exp(s_win[h] - m[h]) for h in nh]
    e_ctx = [jnp.exp(s_ctx[h] - m[h]) for h in nh]
    o_win = [jnp.dot(e_win[h].astype(BF16), v_ref[win, sl].astype(BF16), preferred_element_type=F32)
             for h, sl in enumerate(heads)]
    o_ctx = [jnp.dot(e_ctx[h].astype(BF16), cv_ref[0, 0, :, sl].astype(BF16), preferred_element_type=F32)
             for h, sl in enumerate(heads)]
    for h, sl in enumerate(heads):
        denom = jnp.sum(e_win[h], axis=-1, keepdims=True) + jnp.sum(e_ctx[h], axis=-1, keepdims=True)
        o_ref[:, sl] = ((o_win[h] + o_ctx[h]) / denom).astype(BF16)


def _na_attention(proj, cache_k, cache_v, rpb, y_na, layer):
    q_row0 = T_P // GRID_W
    seq0 = T_P // DEC_SEQ
    kv_spec = lambda j: pl.BlockSpec((DEC_SEQ, NA_WIDTH), lambda b, r: (seq0 + b, j))
    ctx_spec = pl.BlockSpec((1, 1, SEQ, NA_WIDTH), lambda b, r: (b, layer, 0, 0))
    return pl.pallas_call(
        _na_attn_kernel,
        out_shape=jax.ShapeDtypeStruct((T, NA_WIDTH), BF16),
        grid=(DEC_BATCH, NA_ROWS),
        in_specs=[
            pl.BlockSpec((GRID_W, NA_WIDTH), lambda b, r: (q_row0 + b * NA_ROWS + r, 0)),
            kv_spec(1), kv_spec(2), ctx_spec, ctx_spec,
            pl.BlockSpec((1, NA_HEADS, GRID_W, NA_WIN_TOK), lambda b, r: (r - _na_window_start(r), 0, 0, 0)),
            pl.BlockSpec(memory_space=pl.ANY),
        ],
        out_specs=pl.BlockSpec((GRID_W, NA_WIDTH), lambda b, r: (q_row0 + b * NA_ROWS + r, 0)),
        input_output_aliases={6: 0},
        compiler_params=_cparams(("arbitrary", "arbitrary")),
        name="na_attention",
    )(proj, proj, proj, cache_k.reshape(DEC_BATCH, DEPTH, SEQ, NA_WIDTH),
      cache_v.reshape(DEC_BATCH, DEPTH, SEQ, NA_WIDTH), _na_bias_table(rpb), y_na)


HY_CB = 256


def _split_bf16(x):
    hi = x.astype(BF16)
    return hi, (x - hi.astype(F32)).astype(BF16)


def _dot3(a_hi, a_lo, b):
    b_hi, b_lo = _split_bf16(b)
    return (jnp.dot(a_hi, b_hi, preferred_element_type=F32) + jnp.dot(a_lo, b_hi, preferred_element_type=F32)
            + jnp.dot(a_hi, b_lo, preferred_element_type=F32))


def _trig_table_kernel(*out_refs, length, freq_on_rows, scale):
    rows, cols = out_refs[0].shape
    r = lax.broadcasted_iota(jnp.int32, (rows, cols), 0) + pl.program_id(0) * rows
    c = lax.broadcasted_iota(jnp.int32, (rows, cols), 1)
    f, t = (r, c) if freq_on_rows else (c, r)
    k = f & (length - 1)
    quarter = jnp.where(f >= length, length, 0)
    phase = (2 * k + 1) * t + (-quarter if freq_on_rows else quarter)
    phase = phase & (4 * length - 1)
    phase = jnp.where(phase >= 2 * length, phase - 4 * length, phase)
    val = jnp.cos(phase.astype(F32) * (math.pi / (2 * length))) * scale
    for ref, part in zip(out_refs, _split_bf16(val)):
        ref[...] = part


def _trig_table(length, freq_on_rows):
    shape = (2 * length, length) if freq_on_rows else (length, 2 * length)
    tr = 256
    n_parts = 2 if freq_on_rows else 1
    spec = pl.BlockSpec((tr, shape[1]), lambda i: (i, 0))
    return pl.pallas_call(
        functools.partial(_trig_table_kernel, length=length, freq_on_rows=freq_on_rows,
                          scale=1.0 if freq_on_rows else 1.0 / length),
        out_shape=[jax.ShapeDtypeStruct(shape, BF16)] * n_parts,
        grid=(shape[0] // tr,), out_specs=[spec] * n_parts,
        compiler_params=_cparams(("arbitrary",)), name="dft_table",
    )()


def _hy_filter_kernel(tn_ref, band_ref, w1_ref, b1_ref, fr_ref, w2_ref, b2_ref, w3_ref, dec_ref,
                      fhi_ref, flo_ref, o_ref, *, length):
    t_norm = tn_ref[...]
    t_idx = lax.broadcasted_iota(jnp.int32, (length, LANE), 0).astype(F32)
    lane = lax.broadcasted_iota(jnp.int32, (length, LANE), 1)
    ang = (2.0 * math.pi / length) * t_idx * band_ref[...]
    z = jnp.where(lane == 0, t_norm,
                  jnp.where(lane <= HY_BANDS, jnp.cos(ang), jnp.where(lane <= 2 * HY_BANDS, jnp.sin(ang), 0.0)))
    hdn = jnp.sin(fr_ref[0:1] * (jnp.dot(z, w1_ref[...], precision=HIGHEST, preferred_element_type=F32)
                                 + b1_ref[...]))
    hdn = jnp.sin(fr_ref[1:2] * (jnp.dot(hdn, w2_ref[...], precision=HIGHEST, preferred_element_type=F32)
                                 + b2_ref[...]))
    filt = jnp.dot(hdn, w3_ref[...], precision=HIGHEST, preferred_element_type=F32)
    filt = filt * jnp.exp(-t_norm * jnp.abs(dec_ref[...]))
    first = lax.broadcasted_iota(jnp.int32, (length, HY_WIDTH), 0) == 0
    for o in range(HY_ORDER):
        fwd = filt[:, (2 * o) * HY_WIDTH:(2 * o + 1) * HY_WIDTH]
        bwd = filt[:, (2 * o + 1) * HY_WIDTH:(2 * o + 2) * HY_WIDTH]
        bwd = jnp.where(first, 0.0, pltpu.roll(bwd, 1, 0))
        o_ref[o, :length] = _dot3(fhi_ref[:length], flo_ref[:length], fwd + bwd)
        o_ref[o, length:] = _dot3(fhi_ref[length:], flo_ref[length:], bwd - fwd)


def _hy_filter_spectrum(length, f_hi, f_lo, w1, b1, freq, w2, b2, w3, decay):
    emb = 1 + 2 * HY_BANDS
    t_norm = jnp.linspace(0.0, 1.0, length, dtype=F32).reshape(length, 1)
    bands = np.zeros((1, LANE), np.float32)
    bands[0, 1:1 + HY_BANDS] = bands[0, 1 + HY_BANDS:emb] = np.linspace(1e-4, HY_BANDS - 1, HY_BANDS,
                                                                        dtype=np.float32)
    w1p = jnp.zeros((LANE, w1.shape[1]), F32).at[:emb].set(w1)
    return pl.pallas_call(
        functools.partial(_hy_filter_kernel, length=length),
        out_shape=jax.ShapeDtypeStruct((HY_ORDER, 2 * length, HY_WIDTH), F32),
        compiler_params=pltpu.CompilerParams(vmem_limit_bytes=VMEM_LIMIT),
        name="hyena_filter",
    )(t_norm, jnp.asarray(bands), w1p, b1.reshape(1, -1), freq, w2, b2.reshape(1, -1), w3,
      decay.reshape(1, -1), f_hi, f_lo)


def _hyena_kernel(v_ref, x1_ref, x2_ref, cwv_ref, cw1_ref, cw2_ref, cbv_ref, cb1_ref, cb2_ref, skip_ref,
                  hs_ref, f_ref, g_ref, *rest, length):
    o_ref = rest[-1]
    row = lax.broadcasted_iota(jnp.int32, v_ref.shape, 0)
    first, last = row == 0, row == length - 1

    def short_conv(u_ref, cw_ref, cb_ref):
        u = u_ref[...]
        prev = jnp.where(first, 0.0, pltpu.roll(u, 1, 0))
        nxt = jnp.where(last, 0.0, pltpu.roll(u, length - 1, 0))
        return prev * cw_ref[0:1] + u * cw_ref[1:2] + nxt * cw_ref[2:3] + cb_ref[...]

    z = short_conv(v_ref, cwv_ref, cbv_ref)
    gates = (short_conv(x1_ref, cw1_ref, cb1_ref), short_conv(x2_ref, cw2_ref, cb2_ref))
    for o in range(HY_ORDER):
        zf = jnp.dot(f_ref[...], z.astype(BF16), preferred_element_type=F32)
        zc, zs = zf[:length], zf[length:]
        h_re, h_im = hs_ref[o, :length], hs_ref[o, length:]
        p = jnp.concatenate([zc * h_re + zs * h_im, zc * h_im - zs * h_re], axis=0)
        conv = jnp.dot(g_ref[...], p.astype(BF16), preferred_element_type=F32)
        z = gates[o] * (conv + z * skip_ref[o:o + 1])
    o_ref[...] = z.astype(BF16)


def _hyena(proj, y_hy, layer, length, tables, spectrum, conv_w, conv_b, skip):
    nseq = (T_P if length == SEQ else T_S) // length
    row0 = 0 if length == SEQ else T_P // length
    cb = HY_WIDTH if length == SEQ else HY_CB
    ncb = HY_WIDTH // cb
    col0 = C_HY // cb
    u_spec = lambda part: pl.BlockSpec((length, cb), lambda s, j: (row0 + s, col0 + part * ncb + j))
    cw_spec = lambda part: pl.BlockSpec((1, 3, cb), lambda s, j: (layer, 0, part * ncb + j))
    cb_spec = lambda part: pl.BlockSpec((1, 1, cb), lambda s, j: (layer, 0, part * ncb + j))
    resident = functools.partial(pl.BlockSpec, pipeline_mode=pl.Buffered(1))
    tab_specs = [resident(t.shape, lambda s, j: (0, 0)) for t in tables]
    in_specs = ([u_spec(0), u_spec(1), u_spec(2), cw_spec(0), cw_spec(1), cw_spec(2),
                 cb_spec(0), cb_spec(1), cb_spec(2),
                 pl.BlockSpec((1, HY_ORDER, cb), lambda s, j: (layer, 0, j)),
                 pl.BlockSpec((HY_ORDER, 2 * length, cb), lambda s, j: (0, 0, j))] + tab_specs)
    args = [proj] * 3 + [conv_w] * 3 + [conv_b.reshape(DEPTH, 1, -1)] * 3 + [skip, spectrum] + list(tables)
    aliases = {}
    if y_hy is not None:
        in_specs.append(pl.BlockSpec(memory_space=pl.ANY))
        args.append(y_hy)
        aliases = {len(args) - 1: 0}

    def body(v_ref, x1_ref, x2_ref, cwv, cw1, cw2, cbv, cb1, cb2, skip_ref, hs_ref, *rest):
        _hyena_kernel(v_ref, x1_ref, x2_ref, cwv.at[0], cw1.at[0], cw2.at[0], cbv.at[0], cb1.at[0], cb2.at[0],
                      skip_ref.at[0], hs_ref, *rest, length=length)

    return pl.pallas_call(
        body,
        out_shape=jax.ShapeDtypeStruct((T, HY_WIDTH), BF16),
        grid=(nseq, ncb), in_specs=in_specs,
        out_specs=pl.BlockSpec((length, cb), lambda s, j: (row0 + s, j)),
        input_output_aliases=aliases,
        compiler_params=_cparams(("arbitrary", "arbitrary")),
        name="hyena",
    )(*args)


GC = 128
GDN_SCALE = GDN_DK ** -0.5


def _merge_masks(lower):
    ri = lax.broadcasted_iota(jnp.int32, (GC, GC), 0)
    ci = lax.broadcasted_iota(jnp.int32, (GC, GC), 1)
    hi, lo = (ri, ci) if lower else (ci, ri)
    return [((hi >> (b + 1)) == (lo >> (b + 1))) & ((hi >> b) > (lo >> b)) for b in range(int(math.log2(GC)))]


def _unit_tri_inverse(a, masks):
    ri = lax.broadcasted_iota(jnp.int32, (GC, GC), 0)
    ci = lax.broadcasted_iota(jnp.int32, (GC, GC), 1)
    eye = (ri == ci).astype(F32)
    n = range(len(a))
    t = [eye - jnp.where(masks[i][0], a[i], 0.0) for i in n]
    for level in range(1, len(masks[0])):
        tb = [t[i].astype(BF16) for i in n]
        ta = [jnp.dot(tb[i], jnp.where(masks[i][level], a[i], 0.0).astype(BF16), preferred_element_type=F32)
              for i in n]
        tat = [jnp.dot(ta[i].astype(BF16), tb[i], preferred_element_type=F32) for i in n]
        t = [t[i] - tat[i] for i in n]
    return t


def _gdn_kernel(q_ref, k_ref, v_ref, z_ref, ba_ref, cwq_ref, cwk_ref, cwv_ref, alog_ref, dt_ref, ng_ref,
                sf0_ref, sb0_ref, *rest, length):
    y_ref, sf_ref, sb_ref, qn_ref, kn_ref, vn_ref, beta_ref, g_ref, of_ref, ob_ref, s_ref = rest[-11:]
    n_chunks = length // GC
    row = lax.broadcasted_iota(jnp.int32, (length, GDN_WIDTH), 0)
    first, last = row == 0, row == length - 1

    def conv_silu(u_ref, cw_ref):
        u = u_ref[...]
        prev = jnp.where(first, 0.0, pltpu.roll(u, 1, 0))
        nxt = jnp.where(last, 0.0, pltpu.roll(u, length - 1, 0))
        c = prev * cw_ref[0:1] + u * cw_ref[1:2] + nxt * cw_ref[2:3]
        return c * jax.nn.sigmoid(c)

    q = conv_silu(q_ref, cwq_ref)
    k = conv_silu(k_ref, cwk_ref)
    vn_ref[...] = conv_silu(v_ref, cwv_ref)
    for h in range(GDN_HEADS):
        sl = slice(h * GDN_DK, (h + 1) * GDN_DK)
        qh, kh = q[:, sl], k[:, sl]
        qn_ref[:, sl] = qh * lax.rsqrt(jnp.sum(qh * qh, axis=-1, keepdims=True) + NORM_EPS) * GDN_SCALE
        kn_ref[:, sl] = kh * lax.rsqrt(jnp.sum(kh * kh, axis=-1, keepdims=True) + NORM_EPS)
    ba = ba_ref[...]
    beta_ref[...] = jax.nn.sigmoid(ba)
    g_ref[...] = -jnp.exp(alog_ref[...]) * jax.nn.softplus(ba + dt_ref[...])

    ri = lax.broadcasted_iota(jnp.int32, (GC, GC), 0)
    ci = lax.broadcasted_iota(jnp.int32, (GC, GC), 1)
    lower = (ri >= ci).astype(F32)
    upper = (ri <= ci).astype(F32)

    def chunk_step(n, carry):
        rows = [pl.ds(pl.multiple_of((n if d == 0 else n_chunks - 1 - n) * GC, GC), GC) for d in range(2)]
        loaded = {}
        for d in range(2):
            loaded[d] = (g_ref[rows[d], :], beta_ref[rows[d], :])
            for h in range(GDN_HEADS):
                sl = slice(h * GDN_DK, (h + 1) * GDN_DK)
                loaded[d, h] = (qn_ref[rows[d], sl], kn_ref[rows[d], sl], vn_ref[rows[d], sl],
                                s_ref[d * GDN_HEADS + h])
        units = [(d, h) for d in range(2) for h in range(GDN_HEADS)]
        incl = [(ri >= ci), (ri <= ci)]
        strict = [(ri > ci), (ri < ci)]
        masks = [_merge_masks(True), _merge_masks(False)]
        gcs = [jnp.dot(tri, loaded[d][0], precision=HIGHEST, preferred_element_type=F32)
               for d, tri in enumerate((lower, upper))]
        gcs_t = [g.T for g in gcs]
        edge = [GC - 1, 0]
        g_col = [gcs[d][:, 8 + 4 * d + h:9 + 4 * d + h] for d, h in units]
        g_row = [gcs_t[d][8 + 4 * d + h:9 + 4 * d + h, :] for d, h in units]
        g_end = [gcs_t[d][8 + 4 * d + h:9 + 4 * d + h, edge[d]:edge[d] + 1] for d, h in units]
        b_col = [loaded[d][1][:, 4 * d + h:4 * d + h + 1] for d, h in units]
        qc, kc, vc, st = (list(x) for x in zip(*(loaded[u] for u in units)))
        nu = range(len(units))
        decay = [jnp.where(incl[d], jnp.exp(jnp.where(incl[d], g_col[i] - g_row[i], 0.0)), 0.0)
                 for i, (d, h) in enumerate(units)]
        kb = [kc[i] * b_col[i] for i in nu]
        kcb = [kc[i].astype(BF16) for i in nu]
        kk = [_nt_dot(kb[i].astype(BF16), kcb[i]) for i in nu]
        qk = [_nt_dot(qc[i].astype(BF16), kcb[i]) for i in nu]
        a = [jnp.where(strict[d], kk[i] * decay[i], 0.0) for i, (d, h) in enumerate(units)]
        t = _unit_tri_inverse(a, [masks[d] for d, h in units])
        e_col = [jnp.exp(g_col[i]) for i in nu]
        rhs = [jnp.concatenate([vc[i] * b_col[i], kb[i] * e_col[i]], axis=1).astype(BF16) for i in nu]
        sol = [jnp.dot(t[i].astype(BF16), rhs[i], preferred_element_type=F32) for i in nu]
        attn = [jnp.where(incl[d], qk[i] * decay[i], 0.0).astype(BF16) for i, (d, h) in enumerate(units)]
        sb = [st[i].astype(BF16) for i in nu]
        ws = [jnp.dot(sol[i][:, GDN_DV:].astype(BF16), sb[i], preferred_element_type=F32) for i in nu]
        qs = [jnp.dot((qc[i] * e_col[i]).astype(BF16), sb[i], preferred_element_type=F32) for i in nu]
        v_new = [(sol[i][:, :GDN_DV] - ws[i]).astype(BF16) for i in nu]
        av = [jnp.dot(attn[i], v_new[i], preferred_element_type=F32) for i in nu]
        k_dec_t = [(kc[i] * jnp.exp(g_end[i] - g_col[i])).T.astype(BF16) for i in nu]
        kv = [jnp.dot(k_dec_t[i], v_new[i], preferred_element_type=F32) for i in nu]
        results = [(qs[i] + av[i], st[i] * jnp.exp(g_end[i]) + kv[i]) for i in nu]
        for d in range(2):
            for h in range(GDN_HEADS):
                o, s = results[d * GDN_HEADS + h]
                (of_ref if d == 0 else ob_ref)[rows[d], h * GDN_DV:(h + 1) * GDN_DV] = o
                s_ref[d * GDN_HEADS + h] = s
        return carry

    s_ref[:GDN_HEADS] = sf0_ref[0]
    s_ref[GDN_HEADS:] = sb0_ref[0]
    lax.fori_loop(0, n_chunks, chunk_step, 0)
    sf_ref[0, 0] = s_ref[:GDN_HEADS]
    sb_ref[0, 0] = s_ref[GDN_HEADS:]
    for h in range(GDN_HEADS):
        sl = slice(h * GDN_DV, (h + 1) * GDN_DV)
        o = of_ref[:, sl] + ob_ref[:, sl]
        o = o * lax.rsqrt(jnp.mean(o * o, axis=-1, keepdims=True) + NORM_EPS) * ng_ref[...]
        zh = z_ref[:, sl]
        y_ref[:, sl] = (o * (zh * jax.nn.sigmoid(zh))).astype(BF16)


def _gdn(proj, y_gdn, states_out, layer, length, s_f0, s_b0, conv_w, a_log, dt_bias, norm_g):
    nseq = (T_P if length == SEQ else T_S) // length
    row0 = 0 if length == SEQ else T_P // length
    col = lambda c: c // GDN_WIDTH
    blk = lambda c: pl.BlockSpec((length, GDN_WIDTH), lambda s: (row0 + s, col(c)))
    cw = lambda part: pl.BlockSpec((1, 3, GDN_WIDTH), lambda s: (layer, 0, part))
    vec = lambda: pl.BlockSpec((1, 1, LANE), lambda s: (layer, 0, 0))
    st_in = pl.BlockSpec((1, GDN_HEADS, GDN_DK, GDN_DV), lambda s: (s, 0, 0, 0))
    st_out = pl.BlockSpec((1, 1, GDN_HEADS, GDN_DK, GDN_DV), lambda s: (s, layer, 0, 0, 0))
    pad_lanes = lambda x, off: jnp.zeros((DEPTH, 1, LANE), F32).at[:, 0, off:off + 2 * GDN_HEADS].set(
        x.reshape(DEPTH, -1))
    in_specs = [blk(C_GDN), blk(C_GDN + GDN_WIDTH), blk(C_GDN + 2 * GDN_WIDTH), blk(C_Z),
                pl.BlockSpec((length, LANE), lambda s: (row0 + s, C_BA // LANE)),
                cw(0), cw(1), cw(2), vec(), vec(), vec(), st_in, st_in]
    args = [proj] * 5 + [conv_w] * 3 + [pad_lanes(a_log, 2 * GDN_HEADS), pad_lanes(dt_bias, 2 * GDN_HEADS),
                                        norm_g.reshape(DEPTH, 1, GDN_DV), s_f0, s_b0]
    st_shape = jax.ShapeDtypeStruct((nseq, DEPTH, GDN_HEADS, GDN_DK, GDN_DV), F32)
    aliases = {}
    for carried, out_idx in ((y_gdn, 0),) + (((states_out[0], 1), (states_out[1], 2)) if states_out else ()):
        if carried is not None:
            in_specs.append(pl.BlockSpec(memory_space=pl.ANY))
            args.append(carried)
            aliases[len(args) - 1] = out_idx

    def body(q_ref, k_ref, v_ref, z_ref, ba_ref, cwq, cwk, cwv, alog, dt, ng, sf0, sb0, *rest):
        _gdn_kernel(q_ref, k_ref, v_ref, z_ref, ba_ref, cwq.at[0], cwk.at[0], cwv.at[0], alog.at[0], dt.at[0],
                    ng.at[0], sf0, sb0, *rest, length=length)

    seq_buf = lambda w: pltpu.VMEM((length, w), F32)
    return pl.pallas_call(
        body,
        out_shape=[jax.ShapeDtypeStruct((T, GDN_WIDTH), BF16), st_shape, st_shape],
        grid=(nseq,), in_specs=in_specs,
        out_specs=[pl.BlockSpec((length, GDN_WIDTH), lambda s: (row0 + s, 0)), st_out, st_out],
        scratch_shapes=[seq_buf(GDN_WIDTH), seq_buf(GDN_WIDTH), seq_buf(GDN_WIDTH), seq_buf(LANE), seq_buf(LANE),
                        seq_buf(GDN_WIDTH), seq_buf(GDN_WIDTH),
                        pltpu.VMEM((2 * GDN_HEADS, GDN_DK, GDN_DV), F32)],
        input_output_aliases=aliases,
        compiler_params=_cparams(("arbitrary",)),
        name="gated_deltanet",
    )(*args)


def kernel(x_prompt, x_sample, cache_k, cache_v, state_fwd, state_bwd, c, c_ctx, ln1_g, ln2_g, w_mod, b_mod,
           w_in, na_rpb, hy_conv_w, hy_conv_b, hy_w1, hy_b1, hy_freq, hy_w2, hy_b2, hy_w3, hy_decay, hy_skip,
           gdn_conv_w, gdn_a_log, gdn_dt_bias, gdn_norm_g, w_pa, w_pb, w_pc, b_gate, w_out, ffn_w_up,
           ffn_conv_w, ffn_conv_b, ffn_w_down, final_g):
    x = (x_prompt.reshape(T_P, D), x_sample.reshape(T_S, D))
    cond = jnp.concatenate([c_ctx[None], c, jnp.zeros((N_COND - 1 - DEC_BATCH, D), F32)], axis=0)
    mod = _mod_table(cond, w_mod, b_mod).reshape(DEPTH * N_COND, 1, 6 * D)

    w_pa_b, w_pb_b, w_pc_b, w_out_b = (w.astype(BF16) for w in (w_pa, w_pb, w_pc, w_out))
    w_down_b = ffn_w_down.astype(BF16)
    w_in_t = jnp.swapaxes(w_in, 1, 2)
    tables = {n: tuple(_trig_table(n, True)) + tuple(_trig_table(n, False)) for n in (SEQ, DEC_SEQ)}
    zero_state = jnp.zeros((BATCH, GDN_HEADS, GDN_DK, GDN_DV), F32)

    caches = None
    states = None
    h = _norm_mod(x, ln1_g, mod, 0, 0)
    for layer in range(DEPTH):
        proj = _in_proj(h, w_in_t, layer)

        y_na, new_k, new_v = _ctx_attention(proj, layer, caches)
        caches = (new_k, new_v)
        y_na = _na_attention(proj, cache_k, cache_v, na_rpb[layer], y_na, layer)

        y_hy = None
        for n in (SEQ, DEC_SEQ):
            spectrum = _hy_filter_spectrum(n, tables[n][0], tables[n][1], hy_w1[layer], hy_b1[layer],
                                           hy_freq[layer], hy_w2[layer], hy_b2[layer], hy_w3[layer],
                                           hy_decay[layer])
            y_hy = _hyena(proj, y_hy, layer, n, (tables[n][0], tables[n][2]), spectrum,
                          hy_conv_w, hy_conv_b, hy_skip)

        y_gdn, s_f, s_b = _gdn(proj, None, states, layer, SEQ, zero_state, zero_state,
                               gdn_conv_w, gdn_a_log, gdn_dt_bias, gdn_norm_g)
        states = (s_f, s_b)
        y_gdn = _gdn(proj, y_gdn, None, layer, DEC_SEQ, state_fwd[:, layer], state_bwd[:, layer],
                     gdn_conv_w, gdn_a_log, gdn_dt_bias, gdn_norm_g)[0]

        x, h = _mix_out(y_na, y_hy, y_gdn, proj, b_gate, w_pa_b, w_pb_b, w_pc_b, w_out_b, x, mod, ln2_g, layer)
        act = _ffn_up(h, ffn_w_up, ffn_conv_w, ffn_conv_b, layer)
        x, h = _ffn_down(act, w_down_b, x, mod, layer, ln1_g, final_g)

    y_p, y_s = x, h
    cache_shape = (BATCH, DEPTH, SEQ, NA_HEADS, NA_DH)
    return (y_p.reshape(BATCH, SEQ, D), y_s.reshape(DEC_BATCH, DEC_SEQ, D),
            caches[0].reshape(cache_shape), caches[1].reshape(cache_shape), states[0], states[1])
```

```python
import functools
import math

import jax
import jax.numpy as jnp
import numpy as np
from jax import lax
from jax.experimental import pallas as pl
from jax.experimental.pallas import tpu as pltpu

F32 = jnp.float32
BF16 = jnp.bfloat16

D = 2048
BATCH, SEQ = 32, 256
DEC_BATCH, DEC_SEQ = 4, 1024
DEPTH = 2
GRID_W = 64
NA_HEADS, NA_DH = 8, 128
NA_WIDTH = NA_HEADS * NA_DH
NA_WIN_ROWS, NA_WIN_COLS = 8, 16
HY_WIDTH = 512
HY_ORDER = 2
HY_BANDS = 16
GDN_HEADS, GDN_DK, GDN_DV = 4, 128, 128
GDN_WIDTH = GDN_HEADS * GDN_DV
D_FF = 5632
N_BRANCH = 3
NORM_EPS = 1e-6
NEG_INF = -1e30

T_P = BATCH * SEQ
T_S = DEC_BATCH * DEC_SEQ
T = T_P + T_S
N_COND = 8

C_Q, C_K, C_V = 0, NA_WIDTH, 2 * NA_WIDTH
C_HY = 3 * NA_WIDTH
C_GDN = C_HY + 3 * HY_WIDTH
C_Z = C_GDN + 3 * GDN_HEADS * GDN_DK
C_BA = C_Z + GDN_WIDTH
C_GATE = C_BA + 4 * GDN_HEADS
N_IN = C_GATE + N_BRANCH * D
GATE_BLOCK = C_BA
GATE_SHIFT = C_GATE - C_BA

LANE = 128
VMEM_LIMIT = 56 * 1024 * 1024
HIGHEST = lax.Precision.HIGHEST


def _cparams(sem):
    return pltpu.CompilerParams(dimension_semantics=sem, vmem_limit_bytes=VMEM_LIMIT)


def _cond_row(tile, rows_per_tile):
    first_latent = T_P // rows_per_tile
    per_seq = DEC_SEQ // rows_per_tile
    return jnp.where(tile < first_latent, 0, 1 + (tile - first_latent) // per_seq)


def _mod_kernel(c_ref, w_ref, b_ref, o_ref):
    c = c_ref[...]
    s = (c * jax.nn.sigmoid(c)).astype(BF16)
    o_ref[0] = jnp.dot(s, w_ref[0].astype(BF16), preferred_element_type=F32) + b_ref[0]


def _mod_table(cond, w_mod, b_mod):
    tn = 1024
    n = 6 * D
    return pl.pallas_call(
        _mod_kernel,
        out_shape=jax.ShapeDtypeStruct((DEPTH, N_COND, n), F32),
        grid=(DEPTH, n // tn),
        in_specs=[
            pl.BlockSpec((N_COND, D), lambda l, j: (0, 0)),
            pl.BlockSpec((1, D, tn), lambda l, j: (l, 0, j)),
            pl.BlockSpec((1, 1, tn), lambda l, j: (l, 0, j)),
        ],
        out_specs=pl.BlockSpec((1, N_COND, tn), lambda l, j: (l, 0, j)),
        compiler_params=_cparams(("arbitrary", "arbitrary")),
        name="mod_table",
    )(cond, w_mod, b_mod.reshape(DEPTH, 1, n))


def _token_specs(x, tm):
    if not isinstance(x, tuple):
        return [pl.BlockSpec((tm, D), lambda i: (i, 0))], [x]
    n_ctx = T_P // tm
    return ([pl.BlockSpec((tm, D), lambda i: (jnp.minimum(i, n_ctx - 1), 0)),
             pl.BlockSpec((tm, D), lambda i: (jnp.maximum(i - n_ctx, 0), 0))], list(x))


def _token_tile(x_refs, tm):
    if len(x_refs) == 1:
        return x_refs[0][...]
    return jnp.where(pl.program_id(0) < T_P // tm, x_refs[0][...], x_refs[1][...])


def _modulated_norm(x, g, shift, scale):
    y = x * lax.rsqrt(jnp.mean(x * x, axis=-1, keepdims=True) + NORM_EPS) * g
    return (y * (1.0 + scale) + shift).astype(BF16)


NORM_TM = 1024


def _norm_mod_kernel(*refs):
    g_ref, sh_ref, sc_ref, o_ref = refs[-4:]
    o_ref[...] = _modulated_norm(_token_tile(refs[:-4], NORM_TM), g_ref[0], sh_ref[0], sc_ref[0])


def _norm_mod(x, g, mod, layer, shift_chunk):
    tm = NORM_TM
    x_specs, x_args = _token_specs(x, tm)
    return pl.pallas_call(
        _norm_mod_kernel,
        out_shape=jax.ShapeDtypeStruct((T, D), BF16),
        grid=(T // tm,),
        in_specs=x_specs + [
            pl.BlockSpec((1, 1, D), lambda i: (layer, 0, 0)),
            pl.BlockSpec((1, 1, D), lambda i: (layer * N_COND + _cond_row(i, tm), 0, shift_chunk)),
            pl.BlockSpec((1, 1, D), lambda i: (layer * N_COND + _cond_row(i, tm), 0, shift_chunk + 1)),
        ],
        out_specs=pl.BlockSpec((tm, D), lambda i: (i, 0)),
        compiler_params=_cparams(("arbitrary",)),
        name="norm_mod",
    )(*x_args, g.reshape(DEPTH, 1, D), mod, mod)


def _proj_kernel(h_ref, wt_ref, o_ref, wb_ref):
    @pl.when(pl.program_id(1) == 0)
    def _():
        wb_ref[...] = wt_ref[0].astype(BF16)

    o_ref[...] = lax.dot_general(h_ref[...], wb_ref[...], (((1,), (1,)), ((), ())), preferred_element_type=F32)


def _in_proj(h, w_in_t, layer):
    tm, tn = 1536, 1024
    return pl.pallas_call(
        _proj_kernel,
        out_shape=jax.ShapeDtypeStruct((T, N_IN), F32),
        grid=(pl.cdiv(N_IN, tn), T // tm),
        in_specs=[
            pl.BlockSpec((tm, D), lambda j, i: (i, 0)),
            pl.BlockSpec((1, tn, D), lambda j, i: (layer, j, 0)),
        ],
        out_specs=pl.BlockSpec((tm, tn), lambda j, i: (i, j)),
        scratch_shapes=[pltpu.VMEM((tn, D), BF16)],
        compiler_params=_cparams(("arbitrary", "arbitrary")),
        name="in_proj",
    )(h, w_in_t)


MIX_TM = 256


def _mix_out_kernel(yna_ref, yhy_ref, ygdn_ref, gl_ref, bg_ref, wpa_ref, wpb_ref, wpc_ref, wout_ref, *refs):
    gt_ref, g2_ref, sh2_ref, sc2_ref, o_ref, h2_ref = refs[-6:]
    width = gl_ref.shape[1]
    gl = pltpu.roll(gl_ref[...], width - GATE_SHIFT, 1)[:, :N_BRANCH * D] + bg_ref[0]
    gates = jax.nn.sigmoid(gl)
    merged = (gates[:, :D] * jnp.dot(yna_ref[...], wpa_ref[0], preferred_element_type=F32)
              + gates[:, D:2 * D] * jnp.dot(yhy_ref[...], wpb_ref[0], preferred_element_type=F32)
              + gates[:, 2 * D:] * jnp.dot(ygdn_ref[...], wpc_ref[0], preferred_element_type=F32))
    r = jnp.dot(merged.astype(BF16), wout_ref[0], preferred_element_type=F32)
    x = _token_tile(refs[:-6], MIX_TM) + gt_ref[0] * r
    o_ref[...] = x
    h2_ref[...] = _modulated_norm(x, g2_ref[0], sh2_ref[0], sc2_ref[0])


def _mix_out(y_na, y_hy, y_gdn, proj, b_gate, w_pa, w_pb, w_pc, w_out, x, mod, ln2_g, layer):
    tm = MIX_TM
    resident = functools.partial(pl.BlockSpec, pipeline_mode=pl.Buffered(1))
    x_specs, x_args = _token_specs(x, tm)
    mod_spec = lambda chunk: pl.BlockSpec((1, 1, D), lambda i: (layer * N_COND + _cond_row(i, tm), 0, chunk))
    row_spec = pl.BlockSpec((tm, D), lambda i: (i, 0))
    return pl.pallas_call(
        _mix_out_kernel,
        out_shape=[jax.ShapeDtypeStruct((T, D), F32), jax.ShapeDtypeStruct((T, D), BF16)],
        grid=(T // tm,),
        in_specs=[
            pl.BlockSpec((tm, NA_WIDTH), lambda i: (i, 0)),
            pl.BlockSpec((tm, HY_WIDTH), lambda i: (i, 0)),
            pl.BlockSpec((tm, GDN_WIDTH), lambda i: (i, 0)),
            pl.BlockSpec((tm, GATE_BLOCK), lambda i: (i, 1)),
            pl.BlockSpec((1, 1, N_BRANCH * D), lambda i: (layer, 0, 0)),
            resident((1, NA_WIDTH, D), lambda i: (layer, 0, 0)),
            resident((1, HY_WIDTH, D), lambda i: (layer, 0, 0)),
            resident((1, GDN_WIDTH, D), lambda i: (layer, 0, 0)),
            resident((1, D, D), lambda i: (layer, 0, 0)),
        ] + x_specs + [mod_spec(2), pl.BlockSpec((1, 1, D), lambda i: (layer, 0, 0)), mod_spec(3), mod_spec(4)],
        out_specs=[row_spec, row_spec],
        compiler_params=_cparams(("arbitrary",)),
        name="mix_out",
    )(y_na, y_hy, y_gdn, proj, b_gate.reshape(DEPTH, 1, N_BRANCH * D), w_pa, w_pb, w_pc, w_out, *x_args,
      mod, ln2_g.reshape(DEPTH, 1, D), mod, mod)


FFN_TM = 1024


def _ffn_up_kernel(h_ref, wa_ref, wb_ref, cwa_ref, cwb_ref, cba_ref, cbb_ref, o_ref, wab_ref, wbb_ref):
    m = pl.program_id(1)

    @pl.when(m == 0)
    def _():
        wab_ref[...] = wa_ref[0].astype(BF16)
        wbb_ref[...] = wb_ref[0].astype(BF16)

    seq_len = jnp.where(m < T_P // FFN_TM, SEQ, DEC_SEQ)
    pos = lax.broadcasted_iota(jnp.int32, (FFN_TM, 1), 0) & (seq_len - 1)
    has_prev = pos != 0
    has_next = pos != seq_len - 1
    h = h_ref[...]

    def conv(w_ref, cw_ref, cb_ref):
        up = jnp.dot(h, w_ref[...], preferred_element_type=F32)
        prev = jnp.where(has_prev, pltpu.roll(up, 1, 0), 0.0)
        nxt = jnp.where(has_next, pltpu.roll(up, FFN_TM - 1, 0), 0.0)
        cw = cw_ref[0]
        return prev * cw[0:1] + up * cw[1:2] + nxt * cw[2:3] + cb_ref[0]

    ua = conv(wab_ref, cwa_ref, cba_ref)
    ub = conv(wbb_ref, cwb_ref, cbb_ref)
    o_ref[...] = (ua * jax.nn.sigmoid(ua) * ub).astype(BF16)


def _ffn_up(h, w_up, conv_w, conv_b, layer):
    tn = 512
    nt = D_FF // tn
    conv_b = conv_b.reshape(DEPTH, 1, 2 * D_FF)
    return pl.pallas_call(
        _ffn_up_kernel,
        out_shape=jax.ShapeDtypeStruct((T, D_FF), BF16),
        grid=(nt, T // FFN_TM),
        in_specs=[
            pl.BlockSpec((FFN_TM, D), lambda j, i: (i, 0)),
            pl.BlockSpec((1, D, tn), lambda j, i: (layer, 0, j)),
            pl.BlockSpec((1, D, tn), lambda j, i: (layer, 0, nt + j)),
            pl.BlockSpec((1, 3, tn), lambda j, i: (layer, 0, j)),
            pl.BlockSpec((1, 3, tn), lambda j, i: (layer, 0, nt + j)),
            pl.BlockSpec((1, 1, tn), lambda j, i: (layer, 0, j)),
            pl.BlockSpec((1, 1, tn), lambda j, i: (layer, 0, nt + j)),
        ],
        out_specs=pl.BlockSpec((FFN_TM, tn), lambda j, i: (i, j)),
        scratch_shapes=[pltpu.VMEM((D, tn), BF16), pltpu.VMEM((D, tn), BF16)],
        compiler_params=_cparams(("arbitrary", "arbitrary")),
        name="ffn_up",
    )(h, w_up, w_up, conv_w, conv_w, conv_b, conv_b)


DOWN_TM = 256


def _ffn_down_kernel(a_ref, w_ref, x_ref, gt_ref, g_ref, sh_ref, sc_ref, o_ref, h_ref):
    x = x_ref[...] + gt_ref[0] * jnp.dot(a_ref[...], w_ref[0], preferred_element_type=F32)
    o_ref[...] = x
    h_ref[...] = _modulated_norm(x, g_ref[0], sh_ref[0], sc_ref[0])


def _ffn_down_final_kernel(a_ref, w_ref, x_ref, gt_ref, g_ref, yp_ref, ys_ref):
    x = x_ref[...] + gt_ref[0] * jnp.dot(a_ref[...], w_ref[0], preferred_element_type=F32)
    y = x * lax.rsqrt(jnp.mean(x * x, axis=-1, keepdims=True) + NORM_EPS) * g_ref[...]
    is_context = pl.program_id(0) < T_P // DOWN_TM

    @pl.when(is_context)
    def _():
        yp_ref[...] = y

    @pl.when(jnp.logical_not(is_context))
    def _():
        ys_ref[...] = y


def _ffn_down(act, w_down, x, mod, layer, ln1_g, final_g):
    tm = DOWN_TM
    last = layer == DEPTH - 1
    mod_spec = lambda lyr, chunk: pl.BlockSpec((1, 1, D), lambda i: (lyr * N_COND + _cond_row(i, tm), 0, chunk))
    row_spec = pl.BlockSpec((tm, D), lambda i: (i, 0))
    in_specs = [
        pl.BlockSpec((tm, D_FF), lambda i: (i, 0)),
        pl.BlockSpec((1, D_FF, D), lambda i: (layer, 0, 0), pipeline_mode=pl.Buffered(1)),
        row_spec,
        mod_spec(layer, 5),
    ]
    if last:
        n_ctx = T_P // tm
        return pl.pallas_call(
            _ffn_down_final_kernel,
            out_shape=[jax.ShapeDtypeStruct((T_P, D), F32), jax.ShapeDtypeStruct((T_S, D), F32)],
            grid=(T // tm,),
            in_specs=in_specs + [pl.BlockSpec((1, D), lambda i: (0, 0))],
            out_specs=[pl.BlockSpec((tm, D), lambda i: (jnp.minimum(i, n_ctx - 1), 0)),
                       pl.BlockSpec((tm, D), lambda i: (jnp.maximum(i - n_ctx, 0), 0))],
            compiler_params=_cparams(("arbitrary",)),
            name="ffn_down_final",
        )(act, w_down, x, mod, final_g.reshape(1, D))
    return pl.pallas_call(
        _ffn_down_kernel,
        out_shape=[jax.ShapeDtypeStruct((T, D), F32), jax.ShapeDtypeStruct((T, D), BF16)],
        grid=(T // tm,),
        in_specs=in_specs + [pl.BlockSpec((1, 1, D), lambda i: (layer + 1, 0, 0)),
                             mod_spec(layer + 1, 0), mod_spec(layer + 1, 1)],
        out_specs=[row_spec, row_spec],
        compiler_params=_cparams(("arbitrary",)),
        name="ffn_down",
    )(act, w_down, x, mod, ln1_g.reshape(DEPTH, 1, D), mod, mod)


ATT_SCALE = NA_DH ** -0.5


def _nt_dot(a, b):
    return lax.dot_general(a, b, (((1,), (1,)), ((), ())), preferred_element_type=F32)


def _head_slices():
    return [slice(h * NA_DH, (h + 1) * NA_DH) for h in range(NA_HEADS)]


def _ctx_attn_kernel(q_ref, k_ref, v_ref, o_ref, ko_ref, vo_ref):
    heads = _head_slices()
    q = [(q_ref[:, sl] * ATT_SCALE).astype(BF16) for sl in heads]
    s = [_nt_dot(q[h], k_ref[:, sl].astype(BF16)) for h, sl in enumerate(heads)]
    e = [x - jnp.max(x, axis=-1, keepdims=True) for x in s]
    e = [jnp.exp(x) for x in e]
    o = [jnp.dot(e[h].astype(BF16), v_ref[:, sl].astype(BF16), preferred_element_type=F32)
         for h, sl in enumerate(heads)]
    for h, sl in enumerate(heads):
        o_ref[:, sl] = (o[h] / jnp.sum(e[h], axis=-1, keepdims=True)).astype(BF16)
    ko_ref[0, 0] = k_ref[...]
    vo_ref[0, 0] = v_ref[...]


def _ctx_attention(proj, layer, caches):
    cache_shape = jax.ShapeDtypeStruct((BATCH, DEPTH, SEQ, NA_WIDTH), F32)
    cache_spec = pl.BlockSpec((1, 1, SEQ, NA_WIDTH), lambda b: (b, layer, 0, 0))
    qkv_specs = [pl.BlockSpec((SEQ, NA_WIDTH), lambda b, j=j: (b, j)) for j in range(3)]
    out_shape = [jax.ShapeDtypeStruct((T, NA_WIDTH), BF16), cache_shape, cache_shape]
    out_specs = [pl.BlockSpec((SEQ, NA_WIDTH), lambda b: (b, 0)), cache_spec, cache_spec]
    if caches is None:
        return pl.pallas_call(
            _ctx_attn_kernel, out_shape=out_shape, grid=(BATCH,), in_specs=qkv_specs, out_specs=out_specs,
            compiler_params=_cparams(("arbitrary",)), name="ctx_attention",
        )(proj, proj, proj)

    def body(q_ref, k_ref, v_ref, kc_ref, vc_ref, o_ref, ko_ref, vo_ref):
        del kc_ref, vc_ref
        _ctx_attn_kernel(q_ref, k_ref, v_ref, o_ref, ko_ref, vo_ref)

    any_spec = pl.BlockSpec(memory_space=pl.ANY)
    return pl.pallas_call(
        body, out_shape=out_shape, grid=(BATCH,), in_specs=qkv_specs + [any_spec, any_spec],
        out_specs=out_specs, input_output_aliases={3: 1, 4: 2},
        compiler_params=_cparams(("arbitrary",)), name="ctx_attention",
    )(proj, proj, proj, *caches)


NA_ROWS = DEC_SEQ // GRID_W
NA_WIN_TOK = NA_WIN_ROWS * GRID_W


def _na_window_start(r):
    return jnp.clip(r - NA_WIN_ROWS // 2, 0, NA_ROWS - NA_WIN_ROWS)


def _na_bias_table(rpb):
    col = np.arange(GRID_W)
    col_start = np.clip(col - NA_WIN_COLS // 2, 0, GRID_W - NA_WIN_COLS)
    col_mask = (col[None, :] >= col_start[:, None]) & (col[None, :] < col_start[:, None] + NA_WIN_COLS)
    rel_col = np.clip(col[None, :] - col[:, None] + NA_WIN_COLS - 1, 0, 2 * NA_WIN_COLS - 2)
    banded = jnp.where(col_mask, rpb[:, :, rel_col], NEG_INF)
    per_offset = [banded[:, NA_WIN_ROWS - 1 - d:2 * NA_WIN_ROWS - 1 - d] for d in range(NA_WIN_ROWS)]
    tab = jnp.stack(per_offset, axis=0)
    return tab.transpose(0, 1, 3, 2, 4).reshape(NA_WIN_ROWS, NA_HEADS, GRID_W, NA_WIN_TOK)


def _na_attn_kernel(q_ref, k_ref, v_ref, ck_ref, cv_ref, bias_ref, y_in_ref, o_ref):
    del y_in_ref
    r = pl.program_id(1)
    start = pl.multiple_of(_na_window_start(r) * GRID_W, GRID_W)
    win = pl.ds(start, NA_WIN_TOK)
    heads = _head_slices()
    nh = range(NA_HEADS)
    q = [(q_ref[:, sl] * ATT_SCALE).astype(BF16) for sl in heads]
    s_win = [_nt_dot(q[h], k_ref[win, sl].astype(BF16)) + bias_ref[0, h] for h, sl in enumerate(heads)]
    s_ctx = [_nt_dot(q[h], ck_ref[0, 0, :, sl].astype(BF16)) for h, sl in enumerate(heads)]
    m = [jnp.maximum(jnp.max(s_win[h], axis=-1, keepdims=True), jnp.max(s_ctx[h], axis=-1, keepdims=True))
         for h in nh]
    e_win = [jnp.exp(s_win[h] - m[h]) for h in nh]
    e_ctx = [jnp.exp(s_ctx[h] - m[h]) for h in nh]
    o_win = [jnp.dot(e_win[h].astype(BF16), v_ref[win, sl].astype(BF16), preferred_element_type=F32)
             for h, sl in enumerate(heads)]
    o_ctx = [jnp.dot(e_ctx[h].astype(BF16), cv_ref[0, 0, :, sl].astype(BF16), preferred_element_type=F32)
             for h, sl in enumerate(heads)]
    for h, sl in enumerate(heads):
        denom = jnp.sum(e_win[h], axis=-1, keepdims=True) + jnp.sum(e_ctx[h], axis=-1, keepdims=True)
        o_ref[:, sl] = ((o_win[h] + o_ctx[h]) / denom).astype(BF16)


def _na_attention(proj, cache_k, cache_v, rpb, y_na, layer):
    q_row0 = T_P // GRID_W
    seq0 = T_P // DEC_SEQ
    kv_spec = lambda j: pl.BlockSpec((DEC_SEQ, NA_WIDTH), lambda b, r: (seq0 + b, j))
    ctx_spec = pl.BlockSpec((1, 1, SEQ, NA_WIDTH), lambda b, r: (b, layer, 0, 0))
    return pl.pallas_call(
        _na_attn_kernel,
        out_shape=jax.ShapeDtypeStruct((T, NA_WIDTH), BF16),
        grid=(DEC_BATCH, NA_ROWS),
        in_specs=[
            pl.BlockSpec((GRID_W, NA_WIDTH), lambda b, r: (q_row0 + b * NA_ROWS + r, 0)),
            kv_spec(1), kv_spec(2), ctx_spec, ctx_spec,
            pl.BlockSpec((1, NA_HEADS, GRID_W, NA_WIN_TOK), lambda b, r: (r - _na_window_start(r), 0, 0, 0)),
            pl.BlockSpec(memory_space=pl.ANY),
        ],
        out_specs=pl.BlockSpec((GRID_W, NA_WIDTH), lambda b, r: (q_row0 + b * NA_ROWS + r, 0)),
        input_output_aliases={6: 0},
        compiler_params=_cparams(("arbitrary", "arbitrary")),
        name="na_attention",
    )(proj, proj, proj, cache_k.reshape(DEC_BATCH, DEPTH, SEQ, NA_WIDTH),
      cache_v.reshape(DEC_BATCH, DEPTH, SEQ, NA_WIDTH), _na_bias_table(rpb), y_na)


HY_CB = 256


def _split_bf16(x):
    hi = x.astype(BF16)
    return hi, (x - hi.astype(F32)).astype(BF16)


def _dot3(a_hi, a_lo, b):
    b_hi, b_lo = _split_bf16(b)
    return (jnp.dot(a_hi, b_hi, preferred_element_type=F32) + jnp.dot(a_lo, b_hi, preferred_element_type=F32)
            + jnp.dot(a_hi, b_lo, preferred_element_type=F32))


def _trig_table_kernel(*out_refs, length, freq_on_rows, scale):
    rows, cols = out_refs[0].shape
    r = lax.broadcasted_iota(jnp.int32, (rows, cols), 0) + pl.program_id(0) * rows
    c = lax.broadcasted_iota(jnp.int32, (rows, cols), 1)
    f, t = (r, c) if freq_on_rows else (c, r)
    k = f & (length - 1)
    quarter = jnp.where(f >= length, length, 0)
    phase = (2 * k + 1) * t + (-quarter if freq_on_rows else quarter)
    phase = phase & (4 * length - 1)
    phase = jnp.where(phase >= 2 * length, phase - 4 * length, phase)
    val = jnp.cos(phase.astype(F32) * (math.pi / (2 * length))) * scale
    for ref, part in zip(out_refs, _split_bf16(val)):
        ref[...] = part


def _trig_table(length, freq_on_rows):
    shape = (2 * length, length) if freq_on_rows else (length, 2 * length)
    tr = 256
    n_parts = 2 if freq_on_rows else 1
    spec = pl.BlockSpec((tr, shape[1]), lambda i: (i, 0))
    return pl.pallas_call(
        functools.partial(_trig_table_kernel, length=length, freq_on_rows=freq_on_rows,
                          scale=1.0 if freq_on_rows else 1.0 / length),
        out_shape=[jax.ShapeDtypeStruct(shape, BF16)] * n_parts,
        grid=(shape[0] // tr,), out_specs=[spec] * n_parts,
        compiler_params=_cparams(("arbitrary",)), name="dft_table",
    )()


def _hy_filter_kernel(tn_ref, band_ref, w1_ref, b1_ref, fr_ref, w2_ref, b2_ref, w3_ref, dec_ref,
                      fhi_ref, flo_ref, o_ref, *, length):
    t_norm = tn_ref[...]
    t_idx = lax.broadcasted_iota(jnp.int32, (length, LANE), 0).astype(F32)
    lane = lax.broadcasted_iota(jnp.int32, (length, LANE), 1)
    ang = (2.0 * math.pi / length) * t_idx * band_ref[...]
    z = jnp.where(lane == 0, t_norm,
                  jnp.where(lane <= HY_BANDS, jnp.cos(ang), jnp.where(lane <= 2 * HY_BANDS, jnp.sin(ang), 0.0)))
    hdn = jnp.sin(fr_ref[0:1] * (jnp.dot(z, w1_ref[...], precision=HIGHEST, preferred_element_type=F32)
                                 + b1_ref[...]))
    hdn = jnp.sin(fr_ref[1:2] * (jnp.dot(hdn, w2_ref[...], precision=HIGHEST, preferred_element_type=F32)
                                 + b2_ref[...]))
    filt = jnp.dot(hdn, w3_ref[...], precision=HIGHEST, preferred_element_type=F32)
    filt = filt * jnp.exp(-t_norm * jnp.abs(dec_ref[...]))
    first = lax.broadcasted_iota(jnp.int32, (length, HY_WIDTH), 0) == 0
    for o in range(HY_ORDER):
        fwd = filt[:, (2 * o) * HY_WIDTH:(2 * o + 1) * HY_WIDTH]
        bwd = filt[:, (2 * o + 1) * HY_WIDTH:(2 * o + 2) * HY_WIDTH]
        bwd = jnp.where(first, 0.0, pltpu.roll(bwd, 1, 0))
        o_ref[o, :length] = _dot3(fhi_ref[:length], flo_ref[:length], fwd + bwd)
        o_ref[o, length:] = _dot3(fhi_ref[length:], flo_ref[length:], bwd - fwd)


def _hy_filter_spectrum(length, f_hi, f_lo, w1, b1, freq, w2, b2, w3, decay):
    emb = 1 + 2 * HY_BANDS
    t_norm = jnp.linspace(0.0, 1.0, length, dtype=F32).reshape(length, 1)
    bands = np.zeros((1, LANE), np.float32)
    bands[0, 1:1 + HY_BANDS] = bands[0, 1 + HY_BANDS:emb] = np.linspace(1e-4, HY_BANDS - 1, HY_BANDS,
                                                                        dtype=np.float32)
    w1p = jnp.zeros((LANE, w1.shape[1]), F32).at[:emb].set(w1)
    return pl.pallas_call(
        functools.partial(_hy_filter_kernel, length=length),
        out_shape=jax.ShapeDtypeStruct((HY_ORDER, 2 * length, HY_WIDTH), F32),
        compiler_params=pltpu.CompilerParams(vmem_limit_bytes=VMEM_LIMIT),
        name="hyena_filter",
    )(t_norm, jnp.asarray(bands), w1p, b1.reshape(1, -1), freq, w2, b2.reshape(1, -1), w3,
      decay.reshape(1, -1), f_hi, f_lo)


def _hyena_kernel(v_ref, x1_ref, x2_ref, cwv_ref, cw1_ref, cw2_ref, cbv_ref, cb1_ref, cb2_ref, skip_ref,
                  hs_ref, f_ref, g_ref, *rest, length):
    o_ref = rest[-1]
    row = lax.broadcasted_iota(jnp.int32, v_ref.shape, 0)
    first, last = row == 0, row == length - 1

    def short_conv(u_ref, cw_ref, cb_ref):
        u = u_ref[...]
        prev = jnp.where(first, 0.0, pltpu.roll(u, 1, 0))
        nxt = jnp.where(last, 0.0, pltpu.roll(u, length - 1, 0))
        return prev * cw_ref[0:1] + u * cw_ref[1:2] + nxt * cw_ref[2:3] + cb_ref[...]

    z = short_conv(v_ref, cwv_ref, cbv_ref)
    gates = (short_conv(x1_ref, cw1_ref, cb1_ref), short_conv(x2_ref, cw2_ref, cb2_ref))
    for o in range(HY_ORDER):
        zf = jnp.dot(f_ref[...], z.astype(BF16), preferred_element_type=F32)
        zc, zs = zf[:length], zf[length:]
        h_re, h_im = hs_ref[o, :length], hs_ref[o, length:]
        p = jnp.concatenate([zc * h_re + zs * h_im, zc * h_im - zs * h_re], axis=0)
        conv = jnp.dot(g_ref[...], p.astype(BF16), preferred_element_type=F32)
        z = gates[o] * (conv + z * skip_ref[o:o + 1])
    o_ref[...] = z.astype(BF16)


def _hyena(proj, y_hy, layer, length, tables, spectrum, conv_w, conv_b, skip):
    nseq = (T_P if length == SEQ else T_S) // length
    row0 = 0 if length == SEQ else T_P // length
    cb = HY_WIDTH if length == SEQ else HY_CB
    ncb = HY_WIDTH // cb
    col0 = C_HY // cb
    u_spec = lambda part: pl.BlockSpec((length, cb), lambda s, j: (row0 + s, col0 + part * ncb + j))
    cw_spec = lambda part: pl.BlockSpec((1, 3, cb), lambda s, j: (layer, 0, part * ncb + j))
    cb_spec = lambda part: pl.BlockSpec((1, 1, cb), lambda s, j: (layer, 0, part * ncb + j))
    resident = functools.partial(pl.BlockSpec, pipeline_mode=pl.Buffered(1))
    tab_specs = [resident(t.shape, lambda s, j: (0, 0)) for t in tables]
    in_specs = ([u_spec(0), u_spec(1), u_spec(2), cw_spec(0), cw_spec(1), cw_spec(2),
                 cb_spec(0), cb_spec(1), cb_spec(2),
                 pl.BlockSpec((1, HY_ORDER, cb), lambda s, j: (layer, 0, j)),
                 pl.BlockSpec((HY_ORDER, 2 * length, cb), lambda s, j: (0, 0, j))] + tab_specs)
    args = [proj] * 3 + [conv_w] * 3 + [conv_b.reshape(DEPTH, 1, -1)] * 3 + [skip, spectrum] + list(tables)
    aliases = {}
    if y_hy is not None:
        in_specs.append(pl.BlockSpec(memory_space=pl.ANY))
        args.append(y_hy)
        aliases = {len(args) - 1: 0}

    def body(v_ref, x1_ref, x2_ref, cwv, cw1, cw2, cbv, cb1, cb2, skip_ref, hs_ref, *rest):
        _hyena_kernel(v_ref, x1_ref, x2_ref, cwv.at[0], cw1.at[0], cw2.at[0], cbv.at[0], cb1.at[0], cb2.at[0],
                      skip_ref.at[0], hs_ref, *rest, length=length)

    return pl.pallas_call(
        body,
        out_shape=jax.ShapeDtypeStruct((T, HY_WIDTH), BF16),
        grid=(nseq, ncb), in_specs=in_specs,
        out_specs=pl.BlockSpec((length, cb), lambda s, j: (row0 + s, j)),
        input_output_aliases=aliases,
        compiler_params=_cparams(("arbitrary", "arbitrary")),
        name="hyena",
    )(*args)


GC = 128
GDN_SCALE = GDN_DK ** -0.5
GDN_CTX_PER_STEP = 2


def _state_slot(b, d):
    return (2 * b + d) * GDN_HEADS


def _merge_masks(lower):
    ri = lax.broadcasted_iota(jnp.int32, (GC, GC), 0)
    ci = lax.broadcasted_iota(jnp.int32, (GC, GC), 1)
    hi, lo = (ri, ci) if lower else (ci, ri)
    return [((hi >> (b + 1)) == (lo >> (b + 1))) & ((hi >> b) > (lo >> b)) for b in range(int(math.log2(GC)))]


def _unit_tri_inverse(a, masks):
    ri = lax.broadcasted_iota(jnp.int32, (GC, GC), 0)
    ci = lax.broadcasted_iota(jnp.int32, (GC, GC), 1)
    eye = (ri == ci).astype(F32)
    n = range(len(a))
    t = [eye - jnp.where(masks[i][0], a[i], 0.0) for i in n]
    for level in range(1, len(masks[0])):
        tb = [t[i].astype(BF16) for i in n]
        ta = [jnp.dot(tb[i], jnp.where(masks[i][level], a[i], 0.0).astype(BF16), preferred_element_type=F32)
              for i in n]
        tat = [jnp.dot(ta[i].astype(BF16), tb[i], preferred_element_type=F32) for i in n]
        t = [t[i] - tat[i] for i in n]
    return t


def _gdn_kernel(q_ref, k_ref, v_ref, z_ref, ba_ref, cwq_ref, cwk_ref, cwv_ref, alog_ref, dt_ref, ng_ref,
                sf0_ref, sb0_ref, *rest, length, nb):
    y_ref, sf_ref, sb_ref, qn_ref, kn_ref, vn_ref, beta_ref, g_ref, of_ref, ob_ref, s_ref = rest[-11:]
    n_chunks = length // GC
    rows_total = nb * length
    row = lax.broadcasted_iota(jnp.int32, (rows_total, GDN_WIDTH), 0) & (length - 1)
    first, last = row == 0, row == length - 1

    def conv_silu(u_ref, cw_ref):
        u = u_ref[...]
        prev = jnp.where(first, 0.0, pltpu.roll(u, 1, 0))
        nxt = jnp.where(last, 0.0, pltpu.roll(u, rows_total - 1, 0))
        c = prev * cw_ref[0:1] + u * cw_ref[1:2] + nxt * cw_ref[2:3]
        return c * jax.nn.sigmoid(c)

    q = conv_silu(q_ref, cwq_ref)
    k = conv_silu(k_ref, cwk_ref)
    vn_ref[...] = conv_silu(v_ref, cwv_ref)
    for h in range(GDN_HEADS):
        sl = slice(h * GDN_DK, (h + 1) * GDN_DK)
        qh, kh = q[:, sl], k[:, sl]
        qn_ref[:, sl] = qh * lax.rsqrt(jnp.sum(qh * qh, axis=-1, keepdims=True) + NORM_EPS) * GDN_SCALE
        kn_ref[:, sl] = kh * lax.rsqrt(jnp.sum(kh * kh, axis=-1, keepdims=True) + NORM_EPS)
    ba = ba_ref[...]
    beta_ref[...] = jax.nn.sigmoid(ba)
    g_ref[...] = -jnp.exp(alog_ref[...]) * jax.nn.softplus(ba + dt_ref[...])

    ri = lax.broadcasted_iota(jnp.int32, (GC, GC), 0)
    ci = lax.broadcasted_iota(jnp.int32, (GC, GC), 1)
    lower = (ri >= ci).astype(F32)
    upper = (ri <= ci).astype(F32)

    def chunk_step(n, carry):
        scans = [(b, d) for b in range(nb) for d in range(2)]
        rows = {(b, d): pl.ds(pl.multiple_of(b * length + (n if d == 0 else n_chunks - 1 - n) * GC, GC), GC)
                for b, d in scans}
        loaded = {}
        for bd in scans:
            loaded[bd] = (g_ref[rows[bd], :], beta_ref[rows[bd], :])
            for h in range(GDN_HEADS):
                sl = slice(h * GDN_DK, (h + 1) * GDN_DK)
                loaded[bd + (h,)] = (qn_ref[rows[bd], sl], kn_ref[rows[bd], sl], vn_ref[rows[bd], sl],
                                     s_ref[_state_slot(*bd) + h])
        units = [(b, d, h) for b, d in scans for h in range(GDN_HEADS)]
        incl = [(ri >= ci), (ri <= ci)]
        strict = [(ri > ci), (ri < ci)]
        masks = [_merge_masks(True), _merge_masks(False)]
        gcs = {(b, d): jnp.dot((lower, upper)[d], loaded[b, d][0], precision=HIGHEST, preferred_element_type=F32)
               for b, d in scans}
        gcs_t = {bd: g.T for bd, g in gcs.items()}
        edge = [GC - 1, 0]
        g_col = [gcs[b, d][:, 8 + 4 * d + h:9 + 4 * d + h] for b, d, h in units]
        g_row = [gcs_t[b, d][8 + 4 * d + h:9 + 4 * d + h, :] for b, d, h in units]
        g_end = [gcs_t[b, d][8 + 4 * d + h:9 + 4 * d + h, edge[d]:edge[d] + 1] for b, d, h in units]
        b_col = [loaded[b, d][1][:, 4 * d + h:4 * d + h + 1] for b, d, h in units]
        qc, kc, vc, st = (list(x) for x in zip(*(loaded[u] for u in units)))
        nu = range(len(units))
        decay = [jnp.where(incl[d], jnp.exp(jnp.where(incl[d], g_col[i] - g_row[i], 0.0)), 0.0)
                 for i, (b, d, h) in enumerate(units)]
        kb = [kc[i] * b_col[i] for i in nu]
        kcb = [kc[i].astype(BF16) for i in nu]
        kk = [_nt_dot(kb[i].astype(BF16), kcb[i]) for i in nu]
        qk = [_nt_dot(qc[i].astype(BF16), kcb[i]) for i in nu]
        a = [jnp.where(strict[d], kk[i] * decay[i], 0.0) for i, (b, d, h) in enumerate(units)]
        t = _unit_tri_inverse(a, [masks[d] for b, d, h in units])
        e_col = [jnp.exp(g_col[i]) for i in nu]
        rhs = [jnp.concatenate([vc[i] * b_col[i], kb[i] * e_col[i]], axis=1).astype(BF16) for i in nu]
        sol = [jnp.dot(t[i].astype(BF16), rhs[i], preferred_element_type=F32) for i in nu]
        attn = [jnp.where(incl[d], qk[i] * decay[i], 0.0).astype(BF16) for i, (b, d, h) in enumerate(units)]
        sb = [st[i].astype(BF16) for i in nu]
        ws = [jnp.dot(sol[i][:, GDN_DV:].astype(BF16), sb[i], preferred_element_type=F32) for i in nu]
        qs = [jnp.dot((qc[i] * e_col[i]).astype(BF16), sb[i], preferred_element_type=F32) for i in nu]
        v_new = [(sol[i][:, :GDN_DV] - ws[i]).astype(BF16) for i in nu]
        av = [jnp.dot(attn[i], v_new[i], preferred_element_type=F32) for i in nu]
        k_dec_t = [(kc[i] * jnp.exp(g_end[i] - g_col[i])).T.astype(BF16) for i in nu]
        kv = [jnp.dot(k_dec_t[i], v_new[i], preferred_element_type=F32) for i in nu]
        results = [(qs[i] + av[i], st[i] * jnp.exp(g_end[i]) + kv[i]) for i in nu]
        for (b, d, h), (o, s) in zip(units, results):
            (of_ref if d == 0 else ob_ref)[rows[b, d], h * GDN_DV:(h + 1) * GDN_DV] = o
            s_ref[_state_slot(b, d) + h] = s
        return carry

    for b in range(nb):
        s_ref[pl.ds(_state_slot(b, 0), GDN_HEADS)] = sf0_ref[b]
        s_ref[pl.ds(_state_slot(b, 1), GDN_HEADS)] = sb0_ref[b]
    lax.fori_loop(0, n_chunks, chunk_step, 0)
    for b in range(nb):
        sf_ref[b, 0] = s_ref[pl.ds(_state_slot(b, 0), GDN_HEADS)]
        sb_ref[b, 0] = s_ref[pl.ds(_state_slot(b, 1), GDN_HEADS)]
    for h in range(GDN_HEADS):
        sl = slice(h * GDN_DV, (h + 1) * GDN_DV)
        o = of_ref[:, sl] + ob_ref[:, sl]
        o = o * lax.rsqrt(jnp.mean(o * o, axis=-1, keepdims=True) + NORM_EPS) * ng_ref[...]
        zh = z_ref[:, sl]
        y_ref[:, sl] = (o * (zh * jax.nn.sigmoid(zh))).astype(BF16)


def _gdn(proj, y_gdn, states_out, layer, length, s_f0, s_b0, conv_w, a_log, dt_bias, norm_g):
    nb = GDN_CTX_PER_STEP if length == SEQ else 1
    rows = nb * length
    nblk = (T_P if length == SEQ else T_S) // rows
    row0 = 0 if length == SEQ else T_P // rows
    col = lambda c: c // GDN_WIDTH
    blk = lambda c: pl.BlockSpec((rows, GDN_WIDTH), lambda s: (row0 + s, col(c)))
    cw = lambda part: pl.BlockSpec((1, 3, GDN_WIDTH), lambda s: (layer, 0, part))
    vec = lambda: pl.BlockSpec((1, 1, LANE), lambda s: (layer, 0, 0))
    st_in = pl.BlockSpec((nb, GDN_HEADS, GDN_DK, GDN_DV), lambda s: (s, 0, 0, 0))
    st_out = pl.BlockSpec((nb, 1, GDN_HEADS, GDN_DK, GDN_DV), lambda s: (s, layer, 0, 0, 0))
    pad_lanes = lambda x, off: jnp.zeros((DEPTH, 1, LANE), F32).at[:, 0, off:off + 2 * GDN_HEADS].set(
        x.reshape(DEPTH, -1))
    in_specs = [blk(C_GDN), blk(C_GDN + GDN_WIDTH), blk(C_GDN + 2 * GDN_WIDTH), blk(C_Z),
                pl.BlockSpec((rows, LANE), lambda s: (row0 + s, C_BA // LANE)),
                cw(0), cw(1), cw(2), vec(), vec(), vec(), st_in, st_in]
    args = [proj] * 5 + [conv_w] * 3 + [pad_lanes(a_log, 2 * GDN_HEADS), pad_lanes(dt_bias, 2 * GDN_HEADS),
                                        norm_g.reshape(DEPTH, 1, GDN_DV), s_f0, s_b0]
    st_shape = jax.ShapeDtypeStruct((nblk * nb, DEPTH, GDN_HEADS, GDN_DK, GDN_DV), F32)
    aliases = {}
    for carried, out_idx in ((y_gdn, 0),) + (((states_out[0], 1), (states_out[1], 2)) if states_out else ()):
        if carried is not None:
            in_specs.append(pl.BlockSpec(memory_space=pl.ANY))
            args.append(carried)
            aliases[len(args) - 1] = out_idx

    def body(q_ref, k_ref, v_ref, z_ref, ba_ref, cwq, cwk, cwv, alog, dt, ng, sf0, sb0, *rest):
        _gdn_kernel(q_ref, k_ref, v_ref, z_ref, ba_ref, cwq.at[0], cwk.at[0], cwv.at[0], alog.at[0], dt.at[0],
                    ng.at[0], sf0, sb0, *rest, length=length, nb=nb)

    seq_buf = lambda w: pltpu.VMEM((rows, w), F32)
    return pl.pallas_call(
        body,
        out_shape=[jax.ShapeDtypeStruct((T, GDN_WIDTH), BF16), st_shape, st_shape],
        grid=(nblk,), in_specs=in_specs,
        out_specs=[pl.BlockSpec((rows, GDN_WIDTH), lambda s: (row0 + s, 0)), st_out, st_out],
        scratch_shapes=[seq_buf(GDN_WIDTH), seq_buf(GDN_WIDTH), seq_buf(GDN_WIDTH), seq_buf(LANE), seq_buf(LANE),
                        seq_buf(GDN_WIDTH), seq_buf(GDN_WIDTH),
                        pltpu.VMEM((nb * 2 * GDN_HEADS, GDN_DK, GDN_DV), F32)],
        input_output_aliases=aliases,
        compiler_params=_cparams(("arbitrary",)),
        name="gated_deltanet",
    )(*args)


def kernel(x_prompt, x_sample, cache_k, cache_v, state_fwd, state_bwd, c, c_ctx, ln1_g, ln2_g, w_mod, b_mod,
           w_in, na_rpb, hy_conv_w, hy_conv_b, hy_w1, hy_b1, hy_freq, hy_w2, hy_b2, hy_w3, hy_decay, hy_skip,
           gdn_conv_w, gdn_a_log, gdn_dt_bias, gdn_norm_g, w_pa, w_pb, w_pc, b_gate, w_out, ffn_w_up,
           ffn_conv_w, ffn_conv_b, ffn_w_down, final_g):
    x = (x_prompt.reshape(T_P, D), x_sample.reshape(T_S, D))
    cond = jnp.concatenate([c_ctx[None], c, jnp.zeros((N_COND - 1 - DEC_BATCH, D), F32)], axis=0)
    mod = _mod_table(cond, w_mod, b_mod).reshape(DEPTH * N_COND, 1, 6 * D)

    w_pa_b, w_pb_b, w_pc_b, w_out_b = (w.astype(BF16) for w in (w_pa, w_pb, w_pc, w_out))
    w_down_b = ffn_w_down.astype(BF16)
    w_in_t = jnp.swapaxes(w_in, 1, 2)
    tables = {n: tuple(_trig_table(n, True)) + tuple(_trig_table(n, False)) for n in (SEQ, DEC_SEQ)}
    zero_state = jnp.zeros((BATCH, GDN_HEADS, GDN_DK, GDN_DV), F32)

    caches = None
    states = None
    h = _norm_mod(x, ln1_g, mod, 0, 0)
    for layer in range(DEPTH):
        proj = _in_proj(h, w_in_t, layer)

        y_na, new_k, new_v = _ctx_attention(proj, layer, caches)
        caches = (new_k, new_v)
        y_na = _na_attention(proj, cache_k, cache_v, na_rpb[layer], y_na, layer)

        y_hy = None
        for n in (SEQ, DEC_SEQ):
            spectrum = _hy_filter_spectrum(n, tables[n][0], tables[n][1], hy_w1[layer], hy_b1[layer],
                                           hy_freq[layer], hy_w2[layer], hy_b2[layer], hy_w3[layer],
                                           hy_decay[layer])
            y_hy = _hyena(proj, y_hy, layer, n, (tables[n][0], tables[n][2]), spectrum,
                          hy_conv_w, hy_conv_b, hy_skip)

        y_gdn, s_f, s_b = _gdn(proj, None, states, layer, SEQ, zero_state, zero_state,
                               gdn_conv_w, gdn_a_log, gdn_dt_bias, gdn_norm_g)
        states = (s_f, s_b)
        y_gdn = _gdn(proj, y_gdn, None, layer, DEC_SEQ, state_fwd[:, layer], state_bwd[:, layer],
                     gdn_conv_w, gdn_a_log, gdn_dt_bias, gdn_norm_g)[0]

        x, h = _mix_out(y_na, y_hy, y_gdn, proj, b_gate, w_pa_b, w_pb_b, w_pc_b, w_out_b, x, mod, ln2_g, layer)
        act = _ffn_up(h, ffn_w_up, ffn_conv_w, ffn_conv_b, layer)
        x, h = _ffn_down(act, w_down_b, x, mod, layer, ln1_g, final_g)

    y_p, y_s = x, h
    cache_shape = (BATCH, DEPTH, SEQ, NA_HEADS, NA_DH)
    return (y_p.reshape(BATCH, SEQ, D), y_s.reshape(DEC_BATCH, DEC_SEQ, D),
            caches[0].reshape(cache_shape), caches[1].reshape(cache_shape), states[0], states[1])
```

```python
import functools
import math

import jax
import jax.numpy as jnp
import numpy as np
from jax import lax
from jax.experimental import pallas as pl
from jax.experimental.pallas import tpu as pltpu

F32 = jnp.float32
BF16 = jnp.bfloat16

D = 2048
BATCH, SEQ = 32, 256
DEC_BATCH, DEC_SEQ = 4, 1024
DEPTH = 2
GRID_W = 64
NA_HEADS, NA_DH = 8, 128
NA_WIDTH = NA_HEADS * NA_DH
NA_WIN_ROWS, NA_WIN_COLS = 8, 16
HY_WIDTH = 512
HY_ORDER = 2
HY_BANDS = 16
GDN_HEADS, GDN_DK, GDN_DV = 4, 128, 128
GDN_WIDTH = GDN_HEADS * GDN_DV
D_FF = 5632
N_BRANCH = 3
NORM_EPS = 1e-6
NEG_INF = -1e30

T_P = BATCH * SEQ
T_S = DEC_BATCH * DEC_SEQ
T = T_P + T_S
N_COND = 8

C_Q, C_K, C_V = 0, NA_WIDTH, 2 * NA_WIDTH
C_HY = 3 * NA_WIDTH
C_GDN = C_HY + 3 * HY_WIDTH
C_Z = C_GDN + 3 * GDN_HEADS * GDN_DK
C_BA = C_Z + GDN_WIDTH
C_GATE = C_BA + 4 * GDN_HEADS
N_IN = C_GATE + N_BRANCH * D
GATE_BLOCK = C_BA
GATE_SHIFT = C_GATE - C_BA

LANE = 128
VMEM_LIMIT = 56 * 1024 * 1024
HIGHEST = lax.Precision.HIGHEST


def _cparams(sem):
    return pltpu.CompilerParams(dimension_semantics=sem, vmem_limit_bytes=VMEM_LIMIT)


def _cond_row(tile, rows_per_tile):
    first_latent = T_P // rows_per_tile
    per_seq = DEC_SEQ // rows_per_tile
    return jnp.where(tile < first_latent, 0, 1 + (tile - first_latent) // per_seq)


def _mod_kernel(c_ref, w_ref, b_ref, o_ref):
    c = c_ref[...]
    s = (c * jax.nn.sigmoid(c)).astype(BF16)
    o_ref[0] = jnp.dot(s, w_ref[0].astype(BF16), preferred_element_type=F32) + b_ref[0]


def _mod_table(cond, w_mod, b_mod):
    tn = 1024
    n = 6 * D
    return pl.pallas_call(
        _mod_kernel,
        out_shape=jax.ShapeDtypeStruct((DEPTH, N_COND, n), F32),
        grid=(DEPTH, n // tn),
        in_specs=[
            pl.BlockSpec((N_COND, D), lambda l, j: (0, 0)),
            pl.BlockSpec((1, D, tn), lambda l, j: (l, 0, j)),
            pl.BlockSpec((1, 1, tn), lambda l, j: (l, 0, j)),
        ],
        out_specs=pl.BlockSpec((1, N_COND, tn), lambda l, j: (l, 0, j)),
        compiler_params=_cparams(("arbitrary", "arbitrary")),
        name="mod_table",
    )(cond, w_mod, b_mod.reshape(DEPTH, 1, n))


def _token_specs(x, tm):
    if not isinstance(x, tuple):
        return [pl.BlockSpec((tm, D), lambda i: (i, 0))], [x]
    n_ctx = T_P // tm
    return ([pl.BlockSpec((tm, D), lambda i: (jnp.minimum(i, n_ctx - 1), 0)),
             pl.BlockSpec((tm, D), lambda i: (jnp.maximum(i - n_ctx, 0), 0))], list(x))


def _token_tile(x_refs, tm):
    if len(x_refs) == 1:
        return x_refs[0][...]
    return jnp.where(pl.program_id(0) < T_P // tm, x_refs[0][...], x_refs[1][...])


def _modulated_norm(x, g, shift, scale):
    y = x * lax.rsqrt(jnp.mean(x * x, axis=-1, keepdims=True) + NORM_EPS) * g
    return (y * (1.0 + scale) + shift).astype(BF16)


NORM_TM = 1024


def _norm_mod_kernel(*refs):
    g_ref, sh_ref, sc_ref, o_ref = refs[-4:]
    o_ref[...] = _modulated_norm(_token_tile(refs[:-4], NORM_TM), g_ref[0], sh_ref[0], sc_ref[0])


def _norm_mod(x, g, mod, layer, shift_chunk):
    tm = NORM_TM
    x_specs, x_args = _token_specs(x, tm)
    return pl.pallas_call(
        _norm_mod_kernel,
        out_shape=jax.ShapeDtypeStruct((T, D), BF16),
        grid=(T // tm,),
        in_specs=x_specs + [
            pl.BlockSpec((1, 1, D), lambda i: (layer, 0, 0)),
            pl.BlockSpec((1, 1, D), lambda i: (layer * N_COND + _cond_row(i, tm), 0, shift_chunk)),
            pl.BlockSpec((1, 1, D), lambda i: (layer * N_COND + _cond_row(i, tm), 0, shift_chunk + 1)),
        ],
        out_specs=pl.BlockSpec((tm, D), lambda i: (i, 0)),
        compiler_params=_cparams(("arbitrary",)),
        name="norm_mod",
    )(*x_args, g.reshape(DEPTH, 1, D), mod, mod)


def _proj_kernel(h_ref, wt_ref, o_ref, wb_ref):
    @pl.when(pl.program_id(1) == 0)
    def _():
        wb_ref[...] = wt_ref[0].astype(BF16)

    o_ref[...] = lax.dot_general(h_ref[...], wb_ref[...], (((1,), (1,)), ((), ())), preferred_element_type=F32)


def _in_proj(h, w_in_t, layer):
    tm, tn = 1536, 1024
    return pl.pallas_call(
        _proj_kernel,
        out_shape=jax.ShapeDtypeStruct((T, N_IN), F32),
        grid=(pl.cdiv(N_IN, tn), T // tm),
        in_specs=[
            pl.BlockSpec((tm, D), lambda j, i: (i, 0)),
            pl.BlockSpec((1, tn, D), lambda j, i: (layer, j, 0)),
        ],
        out_specs=pl.BlockSpec((tm, tn), lambda j, i: (i, j)),
        scratch_shapes=[pltpu.VMEM((tn, D), BF16)],
        compiler_params=_cparams(("arbitrary", "arbitrary")),
        name="in_proj",
    )(h, w_in_t)


MIX_TM = 256


def _mix_out_kernel(yna_ref, yhy_ref, ygdn_ref, gl_ref, bg_ref, wpa_ref, wpb_ref, wpc_ref, wout_ref, *refs):
    gt_ref, g2_ref, sh2_ref, sc2_ref, o_ref, h2_ref = refs[-6:]
    width = gl_ref.shape[1]
    gl = pltpu.roll(gl_ref[...], width - GATE_SHIFT, 1)[:, :N_BRANCH * D] + bg_ref[0]
    gates = jax.nn.sigmoid(gl)
    merged = (gates[:, :D] * jnp.dot(yna_ref[...], wpa_ref[0], preferred_element_type=F32)
              + gates[:, D:2 * D] * jnp.dot(yhy_ref[...], wpb_ref[0], preferred_element_type=F32)
              + gates[:, 2 * D:] * jnp.dot(ygdn_ref[...], wpc_ref[0], preferred_element_type=F32))
    r = jnp.dot(merged.astype(BF16), wout_ref[0], preferred_element_type=F32)
    x = _token_tile(refs[:-6], MIX_TM) + gt_ref[0] * r
    o_ref[...] = x
    h2_ref[...] = _modulated_norm(x, g2_ref[0], sh2_ref[0], sc2_ref[0])


def _mix_out(y_na, y_hy, y_gdn, proj, b_gate, w_pa, w_pb, w_pc, w_out, x, mod, ln2_g, layer):
    tm = MIX_TM
    resident = functools.partial(pl.BlockSpec, pipeline_mode=pl.Buffered(1))
    x_specs, x_args = _token_specs(x, tm)
    mod_spec = lambda chunk: pl.BlockSpec((1, 1, D), lambda i: (layer * N_COND + _cond_row(i, tm), 0, chunk))
    row_spec = pl.BlockSpec((tm, D), lambda i: (i, 0))
    return pl.pallas_call(
        _mix_out_kernel,
        out_shape=[jax.ShapeDtypeStruct((T, D), F32), jax.ShapeDtypeStruct((T, D), BF16)],
        grid=(T // tm,),
        in_specs=[
            pl.BlockSpec((tm, NA_WIDTH), lambda i: (i, 0)),
            pl.BlockSpec((tm, HY_WIDTH), lambda i: (i, 0)),
            pl.BlockSpec((tm, GDN_WIDTH), lambda i: (i, 0)),
            pl.BlockSpec((tm, GATE_BLOCK), lambda i: (i, 1)),
            pl.BlockSpec((1, 1, N_BRANCH * D), lambda i: (layer, 0, 0)),
            resident((1, NA_WIDTH, D), lambda i: (layer, 0, 0)),
            resident((1, HY_WIDTH, D), lambda i: (layer, 0, 0)),
            resident((1, GDN_WIDTH, D), lambda i: (layer, 0, 0)),
            resident((1, D, D), lambda i: (layer, 0, 0)),
        ] + x_specs + [mod_spec(2), pl.BlockSpec((1, 1, D), lambda i: (layer, 0, 0)), mod_spec(3), mod_spec(4)],
        out_specs=[row_spec, row_spec],
        compiler_params=_cparams(("arbitrary",)),
        name="mix_out",
    )(y_na, y_hy, y_gdn, proj, b_gate.reshape(DEPTH, 1, N_BRANCH * D), w_pa, w_pb, w_pc, w_out, *x_args,
      mod, ln2_g.reshape(DEPTH, 1, D), mod, mod)


FFN_TM = 1024


def _ffn_up_kernel(h_ref, wa_ref, wb_ref, cwa_ref, cwb_ref, cba_ref, cbb_ref, o_ref, wab_ref, wbb_ref):
    m = pl.program_id(1)

    @pl.when(m == 0)
    def _():
        wab_ref[...] = wa_ref[0].astype(BF16)
        wbb_ref[...] = wb_ref[0].astype(BF16)

    seq_len = jnp.where(m < T_P // FFN_TM, SEQ, DEC_SEQ)
    pos = lax.broadcasted_iota(jnp.int32, (FFN_TM, 1), 0) & (seq_len - 1)
    has_prev = pos != 0
    has_next = pos != seq_len - 1
    h = h_ref[...]

    def conv(w_ref, cw_ref, cb_ref):
        up = jnp.dot(h, w_ref[...], preferred_element_type=F32)
        prev = jnp.where(has_prev, pltpu.roll(up, 1, 0), 0.0)
        nxt = jnp.where(has_next, pltpu.roll(up, FFN_TM - 1, 0), 0.0)
        cw = cw_ref[0]
        return prev * cw[0:1] + up * cw[1:2] + nxt * cw[2:3] + cb_ref[0]

    ua = conv(wab_ref, cwa_ref, cba_ref)
    ub = conv(wbb_ref, cwb_ref, cbb_ref)
    o_ref[...] = (ua * jax.nn.sigmoid(ua) * ub).astype(BF16)


def _ffn_up(h, w_up, conv_w, conv_b, layer):
    tn = 512
    nt = D_FF // tn
    conv_b = conv_b.reshape(DEPTH, 1, 2 * D_FF)
    return pl.pallas_call(
        _ffn_up_kernel,
        out_shape=jax.ShapeDtypeStruct((T, D_FF), BF16),
        grid=(nt, T // FFN_TM),
        in_specs=[
            pl.BlockSpec((FFN_TM, D), lambda j, i: (i, 0)),
            pl.BlockSpec((1, D, tn), lambda j, i: (layer, 0, j)),
            pl.BlockSpec((1, D, tn), lambda j, i: (layer, 0, nt + j)),
            pl.BlockSpec((1, 3, tn), lambda j, i: (layer, 0, j)),
            pl.BlockSpec((1, 3, tn), lambda j, i: (layer, 0, nt + j)),
            pl.BlockSpec((1, 1, tn), lambda j, i: (layer, 0, j)),
            pl.BlockSpec((1, 1, tn), lambda j, i: (layer, 0, nt + j)),
        ],
        out_specs=pl.BlockSpec((FFN_TM, tn), lambda j, i: (i, j)),
        scratch_shapes=[pltpu.VMEM((D, tn), BF16), pltpu.VMEM((D, tn), BF16)],
        compiler_params=_cparams(("arbitrary", "arbitrary")),
        name="ffn_up",
    )(h, w_up, w_up, conv_w, conv_w, conv_b, conv_b)


DOWN_TM = 256


def _ffn_down_kernel(a_ref, w_ref, x_ref, gt_ref, g_ref, sh_ref, sc_ref, o_ref, h_ref):
    x = x_ref[...] + gt_ref[0] * jnp.dot(a_ref[...], w_ref[0], preferred_element_type=F32)
    o_ref[...] = x
    h_ref[...] = _modulated_norm(x, g_ref[0], sh_ref[0], sc_ref[0])


def _ffn_down_final_kernel(a_ref, w_ref, x_ref, gt_ref, g_ref, yp_ref, ys_ref):
    x = x_ref[...] + gt_ref[0] * jnp.dot(a_ref[...], w_ref[0], preferred_element_type=F32)
    y = x * lax.rsqrt(jnp.mean(x * x, axis=-1, keepdims=True) + NORM_EPS) * g_ref[...]
    is_context = pl.program_id(0) < T_P // DOWN_TM

    @pl.when(is_context)
    def _():
        yp_ref[...] = y

    @pl.when(jnp.logical_not(is_context))
    def _():
        ys_ref[...] = y


def _ffn_down(act, w_down, x, mod, layer, ln1_g, final_g):
    tm = DOWN_TM
    last = layer == DEPTH - 1
    mod_spec = lambda lyr, chunk: pl.BlockSpec((1, 1, D), lambda i: (lyr * N_COND + _cond_row(i, tm), 0, chunk))
    row_spec = pl.BlockSpec((tm, D), lambda i: (i, 0))
    in_specs = [
        pl.BlockSpec((tm, D_FF), lambda i: (i, 0)),
        pl.BlockSpec((1, D_FF, D), lambda i: (layer, 0, 0), pipeline_mode=pl.Buffered(1)),
        row_spec,
        mod_spec(layer, 5),
    ]
    if last:
        n_ctx = T_P // tm
        return pl.pallas_call(
            _ffn_down_final_kernel,
            out_shape=[jax.ShapeDtypeStruct((T_P, D), F32), jax.ShapeDtypeStruct((T_S, D), F32)],
            grid=(T // tm,),
            in_specs=in_specs + [pl.BlockSpec((1, D), lambda i: (0, 0))],
            out_specs=[pl.BlockSpec((tm, D), lambda i: (jnp.minimum(i, n_ctx - 1), 0)),
                       pl.BlockSpec((tm, D), lambda i: (jnp.maximum(i - n_ctx, 0), 0))],
            compiler_params=_cparams(("arbitrary",)),
            name="ffn_down_final",
        )(act, w_down, x, mod, final_g.reshape(1, D))
    return pl.pallas_call(
        _ffn_down_kernel,
        out_shape=[jax.ShapeDtypeStruct((T, D), F32), jax.ShapeDtypeStruct((T, D), BF16)],
        grid=(T // tm,),
        in_specs=in_specs + [pl.BlockSpec((1, 1, D), lambda i: (layer + 1, 0, 0)),
                             mod_spec(layer + 1, 0), mod_spec(layer + 1, 1)],
        out_specs=[row_spec, row_spec],
        compiler_params=_cparams(("arbitrary",)),
        name="ffn_down",
    )(act, w_down, x, mod, ln1_g.reshape(DEPTH, 1, D), mod, mod)


ATT_SCALE = NA_DH ** -0.5


def _nt_dot(a, b):
    return lax.dot_general(a, b, (((1,), (1,)), ((), ())), preferred_element_type=F32)


def _head_slices():
    return [slice(h * NA_DH, (h + 1) * NA_DH) for h in range(NA_HEADS)]


def _ctx_attn_kernel(q_ref, k_ref, v_ref, o_ref, ko_ref, vo_ref):
    heads = _head_slices()
    q = [(q_ref[:, sl] * ATT_SCALE).astype(BF16) for sl in heads]
    s = [_nt_dot(q[h], k_ref[:, sl].astype(BF16)) for h, sl in enumerate(heads)]
    e = [x - jnp.max(x, axis=-1, keepdims=True) for x in s]
    e = [jnp.exp(x) for x in e]
    o = [jnp.dot(e[h].astype(BF16), v_ref[:, sl].astype(BF16), preferred_element_type=F32)
         for h, sl in enumerate(heads)]
    for h, sl in enumerate(heads):
        o_ref[:, sl] = (o[h] / jnp.sum(e[h], axis=-1, keepdims=True)).astype(BF16)
    ko_ref[0, 0] = k_ref[...]
    vo_ref[0, 0] = v_ref[...]


def _ctx_attention(proj, layer, caches):
    cache_shape = jax.ShapeDtypeStruct((BATCH, DEPTH, SEQ, NA_WIDTH), F32)
    cache_spec = pl.BlockSpec((1, 1, SEQ, NA_WIDTH), lambda b: (b, layer, 0, 0))
    qkv_specs = [pl.BlockSpec((SEQ, NA_WIDTH), lambda b, j=j: (b, j)) for j in range(3)]
    out_shape = [jax.ShapeDtypeStruct((T, NA_WIDTH), BF16), cache_shape, cache_shape]
    out_specs = [pl.BlockSpec((SEQ, NA_WIDTH), lambda b: (b, 0)), cache_spec, cache_spec]
    if caches is None:
        return pl.pallas_call(
            _ctx_attn_kernel, out_shape=out_shape, grid=(BATCH,), in_specs=qkv_specs, out_specs=out_specs,
            compiler_params=_cparams(("arbitrary",)), name="ctx_attention",
        )(proj, proj, proj)

    def body(q_ref, k_ref, v_ref, kc_ref, vc_ref, o_ref, ko_ref, vo_ref):
        del kc_ref, vc_ref
        _ctx_attn_kernel(q_ref, k_ref, v_ref, o_ref, ko_ref, vo_ref)

    any_spec = pl.BlockSpec(memory_space=pl.ANY)
    return pl.pallas_call(
        body, out_shape=out_shape, grid=(BATCH,), in_specs=qkv_specs + [any_spec, any_spec],
        out_specs=out_specs, input_output_aliases={3: 1, 4: 2},
        compiler_params=_cparams(("arbitrary",)), name="ctx_attention",
    )(proj, proj, proj, *caches)


NA_ROWS = DEC_SEQ // GRID_W
NA_WIN_TOK = NA_WIN_ROWS * GRID_W


def _na_window_start(r):
    return jnp.clip(r - NA_WIN_ROWS // 2, 0, NA_ROWS - NA_WIN_ROWS)


def _na_bias_table(rpb):
    col = np.arange(GRID_W)
    col_start = np.clip(col - NA_WIN_COLS // 2, 0, GRID_W - NA_WIN_COLS)
    col_mask = (col[None, :] >= col_start[:, None]) & (col[None, :] < col_start[:, None] + NA_WIN_COLS)
    rel_col = np.clip(col[None, :] - col[:, None] + NA_WIN_COLS - 1, 0, 2 * NA_WIN_COLS - 2)
    n_rel = 2 * NA_WIN_COLS - 1
    onehot = (rel_col[None] == np.arange(n_rel)[:, None, None]).astype(np.float32).reshape(n_rel, -1)
    expanded = jnp.dot(rpb.reshape(-1, n_rel), onehot, precision=HIGHEST)
    banded = jnp.where(col_mask, expanded.reshape(NA_HEADS, 2 * NA_WIN_ROWS - 1, GRID_W, GRID_W), NEG_INF)
    return jnp.concatenate([banded[:, :-1], banded[:, 1:]], axis=-1)


def _na_attn_kernel(q_ref, k_ref, v_ref, ck_ref, cv_ref, bias_ref, y_in_ref, o_ref):
    del y_in_ref
    r = pl.program_id(1)
    start = pl.multiple_of(_na_window_start(r) * GRID_W, GRID_W)
    win = pl.ds(start, NA_WIN_TOK)
    heads = _head_slices()
    nh = range(NA_HEADS)
    q = [(q_ref[:, sl] * ATT_SCALE).astype(BF16) for sl in heads]
    rel0 = NA_WIN_ROWS - 1 - (r - _na_window_start(r))
    bias = [jnp.concatenate([bias_ref[h, rel0 + 2 * p] for p in range(NA_WIN_ROWS // 2)], axis=1) for h in nh]
    s_win = [_nt_dot(q[h], k_ref[win, sl].astype(BF16)) + bias[h] for h, sl in enumerate(heads)]
    s_ctx = [_nt_dot(q[h], ck_ref[0, 0, :, sl].astype(BF16)) for h, sl in enumerate(heads)]
    m = [jnp.maximum(jnp.max(s_win[h], axis=-1, keepdims=True), jnp.max(s_ctx[h], axis=-1, keepdims=True))
         for h in nh]
    e_win = [jnp.exp(s_win[h] - m[h]) for h in nh]
    e_ctx = [jnp.exp(s_ctx[h] - m[h]) for h in nh]
    o_win = [jnp.dot(e_win[h].astype(BF16), v_ref[win, sl].astype(BF16), preferred_element_type=F32)
             for h, sl in enumerate(heads)]
    o_ctx = [jnp.dot(e_ctx[h].astype(BF16), cv_ref[0, 0, :, sl].astype(BF16), preferred_element_type=F32)
             for h, sl in enumerate(heads)]
    for h, sl in enumerate(heads):
        denom = jnp.sum(e_win[h], axis=-1, keepdims=True) + jnp.sum(e_ctx[h], axis=-1, keepdims=True)
        o_ref[:, sl] = ((o_win[h] + o_ctx[h]) / denom).astype(BF16)


def _na_attention(proj, cache_k, cache_v, rpb, y_na, layer):
    q_row0 = T_P // GRID_W
    seq0 = T_P // DEC_SEQ
    kv_spec = lambda j: pl.BlockSpec((DEC_SEQ, NA_WIDTH), lambda b, r: (seq0 + b, j))
    ctx_spec = pl.BlockSpec((1, 1, SEQ, NA_WIDTH), lambda b, r: (b, layer, 0, 0))
    return pl.pallas_call(
        _na_attn_kernel,
        out_shape=jax.ShapeDtypeStruct((T, NA_WIDTH), BF16),
        grid=(DEC_BATCH, NA_ROWS),
        in_specs=[
            pl.BlockSpec((GRID_W, NA_WIDTH), lambda b, r: (q_row0 + b * NA_ROWS + r, 0)),
            kv_spec(1), kv_spec(2), ctx_spec, ctx_spec,
            pl.BlockSpec((NA_HEADS, 2 * NA_WIN_ROWS - 2, GRID_W, 2 * GRID_W), lambda b, r: (0, 0, 0, 0),
                         pipeline_mode=pl.Buffered(1)),
            pl.BlockSpec(memory_space=pl.ANY),
        ],
        out_specs=pl.BlockSpec((GRID_W, NA_WIDTH), lambda b, r: (q_row0 + b * NA_ROWS + r, 0)),
        input_output_aliases={6: 0},
        compiler_params=_cparams(("arbitrary", "arbitrary")),
        name="na_attention",
    )(proj, proj, proj, cache_k.reshape(DEC_BATCH, DEPTH, SEQ, NA_WIDTH),
      cache_v.reshape(DEC_BATCH, DEPTH, SEQ, NA_WIDTH), _na_bias_table(rpb), y_na)


HY_CB = 256


def _split_bf16(x):
    hi = x.astype(BF16)
    return hi, (x - hi.astype(F32)).astype(BF16)


def _dot3(a_hi, a_lo, b):
    b_hi, b_lo = _split_bf16(b)
    return (jnp.dot(a_hi, b_hi, preferred_element_type=F32) + jnp.dot(a_lo, b_hi, preferred_element_type=F32)
            + jnp.dot(a_hi, b_lo, preferred_element_type=F32))


def _trig_table_kernel(*out_refs, length, freq_on_rows, scale):
    rows, cols = out_refs[0].shape
    r = lax.broadcasted_iota(jnp.int32, (rows, cols), 0) + pl.program_id(0) * rows
    c = lax.broadcasted_iota(jnp.int32, (rows, cols), 1)
    f, t = (r, c) if freq_on_rows else (c, r)
    k = f & (length - 1)
    quarter = jnp.where(f >= length, length, 0)
    phase = (2 * k + 1) * t + (-quarter if freq_on_rows else quarter)
    phase = phase & (4 * length - 1)
    phase = jnp.where(phase >= 2 * length, phase - 4 * length, phase)
    val = jnp.cos(phase.astype(F32) * (math.pi / (2 * length))) * scale
    for ref, part in zip(out_refs, _split_bf16(val)):
        ref[...] = part


def _trig_table(length, freq_on_rows):
    shape = (2 * length, length) if freq_on_rows else (length, 2 * length)
    tr = 256
    n_parts = 2 if freq_on_rows else 1
    spec = pl.BlockSpec((tr, shape[1]), lambda i: (i, 0))
    return pl.pallas_call(
        functools.partial(_trig_table_kernel, length=length, freq_on_rows=freq_on_rows,
                          scale=1.0 if freq_on_rows else 1.0 / length),
        out_shape=[jax.ShapeDtypeStruct(shape, BF16)] * n_parts,
        grid=(shape[0] // tr,), out_specs=[spec] * n_parts,
        compiler_params=_cparams(("arbitrary",)), name="dft_table",
    )()


def _hy_filter_kernel(tn_ref, band_ref, w1_ref, b1_ref, fr_ref, w2_ref, b2_ref, w3_ref, dec_ref,
                      fhi_ref, flo_ref, o_ref, *, length):
    t_norm = tn_ref[...]
    t_idx = lax.broadcasted_iota(jnp.int32, (length, LANE), 0).astype(F32)
    lane = lax.broadcasted_iota(jnp.int32, (length, LANE), 1)
    ang = (2.0 * math.pi / length) * t_idx * band_ref[...]
    z = jnp.where(lane == 0, t_norm,
                  jnp.where(lane <= HY_BANDS, jnp.cos(ang), jnp.where(lane <= 2 * HY_BANDS, jnp.sin(ang), 0.0)))
    hdn = jnp.sin(fr_ref[0:1] * (jnp.dot(z, w1_ref[...], precision=HIGHEST, preferred_element_type=F32)
                                 + b1_ref[...]))
    hdn = jnp.sin(fr_ref[1:2] * (jnp.dot(hdn, w2_ref[...], precision=HIGHEST, preferred_element_type=F32)
                                 + b2_ref[...]))
    filt = jnp.dot(hdn, w3_ref[...], precision=HIGHEST, preferred_element_type=F32)
    filt = filt * jnp.exp(-t_norm * jnp.abs(dec_ref[...]))
    first = lax.broadcasted_iota(jnp.int32, (length, HY_WIDTH), 0) == 0
    for o in range(HY_ORDER):
        fwd = filt[:, (2 * o) * HY_WIDTH:(2 * o + 1) * HY_WIDTH]
        bwd = filt[:, (2 * o + 1) * HY_WIDTH:(2 * o + 2) * HY_WIDTH]
        bwd = jnp.where(first, 0.0, pltpu.roll(bwd, 1, 0))
        o_ref[o, :length] = _dot3(fhi_ref[:length], flo_ref[:length], fwd + bwd)
        o_ref[o, length:] = _dot3(fhi_ref[length:], flo_ref[length:], bwd - fwd)


def _hy_filter_spectrum(length, f_hi, f_lo, w1, b1, freq, w2, b2, w3, decay):
    emb = 1 + 2 * HY_BANDS
    t_norm = jnp.linspace(0.0, 1.0, length, dtype=F32).reshape(length, 1)
    bands = np.zeros((1, LANE), np.float32)
    bands[0, 1:1 + HY_BANDS] = bands[0, 1 + HY_BANDS:emb] = np.linspace(1e-4, HY_BANDS - 1, HY_BANDS,
                                                                        dtype=np.float32)
    w1p = jnp.zeros((LANE, w1.shape[1]), F32).at[:emb].set(w1)
    return pl.pallas_call(
        functools.partial(_hy_filter_kernel, length=length),
        out_shape=jax.ShapeDtypeStruct((HY_ORDER, 2 * length, HY_WIDTH), F32),
        compiler_params=pltpu.CompilerParams(vmem_limit_bytes=VMEM_LIMIT),
        name="hyena_filter",
    )(t_norm, jnp.asarray(bands), w1p, b1.reshape(1, -1), freq, w2, b2.reshape(1, -1), w3,
      decay.reshape(1, -1), f_hi, f_lo)


def _hyena_kernel(v_ref, x1_ref, x2_ref, cwv_ref, cw1_ref, cw2_ref, cbv_ref, cb1_ref, cb2_ref, skip_ref,
                  hs_ref, f_ref, g_ref, *rest, length):
    o_ref = rest[-1]
    row = lax.broadcasted_iota(jnp.int32, v_ref.shape, 0)
    first, last = row == 0, row == length - 1

    def short_conv(u_ref, cw_ref, cb_ref):
        u = u_ref[...]
        prev = jnp.where(first, 0.0, pltpu.roll(u, 1, 0))
        nxt = jnp.where(last, 0.0, pltpu.roll(u, length - 1, 0))
        return prev * cw_ref[0:1] + u * cw_ref[1:2] + nxt * cw_ref[2:3] + cb_ref[...]

    z = short_conv(v_ref, cwv_ref, cbv_ref)
    gates = (short_conv(x1_ref, cw1_ref, cb1_ref), short_conv(x2_ref, cw2_ref, cb2_ref))
    for o in range(HY_ORDER):
        zf = jnp.dot(f_ref[...], z.astype(BF16), preferred_element_type=F32)
        zc, zs = zf[:length], zf[length:]
        h_re, h_im = hs_ref[o, :length], hs_ref[o, length:]
        p = jnp.concatenate([zc * h_re + zs * h_im, zc * h_im - zs * h_re], axis=0)
        conv = jnp.dot(g_ref[...], p.astype(BF16), preferred_element_type=F32)
        z = gates[o] * (conv + z * skip_ref[o:o + 1])
    o_ref[...] = z.astype(BF16)


def _hyena(proj, y_hy, layer, length, tables, spectrum, conv_w, conv_b, skip):
    nseq = (T_P if length == SEQ else T_S) // length
    row0 = 0 if length == SEQ else T_P // length
    cb = HY_WIDTH if length == SEQ else HY_CB
    ncb = HY_WIDTH // cb
    col0 = C_HY // cb
    u_spec = lambda part: pl.BlockSpec((length, cb), lambda s, j: (row0 + s, col0 + part * ncb + j))
    cw_spec = lambda part: pl.BlockSpec((1, 3, cb), lambda s, j: (layer, 0, part * ncb + j))
    cb_spec = lambda part: pl.BlockSpec((1, 1, cb), lambda s, j: (layer, 0, part * ncb + j))
    resident = functools.partial(pl.BlockSpec, pipeline_mode=pl.Buffered(1))
    tab_specs = [resident(t.shape, lambda s, j: (0, 0)) for t in tables]
    in_specs = ([u_spec(0), u_spec(1), u_spec(2), cw_spec(0), cw_spec(1), cw_spec(2),
                 cb_spec(0), cb_spec(1), cb_spec(2),
                 pl.BlockSpec((1, HY_ORDER, cb), lambda s, j: (layer, 0, j)),
                 pl.BlockSpec((HY_ORDER, 2 * length, cb), lambda s, j: (0, 0, j))] + tab_specs)
    args = [proj] * 3 + [conv_w] * 3 + [conv_b.reshape(DEPTH, 1, -1)] * 3 + [skip, spectrum] + list(tables)
    aliases = {}
    if y_hy is not None:
        in_specs.append(pl.BlockSpec(memory_space=pl.ANY))
        args.append(y_hy)
        aliases = {len(args) - 1: 0}

    def body(v_ref, x1_ref, x2_ref, cwv, cw1, cw2, cbv, cb1, cb2, skip_ref, hs_ref, *rest):
        _hyena_kernel(v_ref, x1_ref, x2_ref, cwv.at[0], cw1.at[0], cw2.at[0], cbv.at[0], cb1.at[0], cb2.at[0],
                      skip_ref.at[0], hs_ref, *rest, length=length)

    return pl.pallas_call(
        body,
        out_shape=jax.ShapeDtypeStruct((T, HY_WIDTH), BF16),
        grid=(nseq, ncb), in_specs=in_specs,
        out_specs=pl.BlockSpec((length, cb), lambda s, j: (row0 + s, j)),
        input_output_aliases=aliases,
        compiler_params=_cparams(("arbitrary", "arbitrary")),
        name="hyena",
    )(*args)


GC = 128
GDN_SCALE = GDN_DK ** -0.5
GDN_CTX_PER_STEP = 2


def _state_slot(b, d):
    return (2 * b + d) * GDN_HEADS


def _merge_masks(lower):
    ri = lax.broadcasted_iota(jnp.int32, (GC, GC), 0)
    ci = lax.broadcasted_iota(jnp.int32, (GC, GC), 1)
    hi, lo = (ri, ci) if lower else (ci, ri)
    return [((hi >> (b + 1)) == (lo >> (b + 1))) & ((hi >> b) > (lo >> b)) for b in range(int(math.log2(GC)))]


def _unit_tri_inverse(a, masks):
    ri = lax.broadcasted_iota(jnp.int32, (GC, GC), 0)
    ci = lax.broadcasted_iota(jnp.int32, (GC, GC), 1)
    eye = (ri == ci).astype(F32)
    n = range(len(a))
    t = [eye - jnp.where(masks[i][0], a[i], 0.0) for i in n]
    for level in range(1, len(masks[0])):
        tb = [t[i].astype(BF16) for i in n]
        ta = [jnp.dot(tb[i], jnp.where(masks[i][level], a[i], 0.0).astype(BF16), preferred_element_type=F32)
              for i in n]
        tat = [jnp.dot(ta[i].astype(BF16), tb[i], preferred_element_type=F32) for i in n]
        t = [t[i] - tat[i] for i in n]
    return t


def _gdn_kernel(q_ref, k_ref, v_ref, z_ref, ba_ref, cwq_ref, cwk_ref, cwv_ref, alog_ref, dt_ref, ng_ref,
                sf0_ref, sb0_ref, *rest, length, nb):
    y_ref, sf_ref, sb_ref, qn_ref, kn_ref, vn_ref, beta_ref, g_ref, of_ref, ob_ref, s_ref = rest[-11:]
    n_chunks = length // GC
    rows_total = nb * length
    row = lax.broadcasted_iota(jnp.int32, (rows_total, GDN_WIDTH), 0) & (length - 1)
    first, last = row == 0, row == length - 1

    def conv_silu(u_ref, cw_ref):
        u = u_ref[...]
        prev = jnp.where(first, 0.0, pltpu.roll(u, 1, 0))
        nxt = jnp.where(last, 0.0, pltpu.roll(u, rows_total - 1, 0))
        c = prev * cw_ref[0:1] + u * cw_ref[1:2] + nxt * cw_ref[2:3]
        return c * jax.nn.sigmoid(c)

    q = conv_silu(q_ref, cwq_ref)
    k = conv_silu(k_ref, cwk_ref)
    vn_ref[...] = conv_silu(v_ref, cwv_ref)
    for h in range(GDN_HEADS):
        sl = slice(h * GDN_DK, (h + 1) * GDN_DK)
        qh, kh = q[:, sl], k[:, sl]
        qn_ref[:, sl] = qh * lax.rsqrt(jnp.sum(qh * qh, axis=-1, keepdims=True) + NORM_EPS) * GDN_SCALE
        kn_ref[:, sl] = kh * lax.rsqrt(jnp.sum(kh * kh, axis=-1, keepdims=True) + NORM_EPS)
    ba = ba_ref[...]
    beta_ref[...] = jax.nn.sigmoid(ba)
    g_ref[...] = -jnp.exp(alog_ref[...]) * jax.nn.softplus(ba + dt_ref[...])

    ri = lax.broadcasted_iota(jnp.int32, (GC, GC), 0)
    ci = lax.broadcasted_iota(jnp.int32, (GC, GC), 1)
    lower = (ri >= ci).astype(F32)
    upper = (ri <= ci).astype(F32)

    def chunk_step(n, carry):
        scans = [(b, d) for b in range(nb) for d in range(2)]
        rows = {(b, d): pl.ds(pl.multiple_of(b * length + (n if d == 0 else n_chunks - 1 - n) * GC, GC), GC)
                for b, d in scans}
        loaded = {}
        for bd in scans:
            loaded[bd] = (g_ref[rows[bd], :], beta_ref[rows[bd], :])
            for h in range(GDN_HEADS):
                sl = slice(h * GDN_DK, (h + 1) * GDN_DK)
                loaded[bd + (h,)] = (qn_ref[rows[bd], sl], kn_ref[rows[bd], sl], vn_ref[rows[bd], sl],
                                     s_ref[_state_slot(*bd) + h])
        units = [(b, d, h) for b, d in scans for h in range(GDN_HEADS)]
        incl = [(ri >= ci), (ri <= ci)]
        strict = [(ri > ci), (ri < ci)]
        masks = [_merge_masks(True), _merge_masks(False)]
        gcs = {(b, d): jnp.dot((lower, upper)[d], loaded[b, d][0], precision=HIGHEST, preferred_element_type=F32)
               for b, d in scans}
        gcs_t = {bd: g.T for bd, g in gcs.items()}
        edge = [GC - 1, 0]
        g_col = [gcs[b, d][:, 8 + 4 * d + h:9 + 4 * d + h] for b, d, h in units]
        g_row = [gcs_t[b, d][8 + 4 * d + h:9 + 4 * d + h, :] for b, d, h in units]
        g_end = [gcs_t[b, d][8 + 4 * d + h:9 + 4 * d + h, edge[d]:edge[d] + 1] for b, d, h in units]
        b_col = [loaded[b, d][1][:, 4 * d + h:4 * d + h + 1] for b, d, h in units]
        qc, kc, vc, st = (list(x) for x in zip(*(loaded[u] for u in units)))
        nu = range(len(units))
        decay = [jnp.where(incl[d], jnp.exp(jnp.where(incl[d], g_col[i] - g_row[i], 0.0)), 0.0)
                 for i, (b, d, h) in enumerate(units)]
        kb = [kc[i] * b_col[i] for i in nu]
        kcb = [kc[i].astype(BF16) for i in nu]
        kk = [_nt_dot(kb[i].astype(BF16), kcb[i]) for i in nu]
        qk = [_nt_dot(qc[i].astype(BF16), kcb[i]) for i in nu]
        a = [jnp.where(strict[d], kk[i] * decay[i], 0.0) for i, (b, d, h) in enumerate(units)]
        t = _unit_tri_inverse(a, [masks[d] for b, d, h in units])
        e_col = [jnp.exp(g_col[i]) for i in nu]
        rhs = [jnp.concatenate([vc[i] * b_col[i], kb[i] * e_col[i]], axis=1).astype(BF16) for i in nu]
        sol = [jnp.dot(t[i].astype(BF16), rhs[i], preferred_element_type=F32) for i in nu]
        attn = [jnp.where(incl[d], qk[i] * decay[i], 0.0).astype(BF16) for i, (b, d, h) in enumerate(units)]
        sb = [st[i].astype(BF16) for i in nu]
        ws = [jnp.dot(sol[i][:, GDN_DV:].astype(BF16), sb[i], preferred_element_type=F32) for i in nu]
        qs = [jnp.dot((qc[i] * e_col[i]).astype(BF16), sb[i], preferred_element_type=F32) for i in nu]
        v_new = [(sol[i][:, :GDN_DV] - ws[i]).astype(BF16) for i in nu]
        av = [jnp.dot(attn[i], v_new[i], preferred_element_type=F32) for i in nu]
        k_dec_t = [(kc[i] * jnp.exp(g_end[i] - g_col[i])).T.astype(BF16) for i in nu]
        kv = [jnp.dot(k_dec_t[i], v_new[i], preferred_element_type=F32) for i in nu]
        results = [(qs[i] + av[i], st[i] * jnp.exp(g_end[i]) + kv[i]) for i in nu]
        for (b, d, h), (o, s) in zip(units, results):
            (of_ref if d == 0 else ob_ref)[rows[b, d], h * GDN_DV:(h + 1) * GDN_DV] = o
            s_ref[_state_slot(b, d) + h] = s
        return carry

    for b in range(nb):
        s_ref[pl.ds(_state_slot(b, 0), GDN_HEADS)] = sf0_ref[b]
        s_ref[pl.ds(_state_slot(b, 1), GDN_HEADS)] = sb0_ref[b]
    lax.fori_loop(0, n_chunks, chunk_step, 0)
    for b in range(nb):
        sf_ref[b, 0] = s_ref[pl.ds(_state_slot(b, 0), GDN_HEADS)]
        sb_ref[b, 0] = s_ref[pl.ds(_state_slot(b, 1), GDN_HEADS)]
    for h in range(GDN_HEADS):
        sl = slice(h * GDN_DV, (h + 1) * GDN_DV)
        o = of_ref[:, sl] + ob_ref[:, sl]
        o = o * lax.rsqrt(jnp.mean(o * o, axis=-1, keepdims=True) + NORM_EPS) * ng_ref[...]
        zh = z_ref[:, sl]
        y_ref[:, sl] = (o * (zh * jax.nn.sigmoid(zh))).astype(BF16)


def _gdn(proj, y_gdn, states_out, layer, length, s_f0, s_b0, conv_w, a_log, dt_bias, norm_g):
    nb = GDN_CTX_PER_STEP if length == SEQ else 1
    rows = nb * length
    nblk = (T_P if length == SEQ else T_S) // rows
    row0 = 0 if length == SEQ else T_P // rows
    col = lambda c: c // GDN_WIDTH
    blk = lambda c: pl.BlockSpec((rows, GDN_WIDTH), lambda s: (row0 + s, col(c)))
    cw = lambda part: pl.BlockSpec((1, 3, GDN_WIDTH), lambda s: (layer, 0, part))
    vec = lambda: pl.BlockSpec((1, 1, LANE), lambda s: (layer, 0, 0))
    st_in = pl.BlockSpec((nb, GDN_HEADS, GDN_DK, GDN_DV), lambda s: (s, 0, 0, 0))
    st_out = pl.BlockSpec((nb, 1, GDN_HEADS, GDN_DK, GDN_DV), lambda s: (s, layer, 0, 0, 0))
    pad_lanes = lambda x, off: jnp.zeros((DEPTH, 1, LANE), F32).at[:, 0, off:off + 2 * GDN_HEADS].set(
        x.reshape(DEPTH, -1))
    in_specs = [blk(C_GDN), blk(C_GDN + GDN_WIDTH), blk(C_GDN + 2 * GDN_WIDTH), blk(C_Z),
                pl.BlockSpec((rows, LANE), lambda s: (row0 + s, C_BA // LANE)),
                cw(0), cw(1), cw(2), vec(), vec(), vec(), st_in, st_in]
    args = [proj] * 5 + [conv_w] * 3 + [pad_lanes(a_log, 2 * GDN_HEADS), pad_lanes(dt_bias, 2 * GDN_HEADS),
                                        norm_g.reshape(DEPTH, 1, GDN_DV), s_f0, s_b0]
    st_shape = jax.ShapeDtypeStruct((nblk * nb, DEPTH, GDN_HEADS, GDN_DK, GDN_DV), F32)
    aliases = {}
    for carried, out_idx in ((y_gdn, 0),) + (((states_out[0], 1), (states_out[1], 2)) if states_out else ()):
        if carried is not None:
            in_specs.append(pl.BlockSpec(memory_space=pl.ANY))
            args.append(carried)
            aliases[len(args) - 1] = out_idx

    def body(q_ref, k_ref, v_ref, z_ref, ba_ref, cwq, cwk, cwv, alog, dt, ng, sf0, sb0, *rest):
        _gdn_kernel(q_ref, k_ref, v_ref, z_ref, ba_ref, cwq.at[0], cwk.at[0], cwv.at[0], alog.at[0], dt.at[0],
                    ng.at[0], sf0, sb0, *rest, length=length, nb=nb)

    seq_buf = lambda w: pltpu.VMEM((rows, w), F32)
    return pl.pallas_call(
        body,
        out_shape=[jax.ShapeDtypeStruct((T, GDN_WIDTH), BF16), st_shape, st_shape],
        grid=(nblk,), in_specs=in_specs,
        out_specs=[pl.BlockSpec((rows, GDN_WIDTH), lambda s: (row0 + s, 0)), st_out, st_out],
        scratch_shapes=[seq_buf(GDN_WIDTH), seq_buf(GDN_WIDTH), seq_buf(GDN_WIDTH), seq_buf(LANE), seq_buf(LANE),
                        seq_buf(GDN_WIDTH), seq_buf(GDN_WIDTH),
                        pltpu.VMEM((nb * 2 * GDN_HEADS, GDN_DK, GDN_DV), F32)],
        input_output_aliases=aliases,
        compiler_params=_cparams(("arbitrary",)),
        name="gated_deltanet",
    )(*args)


def kernel(x_prompt, x_sample, cache_k, cache_v, state_fwd, state_bwd, c, c_ctx, ln1_g, ln2_g, w_mod, b_mod,
           w_in, na_rpb, hy_conv_w, hy_conv_b, hy_w1, hy_b1, hy_freq, hy_w2, hy_b2, hy_w3, hy_decay, hy_skip,
           gdn_conv_w, gdn_a_log, gdn_dt_bias, gdn_norm_g, w_pa, w_pb, w_pc, b_gate, w_out, ffn_w_up,
           ffn_conv_w, ffn_conv_b, ffn_w_down, final_g):
    x = (x_prompt.reshape(T_P, D), x_sample.reshape(T_S, D))
    cond = jnp.concatenate([c_ctx[None], c, jnp.zeros((N_COND - 1 - DEC_BATCH, D), F32)], axis=0)
    mod = _mod_table(cond, w_mod, b_mod).reshape(DEPTH * N_COND, 1, 6 * D)

    w_pa_b, w_pb_b, w_pc_b, w_out_b = (w.astype(BF16) for w in (w_pa, w_pb, w_pc, w_out))
    w_down_b = ffn_w_down.astype(BF16)
    w_in_t = jnp.swapaxes(w_in, 1, 2)
    tables = {n: tuple(_trig_table(n, True)) + tuple(_trig_table(n, False)) for n in (SEQ, DEC_SEQ)}
    zero_state = jnp.zeros((BATCH, GDN_HEADS, GDN_DK, GDN_DV), F32)

    caches = None
    states = None
    h = _norm_mod(x, ln1_g, mod, 0, 0)
    for layer in range(DEPTH):
        proj = _in_proj(h, w_in_t, layer)

        y_na, new_k, new_v = _ctx_attention(proj, layer, caches)
        caches = (new_k, new_v)
        y_na = _na_attention(proj, cache_k, cache_v, na_rpb[layer], y_na, layer)

        y_hy = None
        for n in (SEQ, DEC_SEQ):
            spectrum = _hy_filter_spectrum(n, tables[n][0], tables[n][1], hy_w1[layer], hy_b1[layer],
                                           hy_freq[layer], hy_w2[layer], hy_b2[layer], hy_w3[layer],
                                           hy_decay[layer])
            y_hy = _hyena(proj, y_hy, layer, n, (tables[n][0], tables[n][2]), spectrum,
                          hy_conv_w, hy_conv_b, hy_skip)

        y_gdn, s_f, s_b = _gdn(proj, None, states, layer, SEQ, zero_state, zero_state,
                               gdn_conv_w, gdn_a_log, gdn_dt_bias, gdn_norm_g)
        states = (s_f, s_b)
        y_gdn = _gdn(proj, y_gdn, None, layer, DEC_SEQ, state_fwd[:, layer], state_bwd[:, layer],
                     gdn_conv_w, gdn_a_log, gdn_dt_bias, gdn_norm_g)[0]

        x, h = _mix_out(y_na, y_hy, y_gdn, proj, b_gate, w_pa_b, w_pb_b, w_pc_b, w_out_b, x, mod, ln2_g, layer)
        act = _ffn_up(h, ffn_w_up, ffn_conv_w, ffn_conv_b, layer)
        x, h = _ffn_down(act, w_down_b, x, mod, layer, ln1_g, final_g)

    y_p, y_s = x, h
    cache_shape = (BATCH, DEPTH, SEQ, NA_HEADS, NA_DH)
    return (y_p.reshape(BATCH, SEQ, D), y_s.reshape(DEC_BATCH, DEC_SEQ, D),
            caches[0].reshape(cache_shape), caches[1].reshape(cache_shape), states[0], states[1])
```

```python
import functools
import math

import jax
import jax.numpy as jnp
import numpy as np
from jax import lax
from jax.experimental import pallas as pl
from jax.experimental.pallas import tpu as pltpu

F32 = jnp.float32
BF16 = jnp.bfloat16

D = 2048
BATCH, SEQ = 32, 256
DEC_BATCH, DEC_SEQ = 4, 1024
DEPTH = 2
GRID_W = 64
NA_HEADS, NA_DH = 8, 128
NA_WIDTH = NA_HEADS * NA_DH
NA_WIN_ROWS, NA_WIN_COLS = 8, 16
HY_WIDTH = 512
HY_ORDER = 2
HY_BANDS = 16
GDN_HEADS, GDN_DK, GDN_DV = 4, 128, 128
GDN_WIDTH = GDN_HEADS * GDN_DV
D_FF = 5632
N_BRANCH = 3
NORM_EPS = 1e-6
NEG_INF = -1e30

T_P = BATCH * SEQ
T_S = DEC_BATCH * DEC_SEQ
T = T_P + T_S
N_COND = 8

C_Q, C_K, C_V = 0, NA_WIDTH, 2 * NA_WIDTH
C_HY = 3 * NA_WIDTH
C_GDN = C_HY + 3 * HY_WIDTH
C_Z = C_GDN + 3 * GDN_HEADS * GDN_DK
C_BA = C_Z + GDN_WIDTH
C_GATE = C_BA + 4 * GDN_HEADS
N_IN = C_GATE + N_BRANCH * D
GATE_BLOCK = C_BA
GATE_SHIFT = C_GATE - C_BA

LANE = 128
VMEM_LIMIT = 56 * 1024 * 1024
HIGHEST = lax.Precision.HIGHEST


def _cparams(sem):
    return pltpu.CompilerParams(dimension_semantics=sem, vmem_limit_bytes=VMEM_LIMIT)


def _cond_row(tile, rows_per_tile):
    first_latent = T_P // rows_per_tile
    per_seq = DEC_SEQ // rows_per_tile
    return jnp.where(tile < first_latent, 0, 1 + (tile - first_latent) // per_seq)


def _mod_kernel(c_ref, w_ref, b_ref, o_ref):
    c = c_ref[...]
    s = (c * jax.nn.sigmoid(c)).astype(BF16)
    o_ref[0] = jnp.dot(s, w_ref[0].astype(BF16), preferred_element_type=F32) + b_ref[0]


def _mod_table(cond, w_mod, b_mod):
    tn = 1024
    n = 6 * D
    return pl.pallas_call(
        _mod_kernel,
        out_shape=jax.ShapeDtypeStruct((DEPTH, N_COND, n), F32),
        grid=(DEPTH, n // tn),
        in_specs=[
            pl.BlockSpec((N_COND, D), lambda l, j: (0, 0)),
            pl.BlockSpec((1, D, tn), lambda l, j: (l, 0, j)),
            pl.BlockSpec((1, 1, tn), lambda l, j: (l, 0, j)),
        ],
        out_specs=pl.BlockSpec((1, N_COND, tn), lambda l, j: (l, 0, j)),
        compiler_params=_cparams(("arbitrary", "arbitrary")),
        name="mod_table",
    )(cond, w_mod, b_mod.reshape(DEPTH, 1, n))


def _token_specs(x, tm):
    if not isinstance(x, tuple):
        return [pl.BlockSpec((tm, D), lambda i: (i, 0))], [x]
    n_ctx = T_P // tm
    return ([pl.BlockSpec((tm, D), lambda i: (jnp.minimum(i, n_ctx - 1), 0)),
             pl.BlockSpec((tm, D), lambda i: (jnp.maximum(i - n_ctx, 0), 0))], list(x))


def _token_tile(x_refs, tm):
    if len(x_refs) == 1:
        return x_refs[0][...]
    return jnp.where(pl.program_id(0) < T_P // tm, x_refs[0][...], x_refs[1][...])


def _modulated_norm(x, g, shift, scale):
    y = x * lax.rsqrt(jnp.mean(x * x, axis=-1, keepdims=True) + NORM_EPS) * g
    return (y * (1.0 + scale) + shift).astype(BF16)


NORM_TM = 1024


def _norm_mod_kernel(*refs):
    g_ref, sh_ref, sc_ref, o_ref = refs[-4:]
    o_ref[...] = _modulated_norm(_token_tile(refs[:-4], NORM_TM), g_ref[0], sh_ref[0], sc_ref[0])


def _norm_mod(x, g, mod, layer, shift_chunk):
    tm = NORM_TM
    x_specs, x_args = _token_specs(x, tm)
    return pl.pallas_call(
        _norm_mod_kernel,
        out_shape=jax.ShapeDtypeStruct((T, D), BF16),
        grid=(T // tm,),
        in_specs=x_specs + [
            pl.BlockSpec((1, 1, D), lambda i: (layer, 0, 0)),
            pl.BlockSpec((1, 1, D), lambda i: (layer * N_COND + _cond_row(i, tm), 0, shift_chunk)),
            pl.BlockSpec((1, 1, D), lambda i: (layer * N_COND + _cond_row(i, tm), 0, shift_chunk + 1)),
        ],
        out_specs=pl.BlockSpec((tm, D), lambda i: (i, 0)),
        compiler_params=_cparams(("arbitrary",)),
        name="norm_mod",
    )(*x_args, g.reshape(DEPTH, 1, D), mod, mod)


def _proj_kernel(h_ref, wt_ref, o_ref, wb_ref):
    @pl.when(pl.program_id(1) == 0)
    def _():
        wb_ref[...] = wt_ref[0].astype(BF16)

    o_ref[...] = lax.dot_general(h_ref[...], wb_ref[...], (((1,), (1,)), ((), ())), preferred_element_type=F32)


def _in_proj(h, w_in_t, layer):
    tm, tn = 1536, 1024
    return pl.pallas_call(
        _proj_kernel,
        out_shape=jax.ShapeDtypeStruct((T, N_IN), F32),
        grid=(pl.cdiv(N_IN, tn), T // tm),
        in_specs=[
            pl.BlockSpec((tm, D), lambda j, i: (i, 0)),
            pl.BlockSpec((1, tn, D), lambda j, i: (layer, j, 0)),
        ],
        out_specs=pl.BlockSpec((tm, tn), lambda j, i: (i, j)),
        scratch_shapes=[pltpu.VMEM((tn, D), BF16)],
        compiler_params=_cparams(("arbitrary", "arbitrary")),
        name="in_proj",
    )(h, w_in_t)


MIX_TM = 256


def _mix_out_kernel(yna_ref, yhy_ref, ygdn_ref, gl_ref, bg_ref, wpa_ref, wpb_ref, wpc_ref, wout_ref, *refs):
    gt_ref, g2_ref, sh2_ref, sc2_ref, o_ref, h2_ref = refs[-6:]
    width = gl_ref.shape[1]
    gl = pltpu.roll(gl_ref[...], width - GATE_SHIFT, 1)[:, :N_BRANCH * D] + bg_ref[0]
    gates = jax.nn.sigmoid(gl)
    merged = (gates[:, :D] * jnp.dot(yna_ref[...], wpa_ref[0], preferred_element_type=F32)
              + gates[:, D:2 * D] * jnp.dot(yhy_ref[...], wpb_ref[0], preferred_element_type=F32)
              + gates[:, 2 * D:] * jnp.dot(ygdn_ref[...], wpc_ref[0], preferred_element_type=F32))
    r = jnp.dot(merged.astype(BF16), wout_ref[0], preferred_element_type=F32)
    x = _token_tile(refs[:-6], MIX_TM) + gt_ref[0] * r
    o_ref[...] = x
    h2_ref[...] = _modulated_norm(x, g2_ref[0], sh2_ref[0], sc2_ref[0])


def _mix_out(y_na, y_hy, y_gdn, proj, b_gate, w_pa, w_pb, w_pc, w_out, x, mod, ln2_g, layer):
    tm = MIX_TM
    resident = functools.partial(pl.BlockSpec, pipeline_mode=pl.Buffered(1))
    x_specs, x_args = _token_specs(x, tm)
    mod_spec = lambda chunk: pl.BlockSpec((1, 1, D), lambda i: (layer * N_COND + _cond_row(i, tm), 0, chunk))
    row_spec = pl.BlockSpec((tm, D), lambda i: (i, 0))
    return pl.pallas_call(
        _mix_out_kernel,
        out_shape=[jax.ShapeDtypeStruct((T, D), F32), jax.ShapeDtypeStruct((T, D), BF16)],
        grid=(T // tm,),
        in_specs=[
            pl.BlockSpec((tm, NA_WIDTH), lambda i: (i, 0)),
            pl.BlockSpec((tm, HY_WIDTH), lambda i: (i, 0)),
            pl.BlockSpec((tm, GDN_WIDTH), lambda i: (i, 0)),
            pl.BlockSpec((tm, GATE_BLOCK), lambda i: (i, 1)),
            pl.BlockSpec((1, 1, N_BRANCH * D), lambda i: (layer, 0, 0)),
            resident((1, NA_WIDTH, D), lambda i: (layer, 0, 0)),
            resident((1, HY_WIDTH, D), lambda i: (layer, 0, 0)),
            resident((1, GDN_WIDTH, D), lambda i: (layer, 0, 0)),
            resident((1, D, D), lambda i: (layer, 0, 0)),
        ] + x_specs + [mod_spec(2), pl.BlockSpec((1, 1, D), lambda i: (layer, 0, 0)), mod_spec(3), mod_spec(4)],
        out_specs=[row_spec, row_spec],
        compiler_params=_cparams(("arbitrary",)),
        name="mix_out",
    )(y_na, y_hy, y_gdn, proj, b_gate.reshape(DEPTH, 1, N_BRANCH * D), w_pa, w_pb, w_pc, w_out, *x_args,
      mod, ln2_g.reshape(DEPTH, 1, D), mod, mod)


FFN_TM = 1024


def _ffn_up_kernel(h_ref, wa_ref, wb_ref, cwa_ref, cwb_ref, cba_ref, cbb_ref, o_ref, wab_ref, wbb_ref):
    m = pl.program_id(1)

    @pl.when(m == 0)
    def _():
        wab_ref[...] = wa_ref[0].astype(BF16)
        wbb_ref[...] = wb_ref[0].astype(BF16)

    seq_len = jnp.where(m < T_P // FFN_TM, SEQ, DEC_SEQ)
    pos = lax.broadcasted_iota(jnp.int32, (FFN_TM, 1), 0) & (seq_len - 1)
    has_prev = pos != 0
    has_next = pos != seq_len - 1
    h = h_ref[...]

    def conv(w_ref, cw_ref, cb_ref):
        up = jnp.dot(h, w_ref[...], preferred_element_type=F32)
        prev = jnp.where(has_prev, pltpu.roll(up, 1, 0), 0.0)
        nxt = jnp.where(has_next, pltpu.roll(up, FFN_TM - 1, 0), 0.0)
        cw = cw_ref[0]
        return prev * cw[0:1] + up * cw[1:2] + nxt * cw[2:3] + cb_ref[0]

    ua = conv(wab_ref, cwa_ref, cba_ref)
    ub = conv(wbb_ref, cwb_ref, cbb_ref)
    o_ref[...] = (ua * jax.nn.sigmoid(ua) * ub).astype(BF16)


def _ffn_up(h, w_up, conv_w, conv_b, layer):
    tn = 512
    nt = D_FF // tn
    conv_b = conv_b.reshape(DEPTH, 1, 2 * D_FF)
    return pl.pallas_call(
        _ffn_up_kernel,
        out_shape=jax.ShapeDtypeStruct((T, D_FF), BF16),
        grid=(nt, T // FFN_TM),
        in_specs=[
            pl.BlockSpec((FFN_TM, D), lambda j, i: (i, 0)),
            pl.BlockSpec((1, D, tn), lambda j, i: (layer, 0, j)),
            pl.BlockSpec((1, D, tn), lambda j, i: (layer, 0, nt + j)),
            pl.BlockSpec((1, 3, tn), lambda j, i: (layer, 0, j)),
            pl.BlockSpec((1, 3, tn), lambda j, i: (layer, 0, nt + j)),
            pl.BlockSpec((1, 1, tn), lambda j, i: (layer, 0, j)),
            pl.BlockSpec((1, 1, tn), lambda j, i: (layer, 0, nt + j)),
        ],
        out_specs=pl.BlockSpec((FFN_TM, tn), lambda j, i: (i, j)),
        scratch_shapes=[pltpu.VMEM((D, tn), BF16), pltpu.VMEM((D, tn), BF16)],
        compiler_params=_cparams(("arbitrary", "arbitrary")),
        name="ffn_up",
    )(h, w_up, w_up, conv_w, conv_w, conv_b, conv_b)


DOWN_TM = 256


def _ffn_down_kernel(a_ref, w_ref, x_ref, gt_ref, g_ref, sh_ref, sc_ref, o_ref, h_ref):
    x = x_ref[...] + gt_ref[0] * jnp.dot(a_ref[...], w_ref[0], preferred_element_type=F32)
    o_ref[...] = x
    h_ref[...] = _modulated_norm(x, g_ref[0], sh_ref[0], sc_ref[0])


def _ffn_down_final_kernel(a_ref, w_ref, x_ref, gt_ref, g_ref, yp_ref, ys_ref):
    x = x_ref[...] + gt_ref[0] * jnp.dot(a_ref[...], w_ref[0], preferred_element_type=F32)
    y = x * lax.rsqrt(jnp.mean(x * x, axis=-1, keepdims=True) + NORM_EPS) * g_ref[...]
    is_context = pl.program_id(0) < T_P // DOWN_TM

    @pl.when(is_context)
    def _():
        yp_ref[...] = y

    @pl.when(jnp.logical_not(is_context))
    def _():
        ys_ref[...] = y


def _ffn_down(act, w_down, x, mod, layer, ln1_g, final_g):
    tm = DOWN_TM
    last = layer == DEPTH - 1
    mod_spec = lambda lyr, chunk: pl.BlockSpec((1, 1, D), lambda i: (lyr * N_COND + _cond_row(i, tm), 0, chunk))
    row_spec = pl.BlockSpec((tm, D), lambda i: (i, 0))
    in_specs = [
        pl.BlockSpec((tm, D_FF), lambda i: (i, 0)),
        pl.BlockSpec((1, D_FF, D), lambda i: (layer, 0, 0), pipeline_mode=pl.Buffered(1)),
        row_spec,
        mod_spec(layer, 5),
    ]
    if last:
        n_ctx = T_P // tm
        return pl.pallas_call(
            _ffn_down_final_kernel,
            out_shape=[jax.ShapeDtypeStruct((T_P, D), F32), jax.ShapeDtypeStruct((T_S, D), F32)],
            grid=(T // tm,),
            in_specs=in_specs + [pl.BlockSpec((1, D), lambda i: (0, 0))],
            out_specs=[pl.BlockSpec((tm, D), lambda i: (jnp.minimum(i, n_ctx - 1), 0)),
                       pl.BlockSpec((tm, D), lambda i: (jnp.maximum(i - n_ctx, 0), 0))],
            compiler_params=_cparams(("arbitrary",)),
            name="ffn_down_final",
        )(act, w_down, x, mod, final_g.reshape(1, D))
    return pl.pallas_call(
        _ffn_down_kernel,
        out_shape=[jax.ShapeDtypeStruct((T, D), F32), jax.ShapeDtypeStruct((T, D), BF16)],
        grid=(T // tm,),
        in_specs=in_specs + [pl.BlockSpec((1, 1, D), lambda i: (layer + 1, 0, 0)),
                             mod_spec(layer + 1, 0), mod_spec(layer + 1, 1)],
        out_specs=[row_spec, row_spec],
        compiler_params=_cparams(("arbitrary",)),
        name="ffn_down",
    )(act, w_down, x, mod, ln1_g.reshape(DEPTH, 1, D), mod, mod)


ATT_SCALE = NA_DH ** -0.5


def _nt_dot(a, b):
    return lax.dot_general(a, b, (((1,), (1,)), ((), ())), preferred_element_type=F32)


def _head_slices():
    return [slice(h * NA_DH, (h + 1) * NA_DH) for h in range(NA_HEADS)]


def _ctx_attn_kernel(q_ref, k_ref, v_ref, o_ref, ko_ref, vo_ref):
    heads = _head_slices()
    q = [(q_ref[:, sl] * ATT_SCALE).astype(BF16) for sl in heads]
    s = [_nt_dot(q[h], k_ref[:, sl].astype(BF16)) for h, sl in enumerate(heads)]
    e = [x - jnp.max(x, axis=-1, keepdims=True) for x in s]
    e = [jnp.exp(x) for x in e]
    o = [jnp.dot(e[h].astype(BF16), v_ref[:, sl].astype(BF16), preferred_element_type=F32)
         for h, sl in enumerate(heads)]
    for h, sl in enumerate(heads):
        o_ref[:, sl] = (o[h] / jnp.sum(e[h], axis=-1, keepdims=True)).astype(BF16)
    ko_ref[0, 0] = k_ref[...]
    vo_ref[0, 0] = v_ref[...]


def _ctx_attention(proj, layer, caches):
    cache_shape = jax.ShapeDtypeStruct((BATCH, DEPTH, SEQ, NA_WIDTH), F32)
    cache_spec = pl.BlockSpec((1, 1, SEQ, NA_WIDTH), lambda b: (b, layer, 0, 0))
    qkv_specs = [pl.BlockSpec((SEQ, NA_WIDTH), lambda b, j=j: (b, j)) for j in range(3)]
    out_shape = [jax.ShapeDtypeStruct((T, NA_WIDTH), BF16), cache_shape, cache_shape]
    out_specs = [pl.BlockSpec((SEQ, NA_WIDTH), lambda b: (b, 0)), cache_spec, cache_spec]
    if caches is None:
        return pl.pallas_call(
            _ctx_attn_kernel, out_shape=out_shape, grid=(BATCH,), in_specs=qkv_specs, out_specs=out_specs,
            compiler_params=_cparams(("arbitrary",)), name="ctx_attention",
        )(proj, proj, proj)

    def body(q_ref, k_ref, v_ref, kc_ref, vc_ref, o_ref, ko_ref, vo_ref):
        del kc_ref, vc_ref
        _ctx_attn_kernel(q_ref, k_ref, v_ref, o_ref, ko_ref, vo_ref)

    any_spec = pl.BlockSpec(memory_space=pl.ANY)
    return pl.pallas_call(
        body, out_shape=out_shape, grid=(BATCH,), in_specs=qkv_specs + [any_spec, any_spec],
        out_specs=out_specs, input_output_aliases={3: 1, 4: 2},
        compiler_params=_cparams(("arbitrary",)), name="ctx_attention",
    )(proj, proj, proj, *caches)


NA_ROWS = DEC_SEQ // GRID_W
NA_WIN_TOK = NA_WIN_ROWS * GRID_W


def _na_window_start(r):
    return jnp.clip(r - NA_WIN_ROWS // 2, 0, NA_ROWS - NA_WIN_ROWS)


def _na_bias_table(rpb):
    col = np.arange(GRID_W)
    col_start = np.clip(col - NA_WIN_COLS // 2, 0, GRID_W - NA_WIN_COLS)
    col_mask = (col[None, :] >= col_start[:, None]) & (col[None, :] < col_start[:, None] + NA_WIN_COLS)
    rel_col = np.clip(col[None, :] - col[:, None] + NA_WIN_COLS - 1, 0, 2 * NA_WIN_COLS - 2)
    n_rel = 2 * NA_WIN_COLS - 1
    onehot = (rel_col[None] == np.arange(n_rel)[:, None, None]).astype(np.float32).reshape(n_rel, -1)
    expanded = jnp.dot(rpb.reshape(-1, n_rel), onehot, precision=HIGHEST)
    banded = jnp.where(col_mask, expanded.reshape(NA_HEADS, 2 * NA_WIN_ROWS - 1, GRID_W, GRID_W), NEG_INF)
    return jnp.concatenate([banded[:, :-1], banded[:, 1:]], axis=-1)


def _na_attn_kernel(q_ref, k_ref, v_ref, ck_ref, cv_ref, bias_ref, y_in_ref, o_ref):
    del y_in_ref
    r = pl.program_id(1)
    start = pl.multiple_of(_na_window_start(r) * GRID_W, GRID_W)
    win = pl.ds(start, NA_WIN_TOK)
    heads = _head_slices()
    nh = range(NA_HEADS)
    q = [(q_ref[:, sl] * ATT_SCALE).astype(BF16) for sl in heads]
    rel0 = NA_WIN_ROWS - 1 - (r - _na_window_start(r))
    bias = [jnp.concatenate([bias_ref[h, rel0 + 2 * p] for p in range(NA_WIN_ROWS // 2)], axis=1) for h in nh]
    s_win = [_nt_dot(q[h], k_ref[win, sl].astype(BF16)) + bias[h] for h, sl in enumerate(heads)]
    s_ctx = [_nt_dot(q[h], ck_ref[0, 0, :, sl].astype(BF16)) for h, sl in enumerate(heads)]
    m = [jnp.maximum(jnp.max(s_win[h], axis=-1, keepdims=True), jnp.max(s_ctx[h], axis=-1, keepdims=True))
         for h in nh]
    e_win = [jnp.exp(s_win[h] - m[h]) for h in nh]
    e_ctx = [jnp.exp(s_ctx[h] - m[h]) for h in nh]
    o_win = [jnp.dot(e_win[h].astype(BF16), v_ref[win, sl].astype(BF16), preferred_element_type=F32)
             for h, sl in enumerate(heads)]
    o_ctx = [jnp.dot(e_ctx[h].astype(BF16), cv_ref[0, 0, :, sl].astype(BF16), preferred_element_type=F32)
             for h, sl in enumerate(heads)]
    for h, sl in enumerate(heads):
        denom = jnp.sum(e_win[h], axis=-1, keepdims=True) + jnp.sum(e_ctx[h], axis=-1, keepdims=True)
        o_ref[:, sl] = ((o_win[h] + o_ctx[h]) / denom).astype(BF16)


def _na_attention(proj, cache_k, cache_v, rpb, y_na, layer):
    q_row0 = T_P // GRID_W
    seq0 = T_P // DEC_SEQ
    kv_spec = lambda j: pl.BlockSpec((DEC_SEQ, NA_WIDTH), lambda b, r: (seq0 + b, j))
    ctx_spec = pl.BlockSpec((1, 1, SEQ, NA_WIDTH), lambda b, r: (b, layer, 0, 0))
    return pl.pallas_call(
        _na_attn_kernel,
        out_shape=jax.ShapeDtypeStruct((T, NA_WIDTH), BF16),
        grid=(DEC_BATCH, NA_ROWS),
        in_specs=[
            pl.BlockSpec((GRID_W, NA_WIDTH), lambda b, r: (q_row0 + b * NA_ROWS + r, 0)),
            kv_spec(1), kv_spec(2), ctx_spec, ctx_spec,
            pl.BlockSpec((NA_HEADS, 2 * NA_WIN_ROWS - 2, GRID_W, 2 * GRID_W), lambda b, r: (0, 0, 0, 0),
                         pipeline_mode=pl.Buffered(1)),
            pl.BlockSpec(memory_space=pl.ANY),
        ],
        out_specs=pl.BlockSpec((GRID_W, NA_WIDTH), lambda b, r: (q_row0 + b * NA_ROWS + r, 0)),
        input_output_aliases={6: 0},
        compiler_params=_cparams(("arbitrary", "arbitrary")),
        name="na_attention",
    )(proj, proj, proj, cache_k.reshape(DEC_BATCH, DEPTH, SEQ, NA_WIDTH),
      cache_v.reshape(DEC_BATCH, DEPTH, SEQ, NA_WIDTH), _na_bias_table(rpb), y_na)


HY_CB = 256


def _split_bf16(x):
    hi = x.astype(BF16)
    return hi, (x - hi.astype(F32)).astype(BF16)


def _dot3(a_hi, a_lo, b):
    b_hi, b_lo = _split_bf16(b)
    return (jnp.dot(a_hi, b_hi, preferred_element_type=F32) + jnp.dot(a_lo, b_hi, preferred_element_type=F32)
            + jnp.dot(a_hi, b_lo, preferred_element_type=F32))


def _trig_table_kernel(*out_refs, length, freq_on_rows, scale):
    rows, cols = out_refs[0].shape
    r = lax.broadcasted_iota(jnp.int32, (rows, cols), 0) + pl.program_id(0) * rows
    c = lax.broadcasted_iota(jnp.int32, (rows, cols), 1)
    f, t = (r, c) if freq_on_rows else (c, r)
    k = f & (length - 1)
    quarter = jnp.where(f >= length, length, 0)
    phase = (2 * k + 1) * t + (-quarter if freq_on_rows else quarter)
    phase = phase & (4 * length - 1)
    phase = jnp.where(phase >= 2 * length, phase - 4 * length, phase)
    val = jnp.cos(phase.astype(F32) * (math.pi / (2 * length))) * scale
    for ref, part in zip(out_refs, _split_bf16(val)):
        ref[...] = part


def _trig_table(length, freq_on_rows):
    shape = (2 * length, length) if freq_on_rows else (length, 2 * length)
    tr = 256
    n_parts = 2 if freq_on_rows else 1
    spec = pl.BlockSpec((tr, shape[1]), lambda i: (i, 0))
    return pl.pallas_call(
        functools.partial(_trig_table_kernel, length=length, freq_on_rows=freq_on_rows,
                          scale=1.0 if freq_on_rows else 1.0 / length),
        out_shape=[jax.ShapeDtypeStruct(shape, BF16)] * n_parts,
        grid=(shape[0] // tr,), out_specs=[spec] * n_parts,
        compiler_params=_cparams(("arbitrary",)), name="dft_table",
    )()


def _hy_filter_kernel(tn_ref, band_ref, w1_ref, b1_ref, fr_ref, w2_ref, b2_ref, w3_ref, dec_ref,
                      fhi_ref, flo_ref, o_ref, *, length):
    t_norm = tn_ref[...]
    t_idx = lax.broadcasted_iota(jnp.int32, (length, LANE), 0).astype(F32)
    lane = lax.broadcasted_iota(jnp.int32, (length, LANE), 1)
    ang = (2.0 * math.pi / length) * t_idx * band_ref[...]
    z = jnp.where(lane == 0, t_norm,
                  jnp.where(lane <= HY_BANDS, jnp.cos(ang), jnp.where(lane <= 2 * HY_BANDS, jnp.sin(ang), 0.0)))
    hdn = jnp.sin(fr_ref[0:1] * (jnp.dot(z, w1_ref[...], precision=HIGHEST, preferred_element_type=F32)
                                 + b1_ref[...]))
    hdn = jnp.sin(fr_ref[1:2] * (jnp.dot(hdn, w2_ref[...], precision=HIGHEST, preferred_element_type=F32)
                                 + b2_ref[...]))
    filt = jnp.dot(hdn, w3_ref[...], precision=HIGHEST, preferred_element_type=F32)
    filt = filt * jnp.exp(-t_norm * jnp.abs(dec_ref[...]))
    first = lax.broadcasted_iota(jnp.int32, (length, HY_WIDTH), 0) == 0
    for o in range(HY_ORDER):
        fwd = filt[:, (2 * o) * HY_WIDTH:(2 * o + 1) * HY_WIDTH]
        bwd = filt[:, (2 * o + 1) * HY_WIDTH:(2 * o + 2) * HY_WIDTH]
        bwd = jnp.where(first, 0.0, pltpu.roll(bwd, 1, 0))
        o_ref[o, :length] = _dot3(fhi_ref[:length], flo_ref[:length], fwd + bwd)
        o_ref[o, length:] = _dot3(fhi_ref[length:], flo_ref[length:], bwd - fwd)


def _hy_filter_spectrum(length, f_hi, f_lo, w1, b1, freq, w2, b2, w3, decay):
    emb = 1 + 2 * HY_BANDS
    t_norm = jnp.linspace(0.0, 1.0, length, dtype=F32).reshape(length, 1)
    bands = np.zeros((1, LANE), np.float32)
    bands[0, 1:1 + HY_BANDS] = bands[0, 1 + HY_BANDS:emb] = np.linspace(1e-4, HY_BANDS - 1, HY_BANDS,
                                                                        dtype=np.float32)
    w1p = jnp.zeros((LANE, w1.shape[1]), F32).at[:emb].set(w1)
    return pl.pallas_call(
        functools.partial(_hy_filter_kernel, length=length),
        out_shape=jax.ShapeDtypeStruct((HY_ORDER, 2 * length, HY_WIDTH), F32),
        compiler_params=pltpu.CompilerParams(vmem_limit_bytes=VMEM_LIMIT),
        name="hyena_filter",
    )(t_norm, jnp.asarray(bands), w1p, b1.reshape(1, -1), freq, w2, b2.reshape(1, -1), w3,
      decay.reshape(1, -1), f_hi, f_lo)


def _hyena_kernel(v_ref, x1_ref, x2_ref, cwv_ref, cw1_ref, cw2_ref, cbv_ref, cb1_ref, cb2_ref, skip_ref,
                  hs_ref, f_ref, g_ref, *rest, length, parts):
    o_ref = rest[-1]
    n = range(len(parts))

    def short_conv(u_ref, cw_ref, cb_ref, rs, cs):
        u = u_ref[rs, cs]
        row = lax.broadcasted_iota(jnp.int32, u.shape, 0)
        prev = jnp.where(row == 0, 0.0, pltpu.roll(u, 1, 0))
        nxt = jnp.where(row == length - 1, 0.0, pltpu.roll(u, length - 1, 0))
        return prev * cw_ref[0:1, cs] + u * cw_ref[1:2, cs] + nxt * cw_ref[2:3, cs] + cb_ref[:, cs]

    z = [short_conv(v_ref, cwv_ref, cbv_ref, rs, cs) for rs, cs in parts]
    gates = [(short_conv(x1_ref, cw1_ref, cb1_ref, rs, cs), short_conv(x2_ref, cw2_ref, cb2_ref, rs, cs))
             for rs, cs in parts]
    for o in range(HY_ORDER):
        zf = [jnp.dot(f_ref[...], z[i].astype(BF16), preferred_element_type=F32) for i in n]
        p = []
        for i, (rs, cs) in enumerate(parts):
            zc, zs = zf[i][:length], zf[i][length:]
            h_re, h_im = hs_ref[o, :length, cs], hs_ref[o, length:, cs]
            p.append(jnp.concatenate([zc * h_re + zs * h_im, zc * h_im - zs * h_re], axis=0).astype(BF16))
        conv = [jnp.dot(g_ref[...], p[i], preferred_element_type=F32) for i in n]
        z = [gates[i][o] * (conv[i] + z[i] * skip_ref[o:o + 1, cs]) for i, (rs, cs) in enumerate(parts)]
    for i, (rs, cs) in enumerate(parts):
        o_ref[rs, cs] = z[i].astype(BF16)


def _hyena(proj, y_hy, layer, length, tables, spectrum, conv_w, conv_b, skip):
    seqs = 2 if length == SEQ else 1
    rows = seqs * length
    nblk = (T_P if length == SEQ else T_S) // rows
    row0 = 0 if length == SEQ else T_P // rows
    if seqs > 1:
        parts = [(slice(b * length, (b + 1) * length), slice(None)) for b in range(seqs)]
    else:
        parts = [(slice(None), slice(c * HY_CB, (c + 1) * HY_CB)) for c in range(HY_WIDTH // HY_CB)]
    col0 = C_HY // HY_WIDTH
    u_spec = lambda part: pl.BlockSpec((rows, HY_WIDTH), lambda s: (row0 + s, col0 + part))
    cw_spec = lambda part: pl.BlockSpec((1, 3, HY_WIDTH), lambda s: (layer, 0, part))
    cb_spec = lambda part: pl.BlockSpec((1, 1, HY_WIDTH), lambda s: (layer, 0, part))
    resident = functools.partial(pl.BlockSpec, pipeline_mode=pl.Buffered(1))
    tab_specs = [resident(t.shape, lambda s: (0, 0)) for t in tables]
    in_specs = ([u_spec(0), u_spec(1), u_spec(2), cw_spec(0), cw_spec(1), cw_spec(2),
                 cb_spec(0), cb_spec(1), cb_spec(2),
                 pl.BlockSpec((1, HY_ORDER, HY_WIDTH), lambda s: (layer, 0, 0)),
                 resident((HY_ORDER, 2 * length, HY_WIDTH), lambda s: (0, 0, 0))] + tab_specs)
    args = [proj] * 3 + [conv_w] * 3 + [conv_b.reshape(DEPTH, 1, -1)] * 3 + [skip, spectrum] + list(tables)
    aliases = {}
    if y_hy is not None:
        in_specs.append(pl.BlockSpec(memory_space=pl.ANY))
        args.append(y_hy)
        aliases = {len(args) - 1: 0}

    def body(v_ref, x1_ref, x2_ref, cwv, cw1, cw2, cbv, cb1, cb2, skip_ref, hs_ref, *rest):
        _hyena_kernel(v_ref, x1_ref, x2_ref, cwv.at[0], cw1.at[0], cw2.at[0], cbv.at[0], cb1.at[0], cb2.at[0],
                      skip_ref.at[0], hs_ref, *rest, length=length, parts=parts)

    return pl.pallas_call(
        body,
        out_shape=jax.ShapeDtypeStruct((T, HY_WIDTH), BF16),
        grid=(nblk,), in_specs=in_specs,
        out_specs=pl.BlockSpec((rows, HY_WIDTH), lambda s: (row0 + s, 0)),
        input_output_aliases=aliases,
        compiler_params=_cparams(("arbitrary",)),
        name="hyena",
    )(*args)


GC = 128
GDN_SCALE = GDN_DK ** -0.5
GDN_CTX_PER_STEP = 2


def _state_slot(b, d):
    return (2 * b + d) * GDN_HEADS


def _merge_masks(lower):
    ri = lax.broadcasted_iota(jnp.int32, (GC, GC), 0)
    ci = lax.broadcasted_iota(jnp.int32, (GC, GC), 1)
    hi, lo = (ri, ci) if lower else (ci, ri)
    return [((hi >> (b + 1)) == (lo >> (b + 1))) & ((hi >> b) > (lo >> b)) for b in range(int(math.log2(GC)))]


def _unit_tri_inverse(a, masks):
    ri = lax.broadcasted_iota(jnp.int32, (GC, GC), 0)
    ci = lax.broadcasted_iota(jnp.int32, (GC, GC), 1)
    eye = (ri == ci).astype(F32)
    n = range(len(a))
    t = [eye - jnp.where(masks[i][0], a[i], 0.0) for i in n]
    for level in range(1, len(masks[0])):
        tb = [t[i].astype(BF16) for i in n]
        ta = [jnp.dot(tb[i], jnp.where(masks[i][level], a[i], 0.0).astype(BF16), preferred_element_type=F32)
              for i in n]
        tat = [jnp.dot(ta[i].astype(BF16), tb[i], preferred_element_type=F32) for i in n]
        t = [t[i] - tat[i] for i in n]
    return t


def _gdn_kernel(q_ref, k_ref, v_ref, z_ref, ba_ref, cwq_ref, cwk_ref, cwv_ref, alog_ref, dt_ref, ng_ref,
                sf0_ref, sb0_ref, *rest, length, nb):
    y_ref, sf_ref, sb_ref, qn_ref, kn_ref, vn_ref, beta_ref, g_ref, of_ref, ob_ref, s_ref = rest[-11:]
    n_chunks = length // GC
    rows_total = nb * length
    row = lax.broadcasted_iota(jnp.int32, (rows_total, GDN_WIDTH), 0) & (length - 1)
    first, last = row == 0, row == length - 1

    def conv_silu(u_ref, cw_ref):
        u = u_ref[...]
        prev = jnp.where(first, 0.0, pltpu.roll(u, 1, 0))
        nxt = jnp.where(last, 0.0, pltpu.roll(u, rows_total - 1, 0))
        c = prev * cw_ref[0:1] + u * cw_ref[1:2] + nxt * cw_ref[2:3]
        return c * jax.nn.sigmoid(c)

    q = conv_silu(q_ref, cwq_ref)
    k = conv_silu(k_ref, cwk_ref)
    vn_ref[...] = conv_silu(v_ref, cwv_ref)
    for h in range(GDN_HEADS):
        sl = slice(h * GDN_DK, (h + 1) * GDN_DK)
        qh, kh = q[:, sl], k[:, sl]
        qn_ref[:, sl] = qh * lax.rsqrt(jnp.sum(qh * qh, axis=-1, keepdims=True) + NORM_EPS) * GDN_SCALE
        kn_ref[:, sl] = kh * lax.rsqrt(jnp.sum(kh * kh, axis=-1, keepdims=True) + NORM_EPS)
    ba = ba_ref[...]
    beta_ref[...] = jax.nn.sigmoid(ba)
    g_ref[...] = -jnp.exp(alog_ref[...]) * jax.nn.softplus(ba + dt_ref[...])

    ri = lax.broadcasted_iota(jnp.int32, (GC, GC), 0)
    ci = lax.broadcasted_iota(jnp.int32, (GC, GC), 1)
    lower = (ri >= ci).astype(F32)
    upper = (ri <= ci).astype(F32)

    def chunk_step(n, carry):
        scans = [(b, d) for b in range(nb) for d in range(2)]
        rows = {(b, d): pl.ds(pl.multiple_of(b * length + (n if d == 0 else n_chunks - 1 - n) * GC, GC), GC)
                for b, d in scans}
        loaded = {}
        for bd in scans:
            loaded[bd] = (g_ref[rows[bd], :], beta_ref[rows[bd], :])
            for h in range(GDN_HEADS):
                sl = slice(h * GDN_DK, (h + 1) * GDN_DK)
                loaded[bd + (h,)] = (qn_ref[rows[bd], sl], kn_ref[rows[bd], sl], vn_ref[rows[bd], sl],
                                     s_ref[_state_slot(*bd) + h])
        units = [(b, d, h) for b, d in scans for h in range(GDN_HEADS)]
        incl = [(ri >= ci), (ri <= ci)]
        strict = [(ri > ci), (ri < ci)]
        masks = [_merge_masks(True), _merge_masks(False)]
        gcs = {(b, d): jnp.dot((lower, upper)[d], loaded[b, d][0], precision=HIGHEST, preferred_element_type=F32)
               for b, d in scans}
        gcs_t = {bd: g.T for bd, g in gcs.items()}
        edge = [GC - 1, 0]
        g_col = [gcs[b, d][:, 8 + 4 * d + h:9 + 4 * d + h] for b, d, h in units]
        g_row = [gcs_t[b, d][8 + 4 * d + h:9 + 4 * d + h, :] for b, d, h in units]
        g_end = [gcs_t[b, d][8 + 4 * d + h:9 + 4 * d + h, edge[d]:edge[d] + 1] for b, d, h in units]
        b_col = [loaded[b, d][1][:, 4 * d + h:4 * d + h + 1] for b, d, h in units]
        qc, kc, vc, st = (list(x) for x in zip(*(loaded[u] for u in units)))
        nu = range(len(units))
        decay = [jnp.where(incl[d], jnp.exp(jnp.where(incl[d], g_col[i] - g_row[i], 0.0)), 0.0)
                 for i, (b, d, h) in enumerate(units)]
        kb = [kc[i] * b_col[i] for i in nu]
        kcb = [kc[i].astype(BF16) for i in nu]
        kk = [_nt_dot(kb[i].astype(BF16), kcb[i]) for i in nu]
        qk = [_nt_dot(qc[i].astype(BF16), kcb[i]) for i in nu]
        a = [jnp.where(strict[d], kk[i] * decay[i], 0.0) for i, (b, d, h) in enumerate(units)]
        t = _unit_tri_inverse(a, [masks[d] for b, d, h in units])
        e_col = [jnp.exp(g_col[i]) for i in nu]
        rhs = [jnp.concatenate([vc[i] * b_col[i], kb[i] * e_col[i]], axis=1).astype(BF16) for i in nu]
        sol = [jnp.dot(t[i].astype(BF16), rhs[i], preferred_element_type=F32) for i in nu]
        attn = [jnp.where(incl[d], qk[i] * decay[i], 0.0).astype(BF16) for i, (b, d, h) in enumerate(units)]
        sb = [st[i].astype(BF16) for i in nu]
        ws = [jnp.dot(sol[i][:, GDN_DV:].astype(BF16), sb[i], preferred_element_type=F32) for i in nu]
        qs = [jnp.dot((qc[i] * e_col[i]).astype(BF16), sb[i], preferred_element_type=F32) for i in nu]
        v_new = [(sol[i][:, :GDN_DV] - ws[i]).astype(BF16) for i in nu]
        av = [jnp.dot(attn[i], v_new[i], preferred_element_type=F32) for i in nu]
        k_dec_t = [(kc[i] * jnp.exp(g_end[i] - g_col[i])).T.astype(BF16) for i in nu]
        kv = [jnp.dot(k_dec_t[i], v_new[i], preferred_element_type=F32) for i in nu]
        results = [(qs[i] + av[i], st[i] * jnp.exp(g_end[i]) + kv[i]) for i in nu]
        for (b, d, h), (o, s) in zip(units, results):
            (of_ref if d == 0 else ob_ref)[rows[b, d], h * GDN_DV:(h + 1) * GDN_DV] = o
            s_ref[_state_slot(b, d) + h] = s
        return carry

    for b in range(nb):
        s_ref[pl.ds(_state_slot(b, 0), GDN_HEADS)] = sf0_ref[b]
        s_ref[pl.ds(_state_slot(b, 1), GDN_HEADS)] = sb0_ref[b]
    lax.fori_loop(0, n_chunks, chunk_step, 0)
    for b in range(nb):
        sf_ref[b, 0] = s_ref[pl.ds(_state_slot(b, 0), GDN_HEADS)]
        sb_ref[b, 0] = s_ref[pl.ds(_state_slot(b, 1), GDN_HEADS)]
    for h in range(GDN_HEADS):
        sl = slice(h * GDN_DV, (h + 1) * GDN_DV)
        o = of_ref[:, sl] + ob_ref[:, sl]
        o = o * lax.rsqrt(jnp.mean(o * o, axis=-1, keepdims=True) + NORM_EPS) * ng_ref[...]
        zh = z_ref[:, sl]
        y_ref[:, sl] = (o * (zh * jax.nn.sigmoid(zh))).astype(BF16)


def _gdn(proj, y_gdn, states_out, layer, length, s_f0, s_b0, conv_w, a_log, dt_bias, norm_g):
    nb = GDN_CTX_PER_STEP if length == SEQ else 1
    rows = nb * length
    nblk = (T_P if length == SEQ else T_S) // rows
    row0 = 0 if length == SEQ else T_P // rows
    col = lambda c: c // GDN_WIDTH
    blk = lambda c: pl.BlockSpec((rows, GDN_WIDTH), lambda s: (row0 + s, col(c)))
    cw = lambda part: pl.BlockSpec((1, 3, GDN_WIDTH), lambda s: (layer, 0, part))
    vec = lambda: pl.BlockSpec((1, 1, LANE), lambda s: (layer, 0, 0))
    st_in = pl.BlockSpec((nb, GDN_HEADS, GDN_DK, GDN_DV), lambda s: (s, 0, 0, 0))
    st_out = pl.BlockSpec((nb, 1, GDN_HEADS, GDN_DK, GDN_DV), lambda s: (s, layer, 0, 0, 0))
    pad_lanes = lambda x, off: jnp.zeros((DEPTH, 1, LANE), F32).at[:, 0, off:off + 2 * GDN_HEADS].set(
        x.reshape(DEPTH, -1))
    in_specs = [blk(C_GDN), blk(C_GDN + GDN_WIDTH), blk(C_GDN + 2 * GDN_WIDTH), blk(C_Z),
                pl.BlockSpec((rows, LANE), lambda s: (row0 + s, C_BA // LANE)),
                cw(0), cw(1), cw(2), vec(), vec(), vec(), st_in, st_in]
    args = [proj] * 5 + [conv_w] * 3 + [pad_lanes(a_log, 2 * GDN_HEADS), pad_lanes(dt_bias, 2 * GDN_HEADS),
                                        norm_g.reshape(DEPTH, 1, GDN_DV), s_f0, s_b0]
    st_shape = jax.ShapeDtypeStruct((nblk * nb, DEPTH, GDN_HEADS, GDN_DK, GDN_DV), F32)
    aliases = {}
    for carried, out_idx in ((y_gdn, 0),) + (((states_out[0], 1), (states_out[1], 2)) if states_out else ()):
        if carried is not None:
            in_specs.append(pl.BlockSpec(memory_space=pl.ANY))
            args.append(carried)
            aliases[len(args) - 1] = out_idx

    def body(q_ref, k_ref, v_ref, z_ref, ba_ref, cwq, cwk, cwv, alog, dt, ng, sf0, sb0, *rest):
        _gdn_kernel(q_ref, k_ref, v_ref, z_ref, ba_ref, cwq.at[0], cwk.at[0], cwv.at[0], alog.at[0], dt.at[0],
                    ng.at[0], sf0, sb0, *rest, length=length, nb=nb)

    seq_buf = lambda w: pltpu.VMEM((rows, w), F32)
    return pl.pallas_call(
        body,
        out_shape=[jax.ShapeDtypeStruct((T, GDN_WIDTH), BF16), st_shape, st_shape],
        grid=(nblk,), in_specs=in_specs,
        out_specs=[pl.BlockSpec((rows, GDN_WIDTH), lambda s: (row0 + s, 0)), st_out, st_out],
        scratch_shapes=[seq_buf(GDN_WIDTH), seq_buf(GDN_WIDTH), seq_buf(GDN_WIDTH), seq_buf(LANE), seq_buf(LANE),
                        seq_buf(GDN_WIDTH), seq_buf(GDN_WIDTH),
                        pltpu.VMEM((nb * 2 * GDN_HEADS, GDN_DK, GDN_DV), F32)],
        input_output_aliases=aliases,
        compiler_params=_cparams(("arbitrary",)),
        name="gated_deltanet",
    )(*args)


def kernel(x_prompt, x_sample, cache_k, cache_v, state_fwd, state_bwd, c, c_ctx, ln1_g, ln2_g, w_mod, b_mod,
           w_in, na_rpb, hy_conv_w, hy_conv_b, hy_w1, hy_b1, hy_freq, hy_w2, hy_b2, hy_w3, hy_decay, hy_skip,
           gdn_conv_w, gdn_a_log, gdn_dt_bias, gdn_norm_g, w_pa, w_pb, w_pc, b_gate, w_out, ffn_w_up,
           ffn_conv_w, ffn_conv_b, ffn_w_down, final_g):
    x = (x_prompt.reshape(T_P, D), x_sample.reshape(T_S, D))
    cond = jnp.concatenate([c_ctx[None], c, jnp.zeros((N_COND - 1 - DEC_BATCH, D), F32)], axis=0)
    mod = _mod_table(cond, w_mod, b_mod).reshape(DEPTH * N_COND, 1, 6 * D)

    w_pa_b, w_pb_b, w_pc_b, w_out_b = (w.astype(BF16) for w in (w_pa, w_pb, w_pc, w_out))
    w_down_b = ffn_w_down.astype(BF16)
    w_in_t = jnp.swapaxes(w_in, 1, 2)
    tables = {n: tuple(_trig_table(n, True)) + tuple(_trig_table(n, False)) for n in (SEQ, DEC_SEQ)}
    zero_state = jnp.zeros((BATCH, GDN_HEADS, GDN_DK, GDN_DV), F32)

    caches = None
    states = None
    h = _norm_mod(x, ln1_g, mod, 0, 0)
    for layer in range(DEPTH):
        proj = _in_proj(h, w_in_t, layer)

        y_na, new_k, new_v = _ctx_attention(proj, layer, caches)
        caches = (new_k, new_v)
        y_na = _na_attention(proj, cache_k, cache_v, na_rpb[layer], y_na, layer)

        y_hy = None
        for n in (SEQ, DEC_SEQ):
            spectrum = _hy_filter_spectrum(n, tables[n][0], tables[n][1], hy_w1[layer], hy_b1[layer],
                                           hy_freq[layer], hy_w2[layer], hy_b2[layer], hy_w3[layer],
                                           hy_decay[layer])
            y_hy = _hyena(proj, y_hy, layer, n, (tables[n][0], tables[n][2]), spectrum,
                          hy_conv_w, hy_conv_b, hy_skip)

        y_gdn, s_f, s_b = _gdn(proj, None, states, layer, SEQ, zero_state, zero_state,
                               gdn_conv_w, gdn_a_log, gdn_dt_bias, gdn_norm_g)
        states = (s_f, s_b)
        y_gdn = _gdn(proj, y_gdn, None, layer, DEC_SEQ, state_fwd[:, layer], state_bwd[:, layer],
                     gdn_conv_w, gdn_a_log, gdn_dt_bias, gdn_norm_g)[0]

        x, h = _mix_out(y_na, y_hy, y_gdn, proj, b_gate, w_pa_b, w_pb_b, w_pc_b, w_out_b, x, mod, ln2_g, layer)
        act = _ffn_up(h, ffn_w_up, ffn_conv_w, ffn_conv_b, layer)
        x, h = _ffn_down(act, w_down_b, x, mod, layer, ln1_g, final_g)

    y_p, y_s = x, h
    cache_shape = (BATCH, DEPTH, SEQ, NA_HEADS, NA_DH)
    return (y_p.reshape(BATCH, SEQ, D), y_s.reshape(DEC_BATCH, DEC_SEQ, D),
            caches[0].reshape(cache_shape), caches[1].reshape(cache_shape), states[0], states[1])
```

```python
import functools
import math

import jax
import jax.numpy as jnp
import numpy as np
from jax import lax
from jax.experimental import pallas as pl
from jax.experimental.pallas import tpu as pltpu

F32 = jnp.float32
BF16 = jnp.bfloat16

D = 2048
BATCH, SEQ = 32, 256
DEC_BATCH, DEC_SEQ = 4, 1024
DEPTH = 2
GRID_W = 64
NA_HEADS, NA_DH = 8, 128
NA_WIDTH = NA_HEADS * NA_DH
NA_WIN_ROWS, NA_WIN_COLS = 8, 16
HY_WIDTH = 512
HY_ORDER = 2
HY_BANDS = 16
GDN_HEADS, GDN_DK, GDN_DV = 4, 128, 128
GDN_WIDTH = GDN_HEADS * GDN_DV
D_FF = 5632
N_BRANCH = 3
NORM_EPS = 1e-6
NEG_INF = -1e30

T_P = BATCH * SEQ
T_S = DEC_BATCH * DEC_SEQ
T = T_P + T_S
N_COND = 8

C_Q, C_K, C_V = 0, NA_WIDTH, 2 * NA_WIDTH
C_HY = 3 * NA_WIDTH
C_GDN = C_HY + 3 * HY_WIDTH
C_Z = C_GDN + 3 * GDN_HEADS * GDN_DK
C_BA = C_Z + GDN_WIDTH
C_GATE = C_BA + 4 * GDN_HEADS
N_IN = C_GATE + N_BRANCH * D
GATE_BLOCK = C_BA
GATE_SHIFT = C_GATE - C_BA

LANE = 128
VMEM_LIMIT = 56 * 1024 * 1024
HIGHEST = lax.Precision.HIGHEST


def _cparams(sem):
    return pltpu.CompilerParams(dimension_semantics=sem, vmem_limit_bytes=VMEM_LIMIT)


def _cond_row(tile, rows_per_tile):
    first_latent = T_P // rows_per_tile
    per_seq = DEC_SEQ // rows_per_tile
    return jnp.where(tile < first_latent, 0, 1 + (tile - first_latent) // per_seq)


def _mod_kernel(c_ref, w_ref, b_ref, o_ref):
    c = c_ref[...]
    s = (c * jax.nn.sigmoid(c)).astype(BF16)
    o_ref[0] = jnp.dot(s, w_ref[0].astype(BF16), preferred_element_type=F32) + b_ref[0]


def _mod_table(cond, w_mod, b_mod):
    tn = 1024
    n = 6 * D
    return pl.pallas_call(
        _mod_kernel,
        out_shape=jax.ShapeDtypeStruct((DEPTH, N_COND, n), F32),
        grid=(DEPTH, n // tn),
        in_specs=[
            pl.BlockSpec((N_COND, D), lambda l, j: (0, 0)),
            pl.BlockSpec((1, D, tn), lambda l, j: (l, 0, j)),
            pl.BlockSpec((1, 1, tn), lambda l, j: (l, 0, j)),
        ],
        out_specs=pl.BlockSpec((1, N_COND, tn), lambda l, j: (l, 0, j)),
        compiler_params=_cparams(("arbitrary", "arbitrary")),
        name="mod_table",
    )(cond, w_mod, b_mod.reshape(DEPTH, 1, n))


def _token_specs(x, tm):
    if not isinstance(x, tuple):
        return [pl.BlockSpec((tm, D), lambda i: (i, 0))], [x]
    n_ctx = T_P // tm
    return ([pl.BlockSpec((tm, D), lambda i: (jnp.minimum(i, n_ctx - 1), 0)),
             pl.BlockSpec((tm, D), lambda i: (jnp.maximum(i - n_ctx, 0), 0))], list(x))


def _token_tile(x_refs, tm):
    if len(x_refs) == 1:
        return x_refs[0][...]
    return jnp.where(pl.program_id(0) < T_P // tm, x_refs[0][...], x_refs[1][...])


def _modulated_norm(x, g, shift, scale):
    y = x * lax.rsqrt(jnp.mean(x * x, axis=-1, keepdims=True) + NORM_EPS) * g
    return (y * (1.0 + scale) + shift).astype(BF16)


NORM_TM = 1024


def _norm_mod_kernel(*refs):
    g_ref, sh_ref, sc_ref, o_ref = refs[-4:]
    o_ref[...] = _modulated_norm(_token_tile(refs[:-4], NORM_TM), g_ref[0], sh_ref[0], sc_ref[0])


def _norm_mod(x, g, mod, layer, shift_chunk):
    tm = NORM_TM
    x_specs, x_args = _token_specs(x, tm)
    return pl.pallas_call(
        _norm_mod_kernel,
        out_shape=jax.ShapeDtypeStruct((T, D), BF16),
        grid=(T // tm,),
        in_specs=x_specs + [
            pl.BlockSpec((1, 1, D), lambda i: (layer, 0, 0)),
            pl.BlockSpec((1, 1, D), lambda i: (layer * N_COND + _cond_row(i, tm), 0, shift_chunk)),
            pl.BlockSpec((1, 1, D), lambda i: (layer * N_COND + _cond_row(i, tm), 0, shift_chunk + 1)),
        ],
        out_specs=pl.BlockSpec((tm, D), lambda i: (i, 0)),
        compiler_params=_cparams(("arbitrary",)),
        name="norm_mod",
    )(*x_args, g.reshape(DEPTH, 1, D), mod, mod)


def _proj_kernel(h_ref, wt_ref, o_ref, wb_ref):
    @pl.when(pl.program_id(1) == 0)
    def _():
        wb_ref[...] = wt_ref[0].astype(BF16)

    o_ref[...] = lax.dot_general(h_ref[...], wb_ref[...], (((1,), (1,)), ((), ())), preferred_element_type=F32)


def _in_proj(h, w_in_t, layer):
    tm, tn = 1536, 1024
    return pl.pallas_call(
        _proj_kernel,
        out_shape=jax.ShapeDtypeStruct((T, N_IN), F32),
        grid=(pl.cdiv(N_IN, tn), T // tm),
        in_specs=[
            pl.BlockSpec((tm, D), lambda j, i: (i, 0)),
            pl.BlockSpec((1, tn, D), lambda j, i: (layer, j, 0)),
        ],
        out_specs=pl.BlockSpec((tm, tn), lambda j, i: (i, j)),
        scratch_shapes=[pltpu.VMEM((tn, D), BF16)],
        compiler_params=_cparams(("arbitrary", "arbitrary")),
        name="in_proj",
    )(h, w_in_t)


MIX_TM = 256


def _mix_out_kernel(yna_ref, yhy_ref, ygdn_ref, gl_ref, bg_ref, wpa_ref, wpb_ref, wpc_ref, wout_ref, *refs):
    gt_ref, g2_ref, sh2_ref, sc2_ref, o_ref, h2_ref = refs[-6:]
    width = gl_ref.shape[1]
    gl = pltpu.roll(gl_ref[...], width - GATE_SHIFT, 1)[:, :N_BRANCH * D] + bg_ref[0]
    gates = jax.nn.sigmoid(gl)
    merged = (gates[:, :D] * jnp.dot(yna_ref[...], wpa_ref[0], preferred_element_type=F32)
              + gates[:, D:2 * D] * jnp.dot(yhy_ref[...], wpb_ref[0], preferred_element_type=F32)
              + gates[:, 2 * D:] * jnp.dot(ygdn_ref[...], wpc_ref[0], preferred_element_type=F32))
    r = jnp.dot(merged.astype(BF16), wout_ref[0], preferred_element_type=F32)
    x = _token_tile(refs[:-6], MIX_TM) + gt_ref[0] * r
    o_ref[...] = x
    h2_ref[...] = _modulated_norm(x, g2_ref[0], sh2_ref[0], sc2_ref[0])


def _mix_out(y_na, y_hy, y_gdn, proj, b_gate, w_pa, w_pb, w_pc, w_out, x, mod, ln2_g, layer):
    tm = MIX_TM
    resident = functools.partial(pl.BlockSpec, pipeline_mode=pl.Buffered(1))
    x_specs, x_args = _token_specs(x, tm)
    mod_spec = lambda chunk: pl.BlockSpec((1, 1, D), lambda i: (layer * N_COND + _cond_row(i, tm), 0, chunk))
    row_spec = pl.BlockSpec((tm, D), lambda i: (i, 0))
    return pl.pallas_call(
        _mix_out_kernel,
        out_shape=[jax.ShapeDtypeStruct((T, D), F32), jax.ShapeDtypeStruct((T, D), BF16)],
        grid=(T // tm,),
        in_specs=[
            pl.BlockSpec((tm, NA_WIDTH), lambda i: (i, 0)),
            pl.BlockSpec((tm, HY_WIDTH), lambda i: (i, 0)),
            pl.BlockSpec((tm, GDN_WIDTH), lambda i: (i, 0)),
            pl.BlockSpec((tm, GATE_BLOCK), lambda i: (i, 1)),
            pl.BlockSpec((1, 1, N_BRANCH * D), lambda i: (layer, 0, 0)),
            resident((1, NA_WIDTH, D), lambda i: (layer, 0, 0)),
            resident((1, HY_WIDTH, D), lambda i: (layer, 0, 0)),
            resident((1, GDN_WIDTH, D), lambda i: (layer, 0, 0)),
            resident((1, D, D), lambda i: (layer, 0, 0)),
        ] + x_specs + [mod_spec(2), pl.BlockSpec((1, 1, D), lambda i: (layer, 0, 0)), mod_spec(3), mod_spec(4)],
        out_specs=[row_spec, row_spec],
        compiler_params=_cparams(("arbitrary",)),
        name="mix_out",
    )(y_na, y_hy, y_gdn, proj, b_gate.reshape(DEPTH, 1, N_BRANCH * D), w_pa, w_pb, w_pc, w_out, *x_args,
      mod, ln2_g.reshape(DEPTH, 1, D), mod, mod)


FFN_TM = 1024


def _ffn_up_kernel(h_ref, wa_ref, wb_ref, cwa_ref, cwb_ref, cba_ref, cbb_ref, o_ref, wab_ref, wbb_ref):
    m = pl.program_id(1)

    @pl.when(m == 0)
    def _():
        wab_ref[...] = wa_ref[0].astype(BF16)
        wbb_ref[...] = wb_ref[0].astype(BF16)

    seq_len = jnp.where(m < T_P // FFN_TM, SEQ, DEC_SEQ)
    pos = lax.broadcasted_iota(jnp.int32, (FFN_TM, 1), 0) & (seq_len - 1)
    has_prev = pos != 0
    has_next = pos != seq_len - 1
    h = h_ref[...]

    def conv(w_ref, cw_ref, cb_ref):
        up = jnp.dot(h, w_ref[...], preferred_element_type=F32)
        prev = jnp.where(has_prev, pltpu.roll(up, 1, 0), 0.0)
        nxt = jnp.where(has_next, pltpu.roll(up, FFN_TM - 1, 0), 0.0)
        cw = cw_ref[0]
        return prev * cw[0:1] + up * cw[1:2] + nxt * cw[2:3] + cb_ref[0]

    ua = conv(wab_ref, cwa_ref, cba_ref)
    ub = conv(wbb_ref, cwb_ref, cbb_ref)
    o_ref[...] = (ua * jax.nn.sigmoid(ua) * ub).astype(BF16)


def _ffn_up(h, w_up, conv_w, conv_b, layer):
    tn = 512
    nt = D_FF // tn
    conv_b = conv_b.reshape(DEPTH, 1, 2 * D_FF)
    return pl.pallas_call(
        _ffn_up_kernel,
        out_shape=jax.ShapeDtypeStruct((T, D_FF), BF16),
        grid=(nt, T // FFN_TM),
        in_specs=[
            pl.BlockSpec((FFN_TM, D), lambda j, i: (i, 0)),
            pl.BlockSpec((1, D, tn), lambda j, i: (layer, 0, j)),
            pl.BlockSpec((1, D, tn), lambda j, i: (layer, 0, nt + j)),
            pl.BlockSpec((1, 3, tn), lambda j, i: (layer, 0, j)),
            pl.BlockSpec((1, 3, tn), lambda j, i: (layer, 0, nt + j)),
            pl.BlockSpec((1, 1, tn), lambda j, i: (layer, 0, j)),
            pl.BlockSpec((1, 1, tn), lambda j, i: (layer, 0, nt + j)),
        ],
        out_specs=pl.BlockSpec((FFN_TM, tn), lambda j, i: (i, j)),
        scratch_shapes=[pltpu.VMEM((D, tn), BF16), pltpu.VMEM((D, tn), BF16)],
        compiler_params=_cparams(("arbitrary", "arbitrary")),
        name="ffn_up",
    )(h, w_up, w_up, conv_w, conv_w, conv_b, conv_b)


DOWN_TM = 256


def _ffn_down_kernel(a_ref, w_ref, x_ref, gt_ref, g_ref, sh_ref, sc_ref, o_ref, h_ref):
    x = x_ref[...] + gt_ref[0] * jnp.dot(a_ref[...], w_ref[0], preferred_element_type=F32)
    o_ref[...] = x
    h_ref[...] = _modulated_norm(x, g_ref[0], sh_ref[0], sc_ref[0])


def _ffn_down_final_kernel(a_ref, w_ref, x_ref, gt_ref, g_ref, yp_ref, ys_ref):
    x = x_ref[...] + gt_ref[0] * jnp.dot(a_ref[...], w_ref[0], preferred_element_type=F32)
    y = x * lax.rsqrt(jnp.mean(x * x, axis=-1, keepdims=True) + NORM_EPS) * g_ref[...]
    is_context = pl.program_id(0) < T_P // DOWN_TM

    @pl.when(is_context)
    def _():
        yp_ref[...] = y

    @pl.when(jnp.logical_not(is_context))
    def _():
        ys_ref[...] = y


def _ffn_down(act, w_down, x, mod, layer, ln1_g, final_g):
    tm = DOWN_TM
    last = layer == DEPTH - 1
    mod_spec = lambda lyr, chunk: pl.BlockSpec((1, 1, D), lambda i: (lyr * N_COND + _cond_row(i, tm), 0, chunk))
    row_spec = pl.BlockSpec((tm, D), lambda i: (i, 0))
    in_specs = [
        pl.BlockSpec((tm, D_FF), lambda i: (i, 0)),
        pl.BlockSpec((1, D_FF, D), lambda i: (layer, 0, 0), pipeline_mode=pl.Buffered(1)),
        row_spec,
        mod_spec(layer, 5),
    ]
    if last:
        n_ctx = T_P // tm
        return pl.pallas_call(
            _ffn_down_final_kernel,
            out_shape=[jax.ShapeDtypeStruct((T_P, D), F32), jax.ShapeDtypeStruct((T_S, D), F32)],
            grid=(T // tm,),
            in_specs=in_specs + [pl.BlockSpec((1, D), lambda i: (0, 0))],
            out_specs=[pl.BlockSpec((tm, D), lambda i: (jnp.minimum(i, n_ctx - 1), 0)),
                       pl.BlockSpec((tm, D), lambda i: (jnp.maximum(i - n_ctx, 0), 0))],
            compiler_params=_cparams(("arbitrary",)),
            name="ffn_down_final",
        )(act, w_down, x, mod, final_g.reshape(1, D))
    return pl.pallas_call(
        _ffn_down_kernel,
        out_shape=[jax.ShapeDtypeStruct((T, D), F32), jax.ShapeDtypeStruct((T, D), BF16)],
        grid=(T // tm,),
        in_specs=in_specs + [pl.BlockSpec((1, 1, D), lambda i: (layer + 1, 0, 0)),
                             mod_spec(layer + 1, 0), mod_spec(layer + 1, 1)],
        out_specs=[row_spec, row_spec],
        compiler_params=_cparams(("arbitrary",)),
        name="ffn_down",
    )(act, w_down, x, mod, ln1_g.reshape(DEPTH, 1, D), mod, mod)


ATT_SCALE = NA_DH ** -0.5


def _nt_dot(a, b):
    return lax.dot_general(a, b, (((1,), (1,)), ((), ())), preferred_element_type=F32)


def _head_slices():
    return [slice(h * NA_DH, (h + 1) * NA_DH) for h in range(NA_HEADS)]


def _ctx_attn_kernel(q_ref, k_ref, v_ref, o_ref, ko_ref, vo_ref):
    heads = _head_slices()
    q = [(q_ref[:, sl] * ATT_SCALE).astype(BF16) for sl in heads]
    s = [_nt_dot(q[h], k_ref[:, sl].astype(BF16)) for h, sl in enumerate(heads)]
    e = [x - jnp.max(x, axis=-1, keepdims=True) for x in s]
    e = [jnp.exp(x) for x in e]
    o = [jnp.dot(e[h].astype(BF16), v_ref[:, sl].astype(BF16), preferred_element_type=F32)
         for h, sl in enumerate(heads)]
    for h, sl in enumerate(heads):
        o_ref[:, sl] = (o[h] / jnp.sum(e[h], axis=-1, keepdims=True)).astype(BF16)
    ko_ref[0, 0] = k_ref[...]
    vo_ref[0, 0] = v_ref[...]


def _ctx_attention(proj, layer, caches):
    cache_shape = jax.ShapeDtypeStruct((BATCH, DEPTH, SEQ, NA_WIDTH), F32)
    cache_spec = pl.BlockSpec((1, 1, SEQ, NA_WIDTH), lambda b: (b, layer, 0, 0))
    qkv_specs = [pl.BlockSpec((SEQ, NA_WIDTH), lambda b, j=j: (b, j)) for j in range(3)]
    out_shape = [jax.ShapeDtypeStruct((T, NA_WIDTH), BF16), cache_shape, cache_shape]
    out_specs = [pl.BlockSpec((SEQ, NA_WIDTH), lambda b: (b, 0)), cache_spec, cache_spec]
    if caches is None:
        return pl.pallas_call(
            _ctx_attn_kernel, out_shape=out_shape, grid=(BATCH,), in_specs=qkv_specs, out_specs=out_specs,
            compiler_params=_cparams(("arbitrary",)), name="ctx_attention",
        )(proj, proj, proj)

    def body(q_ref, k_ref, v_ref, kc_ref, vc_ref, o_ref, ko_ref, vo_ref):
        del kc_ref, vc_ref
        _ctx_attn_kernel(q_ref, k_ref, v_ref, o_ref, ko_ref, vo_ref)

    any_spec = pl.BlockSpec(memory_space=pl.ANY)
    return pl.pallas_call(
        body, out_shape=out_shape, grid=(BATCH,), in_specs=qkv_specs + [any_spec, any_spec],
        out_specs=out_specs, input_output_aliases={3: 1, 4: 2},
        compiler_params=_cparams(("arbitrary",)), name="ctx_attention",
    )(proj, proj, proj, *caches)


NA_ROWS = DEC_SEQ // GRID_W
NA_WIN_TOK = NA_WIN_ROWS * GRID_W


def _na_window_start(r):
    return jnp.clip(r - NA_WIN_ROWS // 2, 0, NA_ROWS - NA_WIN_ROWS)


def _na_bias_table(rpb):
    col = np.arange(GRID_W)
    col_start = np.clip(col - NA_WIN_COLS // 2, 0, GRID_W - NA_WIN_COLS)
    col_mask = (col[None, :] >= col_start[:, None]) & (col[None, :] < col_start[:, None] + NA_WIN_COLS)
    rel_col = np.clip(col[None, :] - col[:, None] + NA_WIN_COLS - 1, 0, 2 * NA_WIN_COLS - 2)
    n_rel = 2 * NA_WIN_COLS - 1
    onehot = (rel_col[None] == np.arange(n_rel)[:, None, None]).astype(np.float32).reshape(n_rel, -1)
    expanded = jnp.dot(rpb.reshape(-1, n_rel), onehot, precision=HIGHEST)
    banded = jnp.where(col_mask, expanded.reshape(NA_HEADS, 2 * NA_WIN_ROWS - 1, GRID_W, GRID_W), NEG_INF)
    return jnp.concatenate([banded[:, :-1], banded[:, 1:]], axis=-1)


def _na_attn_kernel(q_ref, k_ref, v_ref, ck_ref, cv_ref, bias_ref, y_in_ref, o_ref):
    del y_in_ref
    r = pl.program_id(1)
    start = pl.multiple_of(_na_window_start(r) * GRID_W, GRID_W)
    win = pl.ds(start, NA_WIN_TOK)
    heads = _head_slices()
    nh = range(NA_HEADS)
    q = [(q_ref[:, sl] * ATT_SCALE).astype(BF16) for sl in heads]
    rel0 = NA_WIN_ROWS - 1 - (r - _na_window_start(r))
    bias = [jnp.concatenate([bias_ref[h, rel0 + 2 * p] for p in range(NA_WIN_ROWS // 2)], axis=1) for h in nh]
    s_win = [_nt_dot(q[h], k_ref[win, sl].astype(BF16)) + bias[h] for h, sl in enumerate(heads)]
    s_ctx = [_nt_dot(q[h], ck_ref[0, 0, :, sl].astype(BF16)) for h, sl in enumerate(heads)]
    m = [jnp.maximum(jnp.max(s_win[h], axis=-1, keepdims=True), jnp.max(s_ctx[h], axis=-1, keepdims=True))
         for h in nh]
    e_win = [jnp.exp(s_win[h] - m[h]) for h in nh]
    e_ctx = [jnp.exp(s_ctx[h] - m[h]) for h in nh]
    o_win = [jnp.dot(e_win[h].astype(BF16), v_ref[win, sl].astype(BF16), preferred_element_type=F32)
             for h, sl in enumerate(heads)]
    o_ctx = [jnp.dot(e_ctx[h].astype(BF16), cv_ref[0, 0, :, sl].astype(BF16), preferred_element_type=F32)
             for h, sl in enumerate(heads)]
    for h, sl in enumerate(heads):
        denom = jnp.sum(e_win[h], axis=-1, keepdims=True) + jnp.sum(e_ctx[h], axis=-1, keepdims=True)
        o_ref[:, sl] = ((o_win[h] + o_ctx[h]) / denom).astype(BF16)


def _na_attention(proj, cache_k, cache_v, rpb, y_na, layer):
    q_row0 = T_P // GRID_W
    seq0 = T_P // DEC_SEQ
    kv_spec = lambda j: pl.BlockSpec((DEC_SEQ, NA_WIDTH), lambda b, r: (seq0 + b, j))
    ctx_spec = pl.BlockSpec((1, 1, SEQ, NA_WIDTH), lambda b, r: (b, layer, 0, 0))
    return pl.pallas_call(
        _na_attn_kernel,
        out_shape=jax.ShapeDtypeStruct((T, NA_WIDTH), BF16),
        grid=(DEC_BATCH, NA_ROWS),
        in_specs=[
            pl.BlockSpec((GRID_W, NA_WIDTH), lambda b, r: (q_row0 + b * NA_ROWS + r, 0)),
            kv_spec(1), kv_spec(2), ctx_spec, ctx_spec,
            pl.BlockSpec((NA_HEADS, 2 * NA_WIN_ROWS - 2, GRID_W, 2 * GRID_W), lambda b, r: (0, 0, 0, 0),
                         pipeline_mode=pl.Buffered(1)),
            pl.BlockSpec(memory_space=pl.ANY),
        ],
        out_specs=pl.BlockSpec((GRID_W, NA_WIDTH), lambda b, r: (q_row0 + b * NA_ROWS + r, 0)),
        input_output_aliases={6: 0},
        compiler_params=_cparams(("arbitrary", "arbitrary")),
        name="na_attention",
    )(proj, proj, proj, cache_k.reshape(DEC_BATCH, DEPTH, SEQ, NA_WIDTH),
      cache_v.reshape(DEC_BATCH, DEPTH, SEQ, NA_WIDTH), _na_bias_table(rpb), y_na)


HY_CB = 256


def _split_bf16(x):
    hi = x.astype(BF16)
    return hi, (x - hi.astype(F32)).astype(BF16)


def _dot3(a_hi, a_lo, b):
    b_hi, b_lo = _split_bf16(b)
    return (jnp.dot(a_hi, b_hi, preferred_element_type=F32) + jnp.dot(a_lo, b_hi, preferred_element_type=F32)
            + jnp.dot(a_hi, b_lo, preferred_element_type=F32))


def _trig_table_kernel(*out_refs, length, freq_on_rows, scale):
    rows, cols = out_refs[0].shape
    r = lax.broadcasted_iota(jnp.int32, (rows, cols), 0) + pl.program_id(0) * rows
    c = lax.broadcasted_iota(jnp.int32, (rows, cols), 1)
    f, t = (r, c) if freq_on_rows else (c, r)
    k = f & (length - 1)
    quarter = jnp.where(f >= length, length, 0)
    phase = (2 * k + 1) * t + (-quarter if freq_on_rows else quarter)
    phase = phase & (4 * length - 1)
    phase = jnp.where(phase >= 2 * length, phase - 4 * length, phase)
    val = jnp.cos(phase.astype(F32) * (math.pi / (2 * length))) * scale
    for ref, part in zip(out_refs, _split_bf16(val)):
        ref[...] = part


def _trig_table(length, freq_on_rows):
    shape = (2 * length, length) if freq_on_rows else (length, 2 * length)
    tr = 256
    n_parts = 2 if freq_on_rows else 1
    spec = pl.BlockSpec((tr, shape[1]), lambda i: (i, 0))
    return pl.pallas_call(
        functools.partial(_trig_table_kernel, length=length, freq_on_rows=freq_on_rows,
                          scale=1.0 if freq_on_rows else 1.0 / length),
        out_shape=[jax.ShapeDtypeStruct(shape, BF16)] * n_parts,
        grid=(shape[0] // tr,), out_specs=[spec] * n_parts,
        compiler_params=_cparams(("arbitrary",)), name="dft_table",
    )()


def _hy_filter_kernel(tn_ref, band_ref, w1_ref, b1_ref, fr_ref, w2_ref, b2_ref, w3_ref, dec_ref,
                      fhi_ref, flo_ref, o_ref, *, length):
    t_norm = tn_ref[...]
    t_idx = lax.broadcasted_iota(jnp.int32, (length, LANE), 0).astype(F32)
    lane = lax.broadcasted_iota(jnp.int32, (length, LANE), 1)
    ang = (2.0 * math.pi / length) * t_idx * band_ref[...]
    z = jnp.where(lane == 0, t_norm,
                  jnp.where(lane <= HY_BANDS, jnp.cos(ang), jnp.where(lane <= 2 * HY_BANDS, jnp.sin(ang), 0.0)))
    hdn = jnp.sin(fr_ref[0:1] * (jnp.dot(z, w1_ref[...], precision=HIGHEST, preferred_element_type=F32)
                                 + b1_ref[...]))
    hdn = jnp.sin(fr_ref[1:2] * (jnp.dot(hdn, w2_ref[...], precision=HIGHEST, preferred_element_type=F32)
                                 + b2_ref[...]))
    filt = jnp.dot(hdn, w3_ref[...], precision=HIGHEST, preferred_element_type=F32)
    filt = filt * jnp.exp(-t_norm * jnp.abs(dec_ref[...]))
    first = lax.broadcasted_iota(jnp.int32, (length, HY_WIDTH), 0) == 0
    for o in range(HY_ORDER):
        fwd = filt[:, (2 * o) * HY_WIDTH:(2 * o + 1) * HY_WIDTH]
        bwd = filt[:, (2 * o + 1) * HY_WIDTH:(2 * o + 2) * HY_WIDTH]
        bwd = jnp.where(first, 0.0, pltpu.roll(bwd, 1, 0))
        o_ref[o, :length] = _dot3(fhi_ref[:length], flo_ref[:length], fwd + bwd)
        o_ref[o, length:] = _dot3(fhi_ref[length:], flo_ref[length:], bwd - fwd)


def _hy_filter_spectrum(length, f_hi, f_lo, w1, b1, freq, w2, b2, w3, decay):
    emb = 1 + 2 * HY_BANDS
    t_norm = jnp.linspace(0.0, 1.0, length, dtype=F32).reshape(length, 1)
    bands = np.zeros((1, LANE), np.float32)
    bands[0, 1:1 + HY_BANDS] = bands[0, 1 + HY_BANDS:emb] = np.linspace(1e-4, HY_BANDS - 1, HY_BANDS,
                                                                        dtype=np.float32)
    w1p = jnp.zeros((LANE, w1.shape[1]), F32).at[:emb].set(w1)
    return pl.pallas_call(
        functools.partial(_hy_filter_kernel, length=length),
        out_shape=jax.ShapeDtypeStruct((HY_ORDER, 2 * length, HY_WIDTH), F32),
        compiler_params=pltpu.CompilerParams(vmem_limit_bytes=VMEM_LIMIT),
        name="hyena_filter",
    )(t_norm, jnp.asarray(bands), w1p, b1.reshape(1, -1), freq, w2, b2.reshape(1, -1), w3,
      decay.reshape(1, -1), f_hi, f_lo)


def _hyena_kernel(v_ref, x1_ref, x2_ref, cwv_ref, cw1_ref, cw2_ref, cbv_ref, cb1_ref, cb2_ref, skip_ref,
                  hs_ref, f_ref, g_ref, *rest, length, parts):
    o_ref = rest[-1]
    n = range(len(parts))

    def short_conv(u_ref, cw_ref, cb_ref, rs, cs):
        u = u_ref[rs, cs]
        row = lax.broadcasted_iota(jnp.int32, u.shape, 0)
        prev = jnp.where(row == 0, 0.0, pltpu.roll(u, 1, 0))
        nxt = jnp.where(row == length - 1, 0.0, pltpu.roll(u, length - 1, 0))
        return prev * cw_ref[0:1, cs] + u * cw_ref[1:2, cs] + nxt * cw_ref[2:3, cs] + cb_ref[:, cs]

    z = [short_conv(v_ref, cwv_ref, cbv_ref, rs, cs) for rs, cs in parts]
    gates = [(short_conv(x1_ref, cw1_ref, cb1_ref, rs, cs), short_conv(x2_ref, cw2_ref, cb2_ref, rs, cs))
             for rs, cs in parts]
    for o in range(HY_ORDER):
        zf = [jnp.dot(f_ref[...], z[i].astype(BF16), preferred_element_type=F32) for i in n]
        p = []
        for i, (rs, cs) in enumerate(parts):
            zc, zs = zf[i][:length], zf[i][length:]
            h_re, h_im = hs_ref[o, :length, cs], hs_ref[o, length:, cs]
            p.append(jnp.concatenate([zc * h_re + zs * h_im, zc * h_im - zs * h_re], axis=0).astype(BF16))
        conv = [jnp.dot(g_ref[...], p[i], preferred_element_type=F32) for i in n]
        z = [gates[i][o] * (conv[i] + z[i] * skip_ref[o:o + 1, cs]) for i, (rs, cs) in enumerate(parts)]
    for i, (rs, cs) in enumerate(parts):
        o_ref[rs, cs] = z[i].astype(BF16)


def _hyena(proj, y_hy, layer, length, tables, spectrum, conv_w, conv_b, skip):
    seqs = 2 if length == SEQ else 1
    rows = seqs * length
    nblk = (T_P if length == SEQ else T_S) // rows
    row0 = 0 if length == SEQ else T_P // rows
    if seqs > 1:
        parts = [(slice(b * length, (b + 1) * length), slice(None)) for b in range(seqs)]
    else:
        parts = [(slice(None), slice(c * HY_CB, (c + 1) * HY_CB)) for c in range(HY_WIDTH // HY_CB)]
    col0 = C_HY // HY_WIDTH
    u_spec = lambda part: pl.BlockSpec((rows, HY_WIDTH), lambda s: (row0 + s, col0 + part))
    cw_spec = lambda part: pl.BlockSpec((1, 3, HY_WIDTH), lambda s: (layer, 0, part))
    cb_spec = lambda part: pl.BlockSpec((1, 1, HY_WIDTH), lambda s: (layer, 0, part))
    resident = functools.partial(pl.BlockSpec, pipeline_mode=pl.Buffered(1))
    tab_specs = [resident(t.shape, lambda s: (0, 0)) for t in tables]
    in_specs = ([u_spec(0), u_spec(1), u_spec(2), cw_spec(0), cw_spec(1), cw_spec(2),
                 cb_spec(0), cb_spec(1), cb_spec(2),
                 pl.BlockSpec((1, HY_ORDER, HY_WIDTH), lambda s: (layer, 0, 0)),
                 resident((HY_ORDER, 2 * length, HY_WIDTH), lambda s: (0, 0, 0))] + tab_specs)
    args = [proj] * 3 + [conv_w] * 3 + [conv_b.reshape(DEPTH, 1, -1)] * 3 + [skip, spectrum] + list(tables)
    aliases = {}
    if y_hy is not None:
        in_specs.append(pl.BlockSpec(memory_space=pl.ANY))
        args.append(y_hy)
        aliases = {len(args) - 1: 0}

    def body(v_ref, x1_ref, x2_ref, cwv, cw1, cw2, cbv, cb1, cb2, skip_ref, hs_ref, *rest):
        _hyena_kernel(v_ref, x1_ref, x2_ref, cwv.at[0], cw1.at[0], cw2.at[0], cbv.at[0], cb1.at[0], cb2.at[0],
                      skip_ref.at[0], hs_ref, *rest, length=length, parts=parts)

    return pl.pallas_call(
        body,
        out_shape=jax.ShapeDtypeStruct((T, HY_WIDTH), BF16),
        grid=(nblk,), in_specs=in_specs,
        out_specs=pl.BlockSpec((rows, HY_WIDTH), lambda s: (row0 + s, 0)),
        input_output_aliases=aliases,
        compiler_params=_cparams(("arbitrary",)),
        name="hyena",
    )(*args)


GC = 128
GDN_SCALE = GDN_DK ** -0.5
GDN_SEQS_PER_STEP = 2


def _state_slot(b, d):
    return (2 * b + d) * GDN_HEADS


def _merge_masks(lower):
    ri = lax.broadcasted_iota(jnp.int32, (GC, GC), 0)
    ci = lax.broadcasted_iota(jnp.int32, (GC, GC), 1)
    hi, lo = (ri, ci) if lower else (ci, ri)
    return [((hi >> (b + 1)) == (lo >> (b + 1))) & ((hi >> b) > (lo >> b)) for b in range(int(math.log2(GC)))]


def _unit_tri_inverse(a, masks):
    ri = lax.broadcasted_iota(jnp.int32, (GC, GC), 0)
    ci = lax.broadcasted_iota(jnp.int32, (GC, GC), 1)
    eye = (ri == ci).astype(F32)
    n = range(len(a))
    t = [eye - jnp.where(masks[i][0], a[i], 0.0) for i in n]
    for level in range(1, len(masks[0])):
        tb = [t[i].astype(BF16) for i in n]
        ta = [jnp.dot(tb[i], jnp.where(masks[i][level], a[i], 0.0).astype(BF16), preferred_element_type=F32)
              for i in n]
        tat = [jnp.dot(ta[i].astype(BF16), tb[i], preferred_element_type=F32) for i in n]
        t = [t[i] - tat[i] for i in n]
    return t


def _gdn_kernel(q_ref, k_ref, v_ref, z_ref, ba_ref, cwq_ref, cwk_ref, cwv_ref, alog_ref, dt_ref, ng_ref,
                sf0_ref, sb0_ref, *rest, length, nb):
    y_ref, sf_ref, sb_ref, qn_ref, kn_ref, vn_ref, beta_ref, g_ref, osum_ref, s_ref = rest[-10:]
    n_chunks = length // GC
    rows_total = nb * length
    row = lax.broadcasted_iota(jnp.int32, (rows_total, GDN_WIDTH), 0) & (length - 1)
    first, last = row == 0, row == length - 1

    def conv_silu(u_ref, cw_ref):
        u = u_ref[...]
        prev = jnp.where(first, 0.0, pltpu.roll(u, 1, 0))
        nxt = jnp.where(last, 0.0, pltpu.roll(u, rows_total - 1, 0))
        c = prev * cw_ref[0:1] + u * cw_ref[1:2] + nxt * cw_ref[2:3]
        return c * jax.nn.sigmoid(c)

    q = conv_silu(q_ref, cwq_ref)
    k = conv_silu(k_ref, cwk_ref)
    vn_ref[...] = conv_silu(v_ref, cwv_ref)
    for h in range(GDN_HEADS):
        sl = slice(h * GDN_DK, (h + 1) * GDN_DK)
        qh, kh = q[:, sl], k[:, sl]
        qn_ref[:, sl] = qh * lax.rsqrt(jnp.sum(qh * qh, axis=-1, keepdims=True) + NORM_EPS) * GDN_SCALE
        kn_ref[:, sl] = kh * lax.rsqrt(jnp.sum(kh * kh, axis=-1, keepdims=True) + NORM_EPS)
    ba = ba_ref[...]
    beta_ref[...] = jax.nn.sigmoid(ba)
    g_ref[...] = -jnp.exp(alog_ref[...]) * jax.nn.softplus(ba + dt_ref[...])

    ri = lax.broadcasted_iota(jnp.int32, (GC, GC), 0)
    ci = lax.broadcasted_iota(jnp.int32, (GC, GC), 1)
    lower = (ri >= ci).astype(F32)
    upper = (ri <= ci).astype(F32)

    def chunk_step(n, carry):
        scans = [(b, d) for b in range(nb) for d in range(2)]
        rows = {(b, d): pl.ds(pl.multiple_of(b * length + (n if d == 0 else n_chunks - 1 - n) * GC, GC), GC)
                for b, d in scans}
        loaded = {}
        for bd in scans:
            loaded[bd] = (g_ref[rows[bd], :], beta_ref[rows[bd], :])
            for h in range(GDN_HEADS):
                sl = slice(h * GDN_DK, (h + 1) * GDN_DK)
                loaded[bd + (h,)] = (qn_ref[rows[bd], sl], kn_ref[rows[bd], sl], vn_ref[rows[bd], sl],
                                     s_ref[_state_slot(*bd) + h], osum_ref[rows[bd], sl])
        units = [(b, d, h) for b, d in scans for h in range(GDN_HEADS)]
        incl = [(ri >= ci), (ri <= ci)]
        strict = [(ri > ci), (ri < ci)]
        masks = [_merge_masks(True), _merge_masks(False)]
        gcs = {(b, d): jnp.dot((lower, upper)[d], loaded[b, d][0], precision=HIGHEST, preferred_element_type=F32)
               for b, d in scans}
        gcs_t = {bd: g.T for bd, g in gcs.items()}
        edge = [GC - 1, 0]
        g_col = [gcs[b, d][:, 8 + 4 * d + h:9 + 4 * d + h] for b, d, h in units]
        g_row = [gcs_t[b, d][8 + 4 * d + h:9 + 4 * d + h, :] for b, d, h in units]
        g_end = [gcs_t[b, d][8 + 4 * d + h:9 + 4 * d + h, edge[d]:edge[d] + 1] for b, d, h in units]
        b_col = [loaded[b, d][1][:, 4 * d + h:4 * d + h + 1] for b, d, h in units]
        qc, kc, vc, st, o_other = (list(x) for x in zip(*(loaded[u] for u in units)))
        nu = range(len(units))
        decay = [jnp.where(incl[d], jnp.exp(jnp.where(incl[d], g_col[i] - g_row[i], 0.0)), 0.0)
                 for i, (b, d, h) in enumerate(units)]
        kb = [kc[i] * b_col[i] for i in nu]
        kcb = [kc[i].astype(BF16) for i in nu]
        kk = [_nt_dot(kb[i].astype(BF16), kcb[i]) for i in nu]
        qk = [_nt_dot(qc[i].astype(BF16), kcb[i]) for i in nu]
        a = [jnp.where(strict[d], kk[i] * decay[i], 0.0) for i, (b, d, h) in enumerate(units)]
        t = _unit_tri_inverse(a, [masks[d] for b, d, h in units])
        e_col = [jnp.exp(g_col[i]) for i in nu]
        rhs = [jnp.concatenate([vc[i] * b_col[i], kb[i] * e_col[i]], axis=1).astype(BF16) for i in nu]
        sol = [jnp.dot(t[i].astype(BF16), rhs[i], preferred_element_type=F32) for i in nu]
        attn = [jnp.where(incl[d], qk[i] * decay[i], 0.0).astype(BF16) for i, (b, d, h) in enumerate(units)]
        sb = [st[i].astype(BF16) for i in nu]
        ws = [jnp.dot(sol[i][:, GDN_DV:].astype(BF16), sb[i], preferred_element_type=F32) for i in nu]
        qs = [jnp.dot((qc[i] * e_col[i]).astype(BF16), sb[i], preferred_element_type=F32) for i in nu]
        v_new = [(sol[i][:, :GDN_DV] - ws[i]).astype(BF16) for i in nu]
        av = [jnp.dot(attn[i], v_new[i], preferred_element_type=F32) for i in nu]
        k_dec_t = [(kc[i] * jnp.exp(g_end[i] - g_col[i])).T.astype(BF16) for i in nu]
        kv = [jnp.dot(k_dec_t[i], v_new[i], preferred_element_type=F32) for i in nu]
        results = [(o_other[i] + qs[i] + av[i], st[i] * jnp.exp(g_end[i]) + kv[i]) for i in nu]
        for (b, d, h), (o, s) in zip(units, results):
            osum_ref[rows[b, d], h * GDN_DV:(h + 1) * GDN_DV] = o
            s_ref[_state_slot(b, d) + h] = s
        return carry

    osum_ref[...] = jnp.zeros(osum_ref.shape, F32)
    for b in range(nb):
        s_ref[pl.ds(_state_slot(b, 0), GDN_HEADS)] = sf0_ref[b]
        s_ref[pl.ds(_state_slot(b, 1), GDN_HEADS)] = sb0_ref[b]
    lax.fori_loop(0, n_chunks, chunk_step, 0)
    for b in range(nb):
        sf_ref[b, 0] = s_ref[pl.ds(_state_slot(b, 0), GDN_HEADS)]
        sb_ref[b, 0] = s_ref[pl.ds(_state_slot(b, 1), GDN_HEADS)]
    for h in range(GDN_HEADS):
        sl = slice(h * GDN_DV, (h + 1) * GDN_DV)
        o = osum_ref[:, sl]
        o = o * lax.rsqrt(jnp.mean(o * o, axis=-1, keepdims=True) + NORM_EPS) * ng_ref[...]
        zh = z_ref[:, sl]
        y_ref[:, sl] = (o * (zh * jax.nn.sigmoid(zh))).astype(BF16)


def _gdn(proj, y_gdn, states_out, layer, length, s_f0, s_b0, conv_w, a_log, dt_bias, norm_g):
    nb = GDN_SEQS_PER_STEP
    rows = nb * length
    nblk = (T_P if length == SEQ else T_S) // rows
    row0 = 0 if length == SEQ else T_P // rows
    col = lambda c: c // GDN_WIDTH
    buffering = {} if length == SEQ else {"pipeline_mode": pl.Buffered(1)}
    blk = lambda c: pl.BlockSpec((rows, GDN_WIDTH), lambda s: (row0 + s, col(c)), **buffering)
    cw = lambda part: pl.BlockSpec((1, 3, GDN_WIDTH), lambda s: (layer, 0, part))
    vec = lambda: pl.BlockSpec((1, 1, LANE), lambda s: (layer, 0, 0))
    st_in = pl.BlockSpec((nb, GDN_HEADS, GDN_DK, GDN_DV), lambda s: (s, 0, 0, 0))
    st_out = pl.BlockSpec((nb, 1, GDN_HEADS, GDN_DK, GDN_DV), lambda s: (s, layer, 0, 0, 0))
    pad_lanes = lambda x, off: jnp.zeros((DEPTH, 1, LANE), F32).at[:, 0, off:off + 2 * GDN_HEADS].set(
        x.reshape(DEPTH, -1))
    in_specs = [blk(C_GDN), blk(C_GDN + GDN_WIDTH), blk(C_GDN + 2 * GDN_WIDTH), blk(C_Z),
                pl.BlockSpec((rows, LANE), lambda s: (row0 + s, C_BA // LANE)),
                cw(0), cw(1), cw(2), vec(), vec(), vec(), st_in, st_in]
    args = [proj] * 5 + [conv_w] * 3 + [pad_lanes(a_log, 2 * GDN_HEADS), pad_lanes(dt_bias, 2 * GDN_HEADS),
                                        norm_g.reshape(DEPTH, 1, GDN_DV), s_f0, s_b0]
    st_shape = jax.ShapeDtypeStruct((nblk * nb, DEPTH, GDN_HEADS, GDN_DK, GDN_DV), F32)
    aliases = {}
    for carried, out_idx in ((y_gdn, 0),) + (((states_out[0], 1), (states_out[1], 2)) if states_out else ()):
        if carried is not None:
            in_specs.append(pl.BlockSpec(memory_space=pl.ANY))
            args.append(carried)
            aliases[len(args) - 1] = out_idx

    def body(q_ref, k_ref, v_ref, z_ref, ba_ref, cwq, cwk, cwv, alog, dt, ng, sf0, sb0, *rest):
        _gdn_kernel(q_ref, k_ref, v_ref, z_ref, ba_ref, cwq.at[0], cwk.at[0], cwv.at[0], alog.at[0], dt.at[0],
                    ng.at[0], sf0, sb0, *rest, length=length, nb=nb)

    seq_buf = lambda w: pltpu.VMEM((rows, w), F32)
    return pl.pallas_call(
        body,
        out_shape=[jax.ShapeDtypeStruct((T, GDN_WIDTH), BF16), st_shape, st_shape],
        grid=(nblk,), in_specs=in_specs,
        out_specs=[pl.BlockSpec((rows, GDN_WIDTH), lambda s: (row0 + s, 0)), st_out, st_out],
        scratch_shapes=[seq_buf(GDN_WIDTH), seq_buf(GDN_WIDTH), seq_buf(GDN_WIDTH), seq_buf(LANE), seq_buf(LANE),
                        seq_buf(GDN_WIDTH),
                        pltpu.VMEM((nb * 2 * GDN_HEADS, GDN_DK, GDN_DV), F32)],
        input_output_aliases=aliases,
        compiler_params=_cparams(("arbitrary",)),
        name="gated_deltanet",
    )(*args)


def kernel(x_prompt, x_sample, cache_k, cache_v, state_fwd, state_bwd, c, c_ctx, ln1_g, ln2_g, w_mod, b_mod,
           w_in, na_rpb, hy_conv_w, hy_conv_b, hy_w1, hy_b1, hy_freq, hy_w2, hy_b2, hy_w3, hy_decay, hy_skip,
           gdn_conv_w, gdn_a_log, gdn_dt_bias, gdn_norm_g, w_pa, w_pb, w_pc, b_gate, w_out, ffn_w_up,
           ffn_conv_w, ffn_conv_b, ffn_w_down, final_g):
    x = (x_prompt.reshape(T_P, D), x_sample.reshape(T_S, D))
    cond = jnp.concatenate([c_ctx[None], c, jnp.zeros((N_COND - 1 - DEC_BATCH, D), F32)], axis=0)
    mod = _mod_table(cond, w_mod, b_mod).reshape(DEPTH * N_COND, 1, 6 * D)

    w_pa_b, w_pb_b, w_pc_b, w_out_b = (w.astype(BF16) for w in (w_pa, w_pb, w_pc, w_out))
    w_down_b = ffn_w_down.astype(BF16)
    w_in_t = jnp.swapaxes(w_in, 1, 2)
    tables = {n: tuple(_trig_table(n, True)) + tuple(_trig_table(n, False)) for n in (SEQ, DEC_SEQ)}
    zero_state = jnp.zeros((BATCH, GDN_HEADS, GDN_DK, GDN_DV), F32)

    caches = None
    states = None
    h = _norm_mod(x, ln1_g, mod, 0, 0)
    for layer in range(DEPTH):
        proj = _in_proj(h, w_in_t, layer)

        y_na, new_k, new_v = _ctx_attention(proj, layer, caches)
        caches = (new_k, new_v)
        y_na = _na_attention(proj, cache_k, cache_v, na_rpb[layer], y_na, layer)

        y_hy = None
        for n in (SEQ, DEC_SEQ):
            spectrum = _hy_filter_spectrum(n, tables[n][0], tables[n][1], hy_w1[layer], hy_b1[layer],
                                           hy_freq[layer], hy_w2[layer], hy_b2[layer], hy_w3[layer],
                                           hy_decay[layer])
            y_hy = _hyena(proj, y_hy, layer, n, (tables[n][0], tables[n][2]), spectrum,
                          hy_conv_w, hy_conv_b, hy_skip)

        y_gdn, s_f, s_b = _gdn(proj, None, states, layer, SEQ, zero_state, zero_state,
                               gdn_conv_w, gdn_a_log, gdn_dt_bias, gdn_norm_g)
        states = (s_f, s_b)
        y_gdn = _gdn(proj, y_gdn, None, layer, DEC_SEQ, state_fwd[:, layer], state_bwd[:, layer],
                     gdn_conv_w, gdn_a_log, gdn_dt_bias, gdn_norm_g)[0]

        x, h = _mix_out(y_na, y_hy, y_gdn, proj, b_gate, w_pa_b, w_pb_b, w_pc_b, w_out_b, x, mod, ln2_g, layer)
        act = _ffn_up(h, ffn_w_up, ffn_conv_w, ffn_conv_b, layer)
        x, h = _ffn_down(act, w_down_b, x, mod, layer, ln1_g, final_g)

    y_p, y_s = x, h
    cache_shape = (BATCH, DEPTH, SEQ, NA_HEADS, NA_DH)
    return (y_p.reshape(BATCH, SEQ, D), y_s.reshape(DEC_BATCH, DEC_SEQ, D),
            caches[0].reshape(cache_shape), caches[1].reshape(cache_shape), states[0], states[1])
```

```python
import functools
import math

import jax
import jax.numpy as jnp
import numpy as np
from jax import lax
from jax.experimental import pallas as pl
from jax.experimental.pallas import tpu as pltpu

F32 = jnp.float32
BF16 = jnp.bfloat16

D = 2048
BATCH, SEQ = 32, 256
DEC_BATCH, DEC_SEQ = 4, 1024
DEPTH = 2
GRID_W = 64
NA_HEADS, NA_DH = 8, 128
NA_WIDTH = NA_HEADS * NA_DH
NA_WIN_ROWS, NA_WIN_COLS = 8, 16
HY_WIDTH = 512
HY_ORDER = 2
HY_BANDS = 16
GDN_HEADS, GDN_DK, GDN_DV = 4, 128, 128
GDN_WIDTH = GDN_HEADS * GDN_DV
D_FF = 5632
N_BRANCH = 3
NORM_EPS = 1e-6
NEG_INF = -1e30

T_P = BATCH * SEQ
T_S = DEC_BATCH * DEC_SEQ
T = T_P + T_S
N_COND = 8

C_Q, C_K, C_V = 0, NA_WIDTH, 2 * NA_WIDTH
C_HY = 3 * NA_WIDTH
C_GDN = C_HY + 3 * HY_WIDTH
C_Z = C_GDN + 3 * GDN_HEADS * GDN_DK
C_BA = C_Z + GDN_WIDTH
C_GATE = C_BA + 4 * GDN_HEADS
N_IN = C_GATE + N_BRANCH * D
GATE_BLOCK = C_BA
GATE_SHIFT = C_GATE - C_BA

LANE = 128
VMEM_LIMIT = 56 * 1024 * 1024
HIGHEST = lax.Precision.HIGHEST


def _cparams(sem):
    return pltpu.CompilerParams(dimension_semantics=sem, vmem_limit_bytes=VMEM_LIMIT)


def _cond_row(tile, rows_per_tile):
    first_latent = T_P // rows_per_tile
    per_seq = DEC_SEQ // rows_per_tile
    return jnp.where(tile < first_latent, 0, 1 + (tile - first_latent) // per_seq)


def _mod_kernel(c_ref, w_ref, b_ref, o_ref):
    c = c_ref[...]
    s = (c * jax.nn.sigmoid(c)).astype(BF16)
    o_ref[0] = jnp.dot(s, w_ref[0].astype(BF16), preferred_element_type=F32) + b_ref[0]


def _mod_table(cond, w_mod, b_mod):
    tn = 1024
    n = 6 * D
    return pl.pallas_call(
        _mod_kernel,
        out_shape=jax.ShapeDtypeStruct((DEPTH, N_COND, n), F32),
        grid=(DEPTH, n // tn),
        in_specs=[
            pl.BlockSpec((N_COND, D), lambda l, j: (0, 0)),
            pl.BlockSpec((1, D, tn), lambda l, j: (l, 0, j)),
            pl.BlockSpec((1, 1, tn), lambda l, j: (l, 0, j)),
        ],
        out_specs=pl.BlockSpec((1, N_COND, tn), lambda l, j: (l, 0, j)),
        compiler_params=_cparams(("arbitrary", "arbitrary")),
        name="mod_table",
    )(cond, w_mod, b_mod.reshape(DEPTH, 1, n))


def _token_specs(x, tm):
    if not isinstance(x, tuple):
        return [pl.BlockSpec((tm, D), lambda i: (i, 0))], [x]
    n_ctx = T_P // tm
    return ([pl.BlockSpec((tm, D), lambda i: (jnp.minimum(i, n_ctx - 1), 0)),
             pl.BlockSpec((tm, D), lambda i: (jnp.maximum(i - n_ctx, 0), 0))], list(x))


def _token_tile(x_refs, tm):
    if len(x_refs) == 1:
        return x_refs[0][...]
    return jnp.where(pl.program_id(0) < T_P // tm, x_refs[0][...], x_refs[1][...])


def _modulated_norm(x, g, shift, scale):
    y = x * lax.rsqrt(jnp.mean(x * x, axis=-1, keepdims=True) + NORM_EPS) * g
    return (y * (1.0 + scale) + shift).astype(BF16)


NORM_TM = 1024


def _norm_mod_kernel(*refs):
    g_ref, sh_ref, sc_ref, o_ref = refs[-4:]
    o_ref[...] = _modulated_norm(_token_tile(refs[:-4], NORM_TM), g_ref[0], sh_ref[0], sc_ref[0])


def _norm_mod(x, g, mod, layer, shift_chunk):
    tm = NORM_TM
    x_specs, x_args = _token_specs(x, tm)
    return pl.pallas_call(
        _norm_mod_kernel,
        out_shape=jax.ShapeDtypeStruct((T, D), BF16),
        grid=(T // tm,),
        in_specs=x_specs + [
            pl.BlockSpec((1, 1, D), lambda i: (layer, 0, 0)),
            pl.BlockSpec((1, 1, D), lambda i: (layer * N_COND + _cond_row(i, tm), 0, shift_chunk)),
            pl.BlockSpec((1, 1, D), lambda i: (layer * N_COND + _cond_row(i, tm), 0, shift_chunk + 1)),
        ],
        out_specs=pl.BlockSpec((tm, D), lambda i: (i, 0)),
        compiler_params=_cparams(("arbitrary",)),
        name="norm_mod",
    )(*x_args, g.reshape(DEPTH, 1, D), mod, mod)


def _proj_kernel(h_ref, wt_ref, o_ref, wb_ref):
    @pl.when(pl.program_id(1) == 0)
    def _():
        wb_ref[...] = wt_ref[0].astype(BF16)

    o_ref[...] = lax.dot_general(h_ref[...], wb_ref[...], (((1,), (1,)), ((), ())), preferred_element_type=F32)


def _in_proj(h, w_in_t, layer):
    tm, tn = 1536, 1024
    return pl.pallas_call(
        _proj_kernel,
        out_shape=jax.ShapeDtypeStruct((T, N_IN), F32),
        grid=(pl.cdiv(N_IN, tn), T // tm),
        in_specs=[
            pl.BlockSpec((tm, D), lambda j, i: (i, 0)),
            pl.BlockSpec((1, tn, D), lambda j, i: (layer, j, 0)),
        ],
        out_specs=pl.BlockSpec((tm, tn), lambda j, i: (i, j)),
        scratch_shapes=[pltpu.VMEM((tn, D), BF16)],
        compiler_params=_cparams(("arbitrary", "arbitrary")),
        name="in_proj",
    )(h, w_in_t)


MIX_TM = 256


def _mix_out_kernel(yna_ref, yhy_ref, ygdn_ref, gl_ref, bg_ref, wpa_ref, wpb_ref, wpc_ref, wout_ref, *refs):
    gt_ref, g2_ref, sh2_ref, sc2_ref, o_ref, h2_ref = refs[-6:]
    width = gl_ref.shape[1]
    gl = pltpu.roll(gl_ref[...], width - GATE_SHIFT, 1)[:, :N_BRANCH * D] + bg_ref[0]
    gates = jax.nn.sigmoid(gl)
    merged = (gates[:, :D] * jnp.dot(yna_ref[...], wpa_ref[0], preferred_element_type=F32)
              + gates[:, D:2 * D] * jnp.dot(yhy_ref[...], wpb_ref[0], preferred_element_type=F32)
              + gates[:, 2 * D:] * jnp.dot(ygdn_ref[...], wpc_ref[0], preferred_element_type=F32))
    r = jnp.dot(merged.astype(BF16), wout_ref[0], preferred_element_type=F32)
    x = _token_tile(refs[:-6], MIX_TM) + gt_ref[0] * r
    o_ref[...] = x
    h2_ref[...] = _modulated_norm(x, g2_ref[0], sh2_ref[0], sc2_ref[0])


def _mix_out(y_na, y_hy, y_gdn, proj, b_gate, w_pa, w_pb, w_pc, w_out, x, mod, ln2_g, layer):
    tm = MIX_TM
    resident = functools.partial(pl.BlockSpec, pipeline_mode=pl.Buffered(1))
    x_specs, x_args = _token_specs(x, tm)
    mod_spec = lambda chunk: pl.BlockSpec((1, 1, D), lambda i: (layer * N_COND + _cond_row(i, tm), 0, chunk))
    row_spec = pl.BlockSpec((tm, D), lambda i: (i, 0))
    return pl.pallas_call(
        _mix_out_kernel,
        out_shape=[jax.ShapeDtypeStruct((T, D), F32), jax.ShapeDtypeStruct((T, D), BF16)],
        grid=(T // tm,),
        in_specs=[
            pl.BlockSpec((tm, NA_WIDTH), lambda i: (i, 0)),
            pl.BlockSpec((tm, HY_WIDTH), lambda i: (i, 0)),
            pl.BlockSpec((tm, GDN_WIDTH), lambda i: (i, 0)),
            pl.BlockSpec((tm, GATE_BLOCK), lambda i: (i, 1)),
            pl.BlockSpec((1, 1, N_BRANCH * D), lambda i: (layer, 0, 0)),
            resident((1, NA_WIDTH, D), lambda i: (layer, 0, 0)),
            resident((1, HY_WIDTH, D), lambda i: (layer, 0, 0)),
            resident((1, GDN_WIDTH, D), lambda i: (layer, 0, 0)),
            resident((1, D, D), lambda i: (layer, 0, 0)),
        ] + x_specs + [mod_spec(2), pl.BlockSpec((1, 1, D), lambda i: (layer, 0, 0)), mod_spec(3), mod_spec(4)],
        out_specs=[row_spec, row_spec],
        compiler_params=_cparams(("arbitrary",)),
        name="mix_out",
    )(y_na, y_hy, y_gdn, proj, b_gate.reshape(DEPTH, 1, N_BRANCH * D), w_pa, w_pb, w_pc, w_out, *x_args,
      mod, ln2_g.reshape(DEPTH, 1, D), mod, mod)


FFN_TM = 1024


def _ffn_up_kernel(h_ref, wa_ref, wb_ref, cwa_ref, cwb_ref, cba_ref, cbb_ref, o_ref, wab_ref, wbb_ref):
    m = pl.program_id(1)

    @pl.when(m == 0)
    def _():
        wab_ref[...] = wa_ref[0].astype(BF16)
        wbb_ref[...] = wb_ref[0].astype(BF16)

    seq_len = jnp.where(m < T_P // FFN_TM, SEQ, DEC_SEQ)
    pos = lax.broadcasted_iota(jnp.int32, (FFN_TM, 1), 0) & (seq_len - 1)
    has_prev = pos != 0
    has_next = pos != seq_len - 1
    h = h_ref[...]

    def conv(w_ref, cw_ref, cb_ref):
        up = jnp.dot(h, w_ref[...], preferred_element_type=F32)
        prev = jnp.where(has_prev, pltpu.roll(up, 1, 0), 0.0)
        nxt = jnp.where(has_next, pltpu.roll(up, FFN_TM - 1, 0), 0.0)
        cw = cw_ref[0]
        return prev * cw[0:1] + up * cw[1:2] + nxt * cw[2:3] + cb_ref[0]

    ua = conv(wab_ref, cwa_ref, cba_ref)
    ub = conv(wbb_ref, cwb_ref, cbb_ref)
    o_ref[...] = (ua * jax.nn.sigmoid(ua) * ub).astype(BF16)


def _ffn_up(h, w_up, conv_w, conv_b, layer):
    tn = 512
    nt = D_FF // tn
    conv_b = conv_b.reshape(DEPTH, 1, 2 * D_FF)
    return pl.pallas_call(
        _ffn_up_kernel,
        out_shape=jax.ShapeDtypeStruct((T, D_FF), BF16),
        grid=(nt, T // FFN_TM),
        in_specs=[
            pl.BlockSpec((FFN_TM, D), lambda j, i: (i, 0)),
            pl.BlockSpec((1, D, tn), lambda j, i: (layer, 0, j)),
            pl.BlockSpec((1, D, tn), lambda j, i: (layer, 0, nt + j)),
            pl.BlockSpec((1, 3, tn), lambda j, i: (layer, 0, j)),
            pl.BlockSpec((1, 3, tn), lambda j, i: (layer, 0, nt + j)),
            pl.BlockSpec((1, 1, tn), lambda j, i: (layer, 0, j)),
            pl.BlockSpec((1, 1, tn), lambda j, i: (layer, 0, nt + j)),
        ],
        out_specs=pl.BlockSpec((FFN_TM, tn), lambda j, i: (i, j)),
        scratch_shapes=[pltpu.VMEM((D, tn), BF16), pltpu.VMEM((D, tn), BF16)],
        compiler_params=_cparams(("arbitrary", "arbitrary")),
        name="ffn_up",
    )(h, w_up, w_up, conv_w, conv_w, conv_b, conv_b)


DOWN_TM = 256


def _ffn_down_kernel(a_ref, w_ref, x_ref, gt_ref, g_ref, sh_ref, sc_ref, o_ref, h_ref):
    x = x_ref[...] + gt_ref[0] * jnp.dot(a_ref[...], w_ref[0], preferred_element_type=F32)
    o_ref[...] = x
    h_ref[...] = _modulated_norm(x, g_ref[0], sh_ref[0], sc_ref[0])


def _ffn_down_final_kernel(a_ref, w_ref, x_ref, gt_ref, g_ref, yp_ref, ys_ref):
    x = x_ref[...] + gt_ref[0] * jnp.dot(a_ref[...], w_ref[0], preferred_element_type=F32)
    y = x * lax.rsqrt(jnp.mean(x * x, axis=-1, keepdims=True) + NORM_EPS) * g_ref[...]
    is_context = pl.program_id(0) < T_P // DOWN_TM

    @pl.when(is_context)
    def _():
        yp_ref[...] = y

    @pl.when(jnp.logical_not(is_context))
    def _():
        ys_ref[...] = y


def _ffn_down(act, w_down, x, mod, layer, ln1_g, final_g):
    tm = DOWN_TM
    last = layer == DEPTH - 1
    mod_spec = lambda lyr, chunk: pl.BlockSpec((1, 1, D), lambda i: (lyr * N_COND + _cond_row(i, tm), 0, chunk))
    row_spec = pl.BlockSpec((tm, D), lambda i: (i, 0))
    in_specs = [
        pl.BlockSpec((tm, D_FF), lambda i: (i, 0)),
        pl.BlockSpec((1, D_FF, D), lambda i: (layer, 0, 0), pipeline_mode=pl.Buffered(1)),
        row_spec,
        mod_spec(layer, 5),
    ]
    if last:
        n_ctx = T_P // tm
        return pl.pallas_call(
            _ffn_down_final_kernel,
            out_shape=[jax.ShapeDtypeStruct((T_P, D), F32), jax.ShapeDtypeStruct((T_S, D), F32)],
            grid=(T // tm,),
            in_specs=in_specs + [pl.BlockSpec((1, D), lambda i: (0, 0))],
            out_specs=[pl.BlockSpec((tm, D), lambda i: (jnp.minimum(i, n_ctx - 1), 0)),
                       pl.BlockSpec((tm, D), lambda i: (jnp.maximum(i - n_ctx, 0), 0))],
            compiler_params=_cparams(("arbitrary",)),
            name="ffn_down_final",
        )(act, w_down, x, mod, final_g.reshape(1, D))
    return pl.pallas_call(
        _ffn_down_kernel,
        out_shape=[jax.ShapeDtypeStruct((T, D), F32), jax.ShapeDtypeStruct((T, D), BF16)],
        grid=(T // tm,),
        in_specs=in_specs + [pl.BlockSpec((1, 1, D), lambda i: (layer + 1, 0, 0)),
                             mod_spec(layer + 1, 0), mod_spec(layer + 1, 1)],
        out_specs=[row_spec, row_spec],
        compiler_params=_cparams(("arbitrary",)),
        name="ffn_down",
    )(act, w_down, x, mod, ln1_g.reshape(DEPTH, 1, D), mod, mod)


ATT_SCALE = NA_DH ** -0.5


def _nt_dot(a, b):
    return lax.dot_general(a, b, (((1,), (1,)), ((), ())), preferred_element_type=F32)


def _head_slices():
    return [slice(h * NA_DH, (h + 1) * NA_DH) for h in range(NA_HEADS)]


def _ctx_attn_kernel(q_ref, k_ref, v_ref, o_ref, ko_ref, vo_ref):
    heads = _head_slices()
    q = [(q_ref[:, sl] * ATT_SCALE).astype(BF16) for sl in heads]
    s = [_nt_dot(q[h], k_ref[:, sl].astype(BF16)) for h, sl in enumerate(heads)]
    e = [x - jnp.max(x, axis=-1, keepdims=True) for x in s]
    e = [jnp.exp(x) for x in e]
    o = [jnp.dot(e[h].astype(BF16), v_ref[:, sl].astype(BF16), preferred_element_type=F32)
         for h, sl in enumerate(heads)]
    for h, sl in enumerate(heads):
        o_ref[:, sl] = (o[h] / jnp.sum(e[h], axis=-1, keepdims=True)).astype(BF16)
    ko_ref[0, 0] = k_ref[...]
    vo_ref[0, 0] = v_ref[...]


def _ctx_attention(proj, layer, caches):
    cache_shape = jax.ShapeDtypeStruct((BATCH, DEPTH, SEQ, NA_WIDTH), F32)
    cache_spec = pl.BlockSpec((1, 1, SEQ, NA_WIDTH), lambda b: (b, layer, 0, 0))
    qkv_specs = [pl.BlockSpec((SEQ, NA_WIDTH), lambda b, j=j: (b, j)) for j in range(3)]
    out_shape = [jax.ShapeDtypeStruct((T, NA_WIDTH), BF16), cache_shape, cache_shape]
    out_specs = [pl.BlockSpec((SEQ, NA_WIDTH), lambda b: (b, 0)), cache_spec, cache_spec]
    if caches is None:
        return pl.pallas_call(
            _ctx_attn_kernel, out_shape=out_shape, grid=(BATCH,), in_specs=qkv_specs, out_specs=out_specs,
            compiler_params=_cparams(("arbitrary",)), name="ctx_attention",
        )(proj, proj, proj)

    def body(q_ref, k_ref, v_ref, kc_ref, vc_ref, o_ref, ko_ref, vo_ref):
        del kc_ref, vc_ref
        _ctx_attn_kernel(q_ref, k_ref, v_ref, o_ref, ko_ref, vo_ref)

    any_spec = pl.BlockSpec(memory_space=pl.ANY)
    return pl.pallas_call(
        body, out_shape=out_shape, grid=(BATCH,), in_specs=qkv_specs + [any_spec, any_spec],
        out_specs=out_specs, input_output_aliases={3: 1, 4: 2},
        compiler_params=_cparams(("arbitrary",)), name="ctx_attention",
    )(proj, proj, proj, *caches)


NA_ROWS = DEC_SEQ // GRID_W
NA_WIN_TOK = NA_WIN_ROWS * GRID_W
NA_ROWS_PER_STEP = 2


def _na_window_start(r):
    return jnp.clip(r - NA_WIN_ROWS // 2, 0, NA_ROWS - NA_WIN_ROWS)


def _na_bias_table(rpb):
    col = np.arange(GRID_W)
    col_start = np.clip(col - NA_WIN_COLS // 2, 0, GRID_W - NA_WIN_COLS)
    col_mask = (col[None, :] >= col_start[:, None]) & (col[None, :] < col_start[:, None] + NA_WIN_COLS)
    rel_col = np.clip(col[None, :] - col[:, None] + NA_WIN_COLS - 1, 0, 2 * NA_WIN_COLS - 2)
    n_rel = 2 * NA_WIN_COLS - 1
    onehot = (rel_col[None] == np.arange(n_rel)[:, None, None]).astype(np.float32).reshape(n_rel, -1)
    expanded = jnp.dot(rpb.reshape(-1, n_rel), onehot, precision=HIGHEST)
    banded = jnp.where(col_mask, expanded.reshape(NA_HEADS, 2 * NA_WIN_ROWS - 1, GRID_W, GRID_W), NEG_INF)
    return jnp.concatenate([banded[:, :-1], banded[:, 1:]], axis=-1)


def _na_attn_kernel(q_ref, k_ref, v_ref, ck_ref, cv_ref, bias_ref, y_in_ref, o_ref):
    del y_in_ref
    heads = _head_slices()
    units = [(a, h) for a in range(NA_ROWS_PER_STEP) for h in range(NA_HEADS)]
    nu = range(len(units))
    r = [pl.program_id(1) * NA_ROWS_PER_STEP + a for a in range(NA_ROWS_PER_STEP)]
    win = [pl.ds(pl.multiple_of(_na_window_start(ra) * GRID_W, GRID_W), NA_WIN_TOK) for ra in r]
    rel0 = [NA_WIN_ROWS - 1 - (ra - _na_window_start(ra)) for ra in r]
    qrows = [slice(a * GRID_W, (a + 1) * GRID_W) for a in range(NA_ROWS_PER_STEP)]
    q = [(q_ref[:, sl] * ATT_SCALE).astype(BF16) for sl in heads]
    bias = [jnp.concatenate([bias_ref[h, rel0[a] + 2 * p] for p in range(NA_WIN_ROWS // 2)], axis=1)
            for a, h in units]
    s_win = [_nt_dot(q[h][qrows[a]], k_ref[win[a], heads[h]].astype(BF16)) + bias[i] for i, (a, h) in enumerate(units)]
    s_ctx_all = [_nt_dot(q[h], ck_ref[0, 0, :, sl].astype(BF16)) for h, sl in enumerate(heads)]
    s_ctx = [s_ctx_all[h][qrows[a]] for a, h in units]
    m = [jnp.maximum(jnp.max(s_win[i], axis=-1, keepdims=True), jnp.max(s_ctx[i], axis=-1, keepdims=True))
         for i in nu]
    e_win = [jnp.exp(s_win[i] - m[i]) for i in nu]
    e_ctx = [jnp.exp(s_ctx[i] - m[i]) for i in nu]
    o_win = [jnp.dot(e_win[i].astype(BF16), v_ref[win[a], heads[h]].astype(BF16), preferred_element_type=F32)
             for i, (a, h) in enumerate(units)]
    e_ctx_all = [jnp.concatenate([e_ctx[a * NA_HEADS + h] for a in range(NA_ROWS_PER_STEP)], axis=0).astype(BF16)
                 for h in range(NA_HEADS)]
    o_ctx_all = [jnp.dot(e_ctx_all[h], cv_ref[0, 0, :, sl].astype(BF16), preferred_element_type=F32)
                 for h, sl in enumerate(heads)]
    for i, (a, h) in enumerate(units):
        denom = jnp.sum(e_win[i], axis=-1, keepdims=True) + jnp.sum(e_ctx[i], axis=-1, keepdims=True)
        o_ref[qrows[a], heads[h]] = ((o_win[i] + o_ctx_all[h][qrows[a]]) / denom).astype(BF16)


def _na_attention(proj, cache_k, cache_v, rpb, y_na, layer):
    q_rows = NA_ROWS_PER_STEP * GRID_W
    steps = NA_ROWS // NA_ROWS_PER_STEP
    q_row0 = T_P // q_rows
    seq0 = T_P // DEC_SEQ
    kv_spec = lambda j: pl.BlockSpec((DEC_SEQ, NA_WIDTH), lambda b, r: (seq0 + b, j))
    ctx_spec = pl.BlockSpec((1, 1, SEQ, NA_WIDTH), lambda b, r: (b, layer, 0, 0))
    return pl.pallas_call(
        _na_attn_kernel,
        out_shape=jax.ShapeDtypeStruct((T, NA_WIDTH), BF16),
        grid=(DEC_BATCH, steps),
        in_specs=[
            pl.BlockSpec((q_rows, NA_WIDTH), lambda b, r: (q_row0 + b * steps + r, 0)),
            kv_spec(1), kv_spec(2), ctx_spec, ctx_spec,
            pl.BlockSpec((NA_HEADS, 2 * NA_WIN_ROWS - 2, GRID_W, 2 * GRID_W), lambda b, r: (0, 0, 0, 0),
                         pipeline_mode=pl.Buffered(1)),
            pl.BlockSpec(memory_space=pl.ANY),
        ],
        out_specs=pl.BlockSpec((q_rows, NA_WIDTH), lambda b, r: (q_row0 + b * steps + r, 0)),
        input_output_aliases={6: 0},
        compiler_params=_cparams(("arbitrary", "arbitrary")),
        name="na_attention",
    )(proj, proj, proj, cache_k.reshape(DEC_BATCH, DEPTH, SEQ, NA_WIDTH),
      cache_v.reshape(DEC_BATCH, DEPTH, SEQ, NA_WIDTH), _na_bias_table(rpb), y_na)


HY_CB = 256


def _split_bf16(x):
    hi = x.astype(BF16)
    return hi, (x - hi.astype(F32)).astype(BF16)


def _dot3(a_hi, a_lo, b):
    b_hi, b_lo = _split_bf16(b)
    return (jnp.dot(a_hi, b_hi, preferred_element_type=F32) + jnp.dot(a_lo, b_hi, preferred_element_type=F32)
            + jnp.dot(a_hi, b_lo, preferred_element_type=F32))


def _trig_table_kernel(*out_refs, length, freq_on_rows, scale):
    rows, cols = out_refs[0].shape
    r = lax.broadcasted_iota(jnp.int32, (rows, cols), 0) + pl.program_id(0) * rows
    c = lax.broadcasted_iota(jnp.int32, (rows, cols), 1)
    f, t = (r, c) if freq_on_rows else (c, r)
    k = f & (length - 1)
    quarter = jnp.where(f >= length, length, 0)
    phase = (2 * k + 1) * t + (-quarter if freq_on_rows else quarter)
    phase = phase & (4 * length - 1)
    phase = jnp.where(phase >= 2 * length, phase - 4 * length, phase)
    val = jnp.cos(phase.astype(F32) * (math.pi / (2 * length))) * scale
    for ref, part in zip(out_refs, _split_bf16(val)):
        ref[...] = part


def _trig_table(length, freq_on_rows):
    shape = (2 * length, length) if freq_on_rows else (length, 2 * length)
    tr = 256
    n_parts = 2 if freq_on_rows else 1
    spec = pl.BlockSpec((tr, shape[1]), lambda i: (i, 0))
    return pl.pallas_call(
        functools.partial(_trig_table_kernel, length=length, freq_on_rows=freq_on_rows,
                          scale=1.0 if freq_on_rows else 1.0 / length),
        out_shape=[jax.ShapeDtypeStruct(shape, BF16)] * n_parts,
        grid=(shape[0] // tr,), out_specs=[spec] * n_parts,
        compiler_params=_cparams(("arbitrary",)), name="dft_table",
    )()


def _hy_filter_kernel(tn_ref, band_ref, w1_ref, b1_ref, fr_ref, w2_ref, b2_ref, w3_ref, dec_ref,
                      fhi_ref, flo_ref, o_ref, *, length):
    t_norm = tn_ref[...]
    t_idx = lax.broadcasted_iota(jnp.int32, (length, LANE), 0).astype(F32)
    lane = lax.broadcasted_iota(jnp.int32, (length, LANE), 1)
    ang = (2.0 * math.pi / length) * t_idx * band_ref[...]
    z = jnp.where(lane == 0, t_norm,
                  jnp.where(lane <= HY_BANDS, jnp.cos(ang), jnp.where(lane <= 2 * HY_BANDS, jnp.sin(ang), 0.0)))
    hdn = jnp.sin(fr_ref[0:1] * (jnp.dot(z, w1_ref[...], precision=HIGHEST, preferred_element_type=F32)
                                 + b1_ref[...]))
    hdn = jnp.sin(fr_ref[1:2] * (jnp.dot(hdn, w2_ref[...], precision=HIGHEST, preferred_element_type=F32)
                                 + b2_ref[...]))
    filt = jnp.dot(hdn, w3_ref[...], precision=HIGHEST, preferred_element_type=F32)
    filt = filt * jnp.exp(-t_norm * jnp.abs(dec_ref[...]))
    first = lax.broadcasted_iota(jnp.int32, (length, HY_WIDTH), 0) == 0
    for o in range(HY_ORDER):
        fwd = filt[:, (2 * o) * HY_WIDTH:(2 * o + 1) * HY_WIDTH]
        bwd = filt[:, (2 * o + 1) * HY_WIDTH:(2 * o + 2) * HY_WIDTH]
        bwd = jnp.where(first, 0.0, pltpu.roll(bwd, 1, 0))
        o_ref[o, :length] = _dot3(fhi_ref[:length], flo_ref[:length], fwd + bwd)
        o_ref[o, length:] = _dot3(fhi_ref[length:], flo_ref[length:], bwd - fwd)


def _hy_filter_spectrum(length, f_hi, f_lo, w1, b1, freq, w2, b2, w3, decay):
    emb = 1 + 2 * HY_BANDS
    t_norm = jnp.linspace(0.0, 1.0, length, dtype=F32).reshape(length, 1)
    bands = np.zeros((1, LANE), np.float32)
    bands[0, 1:1 + HY_BANDS] = bands[0, 1 + HY_BANDS:emb] = np.linspace(1e-4, HY_BANDS - 1, HY_BANDS,
                                                                        dtype=np.float32)
    w1p = jnp.zeros((LANE, w1.shape[1]), F32).at[:emb].set(w1)
    return pl.pallas_call(
        functools.partial(_hy_filter_kernel, length=length),
        out_shape=jax.ShapeDtypeStruct((HY_ORDER, 2 * length, HY_WIDTH), F32),
        compiler_params=pltpu.CompilerParams(vmem_limit_bytes=VMEM_LIMIT),
        name="hyena_filter",
    )(t_norm, jnp.asarray(bands), w1p, b1.reshape(1, -1), freq, w2, b2.reshape(1, -1), w3,
      decay.reshape(1, -1), f_hi, f_lo)


def _hyena_kernel(v_ref, x1_ref, x2_ref, cwv_ref, cw1_ref, cw2_ref, cbv_ref, cb1_ref, cb2_ref, skip_ref,
                  hs_ref, f_ref, g_ref, *rest, length, parts):
    o_ref = rest[-1]
    n = range(len(parts))

    def short_conv(u_ref, cw_ref, cb_ref, rs, cs):
        u = u_ref[rs, cs]
        row = lax.broadcasted_iota(jnp.int32, u.shape, 0)
        prev = jnp.where(row == 0, 0.0, pltpu.roll(u, 1, 0))
        nxt = jnp.where(row == length - 1, 0.0, pltpu.roll(u, length - 1, 0))
        return prev * cw_ref[0:1, cs] + u * cw_ref[1:2, cs] + nxt * cw_ref[2:3, cs] + cb_ref[:, cs]

    z = [short_conv(v_ref, cwv_ref, cbv_ref, rs, cs) for rs, cs in parts]
    gates = [(short_conv(x1_ref, cw1_ref, cb1_ref, rs, cs), short_conv(x2_ref, cw2_ref, cb2_ref, rs, cs))
             for rs, cs in parts]
    for o in range(HY_ORDER):
        zf = [jnp.dot(f_ref[...], z[i].astype(BF16), preferred_element_type=F32) for i in n]
        p = []
        for i, (rs, cs) in enumerate(parts):
            zc, zs = zf[i][:length], zf[i][length:]
            h_re, h_im = hs_ref[o, :length, cs], hs_ref[o, length:, cs]
            p.append(jnp.concatenate([zc * h_re + zs * h_im, zc * h_im - zs * h_re], axis=0).astype(BF16))
        conv = [jnp.dot(g_ref[...], p[i], preferred_element_type=F32) for i in n]
        z = [gates[i][o] * (conv[i] + z[i] * skip_ref[o:o + 1, cs]) for i, (rs, cs) in enumerate(parts)]
    for i, (rs, cs) in enumerate(parts):
        o_ref[rs, cs] = z[i].astype(BF16)


def _hyena(proj, y_hy, layer, length, tables, spectrum, conv_w, conv_b, skip):
    seqs = 2 if length == SEQ else 1
    rows = seqs * length
    nblk = (T_P if length == SEQ else T_S) // rows
    row0 = 0 if length == SEQ else T_P // rows
    if seqs > 1:
        parts = [(slice(b * length, (b + 1) * length), slice(None)) for b in range(seqs)]
    else:
        parts = [(slice(None), slice(c * HY_CB, (c + 1) * HY_CB)) for c in range(HY_WIDTH // HY_CB)]
    col0 = C_HY // HY_WIDTH
    u_spec = lambda part: pl.BlockSpec((rows, HY_WIDTH), lambda s: (row0 + s, col0 + part))
    cw_spec = lambda part: pl.BlockSpec((1, 3, HY_WIDTH), lambda s: (layer, 0, part))
    cb_spec = lambda part: pl.BlockSpec((1, 1, HY_WIDTH), lambda s: (layer, 0, part))
    resident = functools.partial(pl.BlockSpec, pipeline_mode=pl.Buffered(1))
    tab_specs = [resident(t.shape, lambda s: (0, 0)) for t in tables]
    in_specs = ([u_spec(0), u_spec(1), u_spec(2), cw_spec(0), cw_spec(1), cw_spec(2),
                 cb_spec(0), cb_spec(1), cb_spec(2),
                 pl.BlockSpec((1, HY_ORDER, HY_WIDTH), lambda s: (layer, 0, 0)),
                 resident((HY_ORDER, 2 * length, HY_WIDTH), lambda s: (0, 0, 0))] + tab_specs)
    args = [proj] * 3 + [conv_w] * 3 + [conv_b.reshape(DEPTH, 1, -1)] * 3 + [skip, spectrum] + list(tables)
    aliases = {}
    if y_hy is not None:
        in_specs.append(pl.BlockSpec(memory_space=pl.ANY))
        args.append(y_hy)
        aliases = {len(args) - 1: 0}

    def body(v_ref, x1_ref, x2_ref, cwv, cw1, cw2, cbv, cb1, cb2, skip_ref, hs_ref, *rest):
        _hyena_kernel(v_ref, x1_ref, x2_ref, cwv.at[0], cw1.at[0], cw2.at[0], cbv.at[0], cb1.at[0], cb2.at[0],
                      skip_ref.at[0], hs_ref, *rest, length=length, parts=parts)

    return pl.pallas_call(
        body,
        out_shape=jax.ShapeDtypeStruct((T, HY_WIDTH), BF16),
        grid=(nblk,), in_specs=in_specs,
        out_specs=pl.BlockSpec((rows, HY_WIDTH), lambda s: (row0 + s, 0)),
        input_output_aliases=aliases,
        compiler_params=_cparams(("arbitrary",)),
        name="hyena",
    )(*args)


GC = 128
GDN_SCALE = GDN_DK ** -0.5
GDN_CTX_PER_STEP = 2


def _state_slot(b, d):
    return (2 * b + d) * GDN_HEADS


def _merge_masks(lower):
    ri = lax.broadcasted_iota(jnp.int32, (GC, GC), 0)
    ci = lax.broadcasted_iota(jnp.int32, (GC, GC), 1)
    hi, lo = (ri, ci) if lower else (ci, ri)
    return [((hi >> (b + 1)) == (lo >> (b + 1))) & ((hi >> b) > (lo >> b)) for b in range(int(math.log2(GC)))]


def _unit_tri_inverse(a, masks):
    ri = lax.broadcasted_iota(jnp.int32, (GC, GC), 0)
    ci = lax.broadcasted_iota(jnp.int32, (GC, GC), 1)
    eye = (ri == ci).astype(F32)
    n = range(len(a))
    t = [eye - jnp.where(masks[i][0], a[i], 0.0) for i in n]
    for level in range(1, len(masks[0])):
        tb = [t[i].astype(BF16) for i in n]
        ta = [jnp.dot(tb[i], jnp.where(masks[i][level], a[i], 0.0).astype(BF16), preferred_element_type=F32)
              for i in n]
        tat = [jnp.dot(ta[i].astype(BF16), tb[i], preferred_element_type=F32) for i in n]
        t = [t[i] - tat[i] for i in n]
    return t


def _gdn_kernel(q_ref, k_ref, v_ref, z_ref, ba_ref, cwq_ref, cwk_ref, cwv_ref, alog_ref, dt_ref, ng_ref,
                sf0_ref, sb0_ref, *rest, length, nb):
    y_ref, sf_ref, sb_ref, qn_ref, kn_ref, vn_ref, beta_ref, g_ref, of_ref, ob_ref, s_ref = rest[-11:]
    n_chunks = length // GC
    rows_total = nb * length
    row = lax.broadcasted_iota(jnp.int32, (rows_total, GDN_WIDTH), 0) & (length - 1)
    first, last = row == 0, row == length - 1

    def conv_silu(u_ref, cw_ref):
        u = u_ref[...]
        prev = jnp.where(first, 0.0, pltpu.roll(u, 1, 0))
        nxt = jnp.where(last, 0.0, pltpu.roll(u, rows_total - 1, 0))
        c = prev * cw_ref[0:1] + u * cw_ref[1:2] + nxt * cw_ref[2:3]
        return c * jax.nn.sigmoid(c)

    q = conv_silu(q_ref, cwq_ref)
    k = conv_silu(k_ref, cwk_ref)
    vn_ref[...] = conv_silu(v_ref, cwv_ref)
    for h in range(GDN_HEADS):
        sl = slice(h * GDN_DK, (h + 1) * GDN_DK)
        qh, kh = q[:, sl], k[:, sl]
        qn_ref[:, sl] = qh * lax.rsqrt(jnp.sum(qh * qh, axis=-1, keepdims=True) + NORM_EPS) * GDN_SCALE
        kn_ref[:, sl] = kh * lax.rsqrt(jnp.sum(kh * kh, axis=-1, keepdims=True) + NORM_EPS)
    ba = ba_ref[...]
    beta_ref[...] = jax.nn.sigmoid(ba)
    g_ref[...] = -jnp.exp(alog_ref[...]) * jax.nn.softplus(ba + dt_ref[...])

    ri = lax.broadcasted_iota(jnp.int32, (GC, GC), 0)
    ci = lax.broadcasted_iota(jnp.int32, (GC, GC), 1)
    lower = (ri >= ci).astype(F32)
    upper = (ri <= ci).astype(F32)

    def chunk_step(n, carry):
        scans = [(b, d) for b in range(nb) for d in range(2)]
        rows = {(b, d): pl.ds(pl.multiple_of(b * length + (n if d == 0 else n_chunks - 1 - n) * GC, GC), GC)
                for b, d in scans}
        loaded = {}
        for bd in scans:
            loaded[bd] = (g_ref[rows[bd], :], beta_ref[rows[bd], :])
            for h in range(GDN_HEADS):
                sl = slice(h * GDN_DK, (h + 1) * GDN_DK)
                loaded[bd + (h,)] = (qn_ref[rows[bd], sl], kn_ref[rows[bd], sl], vn_ref[rows[bd], sl],
                                     s_ref[_state_slot(*bd) + h])
        units = [(b, d, h) for b, d in scans for h in range(GDN_HEADS)]
        incl = [(ri >= ci), (ri <= ci)]
        strict = [(ri > ci), (ri < ci)]
        masks = [_merge_masks(True), _merge_masks(False)]
        gcs = {(b, d): jnp.dot((lower, upper)[d], loaded[b, d][0], precision=HIGHEST, preferred_element_type=F32)
               for b, d in scans}
        gcs_t = {bd: g.T for bd, g in gcs.items()}
        edge = [GC - 1, 0]
        g_col = [gcs[b, d][:, 8 + 4 * d + h:9 + 4 * d + h] for b, d, h in units]
        g_row = [gcs_t[b, d][8 + 4 * d + h:9 + 4 * d + h, :] for b, d, h in units]
        g_end = [gcs_t[b, d][8 + 4 * d + h:9 + 4 * d + h, edge[d]:edge[d] + 1] for b, d, h in units]
        b_col = [loaded[b, d][1][:, 4 * d + h:4 * d + h + 1] for b, d, h in units]
        qc, kc, vc, st = (list(x) for x in zip(*(loaded[u] for u in units)))
        nu = range(len(units))
        decay = [jnp.where(incl[d], jnp.exp(jnp.where(incl[d], g_col[i] - g_row[i], 0.0)), 0.0)
                 for i, (b, d, h) in enumerate(units)]
        kb = [kc[i] * b_col[i] for i in nu]
        kcb = [kc[i].astype(BF16) for i in nu]
        kk = [_nt_dot(kb[i].astype(BF16), kcb[i]) for i in nu]
        qk = [_nt_dot(qc[i].astype(BF16), kcb[i]) for i in nu]
        a = [jnp.where(strict[d], kk[i] * decay[i], 0.0) for i, (b, d, h) in enumerate(units)]
        t = _unit_tri_inverse(a, [masks[d] for b, d, h in units])
        e_col = [jnp.exp(g_col[i]) for i in nu]
        rhs = [jnp.concatenate([vc[i] * b_col[i], kb[i] * e_col[i]], axis=1).astype(BF16) for i in nu]
        sol = [jnp.dot(t[i].astype(BF16), rhs[i], preferred_element_type=F32) for i in nu]
        attn = [jnp.where(incl[d], qk[i] * decay[i], 0.0).astype(BF16) for i, (b, d, h) in enumerate(units)]
        sb = [st[i].astype(BF16) for i in nu]
        ws = [jnp.dot(sol[i][:, GDN_DV:].astype(BF16), sb[i], preferred_element_type=F32) for i in nu]
        qs = [jnp.dot((qc[i] * e_col[i]).astype(BF16), sb[i], preferred_element_type=F32) for i in nu]
        v_new = [(sol[i][:, :GDN_DV] - ws[i]).astype(BF16) for i in nu]
        av = [jnp.dot(attn[i], v_new[i], preferred_element_type=F32) for i in nu]
        k_dec_t = [(kc[i] * jnp.exp(g_end[i] - g_col[i])).T.astype(BF16) for i in nu]
        kv = [jnp.dot(k_dec_t[i], v_new[i], preferred_element_type=F32) for i in nu]
        results = [(qs[i] + av[i], st[i] * jnp.exp(g_end[i]) + kv[i]) for i in nu]
        for (b, d, h), (o, s) in zip(units, results):
            (of_ref if d == 0 else ob_ref)[rows[b, d], h * GDN_DV:(h + 1) * GDN_DV] = o
            s_ref[_state_slot(b, d) + h] = s
        return carry

    for b in range(nb):
        s_ref[pl.ds(_state_slot(b, 0), GDN_HEADS)] = sf0_ref[b]
        s_ref[pl.ds(_state_slot(b, 1), GDN_HEADS)] = sb0_ref[b]
    lax.fori_loop(0, n_chunks, chunk_step, 0)
    for b in range(nb):
        sf_ref[b, 0] = s_ref[pl.ds(_state_slot(b, 0), GDN_HEADS)]
        sb_ref[b, 0] = s_ref[pl.ds(_state_slot(b, 1), GDN_HEADS)]
    for h in range(GDN_HEADS):
        sl = slice(h * GDN_DV, (h + 1) * GDN_DV)
        o = of_ref[:, sl] + ob_ref[:, sl]
        o = o * lax.rsqrt(jnp.mean(o * o, axis=-1, keepdims=True) + NORM_EPS) * ng_ref[...]
        zh = z_ref[:, sl]
        y_ref[:, sl] = (o * (zh * jax.nn.sigmoid(zh))).astype(BF16)


def _gdn(proj, y_gdn, states_out, layer, length, s_f0, s_b0, conv_w, a_log, dt_bias, norm_g):
    nb = GDN_CTX_PER_STEP if length == SEQ else 1
    rows = nb * length
    nblk = (T_P if length == SEQ else T_S) // rows
    row0 = 0 if length == SEQ else T_P // rows
    col = lambda c: c // GDN_WIDTH
    blk = lambda c: pl.BlockSpec((rows, GDN_WIDTH), lambda s: (row0 + s, col(c)))
    cw = lambda part: pl.BlockSpec((1, 3, GDN_WIDTH), lambda s: (layer, 0, part))
    vec = lambda: pl.BlockSpec((1, 1, LANE), lambda s: (layer, 0, 0))
    st_in = pl.BlockSpec((nb, GDN_HEADS, GDN_DK, GDN_DV), lambda s: (s, 0, 0, 0))
    st_out = pl.BlockSpec((nb, 1, GDN_HEADS, GDN_DK, GDN_DV), lambda s: (s, layer, 0, 0, 0))
    pad_lanes = lambda x, off: jnp.zeros((DEPTH, 1, LANE), F32).at[:, 0, off:off + 2 * GDN_HEADS].set(
        x.reshape(DEPTH, -1))
    in_specs = [blk(C_GDN), blk(C_GDN + GDN_WIDTH), blk(C_GDN + 2 * GDN_WIDTH), blk(C_Z),
                pl.BlockSpec((rows, LANE), lambda s: (row0 + s, C_BA // LANE)),
                cw(0), cw(1), cw(2), vec(), vec(), vec(), st_in, st_in]
    args = [proj] * 5 + [conv_w] * 3 + [pad_lanes(a_log, 2 * GDN_HEADS), pad_lanes(dt_bias, 2 * GDN_HEADS),
                                        norm_g.reshape(DEPTH, 1, GDN_DV), s_f0, s_b0]
    st_shape = jax.ShapeDtypeStruct((nblk * nb, DEPTH, GDN_HEADS, GDN_DK, GDN_DV), F32)
    aliases = {}
    for carried, out_idx in ((y_gdn, 0),) + (((states_out[0], 1), (states_out[1], 2)) if states_out else ()):
        if carried is not None:
            in_specs.append(pl.BlockSpec(memory_space=pl.ANY))
            args.append(carried)
            aliases[len(args) - 1] = out_idx

    def body(q_ref, k_ref, v_ref, z_ref, ba_ref, cwq, cwk, cwv, alog, dt, ng, sf0, sb0, *rest):
        _gdn_kernel(q_ref, k_ref, v_ref, z_ref, ba_ref, cwq.at[0], cwk.at[0], cwv.at[0], alog.at[0], dt.at[0],
                    ng.at[0], sf0, sb0, *rest, length=length, nb=nb)

    seq_buf = lambda w: pltpu.VMEM((rows, w), F32)
    return pl.pallas_call(
        body,
        out_shape=[jax.ShapeDtypeStruct((T, GDN_WIDTH), BF16), st_shape, st_shape],
        grid=(nblk,), in_specs=in_specs,
        out_specs=[pl.BlockSpec((rows, GDN_WIDTH), lambda s: (row0 + s, 0)), st_out, st_out],
        scratch_shapes=[seq_buf(GDN_WIDTH), seq_buf(GDN_WIDTH), seq_buf(GDN_WIDTH), seq_buf(LANE), seq_buf(LANE),
                        seq_buf(GDN_WIDTH), seq_buf(GDN_WIDTH),
                        pltpu.VMEM((nb * 2 * GDN_HEADS, GDN_DK, GDN_DV), F32)],
        input_output_aliases=aliases,
        compiler_params=_cparams(("arbitrary",)),
        name="gated_deltanet",
    )(*args)


def kernel(x_prompt, x_sample, cache_k, cache_v, state_fwd, state_bwd, c, c_ctx, ln1_g, ln2_g, w_mod, b_mod,
           w_in, na_rpb, hy_conv_w, hy_conv_b, hy_w1, hy_b1, hy_freq, hy_w2, hy_b2, hy_w3, hy_decay, hy_skip,
           gdn_conv_w, gdn_a_log, gdn_dt_bias, gdn_norm_g, w_pa, w_pb, w_pc, b_gate, w_out, ffn_w_up,
           ffn_conv_w, ffn_conv_b, ffn_w_down, final_g):
    x = (x_prompt.reshape(T_P, D), x_sample.reshape(T_S, D))
    cond = jnp.concatenate([c_ctx[None], c, jnp.zeros((N_COND - 1 - DEC_BATCH, D), F32)], axis=0)
    mod = _mod_table(cond, w_mod, b_mod).reshape(DEPTH * N_COND, 1, 6 * D)

    w_pa_b, w_pb_b, w_pc_b, w_out_b = (w.astype(BF16) for w in (w_pa, w_pb, w_pc, w_out))
    w_down_b = ffn_w_down.astype(BF16)
    w_in_t = jnp.swapaxes(w_in, 1, 2)
    tables = {n: tuple(_trig_table(n, True)) + tuple(_trig_table(n, False)) for n in (SEQ, DEC_SEQ)}
    zero_state = jnp.zeros((BATCH, GDN_HEADS, GDN_DK, GDN_DV), F32)

    caches = None
    states = None
    h = _norm_mod(x, ln1_g, mod, 0, 0)
    for layer in range(DEPTH):
        proj = _in_proj(h, w_in_t, layer)

        y_na, new_k, new_v = _ctx_attention(proj, layer, caches)
        caches = (new_k, new_v)
        y_na = _na_attention(proj, cache_k, cache_v, na_rpb[layer], y_na, layer)

        y_hy = None
        for n in (SEQ, DEC_SEQ):
            spectrum = _hy_filter_spectrum(n, tables[n][0], tables[n][1], hy_w1[layer], hy_b1[layer],
                                           hy_freq[layer], hy_w2[layer], hy_b2[layer], hy_w3[layer],
                                           hy_decay[layer])
            y_hy = _hyena(proj, y_hy, layer, n, (tables[n][0], tables[n][2]), spectrum,
                          hy_conv_w, hy_conv_b, hy_skip)

        y_gdn, s_f, s_b = _gdn(proj, None, states, layer, SEQ, zero_state, zero_state,
                               gdn_conv_w, gdn_a_log, gdn_dt_bias, gdn_norm_g)
        states = (s_f, s_b)
        y_gdn = _gdn(proj, y_gdn, None, layer, DEC_SEQ, state_fwd[:, layer], state_bwd[:, layer],
                     gdn_conv_w, gdn_a_log, gdn_dt_bias, gdn_norm_g)[0]

        x, h = _mix_out(y_na, y_hy, y_gdn, proj, b_gate, w_pa_b, w_pb_b, w_pc_b, w_out_b, x, mod, ln2_g, layer)
        act = _ffn_up(h, ffn_w_up, ffn_conv_w, ffn_conv_b, layer)
        x, h = _ffn_down(act, w_down_b, x, mod, layer, ln1_g, final_g)

    y_p, y_s = x, h
    cache_shape = (BATCH, DEPTH, SEQ, NA_HEADS, NA_DH)
    return (y_p.reshape(BATCH, SEQ, D), y_s.reshape(DEC_BATCH, DEC_SEQ, D),
            caches[0].reshape(cache_shape), caches[1].reshape(cache_shape), states[0], states[1])
```

```python
import functools
import math

import jax
import jax.numpy as jnp
import numpy as np
from jax import lax
from jax.experimental import pallas as pl
from jax.experimental.pallas import tpu as pltpu

F32 = jnp.float32
BF16 = jnp.bfloat16

D = 2048
BATCH, SEQ = 32, 256
DEC_BATCH, DEC_SEQ = 4, 1024
DEPTH = 2
GRID_W = 64
NA_HEADS, NA_DH = 8, 128
NA_WIDTH = NA_HEADS * NA_DH
NA_WIN_ROWS, NA_WIN_COLS = 8, 16
HY_WIDTH = 512
HY_ORDER = 2
HY_BANDS = 16
GDN_HEADS, GDN_DK, GDN_DV = 4, 128, 128
GDN_WIDTH = GDN_HEADS * GDN_DV
D_FF = 5632
N_BRANCH = 3
NORM_EPS = 1e-6
NEG_INF = -1e30

T_P = BATCH * SEQ
T_S = DEC_BATCH * DEC_SEQ
T = T_P + T_S
N_COND = 8

C_Q, C_K, C_V = 0, NA_WIDTH, 2 * NA_WIDTH
C_HY = 3 * NA_WIDTH
C_GDN = C_HY + 3 * HY_WIDTH
C_Z = C_GDN + 3 * GDN_HEADS * GDN_DK
C_BA = C_Z + GDN_WIDTH
C_GATE = C_BA + 4 * GDN_HEADS
N_IN = C_GATE + N_BRANCH * D
GATE_BLOCK = C_BA
GATE_SHIFT = C_GATE - C_BA

LANE = 128
VMEM_LIMIT = 56 * 1024 * 1024
HIGHEST = lax.Precision.HIGHEST


def _cparams(sem):
    return pltpu.CompilerParams(dimension_semantics=sem, vmem_limit_bytes=VMEM_LIMIT)


def _cond_row(tile, rows_per_tile):
    first_latent = T_P // rows_per_tile
    per_seq = DEC_SEQ // rows_per_tile
    return jnp.where(tile < first_latent, 0, 1 + (tile - first_latent) // per_seq)


def _mod_kernel(c_ref, w_ref, b_ref, o_ref):
    c = c_ref[...]
    s = (c * jax.nn.sigmoid(c)).astype(BF16)
    o_ref[0] = jnp.dot(s, w_ref[0].astype(BF16), preferred_element_type=F32) + b_ref[0]


def _mod_table(cond, w_mod, b_mod):
    tn = 1024
    n = 6 * D
    return pl.pallas_call(
        _mod_kernel,
        out_shape=jax.ShapeDtypeStruct((DEPTH, N_COND, n), F32),
        grid=(DEPTH, n // tn),
        in_specs=[
            pl.BlockSpec((N_COND, D), lambda l, j: (0, 0)),
            pl.BlockSpec((1, D, tn), lambda l, j: (l, 0, j)),
            pl.BlockSpec((1, 1, tn), lambda l, j: (l, 0, j)),
        ],
        out_specs=pl.BlockSpec((1, N_COND, tn), lambda l, j: (l, 0, j)),
        compiler_params=_cparams(("arbitrary", "arbitrary")),
        name="mod_table",
    )(cond, w_mod, b_mod.reshape(DEPTH, 1, n))


def _token_specs(x, tm):
    if not isinstance(x, tuple):
        return [pl.BlockSpec((tm, D), lambda i: (i, 0))], [x]
    n_ctx = T_P // tm
    return ([pl.BlockSpec((tm, D), lambda i: (jnp.minimum(i, n_ctx - 1), 0)),
             pl.BlockSpec((tm, D), lambda i: (jnp.maximum(i - n_ctx, 0), 0))], list(x))


def _token_tile(x_refs, tm):
    if len(x_refs) == 1:
        return x_refs[0][...]
    return jnp.where(pl.program_id(0) < T_P // tm, x_refs[0][...], x_refs[1][...])


def _modulated_norm(x, g, shift, scale):
    y = x * lax.rsqrt(jnp.mean(x * x, axis=-1, keepdims=True) + NORM_EPS) * g
    return (y * (1.0 + scale) + shift).astype(BF16)


NORM_TM = 1024


def _norm_mod_kernel(*refs):
    g_ref, sh_ref, sc_ref, o_ref = refs[-4:]
    o_ref[...] = _modulated_norm(_token_tile(refs[:-4], NORM_TM), g_ref[0], sh_ref[0], sc_ref[0])


def _norm_mod(x, g, mod, layer, shift_chunk):
    tm = NORM_TM
    x_specs, x_args = _token_specs(x, tm)
    return pl.pallas_call(
        _norm_mod_kernel,
        out_shape=jax.ShapeDtypeStruct((T, D), BF16),
        grid=(T // tm,),
        in_specs=x_specs + [
            pl.BlockSpec((1, 1, D), lambda i: (layer, 0, 0)),
            pl.BlockSpec((1, 1, D), lambda i: (layer * N_COND + _cond_row(i, tm), 0, shift_chunk)),
            pl.BlockSpec((1, 1, D), lambda i: (layer * N_COND + _cond_row(i, tm), 0, shift_chunk + 1)),
        ],
        out_specs=pl.BlockSpec((tm, D), lambda i: (i, 0)),
        compiler_params=_cparams(("arbitrary",)),
        name="norm_mod",
    )(*x_args, g.reshape(DEPTH, 1, D), mod, mod)


def _proj_kernel(h_ref, wt_ref, o_ref, wb_ref):
    @pl.when(pl.program_id(1) == 0)
    def _():
        wb_ref[...] = wt_ref[0].astype(BF16)

    o_ref[...] = lax.dot_general(h_ref[...], wb_ref[...], (((1,), (1,)), ((), ())), preferred_element_type=F32)


def _in_proj(h, w_in_t, layer):
    tm, tn = 1536, 1024
    return pl.pallas_call(
        _proj_kernel,
        out_shape=jax.ShapeDtypeStruct((T, N_IN), F32),
        grid=(pl.cdiv(N_IN, tn), T // tm),
        in_specs=[
            pl.BlockSpec((tm, D), lambda j, i: (i, 0)),
            pl.BlockSpec((1, tn, D), lambda j, i: (layer, j, 0)),
        ],
        out_specs=pl.BlockSpec((tm, tn), lambda j, i: (i, j)),
        scratch_shapes=[pltpu.VMEM((tn, D), BF16)],
        compiler_params=_cparams(("arbitrary", "arbitrary")),
        name="in_proj",
    )(h, w_in_t)


MIX_TM = 256


def _mix_out_kernel(yna_ref, yhy_ref, ygdn_ref, gl_ref, bg_ref, wpa_ref, wpb_ref, wpc_ref, wout_ref, *refs):
    gt_ref, g2_ref, sh2_ref, sc2_ref, o_ref, h2_ref = refs[-6:]
    width = gl_ref.shape[1]
    gl = pltpu.roll(gl_ref[...], width - GATE_SHIFT, 1)[:, :N_BRANCH * D] + bg_ref[0]
    gates = jax.nn.sigmoid(gl)
    merged = (gates[:, :D] * jnp.dot(yna_ref[...], wpa_ref[0], preferred_element_type=F32)
              + gates[:, D:2 * D] * jnp.dot(yhy_ref[...], wpb_ref[0], preferred_element_type=F32)
              + gates[:, 2 * D:] * jnp.dot(ygdn_ref[...], wpc_ref[0], preferred_element_type=F32))
    r = jnp.dot(merged.astype(BF16), wout_ref[0], preferred_element_type=F32)
    x = _token_tile(refs[:-6], MIX_TM) + gt_ref[0] * r
    o_ref[...] = x
    h2_ref[...] = _modulated_norm(x, g2_ref[0], sh2_ref[0], sc2_ref[0])


def _mix_out(y_na, y_hy, y_gdn, proj, b_gate, w_pa, w_pb, w_pc, w_out, x, mod, ln2_g, layer):
    tm = MIX_TM
    resident = functools.partial(pl.BlockSpec, pipeline_mode=pl.Buffered(1))
    x_specs, x_args = _token_specs(x, tm)
    mod_spec = lambda chunk: pl.BlockSpec((1, 1, D), lambda i: (layer * N_COND + _cond_row(i, tm), 0, chunk))
    row_spec = pl.BlockSpec((tm, D), lambda i: (i, 0))
    return pl.pallas_call(
        _mix_out_kernel,
        out_shape=[jax.ShapeDtypeStruct((T, D), F32), jax.ShapeDtypeStruct((T, D), BF16)],
        grid=(T // tm,),
        in_specs=[
            pl.BlockSpec((tm, NA_WIDTH), lambda i: (i, 0)),
            pl.BlockSpec((tm, HY_WIDTH), lambda i: (i, 0)),
            pl.BlockSpec((tm, GDN_WIDTH), lambda i: (i, 0)),
            pl.BlockSpec((tm, GATE_BLOCK), lambda i: (i, 1)),
            pl.BlockSpec((1, 1, N_BRANCH * D), lambda i: (layer, 0, 0)),
            resident((1, NA_WIDTH, D), lambda i: (layer, 0, 0)),
            resident((1, HY_WIDTH, D), lambda i: (layer, 0, 0)),
            resident((1, GDN_WIDTH, D), lambda i: (layer, 0, 0)),
            resident((1, D, D), lambda i: (layer, 0, 0)),
        ] + x_specs + [mod_spec(2), pl.BlockSpec((1, 1, D), lambda i: (layer, 0, 0)), mod_spec(3), mod_spec(4)],
        out_specs=[row_spec, row_spec],
        compiler_params=_cparams(("arbitrary",)),
        name="mix_out",
    )(y_na, y_hy, y_gdn, proj, b_gate.reshape(DEPTH, 1, N_BRANCH * D), w_pa, w_pb, w_pc, w_out, *x_args,
      mod, ln2_g.reshape(DEPTH, 1, D), mod, mod)


FFN_TM = 1024
FFN_SLAB = 16


def _ffn_up_kernel(h_ref, wa_ref, wb_ref, cwa_ref, cwb_ref, cba_ref, cbb_ref, o_ref, wab_ref, wbb_ref):
    m = pl.program_id(1)

    @pl.when(m == 0)
    def _():
        wab_ref[...] = wa_ref[0].astype(BF16)
        wbb_ref[...] = wb_ref[0].astype(BF16)

    h = h_ref[...]
    ups = (jnp.dot(h, wab_ref[...], preferred_element_type=F32), jnp.dot(h, wbb_ref[...], preferred_element_type=F32))
    taps = ((cwa_ref[0], cba_ref[0]), (cwb_ref[0], cbb_ref[0]))

    def swiglu(ua, ub):
        return (ua * jax.nn.sigmoid(ua) * ub).astype(BF16)

    full = [pltpu.roll(up, 1, 0) * cw[0:1] + up * cw[1:2] + pltpu.roll(up, FFN_TM - 1, 0) * cw[2:3] + cb
            for up, (cw, cb) in zip(ups, taps)]
    o_ref[...] = swiglu(*full)

    def redo_edge(r, starts):
        lo = r // FFN_SLAB * FFN_SLAB
        pick = lax.broadcasted_iota(jnp.int32, (FFN_SLAB, 1), 0) == r - lo
        halves = []
        for up, conv, (cw, cb) in zip(ups, full, taps):
            row = up[r:r + 1] * cw[1:2] + cb
            row = row + (up[r + 1:r + 2] * cw[2:3] if starts else up[r - 1:r] * cw[0:1])
            halves.append(jnp.where(pick, row, conv[lo:lo + FFN_SLAB]))
        o_ref[lo:lo + FFN_SLAB, :] = swiglu(*halves)

    redo_edge(0, True)
    redo_edge(FFN_TM - 1, False)

    @pl.when(m < T_P // FFN_TM)
    def _():
        for r in range(SEQ, FFN_TM, SEQ):
            redo_edge(r - 1, False)
            redo_edge(r, True)


def _ffn_up(h, w_up, conv_w, conv_b, layer):
    tn = 512
    nt = D_FF // tn
    conv_b = conv_b.reshape(DEPTH, 1, 2 * D_FF)
    return pl.pallas_call(
        _ffn_up_kernel,
        out_shape=jax.ShapeDtypeStruct((T, D_FF), BF16),
        grid=(nt, T // FFN_TM),
        in_specs=[
            pl.BlockSpec((FFN_TM, D), lambda j, i: (i, 0)),
            pl.BlockSpec((1, D, tn), lambda j, i: (layer, 0, j)),
            pl.BlockSpec((1, D, tn), lambda j, i: (layer, 0, nt + j)),
            pl.BlockSpec((1, 3, tn), lambda j, i: (layer, 0, j)),
            pl.BlockSpec((1, 3, tn), lambda j, i: (layer, 0, nt + j)),
            pl.BlockSpec((1, 1, tn), lambda j, i: (layer, 0, j)),
            pl.BlockSpec((1, 1, tn), lambda j, i: (layer, 0, nt + j)),
        ],
        out_specs=pl.BlockSpec((FFN_TM, tn), lambda j, i: (i, j)),
        scratch_shapes=[pltpu.VMEM((D, tn), BF16), pltpu.VMEM((D, tn), BF16)],
        compiler_params=_cparams(("arbitrary", "arbitrary")),
        name="ffn_up",
    )(h, w_up, w_up, conv_w, conv_w, conv_b, conv_b)


DOWN_TM = 256


def _ffn_down_kernel(a_ref, w_ref, x_ref, gt_ref, g_ref, sh_ref, sc_ref, o_ref, h_ref):
    x = x_ref[...] + gt_ref[0] * jnp.dot(a_ref[...], w_ref[0], preferred_element_type=F32)
    o_ref[...] = x
    h_ref[...] = _modulated_norm(x, g_ref[0], sh_ref[0], sc_ref[0])


def _ffn_down_final_kernel(a_ref, w_ref, x_ref, gt_ref, g_ref, yp_ref, ys_ref):
    x = x_ref[...] + gt_ref[0] * jnp.dot(a_ref[...], w_ref[0], preferred_element_type=F32)
    y = x * lax.rsqrt(jnp.mean(x * x, axis=-1, keepdims=True) + NORM_EPS) * g_ref[...]
    is_context = pl.program_id(0) < T_P // DOWN_TM

    @pl.when(is_context)
    def _():
        yp_ref[...] = y

    @pl.when(jnp.logical_not(is_context))
    def _():
        ys_ref[...] = y


def _ffn_down(act, w_down, x, mod, layer, ln1_g, final_g):
    tm = DOWN_TM
    last = layer == DEPTH - 1
    mod_spec = lambda lyr, chunk: pl.BlockSpec((1, 1, D), lambda i: (lyr * N_COND + _cond_row(i, tm), 0, chunk))
    row_spec = pl.BlockSpec((tm, D), lambda i: (i, 0))
    in_specs = [
        pl.BlockSpec((tm, D_FF), lambda i: (i, 0)),
        pl.BlockSpec((1, D_FF, D), lambda i: (layer, 0, 0), pipeline_mode=pl.Buffered(1)),
        row_spec,
        mod_spec(layer, 5),
    ]
    if last:
        n_ctx = T_P // tm
        return pl.pallas_call(
            _ffn_down_final_kernel,
            out_shape=[jax.ShapeDtypeStruct((T_P, D), F32), jax.ShapeDtypeStruct((T_S, D), F32)],
            grid=(T // tm,),
            in_specs=in_specs + [pl.BlockSpec((1, D), lambda i: (0, 0))],
            out_specs=[pl.BlockSpec((tm, D), lambda i: (jnp.minimum(i, n_ctx - 1), 0)),
                       pl.BlockSpec((tm, D), lambda i: (jnp.maximum(i - n_ctx, 0), 0))],
            compiler_params=_cparams(("arbitrary",)),
            name="ffn_down_final",
        )(act, w_down, x, mod, final_g.reshape(1, D))
    return pl.pallas_call(
        _ffn_down_kernel,
        out_shape=[jax.ShapeDtypeStruct((T, D), F32), jax.ShapeDtypeStruct((T, D), BF16)],
        grid=(T // tm,),
        in_specs=in_specs + [pl.BlockSpec((1, 1, D), lambda i: (layer + 1, 0, 0)),
                             mod_spec(layer + 1, 0), mod_spec(layer + 1, 1)],
        out_specs=[row_spec, row_spec],
        compiler_params=_cparams(("arbitrary",)),
        name="ffn_down",
    )(act, w_down, x, mod, ln1_g.reshape(DEPTH, 1, D), mod, mod)


ATT_SCALE = NA_DH ** -0.5


def _nt_dot(a, b):
    return lax.dot_general(a, b, (((1,), (1,)), ((), ())), preferred_element_type=F32)


def _head_slices():
    return [slice(h * NA_DH, (h + 1) * NA_DH) for h in range(NA_HEADS)]


def _ctx_attn_kernel(q_ref, k_ref, v_ref, o_ref, ko_ref, vo_ref):
    heads = _head_slices()
    q = [(q_ref[:, sl] * ATT_SCALE).astype(BF16) for sl in heads]
    s = [_nt_dot(q[h], k_ref[:, sl].astype(BF16)) for h, sl in enumerate(heads)]
    e = [x - jnp.max(x, axis=-1, keepdims=True) for x in s]
    e = [jnp.exp(x) for x in e]
    o = [jnp.dot(e[h].astype(BF16), v_ref[:, sl].astype(BF16), preferred_element_type=F32)
         for h, sl in enumerate(heads)]
    for h, sl in enumerate(heads):
        o_ref[:, sl] = (o[h] / jnp.sum(e[h], axis=-1, keepdims=True)).astype(BF16)
    ko_ref[0, 0] = k_ref[...]
    vo_ref[0, 0] = v_ref[...]


def _ctx_attention(proj, layer, caches):
    cache_shape = jax.ShapeDtypeStruct((BATCH, DEPTH, SEQ, NA_WIDTH), F32)
    cache_spec = pl.BlockSpec((1, 1, SEQ, NA_WIDTH), lambda b: (b, layer, 0, 0))
    qkv_specs = [pl.BlockSpec((SEQ, NA_WIDTH), lambda b, j=j: (b, j)) for j in range(3)]
    out_shape = [jax.ShapeDtypeStruct((T, NA_WIDTH), BF16), cache_shape, cache_shape]
    out_specs = [pl.BlockSpec((SEQ, NA_WIDTH), lambda b: (b, 0)), cache_spec, cache_spec]
    if caches is None:
        return pl.pallas_call(
            _ctx_attn_kernel, out_shape=out_shape, grid=(BATCH,), in_specs=qkv_specs, out_specs=out_specs,
            compiler_params=_cparams(("arbitrary",)), name="ctx_attention",
        )(proj, proj, proj)

    def body(q_ref, k_ref, v_ref, kc_ref, vc_ref, o_ref, ko_ref, vo_ref):
        del kc_ref, vc_ref
        _ctx_attn_kernel(q_ref, k_ref, v_ref, o_ref, ko_ref, vo_ref)

    any_spec = pl.BlockSpec(memory_space=pl.ANY)
    return pl.pallas_call(
        body, out_shape=out_shape, grid=(BATCH,), in_specs=qkv_specs + [any_spec, any_spec],
        out_specs=out_specs, input_output_aliases={3: 1, 4: 2},
        compiler_params=_cparams(("arbitrary",)), name="ctx_attention",
    )(proj, proj, proj, *caches)


NA_ROWS = DEC_SEQ // GRID_W
NA_WIN_TOK = NA_WIN_ROWS * GRID_W
NA_ROWS_PER_STEP = 2


def _na_window_start(r):
    return jnp.clip(r - NA_WIN_ROWS // 2, 0, NA_ROWS - NA_WIN_ROWS)


def _na_bias_table(rpb):
    col = np.arange(GRID_W)
    col_start = np.clip(col - NA_WIN_COLS // 2, 0, GRID_W - NA_WIN_COLS)
    col_mask = (col[None, :] >= col_start[:, None]) & (col[None, :] < col_start[:, None] + NA_WIN_COLS)
    rel_col = np.clip(col[None, :] - col[:, None] + NA_WIN_COLS - 1, 0, 2 * NA_WIN_COLS - 2)
    n_rel = 2 * NA_WIN_COLS - 1
    onehot = (rel_col[None] == np.arange(n_rel)[:, None, None]).astype(np.float32).reshape(n_rel, -1)
    expanded = jnp.dot(rpb.reshape(-1, n_rel), onehot, precision=HIGHEST)
    banded = jnp.where(col_mask, expanded.reshape(NA_HEADS, 2 * NA_WIN_ROWS - 1, GRID_W, GRID_W), NEG_INF)
    return jnp.concatenate([banded[:, :-1], banded[:, 1:]], axis=-1)


def _na_attn_kernel(q_ref, k_ref, v_ref, ck_ref, cv_ref, bias_ref, y_in_ref, o_ref):
    del y_in_ref
    heads = _head_slices()
    units = [(a, h) for a in range(NA_ROWS_PER_STEP) for h in range(NA_HEADS)]
    nu = range(len(units))
    r = [pl.program_id(1) * NA_ROWS_PER_STEP + a for a in range(NA_ROWS_PER_STEP)]
    win = [pl.ds(pl.multiple_of(_na_window_start(ra) * GRID_W, GRID_W), NA_WIN_TOK) for ra in r]
    rel0 = [NA_WIN_ROWS - 1 - (ra - _na_window_start(ra)) for ra in r]
    qrows = [slice(a * GRID_W, (a + 1) * GRID_W) for a in range(NA_ROWS_PER_STEP)]
    q = [(q_ref[:, sl] * ATT_SCALE).astype(BF16) for sl in heads]
    bias = [jnp.concatenate([bias_ref[h, rel0[a] + 2 * p] for p in range(NA_WIN_ROWS // 2)], axis=1)
            for a, h in units]
    s_win = [_nt_dot(q[h][qrows[a]], k_ref[win[a], heads[h]].astype(BF16)) + bias[i] for i, (a, h) in enumerate(units)]
    s_ctx_all = [_nt_dot(q[h], ck_ref[0, 0, :, sl].astype(BF16)) for h, sl in enumerate(heads)]
    s_ctx = [s_ctx_all[h][qrows[a]] for a, h in units]
    m = [jnp.maximum(jnp.max(s_win[i], axis=-1, keepdims=True), jnp.max(s_ctx[i], axis=-1, keepdims=True))
         for i in nu]
    e_win = [jnp.exp(s_win[i] - m[i]) for i in nu]
    e_ctx = [jnp.exp(s_ctx[i] - m[i]) for i in nu]
    o_win = [jnp.dot(e_win[i].astype(BF16), v_ref[win[a], heads[h]].astype(BF16), preferred_element_type=F32)
             for i, (a, h) in enumerate(units)]
    e_ctx_all = [jnp.concatenate([e_ctx[a * NA_HEADS + h] for a in range(NA_ROWS_PER_STEP)], axis=0).astype(BF16)
                 for h in range(NA_HEADS)]
    o_ctx_all = [jnp.dot(e_ctx_all[h], cv_ref[0, 0, :, sl].astype(BF16), preferred_element_type=F32)
                 for h, sl in enumerate(heads)]
    for i, (a, h) in enumerate(units):
        denom = jnp.sum(e_win[i], axis=-1, keepdims=True) + jnp.sum(e_ctx[i], axis=-1, keepdims=True)
        o_ref[qrows[a], heads[h]] = ((o_win[i] + o_ctx_all[h][qrows[a]]) / denom).astype(BF16)


def _na_attention(proj, cache_k, cache_v, rpb, y_na, layer):
    q_rows = NA_ROWS_PER_STEP * GRID_W
    steps = NA_ROWS // NA_ROWS_PER_STEP
    q_row0 = T_P // q_rows
    seq0 = T_P // DEC_SEQ
    kv_spec = lambda j: pl.BlockSpec((DEC_SEQ, NA_WIDTH), lambda b, r: (seq0 + b, j))
    ctx_spec = pl.BlockSpec((1, 1, SEQ, NA_WIDTH), lambda b, r: (b, layer, 0, 0))
    return pl.pallas_call(
        _na_attn_kernel,
        out_shape=jax.ShapeDtypeStruct((T, NA_WIDTH), BF16),
        grid=(DEC_BATCH, steps),
        in_specs=[
            pl.BlockSpec((q_rows, NA_WIDTH), lambda b, r: (q_row0 + b * steps + r, 0)),
            kv_spec(1), kv_spec(2), ctx_spec, ctx_spec,
            pl.BlockSpec((NA_HEADS, 2 * NA_WIN_ROWS - 2, GRID_W, 2 * GRID_W), lambda b, r: (0, 0, 0, 0),
                         pipeline_mode=pl.Buffered(1)),
            pl.BlockSpec(memory_space=pl.ANY),
        ],
        out_specs=pl.BlockSpec((q_rows, NA_WIDTH), lambda b, r: (q_row0 + b * steps + r, 0)),
        input_output_aliases={6: 0},
        compiler_params=_cparams(("arbitrary", "arbitrary")),
        name="na_attention",
    )(proj, proj, proj, cache_k.reshape(DEC_BATCH, DEPTH, SEQ, NA_WIDTH),
      cache_v.reshape(DEC_BATCH, DEPTH, SEQ, NA_WIDTH), _na_bias_table(rpb), y_na)


HY_CB = 256


def _dft_forward_kernel(o_ref, *, length):
    rows, cols = o_ref.shape
    f = lax.broadcasted_iota(jnp.int32, (rows, cols), 0) + pl.program_id(0) * rows
    t = lax.broadcasted_iota(jnp.int32, (rows, cols), 1)
    k = f & (length - 1)
    phase = (2 * k + 1) * t - jnp.where(f >= length, length, 0)
    phase = phase & (4 * length - 1)
    phase = jnp.where(phase >= 2 * length, phase - 4 * length, phase)
    o_ref[...] = jnp.cos(phase.astype(F32) * (math.pi / (2 * length))).astype(BF16)


def _dft_tables(length):
    tr = 256
    fwd = pl.pallas_call(
        functools.partial(_dft_forward_kernel, length=length),
        out_shape=jax.ShapeDtypeStruct((2 * length, length), BF16),
        grid=(2 * length // tr,), out_specs=pl.BlockSpec((tr, length), lambda i: (i, 0)),
        compiler_params=_cparams(("arbitrary",)), name="dft_table",
    )()
    inv = jnp.concatenate([fwd[:length].T, -fwd[length:].T], axis=1) * (1.0 / length)
    return fwd, inv


def _hy_filter_kernel(tn_ref, band_ref, w1_ref, b1_ref, fr_ref, w2_ref, b2_ref, w3_ref, dec_ref,
                      f_ref, o_ref, *, length):
    t_norm = tn_ref[...]
    t_idx = lax.broadcasted_iota(jnp.int32, (length, LANE), 0).astype(F32)
    lane = lax.broadcasted_iota(jnp.int32, (length, LANE), 1)
    ang = (2.0 * math.pi / length) * t_idx * band_ref[...]
    z = jnp.where(lane == 0, t_norm,
                  jnp.where(lane <= HY_BANDS, jnp.cos(ang), jnp.where(lane <= 2 * HY_BANDS, jnp.sin(ang), 0.0)))
    hdn = jnp.sin(fr_ref[0:1] * (jnp.dot(z, w1_ref[...], precision=HIGHEST, preferred_element_type=F32)
                                 + b1_ref[...]))
    hdn = jnp.sin(fr_ref[1:2] * (jnp.dot(hdn, w2_ref[...], precision=HIGHEST, preferred_element_type=F32)
                                 + b2_ref[...]))
    filt = jnp.dot(hdn, w3_ref[...], precision=HIGHEST, preferred_element_type=F32)
    filt = filt * jnp.exp(-t_norm * jnp.abs(dec_ref[...]))
    first = lax.broadcasted_iota(jnp.int32, (length, HY_WIDTH), 0) == 0
    for o in range(HY_ORDER):
        fwd = filt[:, (2 * o) * HY_WIDTH:(2 * o + 1) * HY_WIDTH]
        bwd = filt[:, (2 * o + 1) * HY_WIDTH:(2 * o + 2) * HY_WIDTH]
        bwd = jnp.where(first, 0.0, pltpu.roll(bwd, 1, 0))
        o_ref[o, :length] = jnp.dot(f_ref[:length], (fwd + bwd).astype(BF16), preferred_element_type=F32)
        o_ref[o, length:] = jnp.dot(f_ref[length:], (bwd - fwd).astype(BF16), preferred_element_type=F32)


def _hy_filter_spectrum(length, f_tab, w1, b1, freq, w2, b2, w3, decay):
    emb = 1 + 2 * HY_BANDS
    t_norm = jnp.linspace(0.0, 1.0, length, dtype=F32).reshape(length, 1)
    bands = np.zeros((1, LANE), np.float32)
    bands[0, 1:1 + HY_BANDS] = bands[0, 1 + HY_BANDS:emb] = np.linspace(1e-4, HY_BANDS - 1, HY_BANDS,
                                                                        dtype=np.float32)
    w1p = jnp.zeros((LANE, w1.shape[1]), F32).at[:emb].set(w1)
    return pl.pallas_call(
        functools.partial(_hy_filter_kernel, length=length),
        out_shape=jax.ShapeDtypeStruct((HY_ORDER, 2 * length, HY_WIDTH), F32),
        compiler_params=pltpu.CompilerParams(vmem_limit_bytes=VMEM_LIMIT),
        name="hyena_filter",
    )(t_norm, jnp.asarray(bands), w1p, b1.reshape(1, -1), freq, w2, b2.reshape(1, -1), w3,
      decay.reshape(1, -1), f_tab)


def _hyena_kernel(v_ref, x1_ref, x2_ref, cwv_ref, cw1_ref, cw2_ref, cbv_ref, cb1_ref, cb2_ref, skip_ref,
                  hs_ref, f_ref, g_ref, *rest, length, parts):
    o_ref = rest[-1]
    n = range(len(parts))

    def short_conv(u_ref, cw_ref, cb_ref, rs, cs):
        u = u_ref[rs, cs]
        row = lax.broadcasted_iota(jnp.int32, u.shape, 0)
        prev = jnp.where(row == 0, 0.0, pltpu.roll(u, 1, 0))
        nxt = jnp.where(row == length - 1, 0.0, pltpu.roll(u, length - 1, 0))
        return prev * cw_ref[0:1, cs] + u * cw_ref[1:2, cs] + nxt * cw_ref[2:3, cs] + cb_ref[:, cs]

    z = [short_conv(v_ref, cwv_ref, cbv_ref, rs, cs) for rs, cs in parts]
    gates = [(short_conv(x1_ref, cw1_ref, cb1_ref, rs, cs), short_conv(x2_ref, cw2_ref, cb2_ref, rs, cs))
             for rs, cs in parts]
    for o in range(HY_ORDER):
        zf = [jnp.dot(f_ref[...], z[i].astype(BF16), preferred_element_type=F32) for i in n]
        p = []
        for i, (rs, cs) in enumerate(parts):
            zc, zs = zf[i][:length], zf[i][length:]
            h_re, h_im = hs_ref[o, :length, cs], hs_ref[o, length:, cs]
            p.append(jnp.concatenate([zc * h_re + zs * h_im, zc * h_im - zs * h_re], axis=0).astype(BF16))
        conv = [jnp.dot(g_ref[...], p[i], preferred_element_type=F32) for i in n]
        z = [gates[i][o] * (conv[i] + z[i] * skip_ref[o:o + 1, cs]) for i, (rs, cs) in enumerate(parts)]
    for i, (rs, cs) in enumerate(parts):
        o_ref[rs, cs] = z[i].astype(BF16)


def _hyena(proj, y_hy, layer, length, tables, spectrum, conv_w, conv_b, skip):
    seqs = 2 if length == SEQ else 1
    rows = seqs * length
    nblk = (T_P if length == SEQ else T_S) // rows
    row0 = 0 if length == SEQ else T_P // rows
    if seqs > 1:
        parts = [(slice(b * length, (b + 1) * length), slice(None)) for b in range(seqs)]
    else:
        parts = [(slice(None), slice(c * HY_CB, (c + 1) * HY_CB)) for c in range(HY_WIDTH // HY_CB)]
    col0 = C_HY // HY_WIDTH
    u_spec = lambda part: pl.BlockSpec((rows, HY_WIDTH), lambda s: (row0 + s, col0 + part))
    cw_spec = lambda part: pl.BlockSpec((1, 3, HY_WIDTH), lambda s: (layer, 0, part))
    cb_spec = lambda part: pl.BlockSpec((1, 1, HY_WIDTH), lambda s: (layer, 0, part))
    resident = functools.partial(pl.BlockSpec, pipeline_mode=pl.Buffered(1))
    tab_specs = [resident(t.shape, lambda s: (0, 0)) for t in tables]
    in_specs = ([u_spec(0), u_spec(1), u_spec(2), cw_spec(0), cw_spec(1), cw_spec(2),
                 cb_spec(0), cb_spec(1), cb_spec(2),
                 pl.BlockSpec((1, HY_ORDER, HY_WIDTH), lambda s: (layer, 0, 0)),
                 resident((HY_ORDER, 2 * length, HY_WIDTH), lambda s: (0, 0, 0))] + tab_specs)
    args = [proj] * 3 + [conv_w] * 3 + [conv_b.reshape(DEPTH, 1, -1)] * 3 + [skip, spectrum] + list(tables)
    aliases = {}
    if y_hy is not None:
        in_specs.append(pl.BlockSpec(memory_space=pl.ANY))
        args.append(y_hy)
        aliases = {len(args) - 1: 0}

    def body(v_ref, x1_ref, x2_ref, cwv, cw1, cw2, cbv, cb1, cb2, skip_ref, hs_ref, *rest):
        _hyena_kernel(v_ref, x1_ref, x2_ref, cwv.at[0], cw1.at[0], cw2.at[0], cbv.at[0], cb1.at[0], cb2.at[0],
                      skip_ref.at[0], hs_ref, *rest, length=length, parts=parts)

    return pl.pallas_call(
        body,
        out_shape=jax.ShapeDtypeStruct((T, HY_WIDTH), BF16),
        grid=(nblk,), in_specs=in_specs,
        out_specs=pl.BlockSpec((rows, HY_WIDTH), lambda s: (row0 + s, 0)),
        input_output_aliases=aliases,
        compiler_params=_cparams(("arbitrary",)),
        name="hyena",
    )(*args)


GC = 128
GDN_SCALE = GDN_DK ** -0.5
GDN_CTX_PER_STEP = 2


def _state_slot(b, d):
    return (2 * b + d) * GDN_HEADS


def _merge_masks(lower):
    ri = lax.broadcasted_iota(jnp.int32, (GC, GC), 0)
    ci = lax.broadcasted_iota(jnp.int32, (GC, GC), 1)
    hi, lo = (ri, ci) if lower else (ci, ri)
    return [((hi >> (b + 1)) == (lo >> (b + 1))) & ((hi >> b) > (lo >> b)) for b in range(int(math.log2(GC)))]


def _unit_tri_inverse(a, masks):
    ri = lax.broadcasted_iota(jnp.int32, (GC, GC), 0)
    ci = lax.broadcasted_iota(jnp.int32, (GC, GC), 1)
    eye = (ri == ci).astype(F32)
    n = range(len(a))
    t = [eye - jnp.where(masks[i][0], a[i], 0.0) for i in n]
    for level in range(1, len(masks[0])):
        tb = [t[i].astype(BF16) for i in n]
        ta = [jnp.dot(tb[i], jnp.where(masks[i][level], a[i], 0.0).astype(BF16), preferred_element_type=F32)
              for i in n]
        tat = [jnp.dot(ta[i].astype(BF16), tb[i], preferred_element_type=F32) for i in n]
        t = [t[i] - tat[i] for i in n]
    return t


def _gdn_kernel(q_ref, k_ref, v_ref, z_ref, ba_ref, cwq_ref, cwk_ref, cwv_ref, alog_ref, dt_ref, ng_ref,
                sf0_ref, sb0_ref, *rest, length, nb):
    y_ref, sf_ref, sb_ref, qn_ref, kn_ref, vn_ref, beta_ref, g_ref, of_ref, ob_ref, s_ref = rest[-11:]
    n_chunks = length // GC
    rows_total = nb * length
    row = lax.broadcasted_iota(jnp.int32, (rows_total, GDN_WIDTH), 0) & (length - 1)
    first, last = row == 0, row == length - 1

    def conv_silu(u_ref, cw_ref):
        u = u_ref[...]
        prev = jnp.where(first, 0.0, pltpu.roll(u, 1, 0))
        nxt = jnp.where(last, 0.0, pltpu.roll(u, rows_total - 1, 0))
        c = prev * cw_ref[0:1] + u * cw_ref[1:2] + nxt * cw_ref[2:3]
        return c * jax.nn.sigmoid(c)

    q = conv_silu(q_ref, cwq_ref)
    k = conv_silu(k_ref, cwk_ref)
    vn_ref[...] = conv_silu(v_ref, cwv_ref)
    for h in range(GDN_HEADS):
        sl = slice(h * GDN_DK, (h + 1) * GDN_DK)
        qh, kh = q[:, sl], k[:, sl]
        qn_ref[:, sl] = qh * lax.rsqrt(jnp.sum(qh * qh, axis=-1, keepdims=True) + NORM_EPS) * GDN_SCALE
        kn_ref[:, sl] = kh * lax.rsqrt(jnp.sum(kh * kh, axis=-1, keepdims=True) + NORM_EPS)
    ba = ba_ref[...]
    beta_ref[...] = jax.nn.sigmoid(ba)
    g_ref[...] = -jnp.exp(alog_ref[...]) * jax.nn.softplus(ba + dt_ref[...])

    ri = lax.broadcasted_iota(jnp.int32, (GC, GC), 0)
    ci = lax.broadcasted_iota(jnp.int32, (GC, GC), 1)
    lower = (ri >= ci).astype(F32)
    upper = (ri <= ci).astype(F32)

    def chunk_step(n, carry):
        scans = [(b, d) for b in range(nb) for d in range(2)]
        rows = {(b, d): pl.ds(pl.multiple_of(b * length + (n if d == 0 else n_chunks - 1 - n) * GC, GC), GC)
                for b, d in scans}
        loaded = {}
        for bd in scans:
            loaded[bd] = (g_ref[rows[bd], :], beta_ref[rows[bd], :])
            for h in range(GDN_HEADS):
                sl = slice(h * GDN_DK, (h + 1) * GDN_DK)
                loaded[bd + (h,)] = (qn_ref[rows[bd], sl], kn_ref[rows[bd], sl], vn_ref[rows[bd], sl],
                                     s_ref[_state_slot(*bd) + h])
        units = [(b, d, h) for b, d in scans for h in range(GDN_HEADS)]
        incl = [(ri >= ci), (ri <= ci)]
        strict = [(ri > ci), (ri < ci)]
        masks = [_merge_masks(True), _merge_masks(False)]
        gcs = {(b, d): jnp.dot((lower, upper)[d], loaded[b, d][0], precision=HIGHEST, preferred_element_type=F32)
               for b, d in scans}
        gcs_t = {bd: g.T for bd, g in gcs.items()}
        edge = [GC - 1, 0]
        g_col = [gcs[b, d][:, 8 + 4 * d + h:9 + 4 * d + h] for b, d, h in units]
        g_row = [gcs_t[b, d][8 + 4 * d + h:9 + 4 * d + h, :] for b, d, h in units]
        g_end = [gcs_t[b, d][8 + 4 * d + h:9 + 4 * d + h, edge[d]:edge[d] + 1] for b, d, h in units]
        b_col = [loaded[b, d][1][:, 4 * d + h:4 * d + h + 1] for b, d, h in units]
        qc, kc, vc, st = (list(x) for x in zip(*(loaded[u] for u in units)))
        nu = range(len(units))
        decay = [jnp.where(incl[d], jnp.exp(jnp.where(incl[d], g_col[i] - g_row[i], 0.0)), 0.0)
                 for i, (b, d, h) in enumerate(units)]
        kb = [kc[i] * b_col[i] for i in nu]
        kcb = [kc[i].astype(BF16) for i in nu]
        kk = [_nt_dot(kb[i].astype(BF16), kcb[i]) for i in nu]
        qk = [_nt_dot(qc[i].astype(BF16), kcb[i]) for i in nu]
        a = [jnp.where(strict[d], kk[i] * decay[i], 0.0) for i, (b, d, h) in enumerate(units)]
        t = _unit_tri_inverse(a, [masks[d] for b, d, h in units])
        e_col = [jnp.exp(g_col[i]) for i in nu]
        rhs = [jnp.concatenate([vc[i] * b_col[i], kb[i] * e_col[i]], axis=1).astype(BF16) for i in nu]
        sol = [jnp.dot(t[i].astype(BF16), rhs[i], preferred_element_type=F32) for i in nu]
        attn = [jnp.where(incl[d], qk[i] * decay[i], 0.0).astype(BF16) for i, (b, d, h) in enumerate(units)]
        sb = [st[i].astype(BF16) for i in nu]
        ws = [jnp.dot(sol[i][:, GDN_DV:].astype(BF16), sb[i], preferred_element_type=F32) for i in nu]
        qs = [jnp.dot((qc[i] * e_col[i]).astype(BF16), sb[i], preferred_element_type=F32) for i in nu]
        v_new = [(sol[i][:, :GDN_DV] - ws[i]).astype(BF16) for i in nu]
        av = [jnp.dot(attn[i], v_new[i], preferred_element_type=F32) for i in nu]
        k_dec_t = [(kc[i] * jnp.exp(g_end[i] - g_col[i])).T.astype(BF16) for i in nu]
        kv = [jnp.dot(k_dec_t[i], v_new[i], preferred_element_type=F32) for i in nu]
        results = [(qs[i] + av[i], st[i] * jnp.exp(g_end[i]) + kv[i]) for i in nu]
        for (b, d, h), (o, s) in zip(units, results):
            (of_ref if d == 0 else ob_ref)[rows[b, d], h * GDN_DV:(h + 1) * GDN_DV] = o
            s_ref[_state_slot(b, d) + h] = s
        return carry

    for b in range(nb):
        s_ref[pl.ds(_state_slot(b, 0), GDN_HEADS)] = sf0_ref[b]
        s_ref[pl.ds(_state_slot(b, 1), GDN_HEADS)] = sb0_ref[b]
    lax.fori_loop(0, n_chunks, chunk_step, 0)
    for b in range(nb):
        sf_ref[b, 0] = s_ref[pl.ds(_state_slot(b, 0), GDN_HEADS)]
        sb_ref[b, 0] = s_ref[pl.ds(_state_slot(b, 1), GDN_HEADS)]
    for h in range(GDN_HEADS):
        sl = slice(h * GDN_DV, (h + 1) * GDN_DV)
        o = of_ref[:, sl] + ob_ref[:, sl]
        o = o * lax.rsqrt(jnp.mean(o * o, axis=-1, keepdims=True) + NORM_EPS) * ng_ref[...]
        zh = z_ref[:, sl]
        y_ref[:, sl] = (o * (zh * jax.nn.sigmoid(zh))).astype(BF16)


def _gdn(proj, y_gdn, states_out, layer, length, s_f0, s_b0, conv_w, a_log, dt_bias, norm_g):
    nb = GDN_CTX_PER_STEP if length == SEQ else 1
    rows = nb * length
    nblk = (T_P if length == SEQ else T_S) // rows
    row0 = 0 if length == SEQ else T_P // rows
    col = lambda c: c // GDN_WIDTH
    blk = lambda c: pl.BlockSpec((rows, GDN_WIDTH), lambda s: (row0 + s, col(c)))
    cw = lambda part: pl.BlockSpec((1, 3, GDN_WIDTH), lambda s: (layer, 0, part))
    vec = lambda: pl.BlockSpec((1, 1, LANE), lambda s: (layer, 0, 0))
    st_in = pl.BlockSpec((nb, GDN_HEADS, GDN_DK, GDN_DV), lambda s: (s, 0, 0, 0))
    st_out = pl.BlockSpec((nb, 1, GDN_HEADS, GDN_DK, GDN_DV), lambda s: (s, layer, 0, 0, 0))
    pad_lanes = lambda x, off: jnp.zeros((DEPTH, 1, LANE), F32).at[:, 0, off:off + 2 * GDN_HEADS].set(
        x.reshape(DEPTH, -1))
    in_specs = [blk(C_GDN), blk(C_GDN + GDN_WIDTH), blk(C_GDN + 2 * GDN_WIDTH), blk(C_Z),
                pl.BlockSpec((rows, LANE), lambda s: (row0 + s, C_BA // LANE)),
                cw(0), cw(1), cw(2), vec(), vec(), vec(), st_in, st_in]
    args = [proj] * 5 + [conv_w] * 3 + [pad_lanes(a_log, 2 * GDN_HEADS), pad_lanes(dt_bias, 2 * GDN_HEADS),
                                        norm_g.reshape(DEPTH, 1, GDN_DV), s_f0, s_b0]
    st_shape = jax.ShapeDtypeStruct((nblk * nb, DEPTH, GDN_HEADS, GDN_DK, GDN_DV), F32)
    aliases = {}
    for carried, out_idx in ((y_gdn, 0),) + (((states_out[0], 1), (states_out[1], 2)) if states_out else ()):
        if carried is not None:
            in_specs.append(pl.BlockSpec(memory_space=pl.ANY))
            args.append(carried)
            aliases[len(args) - 1] = out_idx

    def body(q_ref, k_ref, v_ref, z_ref, ba_ref, cwq, cwk, cwv, alog, dt, ng, sf0, sb0, *rest):
        _gdn_kernel(q_ref, k_ref, v_ref, z_ref, ba_ref, cwq.at[0], cwk.at[0], cwv.at[0], alog.at[0], dt.at[0],
                    ng.at[0], sf0, sb0, *rest, length=length, nb=nb)

    seq_buf = lambda w: pltpu.VMEM((rows, w), F32)
    return pl.pallas_call(
        body,
        out_shape=[jax.ShapeDtypeStruct((T, GDN_WIDTH), BF16), st_shape, st_shape],
        grid=(nblk,), in_specs=in_specs,
        out_specs=[pl.BlockSpec((rows, GDN_WIDTH), lambda s: (row0 + s, 0)), st_out, st_out],
        scratch_shapes=[seq_buf(GDN_WIDTH), seq_buf(GDN_WIDTH), seq_buf(GDN_WIDTH), seq_buf(LANE), seq_buf(LANE),
                        seq_buf(GDN_WIDTH), seq_buf(GDN_WIDTH),
                        pltpu.VMEM((nb * 2 * GDN_HEADS, GDN_DK, GDN_DV), F32)],
        input_output_aliases=aliases,
        compiler_params=_cparams(("arbitrary",)),
        name="gated_deltanet",
    )(*args)


def kernel(x_prompt, x_sample, cache_k, cache_v, state_fwd, state_bwd, c, c_ctx, ln1_g, ln2_g, w_mod, b_mod,
           w_in, na_rpb, hy_conv_w, hy_conv_b, hy_w1, hy_b1, hy_freq, hy_w2, hy_b2, hy_w3, hy_decay, hy_skip,
           gdn_conv_w, gdn_a_log, gdn_dt_bias, gdn_norm_g, w_pa, w_pb, w_pc, b_gate, w_out, ffn_w_up,
           ffn_conv_w, ffn_conv_b, ffn_w_down, final_g):
    x = (x_prompt.reshape(T_P, D), x_sample.reshape(T_S, D))
    cond = jnp.concatenate([c_ctx[None], c, jnp.zeros((N_COND - 1 - DEC_BATCH, D), F32)], axis=0)
    mod = _mod_table(cond, w_mod, b_mod).reshape(DEPTH * N_COND, 1, 6 * D)

    w_pa_b, w_pb_b, w_pc_b, w_out_b = (w.astype(BF16) for w in (w_pa, w_pb, w_pc, w_out))
    w_down_b = ffn_w_down.astype(BF16)
    w_in_t = jnp.swapaxes(w_in, 1, 2)
    tables = {n: _dft_tables(n) for n in (SEQ, DEC_SEQ)}
    zero_state = jnp.zeros((BATCH, GDN_HEADS, GDN_DK, GDN_DV), F32)

    caches = None
    states = None
    h = _norm_mod(x, ln1_g, mod, 0, 0)
    for layer in range(DEPTH):
        proj = _in_proj(h, w_in_t, layer)

        y_na, new_k, new_v = _ctx_attention(proj, layer, caches)
        caches = (new_k, new_v)
        y_na = _na_attention(proj, cache_k, cache_v, na_rpb[layer], y_na, layer)

        y_hy = None
        for n in (SEQ, DEC_SEQ):
            spectrum = _hy_filter_spectrum(n, tables[n][0], hy_w1[layer], hy_b1[layer],
                                           hy_freq[layer], hy_w2[layer], hy_b2[layer], hy_w3[layer],
                                           hy_decay[layer])
            y_hy = _hyena(proj, y_hy, layer, n, tables[n], spectrum,
                          hy_conv_w, hy_conv_b, hy_skip)

        y_gdn, s_f, s_b = _gdn(proj, None, states, layer, SEQ, zero_state, zero_state,
                               gdn_conv_w, gdn_a_log, gdn_dt_bias, gdn_norm_g)
        states = (s_f, s_b)
        y_gdn = _gdn(proj, y_gdn, None, layer, DEC_SEQ, state_fwd[:, layer], state_bwd[:, layer],
                     gdn_conv_w, gdn_a_log, gdn_dt_bias, gdn_norm_g)[0]

        x, h = _mix_out(y_na, y_hy, y_gdn, proj, b_gate, w_pa_b, w_pb_b, w_pc_b, w_out_b, x, mod, ln2_g, layer)
        act = _ffn_up(h, ffn_w_up, ffn_conv_w, ffn_conv_b, layer)
        x, h = _ffn_down(act, w_down_b, x, mod, layer, ln1_g, final_g)

    y_p, y_s = x, h
    cache_shape = (BATCH, DEPTH, SEQ, NA_HEADS, NA_DH)
    return (y_p.reshape(BATCH, SEQ, D), y_s.reshape(DEC_BATCH, DEC_SEQ, D),
            caches[0].reshape(cache_shape), caches[1].reshape(cache_shape), states[0], states[1])
```

```python
import functools
import math

import jax
import jax.numpy as jnp
import numpy as np
from jax import lax
from jax.experimental import pallas as pl
from jax.experimental.pallas import tpu as pltpu

F32 = jnp.float32
BF16 = jnp.bfloat16

D = 2048
BATCH, SEQ = 32, 256
DEC_BATCH, DEC_SEQ = 4, 1024
DEPTH = 2
GRID_W = 64
NA_HEADS, NA_DH = 8, 128
NA_WIDTH = NA_HEADS * NA_DH
NA_WIN_ROWS, NA_WIN_COLS = 8, 16
HY_WIDTH = 512
HY_ORDER = 2
HY_BANDS = 16
GDN_HEADS, GDN_DK, GDN_DV = 4, 128, 128
GDN_WIDTH = GDN_HEADS * GDN_DV
D_FF = 5632
N_BRANCH = 3
NORM_EPS = 1e-6
NEG_INF = -1e30

T_P = BATCH * SEQ
T_S = DEC_BATCH * DEC_SEQ
T = T_P + T_S
N_COND = 8

C_Q, C_K, C_V = 0, NA_WIDTH, 2 * NA_WIDTH
C_HY = 3 * NA_WIDTH
C_GDN = C_HY + 3 * HY_WIDTH
C_Z = C_GDN + 3 * GDN_HEADS * GDN_DK
C_BA = C_Z + GDN_WIDTH
C_GATE = C_BA + 4 * GDN_HEADS
N_IN = C_GATE + N_BRANCH * D
GATE_BLOCK = C_BA
GATE_SHIFT = C_GATE - C_BA

LANE = 128
VMEM_LIMIT = 56 * 1024 * 1024
HIGHEST = lax.Precision.HIGHEST


def _cparams(sem):
    return pltpu.CompilerParams(dimension_semantics=sem, vmem_limit_bytes=VMEM_LIMIT)


def _cond_row(tile, rows_per_tile):
    first_latent = T_P // rows_per_tile
    per_seq = DEC_SEQ // rows_per_tile
    return jnp.where(tile < first_latent, 0, 1 + (tile - first_latent) // per_seq)


def _mod_kernel(c_ref, w_ref, b_ref, o_ref):
    c = c_ref[...]
    s = (c * jax.nn.sigmoid(c)).astype(BF16)
    o_ref[0] = jnp.dot(s, w_ref[0].astype(BF16), preferred_element_type=F32) + b_ref[0]


def _mod_table(cond, w_mod, b_mod):
    tn = 1024
    n = 6 * D
    return pl.pallas_call(
        _mod_kernel,
        out_shape=jax.ShapeDtypeStruct((DEPTH, N_COND, n), F32),
        grid=(DEPTH, n // tn),
        in_specs=[
            pl.BlockSpec((N_COND, D), lambda l, j: (0, 0)),
            pl.BlockSpec((1, D, tn), lambda l, j: (l, 0, j)),
            pl.BlockSpec((1, 1, tn), lambda l, j: (l, 0, j)),
        ],
        out_specs=pl.BlockSpec((1, N_COND, tn), lambda l, j: (l, 0, j)),
        compiler_params=_cparams(("arbitrary", "arbitrary")),
        name="mod_table",
    )(cond, w_mod, b_mod.reshape(DEPTH, 1, n))


def _token_specs(x, tm):
    if not isinstance(x, tuple):
        return [pl.BlockSpec((tm, D), lambda i: (i, 0))], [x]
    n_ctx = T_P // tm
    return ([pl.BlockSpec((tm, D), lambda i: (jnp.minimum(i, n_ctx - 1), 0)),
             pl.BlockSpec((tm, D), lambda i: (jnp.maximum(i - n_ctx, 0), 0))], list(x))


def _token_tile(x_refs, tm):
    if len(x_refs) == 1:
        return x_refs[0][...]
    return jnp.where(pl.program_id(0) < T_P // tm, x_refs[0][...], x_refs[1][...])


def _modulated_norm(x, g, shift, scale):
    y = x * lax.rsqrt(jnp.mean(x * x, axis=-1, keepdims=True) + NORM_EPS) * g
    return (y * (1.0 + scale) + shift).astype(BF16)


NORM_TM = 1024


def _norm_mod_kernel(*refs):
    g_ref, sh_ref, sc_ref, o_ref = refs[-4:]
    o_ref[...] = _modulated_norm(_token_tile(refs[:-4], NORM_TM), g_ref[0], sh_ref[0], sc_ref[0])


def _norm_mod(x, g, mod, layer, shift_chunk):
    tm = NORM_TM
    x_specs, x_args = _token_specs(x, tm)
    return pl.pallas_call(
        _norm_mod_kernel,
        out_shape=jax.ShapeDtypeStruct((T, D), BF16),
        grid=(T // tm,),
        in_specs=x_specs + [
            pl.BlockSpec((1, 1, D), lambda i: (layer, 0, 0)),
            pl.BlockSpec((1, 1, D), lambda i: (layer * N_COND + _cond_row(i, tm), 0, shift_chunk)),
            pl.BlockSpec((1, 1, D), lambda i: (layer * N_COND + _cond_row(i, tm), 0, shift_chunk + 1)),
        ],
        out_specs=pl.BlockSpec((tm, D), lambda i: (i, 0)),
        compiler_params=_cparams(("arbitrary",)),
        name="norm_mod",
    )(*x_args, g.reshape(DEPTH, 1, D), mod, mod)


def _proj_kernel(h_ref, wt_ref, o_ref, wb_ref):
    @pl.when(pl.program_id(1) == 0)
    def _():
        wb_ref[...] = wt_ref[0].astype(BF16)

    o_ref[...] = lax.dot_general(h_ref[...], wb_ref[...], (((1,), (1,)), ((), ())), preferred_element_type=F32)


def _in_proj(h, w_in_t, layer):
    tm, tn = 1536, 1024
    return pl.pallas_call(
        _proj_kernel,
        out_shape=jax.ShapeDtypeStruct((T, N_IN), F32),
        grid=(pl.cdiv(N_IN, tn), T // tm),
        in_specs=[
            pl.BlockSpec((tm, D), lambda j, i: (i, 0)),
            pl.BlockSpec((1, tn, D), lambda j, i: (layer, j, 0)),
        ],
        out_specs=pl.BlockSpec((tm, tn), lambda j, i: (i, j)),
        scratch_shapes=[pltpu.VMEM((tn, D), BF16)],
        compiler_params=_cparams(("arbitrary", "arbitrary")),
        name="in_proj",
    )(h, w_in_t)


MIX_TM = 256


def _mix_out_kernel(yna_ref, yhy_ref, ygdn_ref, gl_ref, bg_ref, wpa_ref, wpb_ref, wpc_ref, wout_ref, *refs):
    gt_ref, g2_ref, sh2_ref, sc2_ref, o_ref, h2_ref = refs[-6:]
    width = gl_ref.shape[1]
    gl = pltpu.roll(gl_ref[...], width - GATE_SHIFT, 1)[:, :N_BRANCH * D] + bg_ref[0]
    gates = jax.nn.sigmoid(gl)
    merged = (gates[:, :D] * jnp.dot(yna_ref[...], wpa_ref[0], preferred_element_type=F32)
              + gates[:, D:2 * D] * jnp.dot(yhy_ref[...], wpb_ref[0], preferred_element_type=F32)
              + gates[:, 2 * D:] * jnp.dot(ygdn_ref[...], wpc_ref[0], preferred_element_type=F32))
    r = jnp.dot(merged.astype(BF16), wout_ref[0], preferred_element_type=F32)
    x = _token_tile(refs[:-6], MIX_TM) + gt_ref[0] * r
    o_ref[...] = x
    h2_ref[...] = _modulated_norm(x, g2_ref[0], sh2_ref[0], sc2_ref[0])


def _mix_out(y_na, y_hy, y_gdn, proj, b_gate, w_pa, w_pb, w_pc, w_out, x, mod, ln2_g, layer):
    tm = MIX_TM
    resident = functools.partial(pl.BlockSpec, pipeline_mode=pl.Buffered(1))
    x_specs, x_args = _token_specs(x, tm)
    mod_spec = lambda chunk: pl.BlockSpec((1, 1, D), lambda i: (layer * N_COND + _cond_row(i, tm), 0, chunk))
    row_spec = pl.BlockSpec((tm, D), lambda i: (i, 0))
    return pl.pallas_call(
        _mix_out_kernel,
        out_shape=[jax.ShapeDtypeStruct((T, D), F32), jax.ShapeDtypeStruct((T, D), BF16)],
        grid=(T // tm,),
        in_specs=[
            pl.BlockSpec((tm, NA_WIDTH), lambda i: (i, 0)),
            pl.BlockSpec((tm, HY_WIDTH), lambda i: (i, 0)),
            pl.BlockSpec((tm, GDN_WIDTH), lambda i: (i, 0)),
            pl.BlockSpec((tm, GATE_BLOCK), lambda i: (i, 1)),
            pl.BlockSpec((1, 1, N_BRANCH * D), lambda i: (layer, 0, 0)),
            resident((1, NA_WIDTH, D), lambda i: (layer, 0, 0)),
            resident((1, HY_WIDTH, D), lambda i: (layer, 0, 0)),
            resident((1, GDN_WIDTH, D), lambda i: (layer, 0, 0)),
            resident((1, D, D), lambda i: (layer, 0, 0)),
        ] + x_specs + [mod_spec(2), pl.BlockSpec((1, 1, D), lambda i: (layer, 0, 0)), mod_spec(3), mod_spec(4)],
        out_specs=[row_spec, row_spec],
        compiler_params=_cparams(("arbitrary",)),
        name="mix_out",
    )(y_na, y_hy, y_gdn, proj, b_gate.reshape(DEPTH, 1, N_BRANCH * D), w_pa, w_pb, w_pc, w_out, *x_args,
      mod, ln2_g.reshape(DEPTH, 1, D), mod, mod)


FFN_TM = 1024
FFN_SLAB = 16


def _ffn_up_kernel(h_ref, wa_ref, wb_ref, cwa_ref, cwb_ref, cba_ref, cbb_ref, o_ref, wab_ref, wbb_ref):
    m = pl.program_id(1)

    @pl.when(m == 0)
    def _():
        wab_ref[...] = wa_ref[0].astype(BF16)
        wbb_ref[...] = wb_ref[0].astype(BF16)

    h = h_ref[...]
    ups = (jnp.dot(h, wab_ref[...], preferred_element_type=F32), jnp.dot(h, wbb_ref[...], preferred_element_type=F32))
    taps = ((cwa_ref[0], cba_ref[0]), (cwb_ref[0], cbb_ref[0]))

    def swiglu(ua, ub):
        return (ua * jax.nn.sigmoid(ua) * ub).astype(BF16)

    full = [pltpu.roll(up, 1, 0) * cw[0:1] + up * cw[1:2] + pltpu.roll(up, FFN_TM - 1, 0) * cw[2:3] + cb
            for up, (cw, cb) in zip(ups, taps)]
    o_ref[...] = swiglu(*full)

    def redo_edge(r, starts):
        lo = r // FFN_SLAB * FFN_SLAB
        pick = lax.broadcasted_iota(jnp.int32, (FFN_SLAB, 1), 0) == r - lo
        halves = []
        for up, conv, (cw, cb) in zip(ups, full, taps):
            row = up[r:r + 1] * cw[1:2] + cb
            row = row + (up[r + 1:r + 2] * cw[2:3] if starts else up[r - 1:r] * cw[0:1])
            halves.append(jnp.where(pick, row, conv[lo:lo + FFN_SLAB]))
        o_ref[lo:lo + FFN_SLAB, :] = swiglu(*halves)

    redo_edge(0, True)
    redo_edge(FFN_TM - 1, False)

    @pl.when(m < T_P // FFN_TM)
    def _():
        for r in range(SEQ, FFN_TM, SEQ):
            redo_edge(r - 1, False)
            redo_edge(r, True)


def _ffn_up(h, w_up, conv_w, conv_b, layer):
    tn = 512
    nt = D_FF // tn
    conv_b = conv_b.reshape(DEPTH, 1, 2 * D_FF)
    return pl.pallas_call(
        _ffn_up_kernel,
        out_shape=jax.ShapeDtypeStruct((T, D_FF), BF16),
        grid=(nt, T // FFN_TM),
        in_specs=[
            pl.BlockSpec((FFN_TM, D), lambda j, i: (i, 0)),
            pl.BlockSpec((1, D, tn), lambda j, i: (layer, 0, j)),
            pl.BlockSpec((1, D, tn), lambda j, i: (layer, 0, nt + j)),
            pl.BlockSpec((1, 3, tn), lambda j, i: (layer, 0, j)),
            pl.BlockSpec((1, 3, tn), lambda j, i: (layer, 0, nt + j)),
            pl.BlockSpec((1, 1, tn), lambda j, i: (layer, 0, j)),
            pl.BlockSpec((1, 1, tn), lambda j, i: (layer, 0, nt + j)),
        ],
        out_specs=pl.BlockSpec((FFN_TM, tn), lambda j, i: (i, j)),
        scratch_shapes=[pltpu.VMEM((D, tn), BF16), pltpu.VMEM((D, tn), BF16)],
        compiler_params=_cparams(("arbitrary", "arbitrary")),
        name="ffn_up",
    )(h, w_up, w_up, conv_w, conv_w, conv_b, conv_b)


DOWN_TM = 256


def _ffn_down_kernel(a_ref, w_ref, x_ref, gt_ref, g_ref, sh_ref, sc_ref, o_ref, h_ref):
    x = x_ref[...] + gt_ref[0] * jnp.dot(a_ref[...], w_ref[0], preferred_element_type=F32)
    o_ref[...] = x
    h_ref[...] = _modulated_norm(x, g_ref[0], sh_ref[0], sc_ref[0])


def _ffn_down_final_kernel(a_ref, w_ref, x_ref, gt_ref, g_ref, yp_ref, ys_ref):
    x = x_ref[...] + gt_ref[0] * jnp.dot(a_ref[...], w_ref[0], preferred_element_type=F32)
    y = x * lax.rsqrt(jnp.mean(x * x, axis=-1, keepdims=True) + NORM_EPS) * g_ref[...]
    is_context = pl.program_id(0) < T_P // DOWN_TM

    @pl.when(is_context)
    def _():
        yp_ref[...] = y

    @pl.when(jnp.logical_not(is_context))
    def _():
        ys_ref[...] = y


def _ffn_down(act, w_down, x, mod, layer, ln1_g, final_g):
    tm = DOWN_TM
    last = layer == DEPTH - 1
    mod_spec = lambda lyr, chunk: pl.BlockSpec((1, 1, D), lambda i: (lyr * N_COND + _cond_row(i, tm), 0, chunk))
    row_spec = pl.BlockSpec((tm, D), lambda i: (i, 0))
    in_specs = [
        pl.BlockSpec((tm, D_FF), lambda i: (i, 0)),
        pl.BlockSpec((1, D_FF, D), lambda i: (layer, 0, 0), pipeline_mode=pl.Buffered(1)),
        row_spec,
        mod_spec(layer, 5),
    ]
    if last:
        n_ctx = T_P // tm
        return pl.pallas_call(
            _ffn_down_final_kernel,
            out_shape=[jax.ShapeDtypeStruct((T_P, D), F32), jax.ShapeDtypeStruct((T_S, D), F32)],
            grid=(T // tm,),
            in_specs=in_specs + [pl.BlockSpec((1, D), lambda i: (0, 0))],
            out_specs=[pl.BlockSpec((tm, D), lambda i: (jnp.minimum(i, n_ctx - 1), 0)),
                       pl.BlockSpec((tm, D), lambda i: (jnp.maximum(i - n_ctx, 0), 0))],
            compiler_params=_cparams(("arbitrary",)),
            name="ffn_down_final",
        )(act, w_down, x, mod, final_g.reshape(1, D))
    return pl.pallas_call(
        _ffn_down_kernel,
        out_shape=[jax.ShapeDtypeStruct((T, D), F32), jax.ShapeDtypeStruct((T, D), BF16)],
        grid=(T // tm,),
        in_specs=in_specs + [pl.BlockSpec((1, 1, D), lambda i: (layer + 1, 0, 0)),
                             mod_spec(layer + 1, 0), mod_spec(layer + 1, 1)],
        out_specs=[row_spec, row_spec],
        compiler_params=_cparams(("arbitrary",)),
        name="ffn_down",
    )(act, w_down, x, mod, ln1_g.reshape(DEPTH, 1, D), mod, mod)


ATT_SCALE = NA_DH ** -0.5


def _nt_dot(a, b):
    return lax.dot_general(a, b, (((1,), (1,)), ((), ())), preferred_element_type=F32)


def _head_slices():
    return [slice(h * NA_DH, (h + 1) * NA_DH) for h in range(NA_HEADS)]


def _ctx_attn_kernel(q_ref, k_ref, v_ref, o_ref, ko_ref, vo_ref):
    heads = _head_slices()
    q = [(q_ref[:, sl] * ATT_SCALE).astype(BF16) for sl in heads]
    s = [_nt_dot(q[h], k_ref[:, sl].astype(BF16)) for h, sl in enumerate(heads)]
    e = [x - jnp.max(x, axis=-1, keepdims=True) for x in s]
    e = [jnp.exp(x) for x in e]
    o = [jnp.dot(e[h].astype(BF16), v_ref[:, sl].astype(BF16), preferred_element_type=F32)
         for h, sl in enumerate(heads)]
    for h, sl in enumerate(heads):
        o_ref[:, sl] = (o[h] / jnp.sum(e[h], axis=-1, keepdims=True)).astype(BF16)
    ko_ref[0, 0] = k_ref[...]
    vo_ref[0, 0] = v_ref[...]


def _ctx_attention(proj, layer, caches):
    cache_shape = jax.ShapeDtypeStruct((BATCH, DEPTH, SEQ, NA_WIDTH), F32)
    cache_spec = pl.BlockSpec((1, 1, SEQ, NA_WIDTH), lambda b: (b, layer, 0, 0))
    qkv_specs = [pl.BlockSpec((SEQ, NA_WIDTH), lambda b, j=j: (b, j)) for j in range(3)]
    out_shape = [jax.ShapeDtypeStruct((T, NA_WIDTH), BF16), cache_shape, cache_shape]
    out_specs = [pl.BlockSpec((SEQ, NA_WIDTH), lambda b: (b, 0)), cache_spec, cache_spec]
    if caches is None:
        return pl.pallas_call(
            _ctx_attn_kernel, out_shape=out_shape, grid=(BATCH,), in_specs=qkv_specs, out_specs=out_specs,
            compiler_params=_cparams(("arbitrary",)), name="ctx_attention",
        )(proj, proj, proj)

    def body(q_ref, k_ref, v_ref, kc_ref, vc_ref, o_ref, ko_ref, vo_ref):
        del kc_ref, vc_ref
        _ctx_attn_kernel(q_ref, k_ref, v_ref, o_ref, ko_ref, vo_ref)

    any_spec = pl.BlockSpec(memory_space=pl.ANY)
    return pl.pallas_call(
        body, out_shape=out_shape, grid=(BATCH,), in_specs=qkv_specs + [any_spec, any_spec],
        out_specs=out_specs, input_output_aliases={3: 1, 4: 2},
        compiler_params=_cparams(("arbitrary",)), name="ctx_attention",
    )(proj, proj, proj, *caches)


NA_ROWS = DEC_SEQ // GRID_W
NA_WIN_TOK = NA_WIN_ROWS * GRID_W
NA_ROWS_PER_STEP = 4


def _na_window_start(r):
    return jnp.clip(r - NA_WIN_ROWS // 2, 0, NA_ROWS - NA_WIN_ROWS)


def _na_bias_table(rpb):
    col = np.arange(GRID_W)
    col_start = np.clip(col - NA_WIN_COLS // 2, 0, GRID_W - NA_WIN_COLS)
    col_mask = (col[None, :] >= col_start[:, None]) & (col[None, :] < col_start[:, None] + NA_WIN_COLS)
    rel_col = np.clip(col[None, :] - col[:, None] + NA_WIN_COLS - 1, 0, 2 * NA_WIN_COLS - 2)
    n_rel = 2 * NA_WIN_COLS - 1
    onehot = (rel_col[None] == np.arange(n_rel)[:, None, None]).astype(np.float32).reshape(n_rel, -1)
    expanded = jnp.dot(rpb.reshape(-1, n_rel), onehot, precision=HIGHEST)
    banded = jnp.where(col_mask, expanded.reshape(NA_HEADS, 2 * NA_WIN_ROWS - 1, GRID_W, GRID_W), NEG_INF)
    return jnp.concatenate([banded[:, :-1], banded[:, 1:]], axis=-1)


def _na_attn_kernel(q_ref, k_ref, v_ref, ck_ref, cv_ref, bias_ref, y_in_ref, o_ref):
    del y_in_ref
    heads = _head_slices()
    units = [(a, h) for a in range(NA_ROWS_PER_STEP) for h in range(NA_HEADS)]
    nu = range(len(units))
    r = [pl.program_id(1) * NA_ROWS_PER_STEP + a for a in range(NA_ROWS_PER_STEP)]
    win = [pl.ds(pl.multiple_of(_na_window_start(ra) * GRID_W, GRID_W), NA_WIN_TOK) for ra in r]
    rel0 = [NA_WIN_ROWS - 1 - (ra - _na_window_start(ra)) for ra in r]
    qrows = [slice(a * GRID_W, (a + 1) * GRID_W) for a in range(NA_ROWS_PER_STEP)]
    q = [(q_ref[:, sl] * ATT_SCALE).astype(BF16) for sl in heads]
    bias = [jnp.concatenate([bias_ref[h, rel0[a] + 2 * p] for p in range(NA_WIN_ROWS // 2)], axis=1)
            for a, h in units]
    s_win = [_nt_dot(q[h][qrows[a]], k_ref[win[a], heads[h]].astype(BF16)) + bias[i] for i, (a, h) in enumerate(units)]
    s_ctx_all = [_nt_dot(q[h], ck_ref[0, 0, :, sl].astype(BF16)) for h, sl in enumerate(heads)]
    s_ctx = [s_ctx_all[h][qrows[a]] for a, h in units]
    m = [jnp.maximum(jnp.max(s_win[i], axis=-1, keepdims=True), jnp.max(s_ctx[i], axis=-1, keepdims=True))
         for i in nu]
    e_win = [jnp.exp(s_win[i] - m[i]) for i in nu]
    e_ctx = [jnp.exp(s_ctx[i] - m[i]) for i in nu]
    o_win = [jnp.dot(e_win[i].astype(BF16), v_ref[win[a], heads[h]].astype(BF16), preferred_element_type=F32)
             for i, (a, h) in enumerate(units)]
    e_ctx_all = [jnp.concatenate([e_ctx[a * NA_HEADS + h] for a in range(NA_ROWS_PER_STEP)], axis=0).astype(BF16)
                 for h in range(NA_HEADS)]
    o_ctx_all = [jnp.dot(e_ctx_all[h], cv_ref[0, 0, :, sl].astype(BF16), preferred_element_type=F32)
                 for h, sl in enumerate(heads)]
    for i, (a, h) in enumerate(units):
        denom = jnp.sum(e_win[i], axis=-1, keepdims=True) + jnp.sum(e_ctx[i], axis=-1, keepdims=True)
        o_ref[qrows[a], heads[h]] = ((o_win[i] + o_ctx_all[h][qrows[a]]) / denom).astype(BF16)


def _na_attention(proj, cache_k, cache_v, rpb, y_na, layer):
    q_rows = NA_ROWS_PER_STEP * GRID_W
    steps = NA_ROWS // NA_ROWS_PER_STEP
    q_row0 = T_P // q_rows
    seq0 = T_P // DEC_SEQ
    kv_spec = lambda j: pl.BlockSpec((DEC_SEQ, NA_WIDTH), lambda b, r: (seq0 + b, j))
    ctx_spec = pl.BlockSpec((1, 1, SEQ, NA_WIDTH), lambda b, r: (b, layer, 0, 0))
    return pl.pallas_call(
        _na_attn_kernel,
        out_shape=jax.ShapeDtypeStruct((T, NA_WIDTH), BF16),
        grid=(DEC_BATCH, steps),
        in_specs=[
            pl.BlockSpec((q_rows, NA_WIDTH), lambda b, r: (q_row0 + b * steps + r, 0)),
            kv_spec(1), kv_spec(2), ctx_spec, ctx_spec,
            pl.BlockSpec((NA_HEADS, 2 * NA_WIN_ROWS - 2, GRID_W, 2 * GRID_W), lambda b, r: (0, 0, 0, 0),
                         pipeline_mode=pl.Buffered(1)),
            pl.BlockSpec(memory_space=pl.ANY),
        ],
        out_specs=pl.BlockSpec((q_rows, NA_WIDTH), lambda b, r: (q_row0 + b * steps + r, 0)),
        input_output_aliases={6: 0},
        compiler_params=_cparams(("arbitrary", "arbitrary")),
        name="na_attention",
    )(proj, proj, proj, cache_k.reshape(DEC_BATCH, DEPTH, SEQ, NA_WIDTH),
      cache_v.reshape(DEC_BATCH, DEPTH, SEQ, NA_WIDTH), _na_bias_table(rpb), y_na)


HY_CB = 256


def _dft_forward_kernel(o_ref, *, length):
    rows, cols = o_ref.shape
    f = lax.broadcasted_iota(jnp.int32, (rows, cols), 0) + pl.program_id(0) * rows
    t = lax.broadcasted_iota(jnp.int32, (rows, cols), 1)
    k = f & (length - 1)
    phase = (2 * k + 1) * t - jnp.where(f >= length, length, 0)
    phase = phase & (4 * length - 1)
    phase = jnp.where(phase >= 2 * length, phase - 4 * length, phase)
    o_ref[...] = jnp.cos(phase.astype(F32) * (math.pi / (2 * length))).astype(BF16)


def _dft_tables(length):
    tr = 256
    fwd = pl.pallas_call(
        functools.partial(_dft_forward_kernel, length=length),
        out_shape=jax.ShapeDtypeStruct((2 * length, length), BF16),
        grid=(2 * length // tr,), out_specs=pl.BlockSpec((tr, length), lambda i: (i, 0)),
        compiler_params=_cparams(("arbitrary",)), name="dft_table",
    )()
    inv = jnp.concatenate([fwd[:length].T, -fwd[length:].T], axis=1) * (1.0 / length)
    return fwd, inv


def _hy_filter_kernel(tn_ref, band_ref, w1_ref, b1_ref, fr_ref, w2_ref, b2_ref, w3_ref, dec_ref,
                      f_ref, o_ref, *, length):
    t_norm = tn_ref[...]
    t_idx = lax.broadcasted_iota(jnp.int32, (length, LANE), 0).astype(F32)
    lane = lax.broadcasted_iota(jnp.int32, (length, LANE), 1)
    ang = (2.0 * math.pi / length) * t_idx * band_ref[...]
    z = jnp.where(lane == 0, t_norm,
                  jnp.where(lane <= HY_BANDS, jnp.cos(ang), jnp.where(lane <= 2 * HY_BANDS, jnp.sin(ang), 0.0)))
    hdn = jnp.sin(fr_ref[0:1] * (jnp.dot(z, w1_ref[...], precision=HIGHEST, preferred_element_type=F32)
                                 + b1_ref[...]))
    hdn = jnp.sin(fr_ref[1:2] * (jnp.dot(hdn, w2_ref[...], precision=HIGHEST, preferred_element_type=F32)
                                 + b2_ref[...]))
    filt = jnp.dot(hdn, w3_ref[...], precision=HIGHEST, preferred_element_type=F32)
    filt = filt * jnp.exp(-t_norm * jnp.abs(dec_ref[...]))
    first = lax.broadcasted_iota(jnp.int32, (length, HY_WIDTH), 0) == 0
    for o in range(HY_ORDER):
        fwd = filt[:, (2 * o) * HY_WIDTH:(2 * o + 1) * HY_WIDTH]
        bwd = filt[:, (2 * o + 1) * HY_WIDTH:(2 * o + 2) * HY_WIDTH]
        bwd = jnp.where(first, 0.0, pltpu.roll(bwd, 1, 0))
        o_ref[o, :length] = jnp.dot(f_ref[:length], (fwd + bwd).astype(BF16), preferred_element_type=F32)
        o_ref[o, length:] = jnp.dot(f_ref[length:], (bwd - fwd).astype(BF16), preferred_element_type=F32)


def _hy_filter_spectrum(length, f_tab, w1, b1, freq, w2, b2, w3, decay):
    emb = 1 + 2 * HY_BANDS
    t_norm = jnp.linspace(0.0, 1.0, length, dtype=F32).reshape(length, 1)
    bands = np.zeros((1, LANE), np.float32)
    bands[0, 1:1 + HY_BANDS] = bands[0, 1 + HY_BANDS:emb] = np.linspace(1e-4, HY_BANDS - 1, HY_BANDS,
                                                                        dtype=np.float32)
    w1p = jnp.zeros((LANE, w1.shape[1]), F32).at[:emb].set(w1)
    return pl.pallas_call(
        functools.partial(_hy_filter_kernel, length=length),
        out_shape=jax.ShapeDtypeStruct((HY_ORDER, 2 * length, HY_WIDTH), F32),
        compiler_params=pltpu.CompilerParams(vmem_limit_bytes=VMEM_LIMIT),
        name="hyena_filter",
    )(t_norm, jnp.asarray(bands), w1p, b1.reshape(1, -1), freq, w2, b2.reshape(1, -1), w3,
      decay.reshape(1, -1), f_tab)


def _hyena_kernel(v_ref, x1_ref, x2_ref, cwv_ref, cw1_ref, cw2_ref, cbv_ref, cb1_ref, cb2_ref, skip_ref,
                  hs_ref, f_ref, g_ref, *rest, length, parts):
    o_ref = rest[-1]
    n = range(len(parts))

    def short_conv(u_ref, cw_ref, cb_ref, rs, cs):
        u = u_ref[rs, cs]
        row = lax.broadcasted_iota(jnp.int32, u.shape, 0)
        prev = jnp.where(row == 0, 0.0, pltpu.roll(u, 1, 0))
        nxt = jnp.where(row == length - 1, 0.0, pltpu.roll(u, length - 1, 0))
        return prev * cw_ref[0:1, cs] + u * cw_ref[1:2, cs] + nxt * cw_ref[2:3, cs] + cb_ref[:, cs]

    z = [short_conv(v_ref, cwv_ref, cbv_ref, rs, cs) for rs, cs in parts]
    gates = [(short_conv(x1_ref, cw1_ref, cb1_ref, rs, cs), short_conv(x2_ref, cw2_ref, cb2_ref, rs, cs))
             for rs, cs in parts]
    for o in range(HY_ORDER):
        zf = [jnp.dot(f_ref[...], z[i].astype(BF16), preferred_element_type=F32) for i in n]
        p = []
        for i, (rs, cs) in enumerate(parts):
            zc, zs = zf[i][:length], zf[i][length:]
            h_re, h_im = hs_ref[o, :length, cs], hs_ref[o, length:, cs]
            p.append(jnp.concatenate([zc * h_re + zs * h_im, zc * h_im - zs * h_re], axis=0).astype(BF16))
        conv = [jnp.dot(g_ref[...], p[i], preferred_element_type=F32) for i in n]
        z = [gates[i][o] * (conv[i] + z[i] * skip_ref[o:o + 1, cs]) for i, (rs, cs) in enumerate(parts)]
    for i, (rs, cs) in enumerate(parts):
        o_ref[rs, cs] = z[i].astype(BF16)


def _hyena(proj, y_hy, layer, length, tables, spectrum, conv_w, conv_b, skip):
    seqs = 2 if length == SEQ else 1
    rows = seqs * length
    nblk = (T_P if length == SEQ else T_S) // rows
    row0 = 0 if length == SEQ else T_P // rows
    if seqs > 1:
        parts = [(slice(b * length, (b + 1) * length), slice(None)) for b in range(seqs)]
    else:
        parts = [(slice(None), slice(c * HY_CB, (c + 1) * HY_CB)) for c in range(HY_WIDTH // HY_CB)]
    col0 = C_HY // HY_WIDTH
    u_spec = lambda part: pl.BlockSpec((rows, HY_WIDTH), lambda s: (row0 + s, col0 + part))
    cw_spec = lambda part: pl.BlockSpec((1, 3, HY_WIDTH), lambda s: (layer, 0, part))
    cb_spec = lambda part: pl.BlockSpec((1, 1, HY_WIDTH), lambda s: (layer, 0, part))
    resident = functools.partial(pl.BlockSpec, pipeline_mode=pl.Buffered(1))
    tab_specs = [resident(t.shape, lambda s: (0, 0)) for t in tables]
    in_specs = ([u_spec(0), u_spec(1), u_spec(2), cw_spec(0), cw_spec(1), cw_spec(2),
                 cb_spec(0), cb_spec(1), cb_spec(2),
                 pl.BlockSpec((1, HY_ORDER, HY_WIDTH), lambda s: (layer, 0, 0)),
                 resident((HY_ORDER, 2 * length, HY_WIDTH), lambda s: (0, 0, 0))] + tab_specs)
    args = [proj] * 3 + [conv_w] * 3 + [conv_b.reshape(DEPTH, 1, -1)] * 3 + [skip, spectrum] + list(tables)
    aliases = {}
    if y_hy is not None:
        in_specs.append(pl.BlockSpec(memory_space=pl.ANY))
        args.append(y_hy)
        aliases = {len(args) - 1: 0}

    def body(v_ref, x1_ref, x2_ref, cwv, cw1, cw2, cbv, cb1, cb2, skip_ref, hs_ref, *rest):
        _hyena_kernel(v_ref, x1_ref, x2_ref, cwv.at[0], cw1.at[0], cw2.at[0], cbv.at[0], cb1.at[0], cb2.at[0],
                      skip_ref.at[0], hs_ref, *rest, length=length, parts=parts)

    return pl.pallas_call(
        body,
        out_shape=jax.ShapeDtypeStruct((T, HY_WIDTH), BF16),
        grid=(nblk,), in_specs=in_specs,
        out_specs=pl.BlockSpec((rows, HY_WIDTH), lambda s: (row0 + s, 0)),
        input_output_aliases=aliases,
        compiler_params=_cparams(("arbitrary",)),
        name="hyena",
    )(*args)


GC = 128
GDN_SCALE = GDN_DK ** -0.5
GDN_CTX_PER_STEP = 2


def _state_slot(b, d):
    return (2 * b + d) * GDN_HEADS


def _merge_masks(lower):
    ri = lax.broadcasted_iota(jnp.int32, (GC, GC), 0)
    ci = lax.broadcasted_iota(jnp.int32, (GC, GC), 1)
    hi, lo = (ri, ci) if lower else (ci, ri)
    return [((hi >> (b + 1)) == (lo >> (b + 1))) & ((hi >> b) > (lo >> b)) for b in range(int(math.log2(GC)))]


def _unit_tri_inverse(a, masks):
    ri = lax.broadcasted_iota(jnp.int32, (GC, GC), 0)
    ci = lax.broadcasted_iota(jnp.int32, (GC, GC), 1)
    eye = (ri == ci).astype(F32)
    n = range(len(a))
    t = [eye - jnp.where(masks[i][0], a[i], 0.0) for i in n]
    for level in range(1, len(masks[0])):
        tb = [t[i].astype(BF16) for i in n]
        ta = [jnp.dot(tb[i], jnp.where(masks[i][level], a[i], 0.0).astype(BF16), preferred_element_type=F32)
              for i in n]
        tat = [jnp.dot(ta[i].astype(BF16), tb[i], preferred_element_type=F32) for i in n]
        t = [t[i] - tat[i] for i in n]
    return t


def _gdn_kernel(q_ref, k_ref, v_ref, z_ref, ba_ref, cwq_ref, cwk_ref, cwv_ref, alog_ref, dt_ref, ng_ref,
                sf0_ref, sb0_ref, *rest, length, nb):
    y_ref, sf_ref, sb_ref, qn_ref, kn_ref, vn_ref, beta_ref, g_ref, of_ref, ob_ref, s_ref = rest[-11:]
    n_chunks = length // GC
    rows_total = nb * length
    row = lax.broadcasted_iota(jnp.int32, (rows_total, GDN_WIDTH), 0) & (length - 1)
    first, last = row == 0, row == length - 1

    def conv_silu(u_ref, cw_ref):
        u = u_ref[...]
        prev = jnp.where(first, 0.0, pltpu.roll(u, 1, 0))
        nxt = jnp.where(last, 0.0, pltpu.roll(u, rows_total - 1, 0))
        c = prev * cw_ref[0:1] + u * cw_ref[1:2] + nxt * cw_ref[2:3]
        return c * jax.nn.sigmoid(c)

    q = conv_silu(q_ref, cwq_ref)
    k = conv_silu(k_ref, cwk_ref)
    vn_ref[...] = conv_silu(v_ref, cwv_ref)
    for h in range(GDN_HEADS):
        sl = slice(h * GDN_DK, (h + 1) * GDN_DK)
        qh, kh = q[:, sl], k[:, sl]
        qn_ref[:, sl] = qh * lax.rsqrt(jnp.sum(qh * qh, axis=-1, keepdims=True) + NORM_EPS) * GDN_SCALE
        kn_ref[:, sl] = kh * lax.rsqrt(jnp.sum(kh * kh, axis=-1, keepdims=True) + NORM_EPS)
    ba = ba_ref[...]
    beta_ref[...] = jax.nn.sigmoid(ba)
    g_ref[...] = -jnp.exp(alog_ref[...]) * jax.nn.softplus(ba + dt_ref[...])

    ri = lax.broadcasted_iota(jnp.int32, (GC, GC), 0)
    ci = lax.broadcasted_iota(jnp.int32, (GC, GC), 1)
    lower = (ri >= ci).astype(F32)
    upper = (ri <= ci).astype(F32)

    def chunk_step(n, carry):
        scans = [(b, d) for b in range(nb) for d in range(2)]
        rows = {(b, d): pl.ds(pl.multiple_of(b * length + (n if d == 0 else n_chunks - 1 - n) * GC, GC), GC)
                for b, d in scans}
        loaded = {}
        for bd in scans:
            loaded[bd] = (g_ref[rows[bd], :], beta_ref[rows[bd], :])
            for h in range(GDN_HEADS):
                sl = slice(h * GDN_DK, (h + 1) * GDN_DK)
                loaded[bd + (h,)] = (qn_ref[rows[bd], sl], kn_ref[rows[bd], sl], vn_ref[rows[bd], sl],
                                     s_ref[_state_slot(*bd) + h])
        units = [(b, d, h) for b, d in scans for h in range(GDN_HEADS)]
        incl = [(ri >= ci), (ri <= ci)]
        strict = [(ri > ci), (ri < ci)]
        masks = [_merge_masks(True), _merge_masks(False)]
        gcs = {(b, d): jnp.dot((lower, upper)[d], loaded[b, d][0], precision=HIGHEST, preferred_element_type=F32)
               for b, d in scans}
        gcs_t = {bd: g.T for bd, g in gcs.items()}
        edge = [GC - 1, 0]
        g_col = [gcs[b, d][:, 8 + 4 * d + h:9 + 4 * d + h] for b, d, h in units]
        g_row = [gcs_t[b, d][8 + 4 * d + h:9 + 4 * d + h, :] for b, d, h in units]
        g_end = [gcs_t[b, d][8 + 4 * d + h:9 + 4 * d + h, edge[d]:edge[d] + 1] for b, d, h in units]
        b_col = [loaded[b, d][1][:, 4 * d + h:4 * d + h + 1] for b, d, h in units]
        qc, kc, vc, st = (list(x) for x in zip(*(loaded[u] for u in units)))
        nu = range(len(units))
        decay = [jnp.where(incl[d], jnp.exp(jnp.where(incl[d], g_col[i] - g_row[i], 0.0)), 0.0)
                 for i, (b, d, h) in enumerate(units)]
        kb = [kc[i] * b_col[i] for i in nu]
        kcb = [kc[i].astype(BF16) for i in nu]
        kq = [_nt_dot(jnp.concatenate([kb[i], qc[i]], axis=0).astype(BF16), kcb[i]) for i in nu]
        kk = [x[:GC] for x in kq]
        qk = [x[GC:] for x in kq]
        a = [jnp.where(strict[d], kk[i] * decay[i], 0.0) for i, (b, d, h) in enumerate(units)]
        t = _unit_tri_inverse(a, [masks[d] for b, d, h in units])
        e_col = [jnp.exp(g_col[i]) for i in nu]
        rhs = [jnp.concatenate([vc[i] * b_col[i], kb[i] * e_col[i]], axis=1).astype(BF16) for i in nu]
        sol = [jnp.dot(t[i].astype(BF16), rhs[i], preferred_element_type=F32) for i in nu]
        attn = [jnp.where(incl[d], qk[i] * decay[i], 0.0).astype(BF16) for i, (b, d, h) in enumerate(units)]
        sb = [st[i].astype(BF16) for i in nu]
        wq = [jnp.dot(jnp.concatenate([sol[i][:, GDN_DV:], qc[i] * e_col[i]], axis=0).astype(BF16), sb[i],
                      preferred_element_type=F32) for i in nu]
        ws = [x[:GC] for x in wq]
        qs = [x[GC:] for x in wq]
        v_new = [(sol[i][:, :GDN_DV] - ws[i]).astype(BF16) for i in nu]
        k_dec_t = [(kc[i] * jnp.exp(g_end[i] - g_col[i])).T.astype(BF16) for i in nu]
        akv = [jnp.dot(jnp.concatenate([attn[i], k_dec_t[i]], axis=0), v_new[i], preferred_element_type=F32)
               for i in nu]
        av = [x[:GC] for x in akv]
        kv = [x[GC:] for x in akv]
        results = [(qs[i] + av[i], st[i] * jnp.exp(g_end[i]) + kv[i]) for i in nu]
        for (b, d, h), (o, s) in zip(units, results):
            (of_ref if d == 0 else ob_ref)[rows[b, d], h * GDN_DV:(h + 1) * GDN_DV] = o
            s_ref[_state_slot(b, d) + h] = s
        return carry

    for b in range(nb):
        s_ref[pl.ds(_state_slot(b, 0), GDN_HEADS)] = sf0_ref[b]
        s_ref[pl.ds(_state_slot(b, 1), GDN_HEADS)] = sb0_ref[b]
    lax.fori_loop(0, n_chunks, chunk_step, 0)
    for b in range(nb):
        sf_ref[b, 0] = s_ref[pl.ds(_state_slot(b, 0), GDN_HEADS)]
        sb_ref[b, 0] = s_ref[pl.ds(_state_slot(b, 1), GDN_HEADS)]
    for h in range(GDN_HEADS):
        sl = slice(h * GDN_DV, (h + 1) * GDN_DV)
        o = of_ref[:, sl] + ob_ref[:, sl]
        o = o * lax.rsqrt(jnp.mean(o * o, axis=-1, keepdims=True) + NORM_EPS) * ng_ref[...]
        zh = z_ref[:, sl]
        y_ref[:, sl] = (o * (zh * jax.nn.sigmoid(zh))).astype(BF16)


def _gdn(proj, y_gdn, states_out, layer, length, s_f0, s_b0, conv_w, a_log, dt_bias, norm_g):
    nb = GDN_CTX_PER_STEP if length == SEQ else 1
    rows = nb * length
    nblk = (T_P if length == SEQ else T_S) // rows
    row0 = 0 if length == SEQ else T_P // rows
    col = lambda c: c // GDN_WIDTH
    blk = lambda c: pl.BlockSpec((rows, GDN_WIDTH), lambda s: (row0 + s, col(c)))
    cw = lambda part: pl.BlockSpec((1, 3, GDN_WIDTH), lambda s: (layer, 0, part))
    vec = lambda: pl.BlockSpec((1, 1, LANE), lambda s: (layer, 0, 0))
    st_in = pl.BlockSpec((nb, GDN_HEADS, GDN_DK, GDN_DV), lambda s: (s, 0, 0, 0))
    st_out = pl.BlockSpec((nb, 1, GDN_HEADS, GDN_DK, GDN_DV), lambda s: (s, layer, 0, 0, 0))
    pad_lanes = lambda x, off: jnp.zeros((DEPTH, 1, LANE), F32).at[:, 0, off:off + 2 * GDN_HEADS].set(
        x.reshape(DEPTH, -1))
    in_specs = [blk(C_GDN), blk(C_GDN + GDN_WIDTH), blk(C_GDN + 2 * GDN_WIDTH), blk(C_Z),
                pl.BlockSpec((rows, LANE), lambda s: (row0 + s, C_BA // LANE)),
                cw(0), cw(1), cw(2), vec(), vec(), vec(), st_in, st_in]
    args = [proj] * 5 + [conv_w] * 3 + [pad_lanes(a_log, 2 * GDN_HEADS), pad_lanes(dt_bias, 2 * GDN_HEADS),
                                        norm_g.reshape(DEPTH, 1, GDN_DV), s_f0, s_b0]
    st_shape = jax.ShapeDtypeStruct((nblk * nb, DEPTH, GDN_HEADS, GDN_DK, GDN_DV), F32)
    aliases = {}
    for carried, out_idx in ((y_gdn, 0),) + (((states_out[0], 1), (states_out[1], 2)) if states_out else ()):
        if carried is not None:
            in_specs.append(pl.BlockSpec(memory_space=pl.ANY))
            args.append(carried)
            aliases[len(args) - 1] = out_idx

    def body(q_ref, k_ref, v_ref, z_ref, ba_ref, cwq, cwk, cwv, alog, dt, ng, sf0, sb0, *rest):
        _gdn_kernel(q_ref, k_ref, v_ref, z_ref, ba_ref, cwq.at[0], cwk.at[0], cwv.at[0], alog.at[0], dt.at[0],
                    ng.at[0], sf0, sb0, *rest, length=length, nb=nb)

    seq_buf = lambda w: pltpu.VMEM((rows, w), F32)
    return pl.pallas_call(
        body,
        out_shape=[jax.ShapeDtypeStruct((T, GDN_WIDTH), BF16), st_shape, st_shape],
        grid=(nblk,), in_specs=in_specs,
        out_specs=[pl.BlockSpec((rows, GDN_WIDTH), lambda s: (row0 + s, 0)), st_out, st_out],
        scratch_shapes=[seq_buf(GDN_WIDTH), seq_buf(GDN_WIDTH), seq_buf(GDN_WIDTH), seq_buf(LANE), seq_buf(LANE),
                        seq_buf(GDN_WIDTH), seq_buf(GDN_WIDTH),
                        pltpu.VMEM((nb * 2 * GDN_HEADS, GDN_DK, GDN_DV), F32)],
        input_output_aliases=aliases,
        compiler_params=_cparams(("arbitrary",)),
        name="gated_deltanet",
    )(*args)


def kernel(x_prompt, x_sample, cache_k, cache_v, state_fwd, state_bwd, c, c_ctx, ln1_g, ln2_g, w_mod, b_mod,
           w_in, na_rpb, hy_conv_w, hy_conv_b, hy_w1, hy_b1, hy_freq, hy_w2, hy_b2, hy_w3, hy_decay, hy_skip,
           gdn_conv_w, gdn_a_log, gdn_dt_bias, gdn_norm_g, w_pa, w_pb, w_pc, b_gate, w_out, ffn_w_up,
           ffn_conv_w, ffn_conv_b, ffn_w_down, final_g):
    x = (x_prompt.reshape(T_P, D), x_sample.reshape(T_S, D))
    cond = jnp.concatenate([c_ctx[None], c, jnp.zeros((N_COND - 1 - DEC_BATCH, D), F32)], axis=0)
    mod = _mod_table(cond, w_mod, b_mod).reshape(DEPTH * N_COND, 1, 6 * D)

    w_pa_b, w_pb_b, w_pc_b, w_out_b = (w.astype(BF16) for w in (w_pa, w_pb, w_pc, w_out))
    w_down_b = ffn_w_down.astype(BF16)
    w_in_t = jnp.swapaxes(w_in, 1, 2)
    tables = {n: _dft_tables(n) for n in (SEQ, DEC_SEQ)}
    zero_state = jnp.zeros((BATCH, GDN_HEADS, GDN_DK, GDN_DV), F32)

    caches = None
    states = None
    h = _norm_mod(x, ln1_g, mod, 0, 0)
    for layer in range(DEPTH):
        proj = _in_proj(h, w_in_t, layer)

        y_na, new_k, new_v = _ctx_attention(proj, layer, caches)
        caches = (new_k, new_v)
        y_na = _na_attention(proj, cache_k, cache_v, na_rpb[layer], y_na, layer)

        y_hy = None
        for n in (SEQ, DEC_SEQ):
            spectrum = _hy_filter_spectrum(n, tables[n][0], hy_w1[layer], hy_b1[layer],
                                           hy_freq[layer], hy_w2[layer], hy_b2[layer], hy_w3[layer],
                                           hy_decay[layer])
            y_hy = _hyena(proj, y_hy, layer, n, tables[n], spectrum,
                          hy_conv_w, hy_conv_b, hy_skip)

        y_gdn, s_f, s_b = _gdn(proj, None, states, layer, SEQ, zero_state, zero_state,
                               gdn_conv_w, gdn_a_log, gdn_dt_bias, gdn_norm_g)
        states = (s_f, s_b)
        y_gdn = _gdn(proj, y_gdn, None, layer, DEC_SEQ, state_fwd[:, layer], state_bwd[:, layer],
                     gdn_conv_w, gdn_a_log, gdn_dt_bias, gdn_norm_g)[0]

        x, h = _mix_out(y_na, y_hy, y_gdn, proj, b_gate, w_pa_b, w_pb_b, w_pc_b, w_out_b, x, mod, ln2_g, layer)
        act = _ffn_up(h, ffn_w_up, ffn_conv_w, ffn_conv_b, layer)
        x, h = _ffn_down(act, w_down_b, x, mod, layer, ln1_g, final_g)

    y_p, y_s = x, h
    cache_shape = (BATCH, DEPTH, SEQ, NA_HEADS, NA_DH)
    return (y_p.reshape(BATCH, SEQ, D), y_s.reshape(DEC_BATCH, DEC_SEQ, D),
            caches[0].reshape(cache_shape), caches[1].reshape(cache_shape), states[0], states[1])
```

```python
import functools
import math

import jax
import jax.numpy as jnp
import numpy as np
from jax import lax
from jax.experimental import pallas as pl
from jax.experimental.pallas import tpu as pltpu

F32 = jnp.float32
BF16 = jnp.bfloat16

D = 2048
BATCH, SEQ = 32, 256
DEC_BATCH, DEC_SEQ = 4, 1024
DEPTH = 2
GRID_W = 64
NA_HEADS, NA_DH = 8, 128
NA_WIDTH = NA_HEADS * NA_DH
NA_WIN_ROWS, NA_WIN_COLS = 8, 16
HY_WIDTH = 512
HY_ORDER = 2
HY_BANDS = 16
GDN_HEADS, GDN_DK, GDN_DV = 4, 128, 128
GDN_WIDTH = GDN_HEADS * GDN_DV
D_FF = 5632
N_BRANCH = 3
NORM_EPS = 1e-6
NEG_INF = -1e30

T_P = BATCH * SEQ
T_S = DEC_BATCH * DEC_SEQ
T = T_P + T_S
N_COND = 8

C_Q, C_K, C_V = 0, NA_WIDTH, 2 * NA_WIDTH
C_HY = 3 * NA_WIDTH
C_GDN = C_HY + 3 * HY_WIDTH
C_Z = C_GDN + 3 * GDN_HEADS * GDN_DK
C_BA = C_Z + GDN_WIDTH
C_GATE = C_BA + 4 * GDN_HEADS
N_IN = C_GATE + N_BRANCH * D
GATE_BLOCK = C_BA
GATE_SHIFT = C_GATE - C_BA

LANE = 128
VMEM_LIMIT = 56 * 1024 * 1024
HIGHEST = lax.Precision.HIGHEST


def _cparams(sem):
    return pltpu.CompilerParams(dimension_semantics=sem, vmem_limit_bytes=VMEM_LIMIT)


def _cond_row(tile, rows_per_tile):
    first_latent = T_P // rows_per_tile
    per_seq = DEC_SEQ // rows_per_tile
    return jnp.where(tile < first_latent, 0, 1 + (tile - first_latent) // per_seq)


def _mod_kernel(c_ref, w_ref, b_ref, o_ref):
    c = c_ref[...]
    s = (c * jax.nn.sigmoid(c)).astype(BF16)
    o_ref[0] = jnp.dot(s, w_ref[0].astype(BF16), preferred_element_type=F32) + b_ref[0]


def _mod_table(cond, w_mod, b_mod):
    tn = 1024
    n = 6 * D
    return pl.pallas_call(
        _mod_kernel,
        out_shape=jax.ShapeDtypeStruct((DEPTH, N_COND, n), F32),
        grid=(DEPTH, n // tn),
        in_specs=[
            pl.BlockSpec((N_COND, D), lambda l, j: (0, 0)),
            pl.BlockSpec((1, D, tn), lambda l, j: (l, 0, j)),
            pl.BlockSpec((1, 1, tn), lambda l, j: (l, 0, j)),
        ],
        out_specs=pl.BlockSpec((1, N_COND, tn), lambda l, j: (l, 0, j)),
        compiler_params=_cparams(("arbitrary", "arbitrary")),
        name="mod_table",
    )(cond, w_mod, b_mod.reshape(DEPTH, 1, n))


def _token_specs(x, tm):
    if not isinstance(x, tuple):
        return [pl.BlockSpec((tm, D), lambda i: (i, 0))], [x]
    n_ctx = T_P // tm
    return ([pl.BlockSpec((tm, D), lambda i: (jnp.minimum(i, n_ctx - 1), 0)),
             pl.BlockSpec((tm, D), lambda i: (jnp.maximum(i - n_ctx, 0), 0))], list(x))


def _token_tile(x_refs, tm):
    if len(x_refs) == 1:
        return x_refs[0][...]
    return jnp.where(pl.program_id(0) < T_P // tm, x_refs[0][...], x_refs[1][...])


def _modulated_norm(x, g, shift, scale):
    y = x * lax.rsqrt(jnp.mean(x * x, axis=-1, keepdims=True) + NORM_EPS) * g
    return (y * (1.0 + scale) + shift).astype(BF16)


NORM_TM = 1024


def _norm_mod_kernel(*refs):
    g_ref, sh_ref, sc_ref, o_ref = refs[-4:]
    o_ref[...] = _modulated_norm(_token_tile(refs[:-4], NORM_TM), g_ref[0], sh_ref[0], sc_ref[0])


def _norm_mod(x, g, mod, layer, shift_chunk):
    tm = NORM_TM
    x_specs, x_args = _token_specs(x, tm)
    return pl.pallas_call(
        _norm_mod_kernel,
        out_shape=jax.ShapeDtypeStruct((T, D), BF16),
        grid=(T // tm,),
        in_specs=x_specs + [
            pl.BlockSpec((1, 1, D), lambda i: (layer, 0, 0)),
            pl.BlockSpec((1, 1, D), lambda i: (layer * N_COND + _cond_row(i, tm), 0, shift_chunk)),
            pl.BlockSpec((1, 1, D), lambda i: (layer * N_COND + _cond_row(i, tm), 0, shift_chunk + 1)),
        ],
        out_specs=pl.BlockSpec((tm, D), lambda i: (i, 0)),
        compiler_params=_cparams(("arbitrary",)),
        name="norm_mod",
    )(*x_args, g.reshape(DEPTH, 1, D), mod, mod)


def _proj_kernel(h_ref, wt_ref, o_ref, wb_ref):
    @pl.when(pl.program_id(1) == 0)
    def _():
        wb_ref[...] = wt_ref[0].astype(BF16)

    o_ref[...] = lax.dot_general(h_ref[...], wb_ref[...], (((1,), (1,)), ((), ())), preferred_element_type=F32)


def _in_proj(h, w_in_t, layer):
    tm, tn = 1536, 1024
    return pl.pallas_call(
        _proj_kernel,
        out_shape=jax.ShapeDtypeStruct((T, N_IN), F32),
        grid=(pl.cdiv(N_IN, tn), T // tm),
        in_specs=[
            pl.BlockSpec((tm, D), lambda j, i: (i, 0)),
            pl.BlockSpec((1, tn, D), lambda j, i: (layer, j, 0)),
        ],
        out_specs=pl.BlockSpec((tm, tn), lambda j, i: (i, j)),
        scratch_shapes=[pltpu.VMEM((tn, D), BF16)],
        compiler_params=_cparams(("arbitrary", "arbitrary")),
        name="in_proj",
    )(h, w_in_t)


MIX_TM = 256


def _mix_out_kernel(yna_ref, yhy_ref, ygdn_ref, gl_ref, bg_ref, wpa_ref, wpb_ref, wpc_ref, wout_ref, *refs):
    gt_ref, g2_ref, sh2_ref, sc2_ref, o_ref, h2_ref = refs[-6:]
    width = gl_ref.shape[1]
    gl = pltpu.roll(gl_ref[...], width - GATE_SHIFT, 1)[:, :N_BRANCH * D] + bg_ref[0]
    gates = jax.nn.sigmoid(gl)
    merged = (gates[:, :D] * jnp.dot(yna_ref[...], wpa_ref[0], preferred_element_type=F32)
              + gates[:, D:2 * D] * jnp.dot(yhy_ref[...], wpb_ref[0], preferred_element_type=F32)
              + gates[:, 2 * D:] * jnp.dot(ygdn_ref[...], wpc_ref[0], preferred_element_type=F32))
    r = jnp.dot(merged.astype(BF16), wout_ref[0], preferred_element_type=F32)
    x = _token_tile(refs[:-6], MIX_TM) + gt_ref[0] * r
    o_ref[...] = x
    h2_ref[...] = _modulated_norm(x, g2_ref[0], sh2_ref[0], sc2_ref[0])


def _mix_out(y_na, y_hy, y_gdn, proj, b_gate, w_pa, w_pb, w_pc, w_out, x, mod, ln2_g, layer):
    tm = MIX_TM
    resident = functools.partial(pl.BlockSpec, pipeline_mode=pl.Buffered(1))
    x_specs, x_args = _token_specs(x, tm)
    mod_spec = lambda chunk: pl.BlockSpec((1, 1, D), lambda i: (layer * N_COND + _cond_row(i, tm), 0, chunk))
    row_spec = pl.BlockSpec((tm, D), lambda i: (i, 0))
    return pl.pallas_call(
        _mix_out_kernel,
        out_shape=[jax.ShapeDtypeStruct((T, D), F32), jax.ShapeDtypeStruct((T, D), BF16)],
        grid=(T // tm,),
        in_specs=[
            pl.BlockSpec((tm, NA_WIDTH), lambda i: (i, 0)),
            pl.BlockSpec((tm, HY_WIDTH), lambda i: (i, 0)),
            pl.BlockSpec((tm, GDN_WIDTH), lambda i: (i, 0)),
            pl.BlockSpec((tm, GATE_BLOCK), lambda i: (i, 1)),
            pl.BlockSpec((1, 1, N_BRANCH * D), lambda i: (layer, 0, 0)),
            resident((1, NA_WIDTH, D), lambda i: (layer, 0, 0)),
            resident((1, HY_WIDTH, D), lambda i: (layer, 0, 0)),
            resident((1, GDN_WIDTH, D), lambda i: (layer, 0, 0)),
            resident((1, D, D), lambda i: (layer, 0, 0)),
        ] + x_specs + [mod_spec(2), pl.BlockSpec((1, 1, D), lambda i: (layer, 0, 0)), mod_spec(3), mod_spec(4)],
        out_specs=[row_spec, row_spec],
        compiler_params=_cparams(("arbitrary",)),
        name="mix_out",
    )(y_na, y_hy, y_gdn, proj, b_gate.reshape(DEPTH, 1, N_BRANCH * D), w_pa, w_pb, w_pc, w_out, *x_args,
      mod, ln2_g.reshape(DEPTH, 1, D), mod, mod)


FFN_TM = 1024
FFN_SLAB = 16


def _ffn_up_kernel(h_ref, wa_ref, wb_ref, cwa_ref, cwb_ref, cba_ref, cbb_ref, o_ref, wab_ref, wbb_ref):
    m = pl.program_id(1)

    @pl.when(m == 0)
    def _():
        wab_ref[...] = wa_ref[0].astype(BF16)
        wbb_ref[...] = wb_ref[0].astype(BF16)

    h = h_ref[...]
    ups = (jnp.dot(h, wab_ref[...], preferred_element_type=F32), jnp.dot(h, wbb_ref[...], preferred_element_type=F32))
    taps = ((cwa_ref[0], cba_ref[0]), (cwb_ref[0], cbb_ref[0]))

    def swiglu(ua, ub):
        return (ua * jax.nn.sigmoid(ua) * ub).astype(BF16)

    full = [pltpu.roll(up, 1, 0) * cw[0:1] + up * cw[1:2] + pltpu.roll(up, FFN_TM - 1, 0) * cw[2:3] + cb
            for up, (cw, cb) in zip(ups, taps)]
    o_ref[...] = swiglu(*full)

    def redo_edge(r, starts):
        lo = r // FFN_SLAB * FFN_SLAB
        pick = lax.broadcasted_iota(jnp.int32, (FFN_SLAB, 1), 0) == r - lo
        halves = []
        for up, conv, (cw, cb) in zip(ups, full, taps):
            row = up[r:r + 1] * cw[1:2] + cb
            row = row + (up[r + 1:r + 2] * cw[2:3] if starts else up[r - 1:r] * cw[0:1])
            halves.append(jnp.where(pick, row, conv[lo:lo + FFN_SLAB]))
        o_ref[lo:lo + FFN_SLAB, :] = swiglu(*halves)

    redo_edge(0, True)
    redo_edge(FFN_TM - 1, False)

    @pl.when(m < T_P // FFN_TM)
    def _():
        for r in range(SEQ, FFN_TM, SEQ):
            redo_edge(r - 1, False)
            redo_edge(r, True)


def _ffn_up(h, w_up, conv_w, conv_b, layer):
    tn = 512
    nt = D_FF // tn
    conv_b = conv_b.reshape(DEPTH, 1, 2 * D_FF)
    return pl.pallas_call(
        _ffn_up_kernel,
        out_shape=jax.ShapeDtypeStruct((T, D_FF), BF16),
        grid=(nt, T // FFN_TM),
        in_specs=[
            pl.BlockSpec((FFN_TM, D), lambda j, i: (i, 0)),
            pl.BlockSpec((1, D, tn), lambda j, i: (layer, 0, j)),
            pl.BlockSpec((1, D, tn), lambda j, i: (layer, 0, nt + j)),
            pl.BlockSpec((1, 3, tn), lambda j, i: (layer, 0, j)),
            pl.BlockSpec((1, 3, tn), lambda j, i: (layer, 0, nt + j)),
            pl.BlockSpec((1, 1, tn), lambda j, i: (layer, 0, j)),
            pl.BlockSpec((1, 1, tn), lambda j, i: (layer, 0, nt + j)),
        ],
        out_specs=pl.BlockSpec((FFN_TM, tn), lambda j, i: (i, j)),
        scratch_shapes=[pltpu.VMEM((D, tn), BF16), pltpu.VMEM((D, tn), BF16)],
        compiler_params=_cparams(("arbitrary", "arbitrary")),
        name="ffn_up",
    )(h, w_up, w_up, conv_w, conv_w, conv_b, conv_b)


DOWN_TM = 256


def _ffn_down_kernel(a_ref, w_ref, x_ref, gt_ref, g_ref, sh_ref, sc_ref, o_ref, h_ref):
    x = x_ref[...] + gt_ref[0] * jnp.dot(a_ref[...], w_ref[0], preferred_element_type=F32)
    o_ref[...] = x
    h_ref[...] = _modulated_norm(x, g_ref[0], sh_ref[0], sc_ref[0])


def _ffn_down_final_kernel(a_ref, w_ref, x_ref, gt_ref, g_ref, yp_ref, ys_ref):
    x = x_ref[...] + gt_ref[0] * jnp.dot(a_ref[...], w_ref[0], preferred_element_type=F32)
    y = x * lax.rsqrt(jnp.mean(x * x, axis=-1, keepdims=True) + NORM_EPS) * g_ref[...]
    is_context = pl.program_id(0) < T_P // DOWN_TM

    @pl.when(is_context)
    def _():
        yp_ref[...] = y

    @pl.when(jnp.logical_not(is_context))
    def _():
        ys_ref[...] = y


def _ffn_down(act, w_down, x, mod, layer, ln1_g, final_g):
    tm = DOWN_TM
    last = layer == DEPTH - 1
    mod_spec = lambda lyr, chunk: pl.BlockSpec((1, 1, D), lambda i: (lyr * N_COND + _cond_row(i, tm), 0, chunk))
    row_spec = pl.BlockSpec((tm, D), lambda i: (i, 0))
    in_specs = [
        pl.BlockSpec((tm, D_FF), lambda i: (i, 0)),
        pl.BlockSpec((1, D_FF, D), lambda i: (layer, 0, 0), pipeline_mode=pl.Buffered(1)),
        row_spec,
        mod_spec(layer, 5),
    ]
    if last:
        n_ctx = T_P // tm
        return pl.pallas_call(
            _ffn_down_final_kernel,
            out_shape=[jax.ShapeDtypeStruct((T_P, D), F32), jax.ShapeDtypeStruct((T_S, D), F32)],
            grid=(T // tm,),
            in_specs=in_specs + [pl.BlockSpec((1, D), lambda i: (0, 0))],
            out_specs=[pl.BlockSpec((tm, D), lambda i: (jnp.minimum(i, n_ctx - 1), 0)),
                       pl.BlockSpec((tm, D), lambda i: (jnp.maximum(i - n_ctx, 0), 0))],
            compiler_params=_cparams(("arbitrary",)),
            name="ffn_down_final",
        )(act, w_down, x, mod, final_g.reshape(1, D))
    return pl.pallas_call(
        _ffn_down_kernel,
        out_shape=[jax.ShapeDtypeStruct((T, D), F32), jax.ShapeDtypeStruct((T, D), BF16)],
        grid=(T // tm,),
        in_specs=in_specs + [pl.BlockSpec((1, 1, D), lambda i: (layer + 1, 0, 0)),
                             mod_spec(layer + 1, 0), mod_spec(layer + 1, 1)],
        out_specs=[row_spec, row_spec],
        compiler_params=_cparams(("arbitrary",)),
        name="ffn_down",
    )(act, w_down, x, mod, ln1_g.reshape(DEPTH, 1, D), mod, mod)


ATT_SCALE = NA_DH ** -0.5


def _nt_dot(a, b):
    return lax.dot_general(a, b, (((1,), (1,)), ((), ())), preferred_element_type=F32)


def _head_slices():
    return [slice(h * NA_DH, (h + 1) * NA_DH) for h in range(NA_HEADS)]


def _ctx_attn_kernel(q_ref, k_ref, v_ref, o_ref, ko_ref, vo_ref):
    heads = _head_slices()
    q = [(q_ref[:, sl] * ATT_SCALE).astype(BF16) for sl in heads]
    s = [_nt_dot(q[h], k_ref[:, sl].astype(BF16)) for h, sl in enumerate(heads)]
    e = [x - jnp.max(x, axis=-1, keepdims=True) for x in s]
    e = [jnp.exp(x) for x in e]
    o = [jnp.dot(e[h].astype(BF16), v_ref[:, sl].astype(BF16), preferred_element_type=F32)
         for h, sl in enumerate(heads)]
    for h, sl in enumerate(heads):
        o_ref[:, sl] = (o[h] / jnp.sum(e[h], axis=-1, keepdims=True)).astype(BF16)
    ko_ref[0, 0] = k_ref[...]
    vo_ref[0, 0] = v_ref[...]


def _ctx_attention(proj, layer, caches):
    cache_shape = jax.ShapeDtypeStruct((BATCH, DEPTH, SEQ, NA_WIDTH), F32)
    cache_spec = pl.BlockSpec((1, 1, SEQ, NA_WIDTH), lambda b: (b, layer, 0, 0))
    qkv_specs = [pl.BlockSpec((SEQ, NA_WIDTH), lambda b, j=j: (b, j)) for j in range(3)]
    out_shape = [jax.ShapeDtypeStruct((T, NA_WIDTH), BF16), cache_shape, cache_shape]
    out_specs = [pl.BlockSpec((SEQ, NA_WIDTH), lambda b: (b, 0)), cache_spec, cache_spec]
    if caches is None:
        return pl.pallas_call(
            _ctx_attn_kernel, out_shape=out_shape, grid=(BATCH,), in_specs=qkv_specs, out_specs=out_specs,
            compiler_params=_cparams(("arbitrary",)), name="ctx_attention",
        )(proj, proj, proj)

    def body(q_ref, k_ref, v_ref, kc_ref, vc_ref, o_ref, ko_ref, vo_ref):
        del kc_ref, vc_ref
        _ctx_attn_kernel(q_ref, k_ref, v_ref, o_ref, ko_ref, vo_ref)

    any_spec = pl.BlockSpec(memory_space=pl.ANY)
    return pl.pallas_call(
        body, out_shape=out_shape, grid=(BATCH,), in_specs=qkv_specs + [any_spec, any_spec],
        out_specs=out_specs, input_output_aliases={3: 1, 4: 2},
        compiler_params=_cparams(("arbitrary",)), name="ctx_attention",
    )(proj, proj, proj, *caches)


NA_ROWS = DEC_SEQ // GRID_W
NA_WIN_TOK = NA_WIN_ROWS * GRID_W
NA_ROWS_PER_STEP = 4


def _na_window_start(r):
    return jnp.clip(r - NA_WIN_ROWS // 2, 0, NA_ROWS - NA_WIN_ROWS)


def _na_bias_table(rpb):
    col = np.arange(GRID_W)
    col_start = np.clip(col - NA_WIN_COLS // 2, 0, GRID_W - NA_WIN_COLS)
    col_mask = (col[None, :] >= col_start[:, None]) & (col[None, :] < col_start[:, None] + NA_WIN_COLS)
    rel_col = np.clip(col[None, :] - col[:, None] + NA_WIN_COLS - 1, 0, 2 * NA_WIN_COLS - 2)
    n_rel = 2 * NA_WIN_COLS - 1
    onehot = (rel_col[None] == np.arange(n_rel)[:, None, None]).astype(np.float32).reshape(n_rel, -1)
    expanded = jnp.dot(rpb.reshape(-1, n_rel), onehot, precision=HIGHEST)
    banded = jnp.where(col_mask, expanded.reshape(NA_HEADS, 2 * NA_WIN_ROWS - 1, GRID_W, GRID_W), NEG_INF)
    return jnp.concatenate([banded[:, :-1], banded[:, 1:]], axis=-1)


def _na_attn_kernel(q_ref, k_ref, v_ref, ck_ref, cv_ref, bias_ref, y_in_ref, o_ref):
    del y_in_ref
    heads = _head_slices()
    units = [(a, h) for a in range(NA_ROWS_PER_STEP) for h in range(NA_HEADS)]
    nu = range(len(units))
    r = [pl.program_id(1) * NA_ROWS_PER_STEP + a for a in range(NA_ROWS_PER_STEP)]
    win = [pl.ds(pl.multiple_of(_na_window_start(ra) * GRID_W, GRID_W), NA_WIN_TOK) for ra in r]
    rel0 = [NA_WIN_ROWS - 1 - (ra - _na_window_start(ra)) for ra in r]
    qrows = [slice(a * GRID_W, (a + 1) * GRID_W) for a in range(NA_ROWS_PER_STEP)]
    q = [(q_ref[:, sl] * ATT_SCALE).astype(BF16) for sl in heads]
    bias = [jnp.concatenate([bias_ref[h, rel0[a] + 2 * p] for p in range(NA_WIN_ROWS // 2)], axis=1)
            for a, h in units]
    s_win = [_nt_dot(q[h][qrows[a]], k_ref[win[a], heads[h]].astype(BF16)) + bias[i] for i, (a, h) in enumerate(units)]
    s_ctx_all = [_nt_dot(q[h], ck_ref[0, 0, :, sl].astype(BF16)) for h, sl in enumerate(heads)]
    s_ctx = [s_ctx_all[h][qrows[a]] for a, h in units]
    m = [jnp.maximum(jnp.max(s_win[i], axis=-1, keepdims=True), jnp.max(s_ctx[i], axis=-1, keepdims=True))
         for i in nu]
    e_win = [jnp.exp(s_win[i] - m[i]) for i in nu]
    e_ctx = [jnp.exp(s_ctx[i] - m[i]) for i in nu]
    o_win = [jnp.dot(e_win[i].astype(BF16), v_ref[win[a], heads[h]].astype(BF16), preferred_element_type=F32)
             for i, (a, h) in enumerate(units)]
    e_ctx_all = [jnp.concatenate([e_ctx[a * NA_HEADS + h] for a in range(NA_ROWS_PER_STEP)], axis=0).astype(BF16)
                 for h in range(NA_HEADS)]
    o_ctx_all = [jnp.dot(e_ctx_all[h], cv_ref[0, 0, :, sl].astype(BF16), preferred_element_type=F32)
                 for h, sl in enumerate(heads)]
    for i, (a, h) in enumerate(units):
        denom = jnp.sum(e_win[i], axis=-1, keepdims=True) + jnp.sum(e_ctx[i], axis=-1, keepdims=True)
        o_ref[qrows[a], heads[h]] = ((o_win[i] + o_ctx_all[h][qrows[a]]) / denom).astype(BF16)


def _na_attention(proj, cache_k, cache_v, rpb, y_na, layer):
    q_rows = NA_ROWS_PER_STEP * GRID_W
    steps = NA_ROWS // NA_ROWS_PER_STEP
    q_row0 = T_P // q_rows
    seq0 = T_P // DEC_SEQ
    kv_spec = lambda j: pl.BlockSpec((DEC_SEQ, NA_WIDTH), lambda b, r: (seq0 + b, j))
    ctx_spec = pl.BlockSpec((1, 1, SEQ, NA_WIDTH), lambda b, r: (b, layer, 0, 0))
    return pl.pallas_call(
        _na_attn_kernel,
        out_shape=jax.ShapeDtypeStruct((T, NA_WIDTH), BF16),
        grid=(DEC_BATCH, steps),
        in_specs=[
            pl.BlockSpec((q_rows, NA_WIDTH), lambda b, r: (q_row0 + b * steps + r, 0)),
            kv_spec(1), kv_spec(2), ctx_spec, ctx_spec,
            pl.BlockSpec((NA_HEADS, 2 * NA_WIN_ROWS - 2, GRID_W, 2 * GRID_W), lambda b, r: (0, 0, 0, 0),
                         pipeline_mode=pl.Buffered(1)),
            pl.BlockSpec(memory_space=pl.ANY),
        ],
        out_specs=pl.BlockSpec((q_rows, NA_WIDTH), lambda b, r: (q_row0 + b * steps + r, 0)),
        input_output_aliases={6: 0},
        compiler_params=_cparams(("arbitrary", "arbitrary")),
        name="na_attention",
    )(proj, proj, proj, cache_k.reshape(DEC_BATCH, DEPTH, SEQ, NA_WIDTH),
      cache_v.reshape(DEC_BATCH, DEPTH, SEQ, NA_WIDTH), _na_bias_table(rpb), y_na)


HY_CB = 256


def _dft_forward_kernel(o_ref, *, length):
    rows, cols = o_ref.shape
    f = lax.broadcasted_iota(jnp.int32, (rows, cols), 0) + pl.program_id(0) * rows
    t = lax.broadcasted_iota(jnp.int32, (rows, cols), 1)
    k = f & (length - 1)
    phase = (2 * k + 1) * t - jnp.where(f >= length, length, 0)
    phase = phase & (4 * length - 1)
    phase = jnp.where(phase >= 2 * length, phase - 4 * length, phase)
    o_ref[...] = jnp.cos(phase.astype(F32) * (math.pi / (2 * length))).astype(BF16)


def _dft_tables(length):
    tr = 256
    fwd = pl.pallas_call(
        functools.partial(_dft_forward_kernel, length=length),
        out_shape=jax.ShapeDtypeStruct((2 * length, length), BF16),
        grid=(2 * length // tr,), out_specs=pl.BlockSpec((tr, length), lambda i: (i, 0)),
        compiler_params=_cparams(("arbitrary",)), name="dft_table",
    )()
    inv = jnp.concatenate([fwd[:length].T, -fwd[length:].T], axis=1) * (1.0 / length)
    return fwd, inv


def _hy_filter_kernel(tn_ref, band_ref, w1_ref, b1_ref, fr_ref, w2_ref, b2_ref, w3_ref, dec_ref,
                      f_ref, o_ref, *, length):
    t_norm = tn_ref[...]
    t_idx = lax.broadcasted_iota(jnp.int32, (length, LANE), 0).astype(F32)
    lane = lax.broadcasted_iota(jnp.int32, (length, LANE), 1)
    ang = (2.0 * math.pi / length) * t_idx * band_ref[...]
    z = jnp.where(lane == 0, t_norm,
                  jnp.where(lane <= HY_BANDS, jnp.cos(ang), jnp.where(lane <= 2 * HY_BANDS, jnp.sin(ang), 0.0)))
    hdn = jnp.sin(fr_ref[0:1] * (jnp.dot(z, w1_ref[...], precision=HIGHEST, preferred_element_type=F32)
                                 + b1_ref[...]))
    hdn = jnp.sin(fr_ref[1:2] * (jnp.dot(hdn, w2_ref[...], precision=HIGHEST, preferred_element_type=F32)
                                 + b2_ref[...]))
    filt = jnp.dot(hdn.astype(BF16), w3_ref[...].astype(BF16), preferred_element_type=F32)
    filt = filt * jnp.exp(-t_norm * jnp.abs(dec_ref[...]))
    first = lax.broadcasted_iota(jnp.int32, (length, HY_WIDTH), 0) == 0
    for o in range(HY_ORDER):
        fwd = filt[:, (2 * o) * HY_WIDTH:(2 * o + 1) * HY_WIDTH]
        bwd = filt[:, (2 * o + 1) * HY_WIDTH:(2 * o + 2) * HY_WIDTH]
        bwd = jnp.where(first, 0.0, pltpu.roll(bwd, 1, 0))
        o_ref[o, :length] = jnp.dot(f_ref[:length], (fwd + bwd).astype(BF16), preferred_element_type=F32)
        o_ref[o, length:] = jnp.dot(f_ref[length:], (bwd - fwd).astype(BF16), preferred_element_type=F32)


def _hy_filter_spectrum(length, f_tab, w1, b1, freq, w2, b2, w3, decay):
    emb = 1 + 2 * HY_BANDS
    t_norm = jnp.linspace(0.0, 1.0, length, dtype=F32).reshape(length, 1)
    bands = np.zeros((1, LANE), np.float32)
    bands[0, 1:1 + HY_BANDS] = bands[0, 1 + HY_BANDS:emb] = np.linspace(1e-4, HY_BANDS - 1, HY_BANDS,
                                                                        dtype=np.float32)
    w1p = jnp.zeros((LANE, w1.shape[1]), F32).at[:emb].set(w1)
    return pl.pallas_call(
        functools.partial(_hy_filter_kernel, length=length),
        out_shape=jax.ShapeDtypeStruct((HY_ORDER, 2 * length, HY_WIDTH), F32),
        compiler_params=pltpu.CompilerParams(vmem_limit_bytes=VMEM_LIMIT),
        name="hyena_filter",
    )(t_norm, jnp.asarray(bands), w1p, b1.reshape(1, -1), freq, w2, b2.reshape(1, -1), w3,
      decay.reshape(1, -1), f_tab)


def _hyena_kernel(v_ref, x1_ref, x2_ref, cwv_ref, cw1_ref, cw2_ref, cbv_ref, cb1_ref, cb2_ref, skip_ref,
                  hs_ref, f_ref, g_ref, *rest, length, parts):
    o_ref = rest[-1]
    n = range(len(parts))

    def short_conv(u_ref, cw_ref, cb_ref, rs, cs):
        u = u_ref[rs, cs]
        row = lax.broadcasted_iota(jnp.int32, u.shape, 0)
        prev = jnp.where(row == 0, 0.0, pltpu.roll(u, 1, 0))
        nxt = jnp.where(row == length - 1, 0.0, pltpu.roll(u, length - 1, 0))
        return prev * cw_ref[0:1, cs] + u * cw_ref[1:2, cs] + nxt * cw_ref[2:3, cs] + cb_ref[:, cs]

    z = [short_conv(v_ref, cwv_ref, cbv_ref, rs, cs) for rs, cs in parts]
    gates = [(short_conv(x1_ref, cw1_ref, cb1_ref, rs, cs), short_conv(x2_ref, cw2_ref, cb2_ref, rs, cs))
             for rs, cs in parts]
    for o in range(HY_ORDER):
        zf = [jnp.dot(f_ref[...], z[i].astype(BF16), preferred_element_type=F32) for i in n]
        p = []
        for i, (rs, cs) in enumerate(parts):
            zc, zs = zf[i][:length], zf[i][length:]
            h_re, h_im = hs_ref[o, :length, cs], hs_ref[o, length:, cs]
            p.append(jnp.concatenate([zc * h_re + zs * h_im, zc * h_im - zs * h_re], axis=0).astype(BF16))
        conv = [jnp.dot(g_ref[...], p[i], preferred_element_type=F32) for i in n]
        z = [gates[i][o] * (conv[i] + z[i] * skip_ref[o:o + 1, cs]) for i, (rs, cs) in enumerate(parts)]
    for i, (rs, cs) in enumerate(parts):
        o_ref[rs, cs] = z[i].astype(BF16)


def _hyena(proj, y_hy, layer, length, tables, spectrum, conv_w, conv_b, skip):
    seqs = 2 if length == SEQ else 1
    rows = seqs * length
    nblk = (T_P if length == SEQ else T_S) // rows
    row0 = 0 if length == SEQ else T_P // rows
    if seqs > 1:
        parts = [(slice(b * length, (b + 1) * length), slice(None)) for b in range(seqs)]
    else:
        parts = [(slice(None), slice(c * HY_CB, (c + 1) * HY_CB)) for c in range(HY_WIDTH // HY_CB)]
    col0 = C_HY // HY_WIDTH
    u_spec = lambda part: pl.BlockSpec((rows, HY_WIDTH), lambda s: (row0 + s, col0 + part))
    cw_spec = lambda part: pl.BlockSpec((1, 3, HY_WIDTH), lambda s: (layer, 0, part))
    cb_spec = lambda part: pl.BlockSpec((1, 1, HY_WIDTH), lambda s: (layer, 0, part))
    resident = functools.partial(pl.BlockSpec, pipeline_mode=pl.Buffered(1))
    tab_specs = [resident(t.shape, lambda s: (0, 0)) for t in tables]
    in_specs = ([u_spec(0), u_spec(1), u_spec(2), cw_spec(0), cw_spec(1), cw_spec(2),
                 cb_spec(0), cb_spec(1), cb_spec(2),
                 pl.BlockSpec((1, HY_ORDER, HY_WIDTH), lambda s: (layer, 0, 0)),
                 resident((HY_ORDER, 2 * length, HY_WIDTH), lambda s: (0, 0, 0))] + tab_specs)
    args = [proj] * 3 + [conv_w] * 3 + [conv_b.reshape(DEPTH, 1, -1)] * 3 + [skip, spectrum] + list(tables)
    aliases = {}
    if y_hy is not None:
        in_specs.append(pl.BlockSpec(memory_space=pl.ANY))
        args.append(y_hy)
        aliases = {len(args) - 1: 0}

    def body(v_ref, x1_ref, x2_ref, cwv, cw1, cw2, cbv, cb1, cb2, skip_ref, hs_ref, *rest):
        _hyena_kernel(v_ref, x1_ref, x2_ref, cwv.at[0], cw1.at[0], cw2.at[0], cbv.at[0], cb1.at[0], cb2.at[0],
                      skip_ref.at[0], hs_ref, *rest, length=length, parts=parts)

    return pl.pallas_call(
        body,
        out_shape=jax.ShapeDtypeStruct((T, HY_WIDTH), BF16),
        grid=(nblk,), in_specs=in_specs,
        out_specs=pl.BlockSpec((rows, HY_WIDTH), lambda s: (row0 + s, 0)),
        input_output_aliases=aliases,
        compiler_params=_cparams(("arbitrary",)),
        name="hyena",
    )(*args)


GC = 128
GDN_SCALE = GDN_DK ** -0.5
GDN_CTX_PER_STEP = 2


def _state_slot(b, d):
    return (2 * b + d) * GDN_HEADS


def _merge_masks(lower):
    ri = lax.broadcasted_iota(jnp.int32, (GC, GC), 0)
    ci = lax.broadcasted_iota(jnp.int32, (GC, GC), 1)
    hi, lo = (ri, ci) if lower else (ci, ri)
    return [((hi >> (b + 1)) == (lo >> (b + 1))) & ((hi >> b) > (lo >> b)) for b in range(int(math.log2(GC)))]


def _unit_tri_inverse(a, masks):
    ri = lax.broadcasted_iota(jnp.int32, (GC, GC), 0)
    ci = lax.broadcasted_iota(jnp.int32, (GC, GC), 1)
    eye = (ri == ci).astype(F32)
    n = range(len(a))
    t = [eye - jnp.where(masks[i][0], a[i], 0.0) for i in n]
    for level in range(1, len(masks[0])):
        tb = [t[i].astype(BF16) for i in n]
        ta = [jnp.dot(tb[i], jnp.where(masks[i][level], a[i], 0.0).astype(BF16), preferred_element_type=F32)
              for i in n]
        tat = [jnp.dot(ta[i].astype(BF16), tb[i], preferred_element_type=F32) for i in n]
        t = [t[i] - tat[i] for i in n]
    return t


def _gdn_kernel(q_ref, k_ref, v_ref, z_ref, ba_ref, cwq_ref, cwk_ref, cwv_ref, alog_ref, dt_ref, ng_ref,
                sf0_ref, sb0_ref, *rest, length, nb):
    y_ref, sf_ref, sb_ref, qn_ref, kn_ref, vn_ref, beta_ref, g_ref, of_ref, ob_ref, s_ref = rest[-11:]
    n_chunks = length // GC
    rows_total = nb * length
    row = lax.broadcasted_iota(jnp.int32, (rows_total, GDN_WIDTH), 0) & (length - 1)
    first, last = row == 0, row == length - 1

    def conv_silu(u_ref, cw_ref):
        u = u_ref[...]
        prev = jnp.where(first, 0.0, pltpu.roll(u, 1, 0))
        nxt = jnp.where(last, 0.0, pltpu.roll(u, rows_total - 1, 0))
        c = prev * cw_ref[0:1] + u * cw_ref[1:2] + nxt * cw_ref[2:3]
        return c * jax.nn.sigmoid(c)

    q = conv_silu(q_ref, cwq_ref)
    k = conv_silu(k_ref, cwk_ref)
    vn_ref[...] = conv_silu(v_ref, cwv_ref)
    for h in range(GDN_HEADS):
        sl = slice(h * GDN_DK, (h + 1) * GDN_DK)
        qh, kh = q[:, sl], k[:, sl]
        qn_ref[:, sl] = qh * lax.rsqrt(jnp.sum(qh * qh, axis=-1, keepdims=True) + NORM_EPS) * GDN_SCALE
        kn_ref[:, sl] = kh * lax.rsqrt(jnp.sum(kh * kh, axis=-1, keepdims=True) + NORM_EPS)
    ba = ba_ref[...]
    beta_ref[...] = jax.nn.sigmoid(ba)
    g_ref[...] = -jnp.exp(alog_ref[...]) * jax.nn.softplus(ba + dt_ref[...])

    ri = lax.broadcasted_iota(jnp.int32, (GC, GC), 0)
    ci = lax.broadcasted_iota(jnp.int32, (GC, GC), 1)
    lower = (ri >= ci).astype(F32)
    upper = (ri <= ci).astype(F32)

    def chunk_step(n, carry):
        scans = [(b, d) for b in range(nb) for d in range(2)]
        rows = {(b, d): pl.ds(pl.multiple_of(b * length + (n if d == 0 else n_chunks - 1 - n) * GC, GC), GC)
                for b, d in scans}
        loaded = {}
        for bd in scans:
            loaded[bd] = (g_ref[rows[bd], :], beta_ref[rows[bd], :])
            for h in range(GDN_HEADS):
                sl = slice(h * GDN_DK, (h + 1) * GDN_DK)
                loaded[bd + (h,)] = (qn_ref[rows[bd], sl], kn_ref[rows[bd], sl], vn_ref[rows[bd], sl],
                                     s_ref[_state_slot(*bd) + h])
        units = [(b, d, h) for b, d in scans for h in range(GDN_HEADS)]
        incl = [(ri >= ci), (ri <= ci)]
        strict = [(ri > ci), (ri < ci)]
        masks = [_merge_masks(True), _merge_masks(False)]
        gcs = {(b, d): jnp.dot((lower, upper)[d], loaded[b, d][0], precision=HIGHEST, preferred_element_type=F32)
               for b, d in scans}
        gcs_t = {bd: g.T for bd, g in gcs.items()}
        edge = [GC - 1, 0]
        g_col = [gcs[b, d][:, 8 + 4 * d + h:9 + 4 * d + h] for b, d, h in units]
        g_row = [gcs_t[b, d][8 + 4 * d + h:9 + 4 * d + h, :] for b, d, h in units]
        g_end = [gcs_t[b, d][8 + 4 * d + h:9 + 4 * d + h, edge[d]:edge[d] + 1] for b, d, h in units]
        b_col = [loaded[b, d][1][:, 4 * d + h:4 * d + h + 1] for b, d, h in units]
        qc, kc, vc, st = (list(x) for x in zip(*(loaded[u] for u in units)))
        nu = range(len(units))
        decay = [jnp.where(incl[d], jnp.exp(jnp.where(incl[d], g_col[i] - g_row[i], 0.0)), 0.0)
                 for i, (b, d, h) in enumerate(units)]
        kb = [kc[i] * b_col[i] for i in nu]
        kcb = [kc[i].astype(BF16) for i in nu]
        kq = [_nt_dot(jnp.concatenate([kb[i], qc[i]], axis=0).astype(BF16), kcb[i]) for i in nu]
        kk = [x[:GC] for x in kq]
        qk = [x[GC:] for x in kq]
        a = [jnp.where(strict[d], kk[i] * decay[i], 0.0) for i, (b, d, h) in enumerate(units)]
        t = _unit_tri_inverse(a, [masks[d] for b, d, h in units])
        e_col = [jnp.exp(g_col[i]) for i in nu]
        rhs = [jnp.concatenate([vc[i] * b_col[i], kb[i] * e_col[i]], axis=1).astype(BF16) for i in nu]
        sol = [jnp.dot(t[i].astype(BF16), rhs[i], preferred_element_type=F32) for i in nu]
        attn = [jnp.where(incl[d], qk[i] * decay[i], 0.0).astype(BF16) for i, (b, d, h) in enumerate(units)]
        sb = [st[i].astype(BF16) for i in nu]
        wq = [jnp.dot(jnp.concatenate([sol[i][:, GDN_DV:], qc[i] * e_col[i]], axis=0).astype(BF16), sb[i],
                      preferred_element_type=F32) for i in nu]
        ws = [x[:GC] for x in wq]
        qs = [x[GC:] for x in wq]
        v_new = [(sol[i][:, :GDN_DV] - ws[i]).astype(BF16) for i in nu]
        k_dec_t = [(kc[i] * jnp.exp(g_end[i] - g_col[i])).T.astype(BF16) for i in nu]
        akv = [jnp.dot(jnp.concatenate([attn[i], k_dec_t[i]], axis=0), v_new[i], preferred_element_type=F32)
               for i in nu]
        av = [x[:GC] for x in akv]
        kv = [x[GC:] for x in akv]
        results = [(qs[i] + av[i], st[i] * jnp.exp(g_end[i]) + kv[i]) for i in nu]
        for (b, d, h), (o, s) in zip(units, results):
            (of_ref if d == 0 else ob_ref)[rows[b, d], h * GDN_DV:(h + 1) * GDN_DV] = o
            s_ref[_state_slot(b, d) + h] = s
        return carry

    for b in range(nb):
        s_ref[pl.ds(_state_slot(b, 0), GDN_HEADS)] = sf0_ref[b]
        s_ref[pl.ds(_state_slot(b, 1), GDN_HEADS)] = sb0_ref[b]
    lax.fori_loop(0, n_chunks, chunk_step, 0)
    for b in range(nb):
        sf_ref[b, 0] = s_ref[pl.ds(_state_slot(b, 0), GDN_HEADS)]
        sb_ref[b, 0] = s_ref[pl.ds(_state_slot(b, 1), GDN_HEADS)]
    for h in range(GDN_HEADS):
        sl = slice(h * GDN_DV, (h + 1) * GDN_DV)
        o = of_ref[:, sl] + ob_ref[:, sl]
        o = o * lax.rsqrt(jnp.mean(o * o, axis=-1, keepdims=True) + NORM_EPS) * ng_ref[...]
        zh = z_ref[:, sl]
        y_ref[:, sl] = (o * (zh * jax.nn.sigmoid(zh))).astype(BF16)


def _gdn(proj, y_gdn, states_out, layer, length, s_f0, s_b0, conv_w, a_log, dt_bias, norm_g):
    nb = GDN_CTX_PER_STEP if length == SEQ else 1
    rows = nb * length
    nblk = (T_P if length == SEQ else T_S) // rows
    row0 = 0 if length == SEQ else T_P // rows
    col = lambda c: c // GDN_WIDTH
    blk = lambda c: pl.BlockSpec((rows, GDN_WIDTH), lambda s: (row0 + s, col(c)))
    cw = lambda part: pl.BlockSpec((1, 3, GDN_WIDTH), lambda s: (layer, 0, part))
    vec = lambda: pl.BlockSpec((1, 1, LANE), lambda s: (layer, 0, 0))
    st_in = pl.BlockSpec((nb, GDN_HEADS, GDN_DK, GDN_DV), lambda s: (s, 0, 0, 0))
    st_out = pl.BlockSpec((nb, 1, GDN_HEADS, GDN_DK, GDN_DV), lambda s: (s, layer, 0, 0, 0))
    pad_lanes = lambda x, off: jnp.zeros((DEPTH, 1, LANE), F32).at[:, 0, off:off + 2 * GDN_HEADS].set(
        x.reshape(DEPTH, -1))
    in_specs = [blk(C_GDN), blk(C_GDN + GDN_WIDTH), blk(C_GDN + 2 * GDN_WIDTH), blk(C_Z),
                pl.BlockSpec((rows, LANE), lambda s: (row0 + s, C_BA // LANE)),
                cw(0), cw(1), cw(2), vec(), vec(), vec(), st_in, st_in]
    args = [proj] * 5 + [conv_w] * 3 + [pad_lanes(a_log, 2 * GDN_HEADS), pad_lanes(dt_bias, 2 * GDN_HEADS),
                                        norm_g.reshape(DEPTH, 1, GDN_DV), s_f0, s_b0]
    st_shape = jax.ShapeDtypeStruct((nblk * nb, DEPTH, GDN_HEADS, GDN_DK, GDN_DV), F32)
    aliases = {}
    for carried, out_idx in ((y_gdn, 0),) + (((states_out[0], 1), (states_out[1], 2)) if states_out else ()):
        if carried is not None:
            in_specs.append(pl.BlockSpec(memory_space=pl.ANY))
            args.append(carried)
            aliases[len(args) - 1] = out_idx

    def body(q_ref, k_ref, v_ref, z_ref, ba_ref, cwq, cwk, cwv, alog, dt, ng, sf0, sb0, *rest):
        _gdn_kernel(q_ref, k_ref, v_ref, z_ref, ba_ref, cwq.at[0], cwk.at[0], cwv.at[0], alog.at[0], dt.at[0],
                    ng.at[0], sf0, sb0, *rest, length=length, nb=nb)

    seq_buf = lambda w: pltpu.VMEM((rows, w), F32)
    return pl.pallas_call(
        body,
        out_shape=[jax.ShapeDtypeStruct((T, GDN_WIDTH), BF16), st_shape, st_shape],
        grid=(nblk,), in_specs=in_specs,
        out_specs=[pl.BlockSpec((rows, GDN_WIDTH), lambda s: (row0 + s, 0)), st_out, st_out],
        scratch_shapes=[seq_buf(GDN_WIDTH), seq_buf(GDN_WIDTH), seq_buf(GDN_WIDTH), seq_buf(LANE), seq_buf(LANE),
                        seq_buf(GDN_WIDTH), seq_buf(GDN_WIDTH),
                        pltpu.VMEM((nb * 2 * GDN_HEADS, GDN_DK, GDN_DV), F32)],
        input_output_aliases=aliases,
        compiler_params=_cparams(("arbitrary",)),
        name="gated_deltanet",
    )(*args)


def kernel(x_prompt, x_sample, cache_k, cache_v, state_fwd, state_bwd, c, c_ctx, ln1_g, ln2_g, w_mod, b_mod,
           w_in, na_rpb, hy_conv_w, hy_conv_b, hy_w1, hy_b1, hy_freq, hy_w2, hy_b2, hy_w3, hy_decay, hy_skip,
           gdn_conv_w, gdn_a_log, gdn_dt_bias, gdn_norm_g, w_pa, w_pb, w_pc, b_gate, w_out, ffn_w_up,
           ffn_conv_w, ffn_conv_b, ffn_w_down, final_g):
    x = (x_prompt.reshape(T_P, D), x_sample.reshape(T_S, D))
    cond = jnp.concatenate([c_ctx[None], c, jnp.zeros((N_COND - 1 - DEC_BATCH, D), F32)], axis=0)
    mod = _mod_table(cond, w_mod, b_mod).reshape(DEPTH * N_COND, 1, 6 * D)

    w_pa_b, w_pb_b, w_pc_b, w_out_b = (w.astype(BF16) for w in (w_pa, w_pb, w_pc, w_out))
    w_down_b = ffn_w_down.astype(BF16)
    w_in_t = jnp.swapaxes(w_in, 1, 2)
    tables = {n: _dft_tables(n) for n in (SEQ, DEC_SEQ)}
    zero_state = jnp.zeros((BATCH, GDN_HEADS, GDN_DK, GDN_DV), F32)

    caches = None
    states = None
    h = _norm_mod(x, ln1_g, mod, 0, 0)
    for layer in range(DEPTH):
        proj = _in_proj(h, w_in_t, layer)

        y_na, new_k, new_v = _ctx_attention(proj, layer, caches)
        caches = (new_k, new_v)
        y_na = _na_attention(proj, cache_k, cache_v, na_rpb[layer], y_na, layer)

        y_hy = None
        for n in (SEQ, DEC_SEQ):
            spectrum = _hy_filter_spectrum(n, tables[n][0], hy_w1[layer], hy_b1[layer],
                                           hy_freq[layer], hy_w2[layer], hy_b2[layer], hy_w3[layer],
                                           hy_decay[layer])
            y_hy = _hyena(proj, y_hy, layer, n, tables[n], spectrum,
                          hy_conv_w, hy_conv_b, hy_skip)

        y_gdn, s_f, s_b = _gdn(proj, None, states, layer, SEQ, zero_state, zero_state,
                               gdn_conv_w, gdn_a_log, gdn_dt_bias, gdn_norm_g)
        states = (s_f, s_b)
        y_gdn = _gdn(proj, y_gdn, None, layer, DEC_SEQ, state_fwd[:, layer], state_bwd[:, layer],
                     gdn_conv_w, gdn_a_log, gdn_dt_bias, gdn_norm_g)[0]

        x, h = _mix_out(y_na, y_hy, y_gdn, proj, b_gate, w_pa_b, w_pb_b, w_pc_b, w_out_b, x, mod, ln2_g, layer)
        act = _ffn_up(h, ffn_w_up, ffn_conv_w, ffn_conv_b, layer)
        x, h = _ffn_down(act, w_down_b, x, mod, layer, ln1_g, final_g)

    y_p, y_s = x, h
    cache_shape = (BATCH, DEPTH, SEQ, NA_HEADS, NA_DH)
    return (y_p.reshape(BATCH, SEQ, D), y_s.reshape(DEC_BATCH, DEC_SEQ, D),
            caches[0].reshape(cache_shape), caches[1].reshape(cache_shape), states[0], states[1])
```
